```python
import math
import jax, jax.numpy as jnp
from jax import lax
import numpy as np

D_MODEL = 1024
BATCH = 16
SEQ = 256
DEPTH = 2
DEC_BATCH = 2
DEC_SEQ = 2048
PAST_LEN = 256

GRID_W = 64
N_MIXERS = 2
N_ATTN_LAYERS = (DEPTH + 1) // 2
N_SSM_LAYERS = DEPTH // 2
N_HEADS = 16
N_KV_HEADS = 4
HEAD_DIM = D_MODEL // N_HEADS
Q_PER_KV = N_HEADS // N_KV_HEADS
QKV_DIM = (N_HEADS + 2 * N_KV_HEADS) * HEAD_DIM
WINDOW = 128
BLOCK = 128
ATTN_SCALE = HEAD_DIM ** -0.5
ROPE_BASE = 10000.0
GROUP_CH = 16
N_GROUPS = D_MODEL // GROUP_CH
STATE_DIM = 64
D_FF = 4 * D_MODEL
N_MOD = 6
RMS_EPS = 1e-6
NEG_INF = -1e30

kernel_name = "hybrid_swa_s5_diffusion_step"


def _rmsnorm(x, g):
    x32 = x.astype(jnp.float32)
    y = x32 * lax.rsqrt(jnp.mean(x32 * x32, axis=-1, keepdims=True) + RMS_EPS)
    return (y * g.astype(jnp.float32)).astype(x.dtype)


def _modulation(cvec, w_mod, b_mod):
    return jnp.split(jax.nn.silu(cvec) @ w_mod + b_mod, N_MOD, axis=-1)


def _modulate(x, g, shift, scale):
    return _rmsnorm(x, g) * (1 + scale) + shift


def _mlp(h, w1, w2):
    a = jax.nn.relu(h @ w1)
    return (a * a) @ w2


def _split_qkv(qkv):
    b, l = qkv.shape[:2]
    q = qkv[..., :N_HEADS * HEAD_DIM].reshape(b, l, N_HEADS, HEAD_DIM)
    k = qkv[..., N_HEADS * HEAD_DIM:(N_HEADS + N_KV_HEADS) * HEAD_DIM].reshape(b, l, N_KV_HEADS, HEAD_DIM)
    v = qkv[..., (N_HEADS + N_KV_HEADS) * HEAD_DIM:].reshape(b, l, N_KV_HEADS, HEAD_DIM)
    return q, k, v


def _grid_positions(n_tokens):
    rows = n_tokens // GRID_W
    row = jnp.repeat(jnp.arange(rows, dtype=jnp.float32), GRID_W)
    col = jnp.tile(jnp.arange(GRID_W, dtype=jnp.float32), rows)
    return row, col


def _rotate(x, ang):
    cos = jnp.cos(ang)[None, :, None, :].astype(x.dtype)
    sin = jnp.sin(ang)[None, :, None, :].astype(x.dtype)
    x1, x2 = jnp.split(x, 2, axis=-1)
    return jnp.concatenate([x1 * cos - x2 * sin, x1 * sin + x2 * cos], axis=-1)


def _axial_rope(x, row, col):
    n_freq = HEAD_DIM // 4
    freqs = ROPE_BASE ** (-jnp.arange(n_freq, dtype=jnp.float32) / n_freq)
    half = HEAD_DIM // 2
    return jnp.concatenate([_rotate(x[..., :half], row[:, None] * freqs),
                            _rotate(x[..., half:], col[:, None] * freqs)], axis=-1)


def _sink_column(sink, lead_shape):
    s = sink.astype(jnp.float32).reshape((1,) * (len(lead_shape) - 3) + (N_KV_HEADS, Q_PER_KV, 1, 1))
    return jnp.broadcast_to(s, lead_shape + (1,))


def _context_attention(q, k, v, sink):
    b, l = q.shape[:2]
    nb = l // BLOCK
    qb = q.reshape(b, nb, BLOCK, N_KV_HEADS, Q_PER_KV, HEAD_DIM).transpose(1, 0, 2, 3, 4, 5)

    def one_block(q_blk):
        s = jnp.einsum('bqkgd,bckd->bkgqc', q_blk, k).astype(jnp.float32) * ATTN_SCALE
        logits = jnp.concatenate([s, _sink_column(sink, s.shape[:-1])], axis=-1)
        p = jax.nn.softmax(logits, axis=-1)[..., :-1].astype(v.dtype)
        o = jnp.einsum('bkgqc,bckd->bqkgd', p, v)
        return o.reshape(b, BLOCK, N_HEADS * HEAD_DIM)

    out = lax.map(one_block, qb)
    return out.transpose(1, 0, 2, 3).reshape(b, l, N_HEADS * HEAD_DIM)


def _latent_window_attention(q, k, v, k_ctx, v_ctx, sink):
    b, l = q.shape[:2]
    nb = l // BLOCK
    lc = k_ctx.shape[1]
    pad = ((0, 0), (BLOCK, BLOCK), (0, 0), (0, 0))
    kp = jnp.pad(k, pad).reshape(b, nb + 2, BLOCK, N_KV_HEADS, HEAD_DIM)
    vp = jnp.pad(v, pad).reshape(b, nb + 2, BLOCK, N_KV_HEADS, HEAD_DIM)
    kw = jnp.concatenate([kp[:, :-2], kp[:, 1:-1], kp[:, 2:]], axis=2)
    vw = jnp.concatenate([vp[:, :-2], vp[:, 1:-1], vp[:, 2:]], axis=2)
    qb = q.reshape(b, nb, BLOCK, N_KV_HEADS, Q_PER_KV, HEAD_DIM)
    s_win = jnp.einsum('bnqkgd,bnskd->bnkgqs', qb, kw).astype(jnp.float32) * ATTN_SCALE
    s_ctx = jnp.einsum('bnqkgd,bckd->bnkgqc', qb, k_ctx).astype(jnp.float32) * ATTN_SCALE
    r = jnp.arange(BLOCK)[:, None]
    j = jnp.arange(3 * BLOCK)[None, :]
    band = jnp.abs(j - BLOCK - r) <= WINDOW
    kpos = jnp.arange(nb)[:, None] * BLOCK - BLOCK + jnp.arange(3 * BLOCK)[None, :]
    valid = (kpos >= 0) & (kpos < l)
    mask = (band[None, :, :] & valid[:, None, :])[None, :, None, None]
    logits = jnp.concatenate([jnp.where(mask, s_win, NEG_INF), s_ctx,
                              _sink_column(sink, s_ctx.shape[:-1])], axis=-1)
    p = jax.nn.softmax(logits, axis=-1)
    p_win = p[..., :3 * BLOCK].astype(v.dtype)
    p_ctx = p[..., 3 * BLOCK:3 * BLOCK + lc].astype(v.dtype)
    o = (jnp.einsum('bnkgqs,bnskd->bnqkgd', p_win, vw)
         + jnp.einsum('bnkgqc,bckd->bnqkgd', p_ctx, v_ctx))
    return o.reshape(b, l, N_HEADS * HEAD_DIM)


def _complex_affine_combine(left, right):
    ar1, ai1, br1, bi1 = left
    ar2, ai2, br2, bi2 = right
    return (ar1 * ar2 - ai1 * ai2,
            ar1 * ai2 + ai1 * ar2,
            ar2 * br1 - ai2 * bi1 + br2,
            ar2 * bi1 + ai2 * br1 + bi2)


def _ssm_scan(u, lam_re, lam_im, log_dt, b_re, b_im, s0_re, s0_im, reverse):
    lam_re = lam_re.astype(jnp.float32)
    lam_im = lam_im.astype(jnp.float32)
    dt = jnp.exp(log_dt.astype(jnp.float32))[:, None]
    mag = jnp.exp(lam_re * dt)
    ang = lam_im * dt
    ab_re = mag * jnp.cos(ang)
    ab_im = mag * jnp.sin(ang)
    den = lam_re * lam_re + lam_im * lam_im
    num_re = ab_re - 1.0
    num_im = ab_im
    f_re = (num_re * lam_re + num_im * lam_im) / den
    f_im = (num_im * lam_re - num_re * lam_im) / den
    b_re = b_re.astype(jnp.float32)
    b_im = b_im.astype(jnp.float32)
    bb_re = f_re[..., None] * b_re - f_im[..., None] * b_im
    bb_im = f_re[..., None] * b_im + f_im[..., None] * b_re
    bu_re = jnp.einsum('blgc,gpc->blgp', u, bb_re)
    bu_im = jnp.einsum('blgc,gpc->blgp', u, bb_im)
    if reverse:
        bu_re = jnp.flip(bu_re, axis=1)
        bu_im = jnp.flip(bu_im, axis=1)
    s0_re = s0_re.astype(jnp.float32)
    s0_im = s0_im.astype(jnp.float32)
    first_re = ab_re * s0_re - ab_im * s0_im + bu_re[:, 0]
    first_im = ab_re * s0_im + ab_im * s0_re + bu_im[:, 0]
    bu_re = bu_re.at[:, 0].set(first_re)
    bu_im = bu_im.at[:, 0].set(first_im)
    a_re = jnp.broadcast_to(ab_re, bu_re.shape)
    a_im = jnp.broadcast_to(ab_im, bu_im.shape)
    _, _, s_re, s_im = lax.associative_scan(_complex_affine_combine, (a_re, a_im, bu_re, bu_im), axis=1)
    if reverse:
        s_re = jnp.flip(s_re, axis=1)
        s_im = jnp.flip(s_im, axis=1)
    return s_re, s_im


def _bidir_s5(h, s0, lam_re, lam_im, log_dt, b_re, b_im, c_re, c_im, d_skip, w_a, w_b):
    b, l, _ = h.shape
    u = h.astype(jnp.float32).reshape(b, l, N_GROUPS, GROUP_CH)
    y = u * d_skip.astype(jnp.float32).reshape(N_GROUPS, GROUP_CH)
    states = []
    for d in range(2):
        s_re, s_im = _ssm_scan(u, lam_re[d], lam_im[d], log_dt[d], b_re[d], b_im[d],
                               s0[:, d, 0], s0[:, d, 1], reverse=(d == 1))
        y = (y + jnp.einsum('blgp,gcp->blgc', s_re, c_re[d].astype(jnp.float32))
             - jnp.einsum('blgp,gcp->blgc', s_im, c_im[d].astype(jnp.float32)))
        states.append((s_re, s_im))
    y = jax.nn.gelu(y.reshape(b, l, D_MODEL).astype(h.dtype))
    return (y @ w_a) * jax.nn.sigmoid(y @ w_b), states


def setup_inputs(seed: int = 0) -> dict:
    key = jax.random.key(seed)
    ks = jax.random.split(key, 27)

    def nrm(k, shape, scale=1.0):
        return jax.random.normal(k, shape, jnp.float32) * scale

    lam_n = jnp.arange(STATE_DIM, dtype=jnp.float32)
    lam_re = -0.5 + nrm(ks[14], (N_SSM_LAYERS, 2, N_GROUPS, STATE_DIM), 0.01)
    lam_im = math.pi * lam_n + nrm(ks[15], (N_SSM_LAYERS, 2, N_GROUPS, STATE_DIM), 0.01)
    log_dt = jax.random.uniform(ks[16], (N_SSM_LAYERS, 2, N_GROUPS), jnp.float32,
                                math.log(1e-3), math.log(1e-1))
    return {
        "x_prompt": nrm(ks[0], (BATCH, SEQ, D_MODEL)),
        "x_sample": nrm(ks[1], (DEC_BATCH, DEC_SEQ, D_MODEL)),
        "cache_k": nrm(ks[2], (DEC_BATCH, N_ATTN_LAYERS, PAST_LEN, N_KV_HEADS, HEAD_DIM)),
        "cache_v": nrm(ks[3], (DEC_BATCH, N_ATTN_LAYERS, PAST_LEN, N_KV_HEADS, HEAD_DIM)),
        "state_ssm": nrm(ks[4], (DEC_BATCH, N_SSM_LAYERS, 2, 2, N_GROUPS, STATE_DIM), 0.1),
        "c": nrm(ks[5], (DEC_BATCH, D_MODEL)),
        "c_ctx": nrm(ks[6], (D_MODEL,)),
        "norm1_g": 1.0 + nrm(ks[7], (DEPTH, D_MODEL), 0.01),
        "norm2_g": 1.0 + nrm(ks[8], (DEPTH, D_MODEL), 0.01),
        "w_mod": nrm(ks[9], (DEPTH, D_MODEL, N_MOD * D_MODEL), 0.5 * D_MODEL ** -0.5),
        "b_mod": nrm(ks[10], (DEPTH, N_MOD * D_MODEL), 0.01),
        "w_qkv": nrm(ks[11], (N_ATTN_LAYERS, D_MODEL, QKV_DIM), D_MODEL ** -0.5),
        "w_o": nrm(ks[12], (N_ATTN_LAYERS, N_HEADS * HEAD_DIM, D_MODEL), (N_HEADS * HEAD_DIM) ** -0.5),
        "attn_sink": nrm(ks[13], (N_ATTN_LAYERS, N_HEADS), 0.5),
        "ssm_lam_re": lam_re,
        "ssm_lam_im": lam_im,
        "ssm_log_dt": log_dt,
        "ssm_b_re": nrm(ks[17], (N_SSM_LAYERS, 2, N_GROUPS, STATE_DIM, GROUP_CH), (2 * GROUP_CH) ** -0.5),
        "ssm_b_im": nrm(ks[18], (N_SSM_LAYERS, 2, N_GROUPS, STATE_DIM, GROUP_CH), (2 * GROUP_CH) ** -0.5),
        "ssm_c_re": nrm(ks[19], (N_SSM_LAYERS, 2, N_GROUPS, GROUP_CH, STATE_DIM), STATE_DIM ** -0.5),
        "ssm_c_im": nrm(ks[20], (N_SSM_LAYERS, 2, N_GROUPS, GROUP_CH, STATE_DIM), STATE_DIM ** -0.5),
        "ssm_d": nrm(ks[21], (N_SSM_LAYERS, D_MODEL)),
        "glu_w_a": nrm(ks[22], (N_SSM_LAYERS, D_MODEL, D_MODEL), D_MODEL ** -0.5),
        "glu_w_b": nrm(ks[23], (N_SSM_LAYERS, D_MODEL, D_MODEL), D_MODEL ** -0.5),
        "mlp_w1": nrm(ks[24], (DEPTH, D_MODEL, D_FF), D_MODEL ** -0.5),
        "mlp_w2": nrm(ks[25], (DEPTH, D_FF, D_MODEL), D_FF ** -0.5),
        "final_norm_g": 1.0 + nrm(ks[26], (D_MODEL,), 0.01),
    }


def reference(x_prompt, x_sample, cache_k, cache_v, state_ssm, c, c_ctx,
              norm1_g, norm2_g, w_mod, b_mod, w_qkv, w_o, attn_sink,
              ssm_lam_re, ssm_lam_im, ssm_log_dt, ssm_b_re, ssm_b_im, ssm_c_re, ssm_c_im,
              ssm_d, glu_w_a, glu_w_b, mlp_w1, mlp_w2, final_norm_g):
    xp = x_prompt
    xx = x_sample
    bp = xp.shape[0]
    row, col = _grid_positions(xx.shape[1])
    new_k, new_v, new_s = [], [], []
    for i in range(DEPTH):
        j = i // N_MIXERS
        sh1p, sc1p, g1p, sh2p, sc2p, g2p = _modulation(c_ctx[None, None, :], w_mod[i], b_mod[i])
        sh1x, sc1x, g1x, sh2x, sc2x, g2x = _modulation(c[:, None, :], w_mod[i], b_mod[i])
        hp = _modulate(xp, norm1_g[i], sh1p, sc1p)
        hx = _modulate(xx, norm1_g[i], sh1x, sc1x)
        if i % N_MIXERS == 0:
            qp, kp, vp = _split_qkv(hp @ w_qkv[j])
            new_k.append(kp)
            new_v.append(vp)
            op = _context_attention(qp, kp, vp, attn_sink[j]) @ w_o[j]
            qx, kx, vx = _split_qkv(hx @ w_qkv[j])
            qx = _axial_rope(qx, row, col)
            kx = _axial_rope(kx, row, col)
            ox = _latent_window_attention(qx, kx, vx, cache_k[:, j], cache_v[:, j], attn_sink[j]) @ w_o[j]
        else:
            ssm_args = (ssm_lam_re[j], ssm_lam_im[j], ssm_log_dt[j], ssm_b_re[j], ssm_b_im[j],
                        ssm_c_re[j], ssm_c_im[j], ssm_d[j], glu_w_a[j], glu_w_b[j])
            s0p = jnp.zeros((bp, 2, 2, N_GROUPS, STATE_DIM), jnp.float32)
            op, st = _bidir_s5(hp, s0p, *ssm_args)
            fwd_final = jnp.stack([st[0][0][:, -1], st[0][1][:, -1]], axis=1)
            bwd_final = jnp.stack([st[1][0][:, 0], st[1][1][:, 0]], axis=1)
            new_s.append(jnp.stack([fwd_final, bwd_final], axis=1))
            ox, _ = _bidir_s5(hx, state_ssm[:, j], *ssm_args)
        xp = xp + g1p * op
        xx = xx + g1x * ox
        xp = xp + g2p * _mlp(_modulate(xp, norm2_g[i], sh2p, sc2p), mlp_w1[i], mlp_w2[i])
        xx = xx + g2x * _mlp(_modulate(xx, norm2_g[i], sh2x, sc2x), mlp_w1[i], mlp_w2[i])
    y_prompt = _rmsnorm(xp, final_norm_g)
    y_sample = _rmsnorm(xx, final_norm_g)
    new_cache_k = jnp.stack(new_k, axis=1)
    new_cache_v = jnp.stack(new_v, axis=1)
    new_state_ssm = jnp.stack(new_s, axis=1)
    return (y_prompt, y_sample, new_cache_k, new_cache_v, new_state_ssm)
```

```python
import functools
import math

import jax
import jax.numpy as jnp
from jax import lax
from jax.experimental import pallas as pl
from jax.experimental.pallas import tpu as pltpu

F32 = jnp.float32
BF16 = jnp.bfloat16

D_MODEL = 1024
N_HEADS = 16
N_KV_HEADS = 4
HEAD_DIM = 64
Q_PER_KV = N_HEADS // N_KV_HEADS
KV_DIM = N_KV_HEADS * HEAD_DIM
QKV_DIM = D_MODEL + 2 * KV_DIM
BLOCK = 128
GRID_W = 64
ROPE_BASE = 10000.0
ATTN_SCALE = HEAD_DIM ** -0.5
N_GROUPS = 64
GROUP_CH = 16
STATE_DIM = 64
D_FF = 4 * D_MODEL
N_MOD = 6
RMS_EPS = 1e-6
NEG_INF = -1e30

LANES = 128
GROUPS_PER_BLOCK = LANES // GROUP_CH
N_GROUP_BLOCKS = N_GROUPS // GROUPS_PER_BLOCK
STATE_LANES = GROUPS_PER_BLOCK * 2 * STATE_DIM
CHUNK = 8
TOKEN_TILE = 512
FF_TILE = 1024
VMEM_LIMIT = 56 * 1024 * 1024


def _cparams(semantics):
    return pltpu.CompilerParams(dimension_semantics=semantics, vmem_limit_bytes=VMEM_LIMIT)


def _rms(x):
    return x * lax.rsqrt(jnp.mean(x * x, axis=-1, keepdims=True) + RMS_EPS)


def _dot(a, b):
    return jnp.dot(a, b, preferred_element_type=F32)


def _dot_nt(a, b):
    return lax.dot_general(a, b, (((1,), (1,)), ((), ())), preferred_element_type=F32)


def _mod_kernel(cv_ref, w_ref, b_ref, o_ref):
    cv = cv_ref[...]
    s = (cv * jax.nn.sigmoid(cv)).astype(BF16)
    o_ref[0] = _dot(s, w_ref[0].astype(BF16)) + b_ref[0]


def _modulation(cvecs, w_mod, b_mod):
    depth = w_mod.shape[0]
    n_col = N_MOD * D_MODEL // D_MODEL
    out = pl.pallas_call(
        _mod_kernel,
        grid=(depth, n_col),
        in_specs=[
            pl.BlockSpec((8, D_MODEL), lambda l, j: (0, 0)),
            pl.BlockSpec((1, D_MODEL, D_MODEL), lambda l, j: (l, 0, j)),
            pl.BlockSpec((1, 1, D_MODEL), lambda l, j: (l, 0, j)),
        ],
        out_specs=pl.BlockSpec((1, 8, D_MODEL), lambda l, j: (l, 0, j)),
        out_shape=jax.ShapeDtypeStruct((depth, 8, N_MOD * D_MODEL), F32),
        compiler_params=_cparams(("arbitrary", "arbitrary")),
        name="modulation",
    )(cvecs, w_mod, b_mod.reshape(depth, 1, N_MOD * D_MODEL))
    return out.reshape(depth, 8, N_MOD, D_MODEL)


def _qkv_kernel(x_ref, mod_ref, g_ref, w_ref, cos_ref, sin_ref, q_ref, k_ref, v_ref):
    h = _rms(x_ref[...]) * g_ref[...] * (1.0 + mod_ref[1:2, :]) + mod_ref[0:1, :]
    qkv = _dot(h.astype(BF16), w_ref[...])
    cos = cos_ref[...]
    sin = sin_ref[...]
    lane = lax.broadcasted_iota(jnp.int32, cos.shape, 1)
    first = (lane & 31) < 16
    n_rot = (D_MODEL + KV_DIM) // LANES
    for blk in range(n_rot):
        t = qkv[:, blk * LANES:(blk + 1) * LANES]
        partner = jnp.where(first, pltpu.roll(t, LANES - 16, 1), pltpu.roll(t, 16, 1))
        r = t * cos + partner * sin
        if blk < D_MODEL // LANES:
            q_ref[:, blk * LANES:(blk + 1) * LANES] = (r * ATTN_SCALE).astype(BF16)
        else:
            c0 = blk * LANES - D_MODEL
            k_ref[:, c0:c0 + LANES] = r
    v_ref[...] = qkv[:, D_MODEL + KV_DIM:]


def _qkv(x, mod, mod_row, g, w_qkv, cos_t, sin_t, rope_blk):
    ntok = x.shape[0]
    nt = ntok // TOKEN_TILE
    return pl.pallas_call(
        _qkv_kernel,
        grid=(nt,),
        in_specs=[
            pl.BlockSpec((TOKEN_TILE, D_MODEL), lambda i: (i, 0)),
            pl.BlockSpec((None, N_MOD, D_MODEL), lambda i: (mod_row(i), 0, 0)),
            pl.BlockSpec((1, D_MODEL), lambda i: (0, 0)),
            pl.BlockSpec((D_MODEL, QKV_DIM), lambda i: (0, 0)),
            pl.BlockSpec((TOKEN_TILE, LANES), lambda i: (rope_blk(i), 0)),
            pl.BlockSpec((TOKEN_TILE, LANES), lambda i: (rope_blk(i), 0)),
        ],
        out_specs=[
            pl.BlockSpec((TOKEN_TILE, D_MODEL), lambda i: (i, 0)),
            pl.BlockSpec((TOKEN_TILE, KV_DIM), lambda i: (i, 0)),
            pl.BlockSpec((TOKEN_TILE, KV_DIM), lambda i: (i, 0)),
        ],
        out_shape=[
            jax.ShapeDtypeStruct((ntok, D_MODEL), BF16),
            jax.ShapeDtypeStruct((ntok, KV_DIM), F32),
            jax.ShapeDtypeStruct((ntok, KV_DIM), F32),
        ],
        compiler_params=_cparams(("arbitrary",)),
        name="norm_qkv_rope",
    )(x, mod, g, w_qkv, cos_t, sin_t)


def _softmax_pv(parts, sink):
    m = sink
    for s, _ in parts:
        m = jnp.maximum(jnp.max(s, axis=-1, keepdims=True), m)
    den = jnp.exp(sink - m)
    acc = None
    for s, v in parts:
        p = jnp.exp(s - m)
        den = den + jnp.sum(p, axis=-1, keepdims=True)
        pv = _dot(p.astype(BF16), v)
        acc = pv if acc is None else acc + pv
    return acc / den


def _ctx_attn_kernel(sink_ref, q_ref, k_ref, v_ref, o_ref):
    for kv in range(N_KV_HEADS):
        c0 = kv * HEAD_DIM
        kh = k_ref[:, c0:c0 + HEAD_DIM].astype(BF16)
        vh = v_ref[:, c0:c0 + HEAD_DIM].astype(BF16)
        for g in range(Q_PER_KV):
            h = kv * Q_PER_KV + g
            qh = q_ref[:, h * HEAD_DIM:(h + 1) * HEAD_DIM]
            o = _softmax_pv([(_dot_nt(qh, kh), vh)], sink_ref[h])
            o_ref[:, h * HEAD_DIM:(h + 1) * HEAD_DIM] = o.astype(BF16)


def _ctx_attention(sink, q, k, v, n_batch, seq):
    return pl.pallas_call(
        _ctx_attn_kernel,
        grid=(n_batch,),
        in_specs=[
            pl.BlockSpec(memory_space=pltpu.SMEM),
            pl.BlockSpec((seq, D_MODEL), lambda b: (b, 0)),
            pl.BlockSpec((seq, KV_DIM), lambda b: (b, 0)),
            pl.BlockSpec((seq, KV_DIM), lambda b: (b, 0)),
        ],
        out_specs=pl.BlockSpec((seq, D_MODEL), lambda b: (b, 0)),
        out_shape=jax.ShapeDtypeStruct((n_batch * seq, D_MODEL), BF16),
        compiler_params=_cparams(("arbitrary",)),
        name="context_attention",
    )(sink, q, k, v)


def _lat_attn_kernel(seq, sink_ref, q_ref, k_ref, v_ref, ck_ref, cv_ref, o_ref):
    n = pl.program_id(1)
    win = 3 * BLOCK
    start = pl.multiple_of(jnp.clip((n - 1) * BLOCK, 0, seq - win), BLOCK)
    kw = k_ref[pl.ds(start, win), :].astype(BF16)
    vw = v_ref[pl.ds(start, win), :].astype(BF16)
    ck = ck_ref[...].astype(BF16)
    cv = cv_ref[...].astype(BF16)
    qpos = n * BLOCK + lax.broadcasted_iota(jnp.int32, (BLOCK, win), 0)
    kpos = start + lax.broadcasted_iota(jnp.int32, (BLOCK, win), 1)
    band = jnp.abs(kpos - qpos) <= BLOCK
    for kv in range(N_KV_HEADS):
        c0 = kv * HEAD_DIM
        kh = kw[:, c0:c0 + HEAD_DIM]
        vh = vw[:, c0:c0 + HEAD_DIM]
        ckh = ck[:, c0:c0 + HEAD_DIM]
        cvh = cv[:, c0:c0 + HEAD_DIM]
        for g in range(Q_PER_KV):
            h = kv * Q_PER_KV + g
            qh = q_ref[:, h * HEAD_DIM:(h + 1) * HEAD_DIM]
            s_win = jnp.where(band, _dot_nt(qh, kh), NEG_INF)
            s_ctx = _dot_nt(qh, ckh)
            o = _softmax_pv([(s_win, vh), (s_ctx, cvh)], sink_ref[h])
            o_ref[:, h * HEAD_DIM:(h + 1) * HEAD_DIM] = o.astype(BF16)


def _lat_attention(sink, q, k, v, ck, cv, n_batch, seq):
    nb = seq // BLOCK
    past = ck.shape[1]
    return pl.pallas_call(
        functools.partial(_lat_attn_kernel, seq),
        grid=(n_batch, nb),
        in_specs=[
            pl.BlockSpec(memory_space=pltpu.SMEM),
            pl.BlockSpec((BLOCK, D_MODEL), lambda b, n: (b * nb + n, 0)),
            pl.BlockSpec((seq, KV_DIM), lambda b, n: (b, 0)),
            pl.BlockSpec((seq, KV_DIM), lambda b, n: (b, 0)),
            pl.BlockSpec((None, past, KV_DIM), lambda b, n: (b, 0, 0)),
            pl.BlockSpec((None, past, KV_DIM), lambda b, n: (b, 0, 0)),
        ],
        out_specs=pl.BlockSpec((BLOCK, D_MODEL), lambda b, n: (b * nb + n, 0)),
        out_shape=jax.ShapeDtypeStruct((n_batch * seq, D_MODEL), BF16),
        compiler_params=_cparams(("arbitrary", "arbitrary")),
        name="latent_attention",
    )(sink, q, k, v, ck, cv)


def _gelu_tanh(x):
    c = math.sqrt(2.0 / math.pi)
    return x * (0.5 * (1.0 + jnp.tanh(c * (x + 0.044715 * (x * x * x)))))


def _post_kernel(is_attn, emit_next, final, *refs):
    refs = list(refs)
    x_ref, mix_ref, mod_ref, g2_ref, wa_ref = refs[:5]
    refs = refs[5:]
    wb_ref = None if is_attn else refs.pop(0)
    w1_ref, w2_ref = refs[:2]
    refs = refs[2:]
    modn_ref = gn_ref = fg_ref = hn_ref = None
    if emit_next:
        modn_ref, gn_ref = refs[:2]
        refs = refs[2:]
    if final:
        fg_ref = refs.pop(0)
    xo_ref = refs.pop(0)
    if emit_next:
        hn_ref = refs.pop(0)
    x1_scr, h2_scr, acc_scr = refs

    f = pl.program_id(1)

    @pl.when(f == 0)
    def _():
        if is_attn:
            mix = _dot(mix_ref[...], wa_ref[...])
        else:
            yg = _gelu_tanh(mix_ref[...]).astype(BF16)
            mix = _dot(yg, wa_ref[...]) * jax.nn.sigmoid(_dot(yg, wb_ref[...]))
        x1 = x_ref[...] + mod_ref[2:3, :] * mix
        x1_scr[...] = x1
        h2 = _rms(x1) * g2_ref[...] * (1.0 + mod_ref[4:5, :]) + mod_ref[3:4, :]
        h2_scr[...] = h2.astype(BF16)
        acc_scr[...] = jnp.zeros_like(acc_scr)

    a = jnp.maximum(_dot(h2_scr[...], w1_ref[...]), 0.0)
    acc_scr[...] += _dot((a * a).astype(BF16), w2_ref[...])

    @pl.when(f == pl.num_programs(1) - 1)
    def _():
        x2 = x1_scr[...] + mod_ref[5:6, :] * acc_scr[...]
        if emit_next:
            hn_ref[...] = _rms(x2) * gn_ref[...] * (1.0 + modn_ref[1:2, :]) + modn_ref[0:1, :]
        if final:
            xo_ref[...] = _rms(x2) * fg_ref[...]
        else:
            xo_ref[...] = x2


def _post(x, mix, mod, mod_row, g2, w_a, w_b, w1, w2, mod_next=None, g_next=None, final_g=None):
    is_attn = w_b is None
    emit_next = mod_next is not None
    final = final_g is not None
    ntok = x.shape[0]
    nt = ntok // TOKEN_TILE
    nf = D_FF // FF_TILE
    tile = pl.BlockSpec((TOKEN_TILE, D_MODEL), lambda i, f: (i, 0))
    row = pl.BlockSpec((1, D_MODEL), lambda i, f: (0, 0))
    modspec = pl.BlockSpec((None, N_MOD, D_MODEL), lambda i, f: (mod_row(i), 0, 0))
    wsq = pl.BlockSpec((D_MODEL, D_MODEL), lambda i, f: (0, 0))
    in_specs = [tile, tile, modspec, row, wsq]
    args = [x, mix, mod, g2, w_a]
    if not is_attn:
        in_specs.append(wsq)
        args.append(w_b)
    in_specs += [pl.BlockSpec((D_MODEL, FF_TILE), lambda i, f: (0, f)),
                 pl.BlockSpec((FF_TILE, D_MODEL), lambda i, f: (f, 0))]
    args += [w1, w2]
    if emit_next:
        in_specs += [modspec, row]
        args += [mod_next, g_next]
    if final:
        in_specs.append(row)
        args.append(final_g)
    out_specs = [tile]
    out_shape = [jax.ShapeDtypeStruct((ntok, D_MODEL), F32)]
    if emit_next:
        out_specs.append(tile)
        out_shape.append(jax.ShapeDtypeStruct((ntok, D_MODEL), F32))
    return pl.pallas_call(
        functools.partial(_post_kernel, is_attn, emit_next, final),
        grid=(nt, nf),
        in_specs=in_specs,
        out_specs=out_specs,
        out_shape=out_shape,
        scratch_shapes=[
            pltpu.VMEM((TOKEN_TILE, D_MODEL), F32),
            pltpu.VMEM((TOKEN_TILE, D_MODEL), BF16),
            pltpu.VMEM((TOKEN_TILE, D_MODEL), F32),
        ],
        compiler_params=_cparams(("arbitrary", "arbitrary")),
        name="attn_proj_mlp" if is_attn else "glu_mlp_final",
    )(*args)


def _s5_operators(lam_re, lam_im, log_dt, b_re, b_im, c_re, c_im):
    hp = lax.Precision.HIGHEST
    lam_re = lam_re.astype(F32)
    lam_im = lam_im.astype(F32)
    dt = jnp.exp(log_dt.astype(F32))[..., None]
    lr = lam_re * dt
    li = lam_im * dt

    def apow(k):
        kk = k.astype(F32)[:, :, None, None]
        mag = jnp.exp(kk * lr[None])
        return mag * jnp.cos(kk * li[None]), mag * jnp.sin(kk * li[None])

    ab_re, ab_im = apow(jnp.ones((1, 2), jnp.int32))
    ab_re, ab_im = ab_re[0], ab_im[0]
    den = lam_re * lam_re + lam_im * lam_im
    num_re = ab_re - 1.0
    num_im = ab_im
    f_re = (num_re * lam_re + num_im * lam_im) / den
    f_im = (num_im * lam_re - num_re * lam_im) / den
    b_re = b_re.astype(F32)
    b_im = b_im.astype(F32)
    bb_re = f_re[..., None] * b_re - f_im[..., None] * b_im
    bb_im = f_re[..., None] * b_im + f_im[..., None] * b_re
    c_re = c_re.astype(F32)
    c_im = c_im.astype(F32)

    ar = jnp.arange(CHUNK)
    gb, gpb = N_GROUP_BLOCKS, GROUPS_PER_BLOCK

    e_re, e_im = apow(jnp.stack([CHUNK - 1 - ar, ar], axis=1))
    fr = e_re[..., None] * bb_re[None] - e_im[..., None] * bb_im[None]
    fi = e_re[..., None] * bb_im[None] + e_im[..., None] * bb_re[None]
    fc = jnp.stack([fr, fi], axis=-1)
    fc = fc.reshape(CHUNK, 2, gb, gpb, STATE_DIM, GROUP_CH, 2)
    fc = fc.transpose(1, 2, 0, 3, 5, 6, 4)
    fc = fc.reshape(2, gb, CHUNK, LANES, 2 * STATE_DIM).astype(BF16)

    e_re, e_im = apow(jnp.stack([ar + 1, CHUNK - ar], axis=1))
    wr = c_re[None] * e_re[:, :, :, None, :] - c_im[None] * e_im[:, :, :, None, :]
    wi = c_re[None] * e_im[:, :, :, None, :] + c_im[None] * e_re[:, :, :, None, :]
    ec = jnp.stack([wr, -wi], axis=-2)
    ec = ec.reshape(CHUNK, 2, gb, gpb, GROUP_CH, 2, STATE_DIM)
    ec = ec.transpose(1, 2, 0, 3, 4, 5, 6)
    ec = ec.reshape(2, gb, CHUNK, LANES, 2 * STATE_DIM).astype(BF16)

    e_re, e_im = apow(jnp.stack([ar, ar], axis=1))
    mr = e_re[..., None] * bb_re[None] - e_im[..., None] * bb_im[None]
    mi = e_re[..., None] * bb_im[None] + e_im[..., None] * bb_re[None]
    kk = (jnp.einsum('dgcp,kdgpe->kdgce', c_re, mr, precision=hp)
          - jnp.einsum('dgcp,kdgpe->kdgce', c_im, mi, precision=hp))
    k_f, k_b = kk[:, 0], kk[:, 1]
    lag0 = (k_f[0] + k_b[0])[None]
    ktot = jnp.concatenate([k_b[:0:-1], lag0, k_f[1:]], axis=0)
    ktot = ktot.transpose(0, 1, 3, 2)
    ktot = ktot.reshape(2 * CHUNK - 1, gb, gpb, GROUP_CH, GROUP_CH)
    eye = jnp.eye(gpb, dtype=F32)
    bd = ktot[:, :, :, :, None, :] * eye[None, None, :, None, :, None]
    bd = bd.transpose(1, 0, 2, 3, 4, 5).reshape(gb, 2 * CHUNK - 1, LANES, LANES).astype(BF16)

    d_re, d_im = apow(jnp.full((1, 2), CHUNK, jnp.int32))
    d_re, d_im = d_re[0], d_im[0]
    ar_a = jnp.stack([d_re, d_re], axis=2)
    ai_a = jnp.stack([-d_im, d_im], axis=2)
    dec = jnp.stack([ar_a, ai_a], axis=1)
    dec = dec.reshape(2, 2, gb, STATE_LANES).transpose(2, 0, 1, 3)
    return fc, ec, bd, dec


def _s5_kernel(n_seq, h_ref, fc_ref, ec_ref, bd_ref, dec_ref, dsk_ref, s0_ref,
               y_ref, sfin_ref, f_scr, e_scr, k_scr, sf_scr, sb_scr):
    ntok = h_ref.shape[0]
    nc = ntok // CHUNK
    cps = nc // n_seq

    row_g = lax.broadcasted_iota(jnp.int32, (LANES, STATE_LANES), 0) >> 4
    col_g = lax.broadcasted_iota(jnp.int32, (LANES, STATE_LANES), 1) >> 7
    diag = row_g == col_g

    def expand(w):
        return jnp.where(diag, jnp.concatenate([w] * GROUPS_PER_BLOCK, axis=1), jnp.zeros((), BF16))

    for d in range(2):
        for j in range(CHUNK):
            f_scr[d, j * LANES:(j + 1) * LANES, :] = expand(fc_ref[d, j])
            e_scr[d, j * LANES:(j + 1) * LANES, :] = expand(ec_ref[d, j])
    for j in range(CHUNK):
        for t in range(CHUNK):
            k_scr[j * LANES:(j + 1) * LANES, t * LANES:(t + 1) * LANES] = bd_ref[t - j + CHUNK - 1]

    xcat = jnp.concatenate(
        [h_ref[pl.ds(j, nc, stride=CHUNK), :].astype(BF16) for j in range(CHUNK)], axis=1)

    gpb = GROUPS_PER_BLOCK
    for d, scr in ((0, sf_scr), (1, sb_scr)):
        loc_all = _dot(xcat, f_scr[d])
        for k in range(gpb):
            scr[k] = loc_all[:, k * LANES:(k + 1) * LANES]

    def scan(scr, d, reverse):
        a_r = [dec_ref[d, 0:1, k * LANES:(k + 1) * LANES] for k in range(gpb)]
        a_i = [dec_ref[d, 1:2, k * LANES:(k + 1) * LANES] for k in range(gpb)]

        def body(i, st):
            c = (cps - 1 - i) if reverse else i
            rows = pl.ds(c, n_seq, stride=cps)
            new = []
            for k in range(gpb):
                loc = scr[k, rows, :]
                scr[k, rows, :] = st[k]
                new.append(a_r[k] * st[k] + a_i[k] * pltpu.roll(st[k], LANES // 2, 1) + loc)
            return tuple(new)

        init = tuple(s0_ref[d, :, k * LANES:(k + 1) * LANES] for k in range(gpb))
        fin = lax.fori_loop(0, cps, body, init)
        for k in range(gpb):
            sfin_ref[d, :, k * LANES:(k + 1) * LANES] = fin[k]

    scan(sf_scr, 0, False)
    scan(sb_scr, 1, True)

    s_f = jnp.concatenate([sf_scr[k].astype(BF16) for k in range(gpb)], axis=1)
    s_b = jnp.concatenate([sb_scr[k].astype(BF16) for k in range(gpb)], axis=1)
    yall = _dot(xcat, k_scr[...]) + _dot_nt(s_f, e_scr[0]) + _dot_nt(s_b, e_scr[1])
    dsk = dsk_ref[...]
    for t in range(CHUNK):
        rows = pl.ds(t, nc, stride=CHUNK)
        y_ref[rows, :] = yall[:, t * LANES:(t + 1) * LANES] + h_ref[rows, :] * dsk


def _s5(h, ops, d_skip, s0, n_seq):
    fc, ec, bd, dec = ops
    ntok = h.shape[0]
    nc = ntok // CHUNK
    kdim = CHUNK * LANES
    return pl.pallas_call(
        functools.partial(_s5_kernel, n_seq),
        grid=(N_GROUP_BLOCKS,),
        in_specs=[
            pl.BlockSpec((ntok, LANES), lambda g: (0, g)),
            pl.BlockSpec((2, None, CHUNK, LANES, LANES), lambda g: (0, g, 0, 0, 0)),
            pl.BlockSpec((2, None, CHUNK, LANES, LANES), lambda g: (0, g, 0, 0, 0)),
            pl.BlockSpec((None, 2 * CHUNK - 1, LANES, LANES), lambda g: (g, 0, 0, 0)),
            pl.BlockSpec((None, 2, 2, STATE_LANES), lambda g: (g, 0, 0, 0)),
            pl.BlockSpec((1, LANES), lambda g: (0, g)),
            pl.BlockSpec((None, 2, n_seq, STATE_LANES), lambda g: (g, 0, 0, 0)),
        ],
        out_specs=[
            pl.BlockSpec((ntok, LANES), lambda g: (0, g)),
            pl.BlockSpec((None, 2, n_seq, STATE_LANES), lambda g: (g, 0, 0, 0)),
        ],
        out_shape=[
            jax.ShapeDtypeStruct((ntok, D_MODEL), F32),
            jax.ShapeDtypeStruct((N_GROUP_BLOCKS, 2, n_seq, STATE_LANES), F32),
        ],
        scratch_shapes=[
            pltpu.VMEM((2, kdim, STATE_LANES), BF16),
            pltpu.VMEM((2, kdim, STATE_LANES), BF16),
            pltpu.VMEM((kdim, kdim), BF16),
            pltpu.VMEM((GROUPS_PER_BLOCK, nc, LANES), F32),
            pltpu.VMEM((GROUPS_PER_BLOCK, nc, LANES), F32),
        ],
        compiler_params=_cparams(("arbitrary",)),
        name="s5_chunked_scan",
    )(h, fc, ec, bd, dec, d_skip, s0)


def _state_to_blocks(s):
    b = s.shape[0]
    s = s.reshape(b, 2, 2, N_GROUP_BLOCKS, GROUPS_PER_BLOCK, STATE_DIM)
    return s.transpose(3, 1, 0, 4, 2, 5).reshape(N_GROUP_BLOCKS, 2, b, STATE_LANES)


def _blocks_to_state(s):
    b = s.shape[2]
    s = s.reshape(N_GROUP_BLOCKS, 2, b, GROUPS_PER_BLOCK, 2, STATE_DIM)
    return s.transpose(2, 1, 4, 0, 3, 5).reshape(b, 2, 2, N_GROUPS, STATE_DIM)


def _rope_tables(n_tokens):
    pos = jnp.arange(n_tokens, dtype=jnp.int32)
    row = (pos // GRID_W).astype(F32)
    col = (pos % GRID_W).astype(F32)
    n_freq = HEAD_DIM // 4
    freqs = ROPE_BASE ** (-jnp.arange(n_freq, dtype=F32) / n_freq)
    ang_r = row[:, None] * freqs
    ang_c = col[:, None] * freqs
    cos_h = jnp.concatenate([jnp.cos(ang_r), jnp.cos(ang_r), jnp.cos(ang_c), jnp.cos(ang_c)], axis=1)
    sin_h = jnp.concatenate([-jnp.sin(ang_r), jnp.sin(ang_r), -jnp.sin(ang_c), jnp.sin(ang_c)], axis=1)
    cos_t = jnp.concatenate([cos_h, cos_h], axis=1)
    sin_t = jnp.concatenate([sin_h, sin_h], axis=1)
    ident_c = jnp.ones((TOKEN_TILE, LANES), F32)
    ident_s = jnp.zeros((TOKEN_TILE, LANES), F32)
    return jnp.concatenate([ident_c, cos_t], axis=0), jnp.concatenate([ident_s, sin_t], axis=0)


def kernel(x_prompt, x_sample, cache_k, cache_v, state_ssm, c, c_ctx, norm1_g, norm2_g, w_mod, b_mod,
           w_qkv, w_o, attn_sink, ssm_lam_re, ssm_lam_im, ssm_log_dt, ssm_b_re, ssm_b_im, ssm_c_re,
           ssm_c_im, ssm_d, glu_w_a, glu_w_b, mlp_w1, mlp_w2, final_norm_g):
    bp, lp, _ = x_prompt.shape
    bx, lx, _ = x_sample.shape
    assert lx % TOKEN_TILE == 0 and (bp * lp) % TOKEN_TILE == 0
    tiles_per_lat = lx // TOKEN_TILE

    xp = x_prompt.reshape(bp * lp, D_MODEL)
    xx = x_sample.reshape(bx * lx, D_MODEL)

    cvecs = jnp.zeros((8, D_MODEL), F32).at[0].set(c_ctx).at[1:1 + bx].set(c)
    mod = _modulation(cvecs, w_mod, b_mod)

    ctx_row = lambda i: 0
    lat_row = lambda i: 1 + i // tiles_per_lat
    ctx_rope = lambda i: 0
    lat_rope = lambda i: 1 + i % tiles_per_lat

    cos_t, sin_t = _rope_tables(lx)
    wqkv = w_qkv[0].astype(BF16)
    g1 = norm1_g[0].reshape(1, D_MODEL)
    sink = attn_sink[0].astype(F32)
    qp, kp, vp = _qkv(xp, mod[0], ctx_row, g1, wqkv, cos_t, sin_t, ctx_rope)
    qx, kx, vx = _qkv(xx, mod[0], lat_row, g1, wqkv, cos_t, sin_t, lat_rope)
    op = _ctx_attention(sink, qp, kp, vp, bp, lp)
    ck = cache_k[:, 0].reshape(bx, -1, KV_DIM)
    cv = cache_v[:, 0].reshape(bx, -1, KV_DIM)
    ox = _lat_attention(sink, qx, kx, vx, ck, cv, bx, lx)

    wo = w_o[0].astype(BF16)
    w1 = mlp_w1.astype(BF16)
    w2 = mlp_w2.astype(BF16)
    g2 = norm2_g.reshape(-1, 1, D_MODEL)
    gn = norm1_g[1].reshape(1, D_MODEL)
    xp, hp = _post(xp, op, mod[0], ctx_row, g2[0], wo, None, w1[0], w2[0], mod_next=mod[1], g_next=gn)
    xx, hx = _post(xx, ox, mod[0], lat_row, g2[0], wo, None, w1[0], w2[0], mod_next=mod[1], g_next=gn)

    ops = _s5_operators(ssm_lam_re[0], ssm_lam_im[0], ssm_log_dt[0], ssm_b_re[0], ssm_b_im[0],
                        ssm_c_re[0], ssm_c_im[0])
    dsk = ssm_d[0].astype(F32).reshape(1, D_MODEL)
    s0p = jnp.zeros((N_GROUP_BLOCKS, 2, bp, STATE_LANES), F32)
    s0x = _state_to_blocks(state_ssm[:, 0].astype(F32))
    yp, sfin = _s5(hp, ops, dsk, s0p, bp)
    yx, _ = _s5(hx, ops, dsk, s0x, bx)
    new_state = _blocks_to_state(sfin)[:, None]

    wa = glu_w_a[0].astype(BF16)
    wb = glu_w_b[0].astype(BF16)
    fg = final_norm_g.reshape(1, D_MODEL)
    (yp_out,) = _post(xp, yp, mod[1], ctx_row, g2[1], wa, wb, w1[1], w2[1], final_g=fg)
    (yx_out,) = _post(xx, yx, mod[1], lat_row, g2[1], wa, wb, w1[1], w2[1], final_g=fg)

    new_k = kp.reshape(bp, 1, lp, N_KV_HEADS, HEAD_DIM)
    new_v = vp.reshape(bp, 1, lp, N_KV_HEADS, HEAD_DIM)
    return (yp_out.reshape(bp, lp, D_MODEL), yx_out.reshape(bx, lx, D_MODEL), new_k, new_v, new_state)
```

```python
import functools
import math

import numpy as np
import jax
import jax.numpy as jnp
from jax import lax
from jax.experimental import pallas as pl
from jax.experimental.pallas import tpu as pltpu

F32 = jnp.float32
BF16 = jnp.bfloat16

D_MODEL = 1024
N_HEADS = 16
N_KV_HEADS = 4
HEAD_DIM = 64
Q_PER_KV = N_HEADS // N_KV_HEADS
KV_DIM = N_KV_HEADS * HEAD_DIM
QKV_DIM = D_MODEL + 2 * KV_DIM
BLOCK = 128
GRID_W = 64
ROPE_BASE = 10000.0
ATTN_SCALE = HEAD_DIM ** -0.5
N_GROUPS = 64
GROUP_CH = 16
STATE_DIM = 64
D_FF = 4 * D_MODEL
N_MOD = 6
RMS_EPS = 1e-6
NEG_INF = -1e30

LANES = 128
SUBLANES = 8
GROUPS_PER_BLOCK = LANES // GROUP_CH
N_GROUP_BLOCKS = N_GROUPS // GROUPS_PER_BLOCK
STATE_LANES = GROUPS_PER_BLOCK * 2 * STATE_DIM
CHUNK = SUBLANES
TOKEN_TILE = 512
FF_TILE = 1024
VMEM_LIMIT = 56 * 1024 * 1024


def _cparams(semantics):
    return pltpu.CompilerParams(dimension_semantics=semantics, vmem_limit_bytes=VMEM_LIMIT)


def _rms(x):
    return x * lax.rsqrt(jnp.mean(x * x, axis=-1, keepdims=True) + RMS_EPS)


def _dot(a, b):
    return jnp.dot(a, b, preferred_element_type=F32)


def _dot_nt(a, b):
    return lax.dot_general(a, b, (((1,), (1,)), ((), ())), preferred_element_type=F32)


def _mod_kernel(cv_ref, w_ref, b_ref, o_ref):
    cv = cv_ref[...]
    s = (cv * jax.nn.sigmoid(cv)).astype(BF16)
    o_ref[0] = _dot(s, w_ref[0].astype(BF16)) + b_ref[0]


def _modulation(cvecs, w_mod, b_mod):
    depth = w_mod.shape[0]
    out = pl.pallas_call(
        _mod_kernel,
        grid=(depth, N_MOD),
        in_specs=[
            pl.BlockSpec((8, D_MODEL), lambda l, j: (0, 0)),
            pl.BlockSpec((1, D_MODEL, D_MODEL), lambda l, j: (l, 0, j)),
            pl.BlockSpec((1, 1, D_MODEL), lambda l, j: (l, 0, j)),
        ],
        out_specs=pl.BlockSpec((1, 8, D_MODEL), lambda l, j: (l, 0, j)),
        out_shape=jax.ShapeDtypeStruct((depth, 8, N_MOD * D_MODEL), F32),
        compiler_params=_cparams(("arbitrary", "arbitrary")),
        name="modulation",
    )(cvecs, w_mod, b_mod.reshape(depth, 1, N_MOD * D_MODEL))
    return out.reshape(depth, 8, N_MOD, D_MODEL)


def _qkv_kernel(x_ref, mod_ref, g_ref, w_ref, cos_ref, sin_ref, q_ref, k_ref, v_ref):
    h = _rms(x_ref[...]) * g_ref[...] * (1.0 + mod_ref[1:2, :]) + mod_ref[0:1, :]
    qkv = _dot(h.astype(BF16), w_ref[...])
    cos = cos_ref[...]
    sin = sin_ref[...]
    lane = lax.broadcasted_iota(jnp.int32, cos.shape, 1)
    first = (lane & 31) < 16
    n_rot = (D_MODEL + KV_DIM) // LANES
    for blk in range(n_rot):
        t = qkv[:, blk * LANES:(blk + 1) * LANES]
        partner = jnp.where(first, pltpu.roll(t, LANES - 16, 1), pltpu.roll(t, 16, 1))
        r = t * cos + partner * sin
        if blk < D_MODEL // LANES:
            q_ref[:, blk * LANES:(blk + 1) * LANES] = (r * ATTN_SCALE).astype(BF16)
        else:
            c0 = blk * LANES - D_MODEL
            k_ref[:, c0:c0 + LANES] = r
    v_ref[...] = qkv[:, D_MODEL + KV_DIM:]


def _qkv(x, mod, mod_row, g, w_qkv, cos_t, sin_t, rope_blk):
    ntok = x.shape[0]
    nt = ntok // TOKEN_TILE
    return pl.pallas_call(
        _qkv_kernel,
        grid=(nt,),
        in_specs=[
            pl.BlockSpec((TOKEN_TILE, D_MODEL), lambda i: (i, 0)),
            pl.BlockSpec((None, N_MOD, D_MODEL), lambda i: (mod_row(i), 0, 0)),
            pl.BlockSpec((1, D_MODEL), lambda i: (0, 0)),
            pl.BlockSpec((D_MODEL, QKV_DIM), lambda i: (0, 0)),
            pl.BlockSpec((TOKEN_TILE, LANES), lambda i: (rope_blk(i), 0)),
            pl.BlockSpec((TOKEN_TILE, LANES), lambda i: (rope_blk(i), 0)),
        ],
        out_specs=[
            pl.BlockSpec((TOKEN_TILE, D_MODEL), lambda i: (i, 0)),
            pl.BlockSpec((TOKEN_TILE, KV_DIM), lambda i: (i, 0)),
            pl.BlockSpec((TOKEN_TILE, KV_DIM), lambda i: (i, 0)),
        ],
        out_shape=[
            jax.ShapeDtypeStruct((ntok, D_MODEL), BF16),
            jax.ShapeDtypeStruct((ntok, KV_DIM), F32),
            jax.ShapeDtypeStruct((ntok, KV_DIM), F32),
        ],
        compiler_params=_cparams(("arbitrary",)),
        name="norm_qkv_rope",
    )(x, mod, g, w_qkv, cos_t, sin_t)


def _softmax_pv(parts, sink):
    m = sink
    for s, _ in parts:
        m = jnp.maximum(jnp.max(s, axis=-1, keepdims=True), m)
    den = jnp.exp(sink - m)
    acc = None
    for s, v in parts:
        p = jnp.exp(s - m)
        den = den + jnp.sum(p, axis=-1, keepdims=True)
        pv = _dot(p.astype(BF16), v)
        acc = pv if acc is None else acc + pv
    return acc / den


def _ctx_attn_kernel(sink_ref, q_ref, k_ref, v_ref, o_ref):
    for kv in range(N_KV_HEADS):
        c0 = kv * HEAD_DIM
        kh = k_ref[:, c0:c0 + HEAD_DIM].astype(BF16)
        vh = v_ref[:, c0:c0 + HEAD_DIM].astype(BF16)
        for g in range(Q_PER_KV):
            h = kv * Q_PER_KV + g
            qh = q_ref[:, h * HEAD_DIM:(h + 1) * HEAD_DIM]
            o = _softmax_pv([(_dot_nt(qh, kh), vh)], sink_ref[h])
            o_ref[:, h * HEAD_DIM:(h + 1) * HEAD_DIM] = o.astype(BF16)


def _ctx_attention(sink, q, k, v, n_batch, seq):
    return pl.pallas_call(
        _ctx_attn_kernel,
        grid=(n_batch,),
        in_specs=[
            pl.BlockSpec(memory_space=pltpu.SMEM),
            pl.BlockSpec((seq, D_MODEL), lambda b: (b, 0)),
            pl.BlockSpec((seq, KV_DIM), lambda b: (b, 0)),
            pl.BlockSpec((seq, KV_DIM), lambda b: (b, 0)),
        ],
        out_specs=pl.BlockSpec((seq, D_MODEL), lambda b: (b, 0)),
        out_shape=jax.ShapeDtypeStruct((n_batch * seq, D_MODEL), BF16),
        compiler_params=_cparams(("arbitrary",)),
        name="context_attention",
    )(sink, q, k, v)


def _lat_attn_kernel(seq, sink_ref, q_ref, k_ref, v_ref, ck_ref, cv_ref, o_ref):
    n = pl.program_id(1)
    win = 3 * BLOCK
    start = pl.multiple_of(jnp.clip((n - 1) * BLOCK, 0, seq - win), BLOCK)
    kw = k_ref[pl.ds(start, win), :].astype(BF16)
    vw = v_ref[pl.ds(start, win), :].astype(BF16)
    ck = ck_ref[...].astype(BF16)
    cv = cv_ref[...].astype(BF16)
    qpos = n * BLOCK + lax.broadcasted_iota(jnp.int32, (BLOCK, win), 0)
    kpos = start + lax.broadcasted_iota(jnp.int32, (BLOCK, win), 1)
    band = jnp.abs(kpos - qpos) <= BLOCK
    for kv in range(N_KV_HEADS):
        c0 = kv * HEAD_DIM
        kh = kw[:, c0:c0 + HEAD_DIM]
        vh = vw[:, c0:c0 + HEAD_DIM]
        ckh = ck[:, c0:c0 + HEAD_DIM]
        cvh = cv[:, c0:c0 + HEAD_DIM]
        for g in range(Q_PER_KV):
            h = kv * Q_PER_KV + g
            qh = q_ref[:, h * HEAD_DIM:(h + 1) * HEAD_DIM]
            s_win = jnp.where(band, _dot_nt(qh, kh), NEG_INF)
            s_ctx = _dot_nt(qh, ckh)
            o = _softmax_pv([(s_win, vh), (s_ctx, cvh)], sink_ref[h])
            o_ref[:, h * HEAD_DIM:(h + 1) * HEAD_DIM] = o.astype(BF16)


def _lat_attention(sink, q, k, v, ck, cv, n_batch, seq):
    nb = seq // BLOCK
    past = ck.shape[1]
    return pl.pallas_call(
        functools.partial(_lat_attn_kernel, seq),
        grid=(n_batch, nb),
        in_specs=[
            pl.BlockSpec(memory_space=pltpu.SMEM),
            pl.BlockSpec((BLOCK, D_MODEL), lambda b, n: (b * nb + n, 0)),
            pl.BlockSpec((seq, KV_DIM), lambda b, n: (b, 0)),
            pl.BlockSpec((seq, KV_DIM), lambda b, n: (b, 0)),
            pl.BlockSpec((None, past, KV_DIM), lambda b, n: (b, 0, 0)),
            pl.BlockSpec((None, past, KV_DIM), lambda b, n: (b, 0, 0)),
        ],
        out_specs=pl.BlockSpec((BLOCK, D_MODEL), lambda b, n: (b * nb + n, 0)),
        out_shape=jax.ShapeDtypeStruct((n_batch * seq, D_MODEL), BF16),
        compiler_params=_cparams(("arbitrary", "arbitrary")),
        name="latent_attention",
    )(sink, q, k, v, ck, cv)


def _gelu_tanh(x):
    c = math.sqrt(2.0 / math.pi)
    return x * (0.5 * (1.0 + jnp.tanh(c * (x + 0.044715 * (x * x * x)))))


def _post_kernel(is_attn, emit_next, final, *refs):
    refs = list(refs)
    x_ref, mix_ref, mod_ref, g2_ref, wa_ref = refs[:5]
    refs = refs[5:]
    wb_ref = None if is_attn else refs.pop(0)
    w1_ref, w2_ref = refs[:2]
    refs = refs[2:]
    modn_ref = gn_ref = fg_ref = hn_ref = None
    if emit_next:
        modn_ref, gn_ref = refs[:2]
        refs = refs[2:]
    if final:
        fg_ref = refs.pop(0)
    xo_ref = refs.pop(0)
    if emit_next:
        hn_ref = refs.pop(0)
    x1_scr, h2_scr, acc_scr = refs

    f = pl.program_id(1)

    @pl.when(f == 0)
    def _():
        if is_attn:
            mix = _dot(mix_ref[...], wa_ref[...])
        else:
            yg = _gelu_tanh(mix_ref[...]).astype(BF16)
            mix = _dot(yg, wa_ref[...]) * jax.nn.sigmoid(_dot(yg, wb_ref[...]))
        x1 = x_ref[...] + mod_ref[2:3, :] * mix
        x1_scr[...] = x1
        h2 = _rms(x1) * g2_ref[...] * (1.0 + mod_ref[4:5, :]) + mod_ref[3:4, :]
        h2_scr[...] = h2.astype(BF16)
        acc_scr[...] = jnp.zeros_like(acc_scr)

    a = jnp.maximum(_dot(h2_scr[...], w1_ref[...]), 0.0)
    acc_scr[...] += _dot((a * a).astype(BF16), w2_ref[...])

    @pl.when(f == pl.num_programs(1) - 1)
    def _():
        x2 = x1_scr[...] + mod_ref[5:6, :] * acc_scr[...]
        if emit_next:
            hn_ref[...] = _rms(x2) * gn_ref[...] * (1.0 + modn_ref[1:2, :]) + modn_ref[0:1, :]
        if final:
            xo_ref[...] = _rms(x2) * fg_ref[...]
        else:
            xo_ref[...] = x2


def _post(x, mix, mod, mod_row, g2, w_a, w_b, w1, w2, mod_next=None, g_next=None, final_g=None):
    is_attn = w_b is None
    emit_next = mod_next is not None
    final = final_g is not None
    ntok = x.shape[0]
    nt = ntok // TOKEN_TILE
    nf = D_FF // FF_TILE
    tile = pl.BlockSpec((TOKEN_TILE, D_MODEL), lambda i, f: (i, 0))
    row = pl.BlockSpec((1, D_MODEL), lambda i, f: (0, 0))
    modspec = pl.BlockSpec((None, N_MOD, D_MODEL), lambda i, f: (mod_row(i), 0, 0))
    wsq = pl.BlockSpec((D_MODEL, D_MODEL), lambda i, f: (0, 0))
    in_specs = [tile, tile, modspec, row, wsq]
    args = [x, mix, mod, g2, w_a]
    if not is_attn:
        in_specs.append(wsq)
        args.append(w_b)
    in_specs += [pl.BlockSpec((D_MODEL, FF_TILE), lambda i, f: (0, f)),
                 pl.BlockSpec((FF_TILE, D_MODEL), lambda i, f: (f, 0))]
    args += [w1, w2]
    if emit_next:
        in_specs += [modspec, row]
        args += [mod_next, g_next]
    if final:
        in_specs.append(row)
        args.append(final_g)
    out_specs = [tile]
    out_shape = [jax.ShapeDtypeStruct((ntok, D_MODEL), F32)]
    if emit_next:
        out_specs.append(tile)
        out_shape.append(jax.ShapeDtypeStruct((ntok, D_MODEL), F32))
    return pl.pallas_call(
        functools.partial(_post_kernel, is_attn, emit_next, final),
        grid=(nt, nf),
        in_specs=in_specs,
        out_specs=out_specs,
        out_shape=out_shape,
        scratch_shapes=[
            pltpu.VMEM((TOKEN_TILE, D_MODEL), F32),
            pltpu.VMEM((TOKEN_TILE, D_MODEL), BF16),
            pltpu.VMEM((TOKEN_TILE, D_MODEL), F32),
        ],
        compiler_params=_cparams(("arbitrary", "arbitrary")),
        name="attn_proj_mlp" if is_attn else "glu_mlp_final",
    )(*args)


def _swap(x):
    return pltpu.roll(x, LANES // 2, 1)


def _cmul(z, w_r, w_i):
    return z * w_r + _swap(z) * w_i


def _multiplier(z, lo):
    zs = _swap(z)
    return jnp.where(lo, z, zs), jnp.where(lo, -zs, z)


def _rep_rows(x):
    return jnp.concatenate(
        [jnp.broadcast_to(x[g:g + 1, :], (GROUP_CH, LANES)) for g in range(GROUPS_PER_BLOCK)], axis=0)


def _s5_kernel(n_seq, h_ref, lamr_ref, lami_ref, ldt_ref, bt_ref, cp_ref, dsk_ref, s0_ref,
               y_ref, sfin_ref, f_scr, e_scr, k_scr, sf_scr, sb_scr, swf_scr, swb_scr):
    ntok = h_ref.shape[0]
    nc = ntok // CHUNK
    cps = nc // n_seq
    gpb = GROUPS_PER_BLOCK

    lo8 = lax.broadcasted_iota(jnp.int32, (gpb, LANES), 1) < STATE_DIM
    lo = lax.broadcasted_iota(jnp.int32, (LANES, LANES), 1) < STATE_DIM
    conj = jnp.where(lo, 1.0, -1.0)
    row_g = lax.broadcasted_iota(jnp.int32, (LANES, STATE_LANES), 0) >> 4
    col_g = lax.broadcasted_iota(jnp.int32, (LANES, STATE_LANES), 1) >> 7
    diag_wide = row_g == col_g
    diag = (lax.broadcasted_iota(jnp.int32, (LANES, LANES), 0) >> 4) == (
        lax.broadcasted_iota(jnp.int32, (LANES, LANES), 1) >> 4)

    def expand(w):
        return jnp.where(diag_wide, jnp.concatenate([w] * gpb, axis=1), jnp.zeros((), BF16))

    decay = []
    lag = []
    for d in range(2):
        lam_r = lamr_ref[d]
        lam_i = lami_ref[d]
        dt = jnp.exp(ldt_ref[d])
        mag = jnp.exp(lam_r * dt)
        ang = lam_i * dt
        a_r = mag * jnp.cos(ang)
        a_im = mag * jnp.sin(ang)
        a_i = jnp.where(lo8, -a_im, a_im)
        den = lam_r * lam_r + lam_i * lam_i
        num = jnp.where(lo8, a_r - 1.0, a_im)
        f = _cmul(num, lam_r / den, jnp.where(lo8, lam_i, -lam_i) / den)
        pw = [jnp.where(lo8, 1.0, 0.0)]
        for _ in range(CHUNK):
            pw.append(_cmul(pw[-1], a_r, a_i))
        decay.append(_multiplier(pw[CHUNK], lo8))
        pw = [_rep_rows(p) for p in pw]
        f_r, f_i = _multiplier(_rep_rows(f), lo)
        bb_r, bb_i = _multiplier(_cmul(bt_ref[d], f_r, f_i), lo)
        c_r, c_i = _multiplier(cp_ref[d], lo)
        cm = (cp_ref[d] * conj).astype(BF16)
        fpow = [_cmul(p, bb_r, bb_i).astype(BF16) for p in pw[:CHUNK]]
        for j in range(CHUNK):
            e = (CHUNK - 1 - j) if d == 0 else j
            f_scr[d, j * LANES:(j + 1) * LANES, :] = expand(fpow[e])
        for t in range(CHUNK):
            e = (t + 1) if d == 0 else (CHUNK - t)
            w = _cmul(pw[e], c_r, c_i) * conj
            e_scr[d, t * LANES:(t + 1) * LANES, :] = expand(w.astype(BF16))
        lag.append([jnp.where(diag, _dot_nt(fp, cm), 0.0) for fp in fpow])

    for j in range(CHUNK):
        for t in range(CHUNK):
            k = t - j
            tile = lag[0][k] if k > 0 else (lag[1][-k] if k < 0 else lag[0][0] + lag[1][0])
            k_scr[j * LANES:(j + 1) * LANES, t * LANES:(t + 1) * LANES] = tile.astype(BF16)

    xcat = jnp.concatenate(
        [h_ref[pl.ds(j, nc, stride=CHUNK), :].astype(BF16) for j in range(CHUNK)], axis=1)

    for d, scr, sw_scr in ((0, sf_scr, swf_scr), (1, sb_scr, swb_scr)):
        loc_all = _dot(xcat, f_scr[d])
        for k in range(gpb):
            loc = loc_all[:, k * LANES:(k + 1) * LANES]
            scr[k] = loc
            sw_scr[k] = _swap(loc)

    def scan(scr, sw_scr, d, reverse):
        a_r = [decay[d][0][k:k + 1, :] for k in range(gpb)]
        a_i = [decay[d][1][k:k + 1, :] for k in range(gpb)]

        def body(i, carry):
            st, sw = carry
            c = (cps - 1 - i) if reverse else i
            rows = pl.ds(c, n_seq, stride=cps)
            new_st, new_sw = [], []
            for k in range(gpb):
                loc = scr[k, rows, :]
                loc_sw = sw_scr[k, rows, :]
                scr[k, rows, :] = st[k]
                new_st.append(a_r[k] * st[k] + a_i[k] * sw[k] + loc)
                new_sw.append(a_r[k] * sw[k] - a_i[k] * st[k] + loc_sw)
            return tuple(new_st), tuple(new_sw)

        st0 = tuple(s0_ref[d, :, k * LANES:(k + 1) * LANES] for k in range(gpb))
        sw0 = tuple(_swap(s) for s in st0)
        fin, _ = lax.fori_loop(0, cps, body, (st0, sw0))
        for k in range(gpb):
            sfin_ref[d, :, k * LANES:(k + 1) * LANES] = fin[k]

    scan(sf_scr, swf_scr, 0, False)
    scan(sb_scr, swb_scr, 1, True)

    s_f = jnp.concatenate([sf_scr[k].astype(BF16) for k in range(gpb)], axis=1)
    s_b = jnp.concatenate([sb_scr[k].astype(BF16) for k in range(gpb)], axis=1)
    yall = _dot(xcat, k_scr[...]) + _dot_nt(s_f, e_scr[0]) + _dot_nt(s_b, e_scr[1])
    dsk = dsk_ref[...]
    for t in range(CHUNK):
        rows = pl.ds(t, nc, stride=CHUNK)
        y_ref[rows, :] = yall[:, t * LANES:(t + 1) * LANES] + h_ref[rows, :] * dsk


def _s5_params(lam_re, lam_im, log_dt, b_re, b_im, c_re, c_im):
    lamr = jnp.concatenate([lam_re, lam_re], axis=-1).astype(F32)
    lami = jnp.concatenate([lam_im, lam_im], axis=-1).astype(F32)
    ldt = jnp.broadcast_to(log_dt.astype(F32)[..., None], lamr.shape)
    bt = jnp.concatenate([b_re.transpose(0, 1, 3, 2), b_im.transpose(0, 1, 3, 2)], axis=-1)
    cp = jnp.concatenate([c_re, c_im], axis=-1)
    return (lamr, lami, ldt, bt.reshape(2, D_MODEL, LANES).astype(F32), cp.reshape(2, D_MODEL, LANES).astype(F32))


def _s5(h, params, d_skip, s0, n_seq):
    lamr, lami, ldt, bt, cp = params
    ntok = h.shape[0]
    nc = ntok // CHUNK
    kdim = CHUNK * LANES
    gspec = pl.BlockSpec((2, GROUPS_PER_BLOCK, LANES), lambda g: (0, g, 0))
    rspec = pl.BlockSpec((2, LANES, LANES), lambda g: (0, g, 0))
    sspec = pl.BlockSpec((None, 2, n_seq, STATE_LANES), lambda g: (g, 0, 0, 0))
    state_scr = pltpu.VMEM((GROUPS_PER_BLOCK, nc, LANES), F32)
    return pl.pallas_call(
        functools.partial(_s5_kernel, n_seq),
        grid=(N_GROUP_BLOCKS,),
        in_specs=[
            pl.BlockSpec((ntok, LANES), lambda g: (0, g)),
            gspec, gspec, gspec, rspec, rspec,
            pl.BlockSpec((1, LANES), lambda g: (0, g)),
            sspec,
        ],
        out_specs=[pl.BlockSpec((ntok, LANES), lambda g: (0, g)), sspec],
        out_shape=[
            jax.ShapeDtypeStruct((ntok, D_MODEL), F32),
            jax.ShapeDtypeStruct((N_GROUP_BLOCKS, 2, n_seq, STATE_LANES), F32),
        ],
        scratch_shapes=[
            pltpu.VMEM((2, kdim, STATE_LANES), BF16),
            pltpu.VMEM((2, kdim, STATE_LANES), BF16),
            pltpu.VMEM((kdim, kdim), BF16),
            state_scr, state_scr, state_scr, state_scr,
        ],
        compiler_params=_cparams(("arbitrary",)),
        name="s5_chunked_scan",
    )(h, lamr, lami, ldt, bt, cp, d_skip, s0)


def _state_to_blocks(s):
    b = s.shape[0]
    s = s.reshape(b, 2, 2, N_GROUP_BLOCKS, GROUPS_PER_BLOCK, STATE_DIM)
    return s.transpose(3, 1, 0, 4, 2, 5).reshape(N_GROUP_BLOCKS, 2, b, STATE_LANES)


def _blocks_to_state(s):
    b = s.shape[2]
    s = s.reshape(N_GROUP_BLOCKS, 2, b, GROUPS_PER_BLOCK, 2, STATE_DIM)
    return s.transpose(2, 1, 4, 0, 3, 5).reshape(b, 2, 2, N_GROUPS, STATE_DIM)


def _rope_tables(n_tokens):
    pos = np.arange(n_tokens)
    n_freq = HEAD_DIM // 4
    freqs = ROPE_BASE ** (-np.arange(n_freq, dtype=np.float64) / n_freq)
    ang_r = (pos // GRID_W)[:, None] * freqs
    ang_c = (pos % GRID_W)[:, None] * freqs
    cos_h = np.concatenate([np.cos(ang_r), np.cos(ang_r), np.cos(ang_c), np.cos(ang_c)], axis=1)
    sin_h = np.concatenate([-np.sin(ang_r), np.sin(ang_r), -np.sin(ang_c), np.sin(ang_c)], axis=1)
    cos_t = np.concatenate([np.ones((TOKEN_TILE, LANES)), np.tile(cos_h, (1, 2))], axis=0)
    sin_t = np.concatenate([np.zeros((TOKEN_TILE, LANES)), np.tile(sin_h, (1, 2))], axis=0)
    return jnp.asarray(cos_t, F32), jnp.asarray(sin_t, F32)


def kernel(x_prompt, x_sample, cache_k, cache_v, state_ssm, c, c_ctx, norm1_g, norm2_g, w_mod, b_mod,
           w_qkv, w_o, attn_sink, ssm_lam_re, ssm_lam_im, ssm_log_dt, ssm_b_re, ssm_b_im, ssm_c_re,
           ssm_c_im, ssm_d, glu_w_a, glu_w_b, mlp_w1, mlp_w2, final_norm_g):
    bp, lp, _ = x_prompt.shape
    bx, lx, _ = x_sample.shape
    assert lx % TOKEN_TILE == 0 and (bp * lp) % TOKEN_TILE == 0
    tiles_per_lat = lx // TOKEN_TILE

    xp = x_prompt.reshape(bp * lp, D_MODEL)
    xx = x_sample.reshape(bx * lx, D_MODEL)

    cvecs = jnp.zeros((8, D_MODEL), F32).at[0].set(c_ctx).at[1:1 + bx].set(c)
    mod = _modulation(cvecs, w_mod, b_mod)

    ctx_row = lambda i: 0
    lat_row = lambda i: 1 + i // tiles_per_lat
    ctx_rope = lambda i: 0
    lat_rope = lambda i: 1 + i % tiles_per_lat

    cos_t, sin_t = _rope_tables(lx)
    wqkv = w_qkv[0].astype(BF16)
    g1 = norm1_g[0].reshape(1, D_MODEL)
    sink = attn_sink[0].astype(F32)
    qp, kp, vp = _qkv(xp, mod[0], ctx_row, g1, wqkv, cos_t, sin_t, ctx_rope)
    qx, kx, vx = _qkv(xx, mod[0], lat_row, g1, wqkv, cos_t, sin_t, lat_rope)
    op = _ctx_attention(sink, qp, kp, vp, bp, lp)
    ck = cache_k[:, 0].reshape(bx, -1, KV_DIM)
    cv = cache_v[:, 0].reshape(bx, -1, KV_DIM)
    ox = _lat_attention(sink, qx, kx, vx, ck, cv, bx, lx)

    wo = w_o[0].astype(BF16)
    w1 = mlp_w1.astype(BF16)
    w2 = mlp_w2.astype(BF16)
    g2 = norm2_g.reshape(-1, 1, D_MODEL)
    gn = norm1_g[1].reshape(1, D_MODEL)
    xp, hp = _post(xp, op, mod[0], ctx_row, g2[0], wo, None, w1[0], w2[0], mod_next=mod[1], g_next=gn)
    xx, hx = _post(xx, ox, mod[0], lat_row, g2[0], wo, None, w1[0], w2[0], mod_next=mod[1], g_next=gn)

    params = _s5_params(ssm_lam_re[0], ssm_lam_im[0], ssm_log_dt[0], ssm_b_re[0], ssm_b_im[0],
                        ssm_c_re[0], ssm_c_im[0])
    dsk = ssm_d[0].astype(F32).reshape(1, D_MODEL)
    s0p = jnp.zeros((N_GROUP_BLOCKS, 2, bp, STATE_LANES), F32)
    s0x = _state_to_blocks(state_ssm[:, 0].astype(F32))
    yp, sfin = _s5(hp, params, dsk, s0p, bp)
    yx, _ = _s5(hx, params, dsk, s0x, bx)
    new_state = _blocks_to_state(sfin)[:, None]

    wa = glu_w_a[0].astype(BF16)
    wb = glu_w_b[0].astype(BF16)
    fg = final_norm_g.reshape(1, D_MODEL)
    (yp_out,) = _post(xp, yp, mod[1], ctx_row, g2[1], wa, wb, w1[1], w2[1], final_g=fg)
    (yx_out,) = _post(xx, yx, mod[1], lat_row, g2[1], wa, wb, w1[1], w2[1], final_g=fg)

    new_k = kp.reshape(bp, 1, lp, N_KV_HEADS, HEAD_DIM)
    new_v = vp.reshape(bp, 1, lp, N_KV_HEADS, HEAD_DIM)
    return (yp_out.reshape(bp, lp, D_MODEL), yx_out.reshape(bx, lx, D_MODEL), new_k, new_v, new_state)
```

```python
import functools
import math

import numpy as np
import jax
import jax.numpy as jnp
from jax import lax
from jax.experimental import pallas as pl
from jax.experimental.pallas import tpu as pltpu

F32 = jnp.float32
BF16 = jnp.bfloat16

D_MODEL = 1024
N_HEADS = 16
N_KV_HEADS = 4
HEAD_DIM = 64
Q_PER_KV = N_HEADS // N_KV_HEADS
KV_DIM = N_KV_HEADS * HEAD_DIM
QKV_DIM = D_MODEL + 2 * KV_DIM
BLOCK = 128
GRID_W = 64
ROPE_BASE = 10000.0
ATTN_SCALE = HEAD_DIM ** -0.5
N_GROUPS = 64
GROUP_CH = 16
STATE_DIM = 64
D_FF = 4 * D_MODEL
N_MOD = 6
RMS_EPS = 1e-6
NEG_INF = -1e30

LANES = 128
SUBLANES = 8
GROUPS_PER_BLOCK = LANES // GROUP_CH
N_GROUP_BLOCKS = N_GROUPS // GROUPS_PER_BLOCK
STATE_LANES = GROUPS_PER_BLOCK * 2 * STATE_DIM
CHUNK = SUBLANES
TOKEN_TILE = 512
FF_TILE = 1024
VMEM_LIMIT = 56 * 1024 * 1024


def _cparams(semantics):
    return pltpu.CompilerParams(dimension_semantics=semantics, vmem_limit_bytes=VMEM_LIMIT)


def _rms(x):
    return x * lax.rsqrt(jnp.mean(x * x, axis=-1, keepdims=True) + RMS_EPS)


def _dot(a, b):
    return jnp.dot(a, b, preferred_element_type=F32)


def _dot_nt(a, b):
    return lax.dot_general(a, b, (((1,), (1,)), ((), ())), preferred_element_type=F32)


def _mod_kernel(cv_ref, w_ref, b_ref, o_ref):
    cv = cv_ref[...]
    s = (cv * jax.nn.sigmoid(cv)).astype(BF16)
    o_ref[0] = _dot(s, w_ref[0].astype(BF16)) + b_ref[0]


def _modulation(cvecs, w_mod, b_mod):
    depth = w_mod.shape[0]
    out = pl.pallas_call(
        _mod_kernel,
        grid=(depth, N_MOD),
        in_specs=[
            pl.BlockSpec((8, D_MODEL), lambda l, j: (0, 0)),
            pl.BlockSpec((1, D_MODEL, D_MODEL), lambda l, j: (l, 0, j)),
            pl.BlockSpec((1, 1, D_MODEL), lambda l, j: (l, 0, j)),
        ],
        out_specs=pl.BlockSpec((1, 8, D_MODEL), lambda l, j: (l, 0, j)),
        out_shape=jax.ShapeDtypeStruct((depth, 8, N_MOD * D_MODEL), F32),
        compiler_params=_cparams(("arbitrary", "arbitrary")),
        name="modulation",
    )(cvecs, w_mod, b_mod.reshape(depth, 1, N_MOD * D_MODEL))
    return out.reshape(depth, 8, N_MOD, D_MODEL)


def _qkv_kernel(x_ref, mod_ref, g_ref, w_ref, cos_ref, sin_ref, q_ref, k_ref, v_ref):
    h = _rms(x_ref[...]) * g_ref[...] * (1.0 + mod_ref[1:2, :]) + mod_ref[0:1, :]
    qkv = _dot(h.astype(BF16), w_ref[...])
    cos = cos_ref[...]
    sin = sin_ref[...]
    lane = lax.broadcasted_iota(jnp.int32, cos.shape, 1)
    first = (lane & 31) < 16
    n_rot = (D_MODEL + KV_DIM) // LANES
    for blk in range(n_rot):
        t = qkv[:, blk * LANES:(blk + 1) * LANES]
        partner = jnp.where(first, pltpu.roll(t, LANES - 16, 1), pltpu.roll(t, 16, 1))
        r = t * cos + partner * sin
        if blk < D_MODEL // LANES:
            q_ref[:, blk * LANES:(blk + 1) * LANES] = (r * ATTN_SCALE).astype(BF16)
        else:
            c0 = blk * LANES - D_MODEL
            k_ref[:, c0:c0 + LANES] = r
    v_ref[...] = qkv[:, D_MODEL + KV_DIM:]


def _qkv(x, mod, mod_row, g, w_qkv, cos_t, sin_t, rope_blk):
    ntok = x.shape[0]
    nt = ntok // TOKEN_TILE
    return pl.pallas_call(
        _qkv_kernel,
        grid=(nt,),
        in_specs=[
            pl.BlockSpec((TOKEN_TILE, D_MODEL), lambda i: (i, 0)),
            pl.BlockSpec((None, N_MOD, D_MODEL), lambda i: (mod_row(i), 0, 0)),
            pl.BlockSpec((1, D_MODEL), lambda i: (0, 0)),
            pl.BlockSpec((D_MODEL, QKV_DIM), lambda i: (0, 0)),
            pl.BlockSpec((TOKEN_TILE, LANES), lambda i: (rope_blk(i), 0)),
            pl.BlockSpec((TOKEN_TILE, LANES), lambda i: (rope_blk(i), 0)),
        ],
        out_specs=[
            pl.BlockSpec((TOKEN_TILE, D_MODEL), lambda i: (i, 0)),
            pl.BlockSpec((TOKEN_TILE, KV_DIM), lambda i: (i, 0)),
            pl.BlockSpec((TOKEN_TILE, KV_DIM), lambda i: (i, 0)),
        ],
        out_shape=[
            jax.ShapeDtypeStruct((ntok, D_MODEL), BF16),
            jax.ShapeDtypeStruct((ntok, KV_DIM), F32),
            jax.ShapeDtypeStruct((ntok, KV_DIM), F32),
        ],
        compiler_params=_cparams(("arbitrary",)),
        name="norm_qkv_rope",
    )(x, mod, g, w_qkv, cos_t, sin_t)


def _softmax_pv(parts, sink):
    m = sink
    for s, _ in parts:
        m = jnp.maximum(jnp.max(s, axis=-1, keepdims=True), m)
    den = jnp.exp(sink - m)
    acc = None
    for s, v in parts:
        p = jnp.exp(s - m)
        den = den + jnp.sum(p, axis=-1, keepdims=True)
        pv = _dot(p.astype(BF16), v)
        acc = pv if acc is None else acc + pv
    return acc / den


def _ctx_attn_kernel(sink_ref, q_ref, k_ref, v_ref, o_ref):
    for kv in range(N_KV_HEADS):
        c0 = kv * HEAD_DIM
        kh = k_ref[:, c0:c0 + HEAD_DIM].astype(BF16)
        vh = v_ref[:, c0:c0 + HEAD_DIM].astype(BF16)
        for g in range(Q_PER_KV):
            h = kv * Q_PER_KV + g
            qh = q_ref[:, h * HEAD_DIM:(h + 1) * HEAD_DIM]
            o = _softmax_pv([(_dot_nt(qh, kh), vh)], sink_ref[h])
            o_ref[:, h * HEAD_DIM:(h + 1) * HEAD_DIM] = o.astype(BF16)


def _ctx_attention(sink, q, k, v, n_batch, seq):
    return pl.pallas_call(
        _ctx_attn_kernel,
        grid=(n_batch,),
        in_specs=[
            pl.BlockSpec(memory_space=pltpu.SMEM),
            pl.BlockSpec((seq, D_MODEL), lambda b: (b, 0)),
            pl.BlockSpec((seq, KV_DIM), lambda b: (b, 0)),
            pl.BlockSpec((seq, KV_DIM), lambda b: (b, 0)),
        ],
        out_specs=pl.BlockSpec((seq, D_MODEL), lambda b: (b, 0)),
        out_shape=jax.ShapeDtypeStruct((n_batch * seq, D_MODEL), BF16),
        compiler_params=_cparams(("arbitrary",)),
        name="context_attention",
    )(sink, q, k, v)


def _lat_attn_kernel(seq, sink_ref, q_ref, k_ref, v_ref, ck_ref, cv_ref, o_ref):
    n = pl.program_id(1)
    win = 3 * BLOCK
    start = pl.multiple_of(jnp.clip((n - 1) * BLOCK, 0, seq - win), BLOCK)
    kw = k_ref[pl.ds(start, win), :].astype(BF16)
    vw = v_ref[pl.ds(start, win), :].astype(BF16)
    ck = ck_ref[...].astype(BF16)
    cv = cv_ref[...].astype(BF16)
    qpos = n * BLOCK + lax.broadcasted_iota(jnp.int32, (BLOCK, win), 0)
    kpos = start + lax.broadcasted_iota(jnp.int32, (BLOCK, win), 1)
    band = jnp.abs(kpos - qpos) <= BLOCK
    for kv in range(N_KV_HEADS):
        c0 = kv * HEAD_DIM
        kh = kw[:, c0:c0 + HEAD_DIM]
        vh = vw[:, c0:c0 + HEAD_DIM]
        ckh = ck[:, c0:c0 + HEAD_DIM]
        cvh = cv[:, c0:c0 + HEAD_DIM]
        for g in range(Q_PER_KV):
            h = kv * Q_PER_KV + g
            qh = q_ref[:, h * HEAD_DIM:(h + 1) * HEAD_DIM]
            s_win = jnp.where(band, _dot_nt(qh, kh), NEG_INF)
            s_ctx = _dot_nt(qh, ckh)
            o = _softmax_pv([(s_win, vh), (s_ctx, cvh)], sink_ref[h])
            o_ref[:, h * HEAD_DIM:(h + 1) * HEAD_DIM] = o.astype(BF16)


def _lat_attention(sink, q, k, v, ck, cv, n_batch, seq):
    nb = seq // BLOCK
    past = ck.shape[1]
    return pl.pallas_call(
        functools.partial(_lat_attn_kernel, seq),
        grid=(n_batch, nb),
        in_specs=[
            pl.BlockSpec(memory_space=pltpu.SMEM),
            pl.BlockSpec((BLOCK, D_MODEL), lambda b, n: (b * nb + n, 0)),
            pl.BlockSpec((seq, KV_DIM), lambda b, n: (b, 0)),
            pl.BlockSpec((seq, KV_DIM), lambda b, n: (b, 0)),
            pl.BlockSpec((None, past, KV_DIM), lambda b, n: (b, 0, 0)),
            pl.BlockSpec((None, past, KV_DIM), lambda b, n: (b, 0, 0)),
        ],
        out_specs=pl.BlockSpec((BLOCK, D_MODEL), lambda b, n: (b * nb + n, 0)),
        out_shape=jax.ShapeDtypeStruct((n_batch * seq, D_MODEL), BF16),
        compiler_params=_cparams(("arbitrary", "arbitrary")),
        name="latent_attention",
    )(sink, q, k, v, ck, cv)


def _gelu_tanh(x):
    c = math.sqrt(2.0 / math.pi)
    return x * (0.5 * (1.0 + jnp.tanh(c * (x + 0.044715 * (x * x * x)))))


def _post_kernel(is_attn, emit_next, final, *refs):
    refs = list(refs)
    x_ref, mix_ref, mod_ref, g2_ref, wa_ref = refs[:5]
    refs = refs[5:]
    wb_ref = None if is_attn else refs.pop(0)
    w1_ref, w2_ref = refs[:2]
    refs = refs[2:]
    modn_ref = gn_ref = fg_ref = hn_ref = None
    if emit_next:
        modn_ref, gn_ref = refs[:2]
        refs = refs[2:]
    if final:
        fg_ref = refs.pop(0)
    xo_ref = refs.pop(0)
    if emit_next:
        hn_ref = refs.pop(0)
    x1_scr, h2_scr, acc_scr = refs

    f = pl.program_id(1)

    @pl.when(f == 0)
    def _():
        if is_attn:
            mix = _dot(mix_ref[...], wa_ref[...])
        else:
            y = jnp.concatenate([mix_ref[g] for g in range(N_GROUP_BLOCKS)], axis=1)
            yg = _gelu_tanh(y).astype(BF16)
            mix = _dot(yg, wa_ref[...]) * jax.nn.sigmoid(_dot(yg, wb_ref[...]))
        x1 = x_ref[...] + mod_ref[2:3, :] * mix
        x1_scr[...] = x1
        h2 = _rms(x1) * g2_ref[...] * (1.0 + mod_ref[4:5, :]) + mod_ref[3:4, :]
        h2_scr[...] = h2.astype(BF16)
        acc_scr[...] = jnp.zeros_like(acc_scr)

    a = jnp.maximum(_dot(h2_scr[...], w1_ref[...]), 0.0)
    acc_scr[...] += _dot((a * a).astype(BF16), w2_ref[...])

    @pl.when(f == pl.num_programs(1) - 1)
    def _():
        x2 = x1_scr[...] + mod_ref[5:6, :] * acc_scr[...]
        if emit_next:
            hn = _rms(x2) * gn_ref[...] * (1.0 + modn_ref[1:2, :]) + modn_ref[0:1, :]
            for g in range(N_GROUP_BLOCKS):
                hn_ref[g] = hn[:, g * LANES:(g + 1) * LANES]
        if final:
            xo_ref[...] = _rms(x2) * fg_ref[...]
        else:
            xo_ref[...] = x2


def _post(x, mix, mod, mod_row, g2, w_a, w_b, w1, w2, mod_next=None, g_next=None, final_g=None):
    is_attn = w_b is None
    emit_next = mod_next is not None
    final = final_g is not None
    ntok = x.shape[0]
    nt = ntok // TOKEN_TILE
    nf = D_FF // FF_TILE
    tile = pl.BlockSpec((TOKEN_TILE, D_MODEL), lambda i, f: (i, 0))
    row = pl.BlockSpec((1, D_MODEL), lambda i, f: (0, 0))
    modspec = pl.BlockSpec((None, N_MOD, D_MODEL), lambda i, f: (mod_row(i), 0, 0))
    wsq = pl.BlockSpec((D_MODEL, D_MODEL), lambda i, f: (0, 0))
    gtile = pl.BlockSpec((N_GROUP_BLOCKS, TOKEN_TILE, LANES), lambda i, f: (0, i, 0))
    in_specs = [tile, tile if is_attn else gtile, modspec, row, wsq]
    args = [x, mix, mod, g2, w_a]
    if not is_attn:
        in_specs.append(wsq)
        args.append(w_b)
    in_specs += [pl.BlockSpec((D_MODEL, FF_TILE), lambda i, f: (0, f)),
                 pl.BlockSpec((FF_TILE, D_MODEL), lambda i, f: (f, 0))]
    args += [w1, w2]
    if emit_next:
        in_specs += [modspec, row]
        args += [mod_next, g_next]
    if final:
        in_specs.append(row)
        args.append(final_g)
    out_specs = [tile]
    out_shape = [jax.ShapeDtypeStruct((ntok, D_MODEL), F32)]
    if emit_next:
        out_specs.append(gtile)
        out_shape.append(jax.ShapeDtypeStruct((N_GROUP_BLOCKS, ntok, LANES), F32))
    return pl.pallas_call(
        functools.partial(_post_kernel, is_attn, emit_next, final),
        grid=(nt, nf),
        in_specs=in_specs,
        out_specs=out_specs,
        out_shape=out_shape,
        scratch_shapes=[
            pltpu.VMEM((TOKEN_TILE, D_MODEL), F32),
            pltpu.VMEM((TOKEN_TILE, D_MODEL), BF16),
            pltpu.VMEM((TOKEN_TILE, D_MODEL), F32),
        ],
        compiler_params=_cparams(("arbitrary", "arbitrary")),
        name="attn_proj_mlp" if is_attn else "glu_mlp_final",
    )(*args)


def _swap(x):
    return pltpu.roll(x, LANES // 2, 1)


def _cmul(z, w_r, w_i):
    return z * w_r + _swap(z) * w_i


def _multiplier(z, lo):
    zs = _swap(z)
    return jnp.where(lo, z, zs), jnp.where(lo, -zs, z)


def _rep_rows(x):
    return jnp.concatenate(
        [jnp.broadcast_to(x[g:g + 1, :], (GROUP_CH, LANES)) for g in range(GROUPS_PER_BLOCK)], axis=0)


def _s5_kernel(n_seq, h_ref, lamr_ref, lami_ref, ldt_ref, bt_ref, cp_ref, dsk_ref, s0_ref,
               y_ref, sfin_ref, f_scr, e_scr, k_scr, sf_scr, sb_scr, swf_scr, swb_scr):
    ntok = h_ref.shape[0]
    nc = ntok // CHUNK
    cps = nc // n_seq
    gpb = GROUPS_PER_BLOCK

    lo8 = lax.broadcasted_iota(jnp.int32, (gpb, LANES), 1) < STATE_DIM
    lo = lax.broadcasted_iota(jnp.int32, (LANES, LANES), 1) < STATE_DIM
    conj = jnp.where(lo, 1.0, -1.0)
    row_g = lax.broadcasted_iota(jnp.int32, (LANES, STATE_LANES), 0) >> 4
    col_g = lax.broadcasted_iota(jnp.int32, (LANES, STATE_LANES), 1) >> 7
    diag_wide = row_g == col_g
    diag = (lax.broadcasted_iota(jnp.int32, (LANES, LANES), 0) >> 4) == (
        lax.broadcasted_iota(jnp.int32, (LANES, LANES), 1) >> 4)

    def expand(w):
        return jnp.where(diag_wide, jnp.concatenate([w] * gpb, axis=1), jnp.zeros((), BF16))

    decay = []
    lag = []
    for d in range(2):
        lam_r = lamr_ref[d]
        lam_i = lami_ref[d]
        dt = jnp.exp(ldt_ref[d])
        mag = jnp.exp(lam_r * dt)
        ang = lam_i * dt
        a_r = mag * jnp.cos(ang)
        a_im = mag * jnp.sin(ang)
        a_i = jnp.where(lo8, -a_im, a_im)
        den = lam_r * lam_r + lam_i * lam_i
        num = jnp.where(lo8, a_r - 1.0, a_im)
        f = _cmul(num, lam_r / den, jnp.where(lo8, lam_i, -lam_i) / den)
        pw = [jnp.where(lo8, 1.0, 0.0)]
        for _ in range(CHUNK):
            pw.append(_cmul(pw[-1], a_r, a_i))
        decay.append(_multiplier(pw[CHUNK], lo8))
        pw = [_rep_rows(p) for p in pw]
        f_r, f_i = _multiplier(_rep_rows(f), lo)
        bb_r, bb_i = _multiplier(_cmul(bt_ref[d], f_r, f_i), lo)
        c_r, c_i = _multiplier(cp_ref[d], lo)
        cm = (cp_ref[d] * conj).astype(BF16)
        fpow = [_cmul(p, bb_r, bb_i).astype(BF16) for p in pw[:CHUNK]]
        for j in range(CHUNK):
            e = (CHUNK - 1 - j) if d == 0 else j
            f_scr[d, j * LANES:(j + 1) * LANES, :] = expand(fpow[e])
        for t in range(CHUNK):
            e = (t + 1) if d == 0 else (CHUNK - t)
            w = _cmul(pw[e], c_r, c_i) * conj
            e_scr[d, t * LANES:(t + 1) * LANES, :] = expand(w.astype(BF16))
        lag.append([jnp.where(diag, _dot_nt(fp, cm), 0.0) for fp in fpow])

    for j in range(CHUNK):
        for t in range(CHUNK):
            k = t - j
            tile = lag[0][k] if k > 0 else (lag[1][-k] if k < 0 else lag[0][0] + lag[1][0])
            k_scr[j * LANES:(j + 1) * LANES, t * LANES:(t + 1) * LANES] = tile.astype(BF16)

    xcat = jnp.concatenate(
        [h_ref[pl.ds(j, nc, stride=CHUNK), :].astype(BF16) for j in range(CHUNK)], axis=1)

    for d, scr, sw_scr in ((0, sf_scr, swf_scr), (1, sb_scr, swb_scr)):
        loc_all = _dot(xcat, f_scr[d])
        for k in range(gpb):
            loc = loc_all[:, k * LANES:(k + 1) * LANES]
            scr[k] = loc
            sw_scr[k] = _swap(loc)

    def scan(scr, sw_scr, d, reverse):
        a_r = [decay[d][0][k:k + 1, :] for k in range(gpb)]
        a_i = [decay[d][1][k:k + 1, :] for k in range(gpb)]

        def body(i, carry):
            st, sw = carry
            c = (cps - 1 - i) if reverse else i
            rows = pl.ds(c, n_seq, stride=cps)
            new_st, new_sw = [], []
            for k in range(gpb):
                loc = scr[k, rows, :]
                loc_sw = sw_scr[k, rows, :]
                scr[k, rows, :] = st[k]
                new_st.append(a_r[k] * st[k] + a_i[k] * sw[k] + loc)
                new_sw.append(a_r[k] * sw[k] - a_i[k] * st[k] + loc_sw)
            return tuple(new_st), tuple(new_sw)

        st0 = tuple(s0_ref[d, :, k * LANES:(k + 1) * LANES] for k in range(gpb))
        sw0 = tuple(_swap(s) for s in st0)
        fin, _ = lax.fori_loop(0, cps, body, (st0, sw0))
        for k in range(gpb):
            sfin_ref[d, :, k * LANES:(k + 1) * LANES] = fin[k]

    scan(sf_scr, swf_scr, 0, False)
    scan(sb_scr, swb_scr, 1, True)

    s_f = jnp.concatenate([sf_scr[k].astype(BF16) for k in range(gpb)], axis=1)
    s_b = jnp.concatenate([sb_scr[k].astype(BF16) for k in range(gpb)], axis=1)
    yall = _dot(xcat, k_scr[...]) + _dot_nt(s_f, e_scr[0]) + _dot_nt(s_b, e_scr[1])
    dsk = dsk_ref[...]
    for t in range(CHUNK):
        rows = pl.ds(t, nc, stride=CHUNK)
        y_ref[rows, :] = yall[:, t * LANES:(t + 1) * LANES] + h_ref[rows, :] * dsk


def _s5_params(lam_re, lam_im, log_dt, b_re, b_im, c_re, c_im):
    lamr = jnp.concatenate([lam_re, lam_re], axis=-1).astype(F32)
    lami = jnp.concatenate([lam_im, lam_im], axis=-1).astype(F32)
    ldt = jnp.broadcast_to(log_dt.astype(F32)[..., None], lamr.shape)
    bt = jnp.concatenate([b_re.transpose(0, 1, 3, 2), b_im.transpose(0, 1, 3, 2)], axis=-1)
    cp = jnp.concatenate([c_re, c_im], axis=-1)
    return (lamr, lami, ldt, bt.reshape(2, D_MODEL, LANES).astype(F32), cp.reshape(2, D_MODEL, LANES).astype(F32))


def _s5(h, params, d_skip, s0, n_seq):
    lamr, lami, ldt, bt, cp = params
    ntok = h.shape[1]
    nc = ntok // CHUNK
    hspec = pl.BlockSpec((None, ntok, LANES), lambda g: (g, 0, 0))
    kdim = CHUNK * LANES
    gspec = pl.BlockSpec((2, GROUPS_PER_BLOCK, LANES), lambda g: (0, g, 0))
    rspec = pl.BlockSpec((2, LANES, LANES), lambda g: (0, g, 0))
    sspec = pl.BlockSpec((None, 2, n_seq, STATE_LANES), lambda g: (g, 0, 0, 0))
    state_scr = pltpu.VMEM((GROUPS_PER_BLOCK, nc, LANES), F32)
    return pl.pallas_call(
        functools.partial(_s5_kernel, n_seq),
        grid=(N_GROUP_BLOCKS,),
        in_specs=[
            hspec,
            gspec, gspec, gspec, rspec, rspec,
            pl.BlockSpec((1, LANES), lambda g: (0, g)),
            sspec,
        ],
        out_specs=[hspec, sspec],
        out_shape=[
            jax.ShapeDtypeStruct((N_GROUP_BLOCKS, ntok, LANES), F32),
            jax.ShapeDtypeStruct((N_GROUP_BLOCKS, 2, n_seq, STATE_LANES), F32),
        ],
        scratch_shapes=[
            pltpu.VMEM((2, kdim, STATE_LANES), BF16),
            pltpu.VMEM((2, kdim, STATE_LANES), BF16),
            pltpu.VMEM((kdim, kdim), BF16),
            state_scr, state_scr, state_scr, state_scr,
        ],
        compiler_params=_cparams(("arbitrary",)),
        name="s5_chunked_scan",
    )(h, lamr, lami, ldt, bt, cp, d_skip, s0)


def _state_to_blocks(s):
    b = s.shape[0]
    s = s.reshape(b, 2, 2, N_GROUP_BLOCKS, GROUPS_PER_BLOCK, STATE_DIM)
    return s.transpose(3, 1, 0, 4, 2, 5).reshape(N_GROUP_BLOCKS, 2, b, STATE_LANES)


def _blocks_to_state(s):
    b = s.shape[2]
    s = s.reshape(N_GROUP_BLOCKS, 2, b, GROUPS_PER_BLOCK, 2, STATE_DIM)
    return s.transpose(2, 1, 4, 0, 3, 5).reshape(b, 2, 2, N_GROUPS, STATE_DIM)


def _rope_tables(n_tokens):
    pos = np.arange(n_tokens)
    n_freq = HEAD_DIM // 4
    freqs = ROPE_BASE ** (-np.arange(n_freq, dtype=np.float64) / n_freq)
    ang_r = (pos // GRID_W)[:, None] * freqs
    ang_c = (pos % GRID_W)[:, None] * freqs
    cos_h = np.concatenate([np.cos(ang_r), np.cos(ang_r), np.cos(ang_c), np.cos(ang_c)], axis=1)
    sin_h = np.concatenate([-np.sin(ang_r), np.sin(ang_r), -np.sin(ang_c), np.sin(ang_c)], axis=1)
    cos_t = np.concatenate([np.ones((TOKEN_TILE, LANES)), np.tile(cos_h, (1, 2))], axis=0)
    sin_t = np.concatenate([np.zeros((TOKEN_TILE, LANES)), np.tile(sin_h, (1, 2))], axis=0)
    return jnp.asarray(cos_t, F32), jnp.asarray(sin_t, F32)


def kernel(x_prompt, x_sample, cache_k, cache_v, state_ssm, c, c_ctx, norm1_g, norm2_g, w_mod, b_mod,
           w_qkv, w_o, attn_sink, ssm_lam_re, ssm_lam_im, ssm_log_dt, ssm_b_re, ssm_b_im, ssm_c_re,
           ssm_c_im, ssm_d, glu_w_a, glu_w_b, mlp_w1, mlp_w2, final_norm_g):
    bp, lp, _ = x_prompt.shape
    bx, lx, _ = x_sample.shape
    assert lx % TOKEN_TILE == 0 and (bp * lp) % TOKEN_TILE == 0
    tiles_per_lat = lx // TOKEN_TILE

    xp = x_prompt.reshape(bp * lp, D_MODEL)
    xx = x_sample.reshape(bx * lx, D_MODEL)

    cvecs = jnp.zeros((8, D_MODEL), F32).at[0].set(c_ctx).at[1:1 + bx].set(c)
    mod = _modulation(cvecs, w_mod, b_mod)

    ctx_row = lambda i: 0
    lat_row = lambda i: 1 + i // tiles_per_lat
    ctx_rope = lambda i: 0
    lat_rope = lambda i: 1 + i % tiles_per_lat

    cos_t, sin_t = _rope_tables(lx)
    wqkv = w_qkv[0].astype(BF16)
    g1 = norm1_g[0].reshape(1, D_MODEL)
    sink = attn_sink[0].astype(F32)
    qp, kp, vp = _qkv(xp, mod[0], ctx_row, g1, wqkv, cos_t, sin_t, ctx_rope)
    qx, kx, vx = _qkv(xx, mod[0], lat_row, g1, wqkv, cos_t, sin_t, lat_rope)
    op = _ctx_attention(sink, qp, kp, vp, bp, lp)
    ck = cache_k[:, 0].reshape(bx, -1, KV_DIM)
    cv = cache_v[:, 0].reshape(bx, -1, KV_DIM)
    ox = _lat_attention(sink, qx, kx, vx, ck, cv, bx, lx)

    wo = w_o[0].astype(BF16)
    w1 = mlp_w1.astype(BF16)
    w2 = mlp_w2.astype(BF16)
    g2 = norm2_g.reshape(-1, 1, D_MODEL)
    gn = norm1_g[1].reshape(1, D_MODEL)
    xp, hp = _post(xp, op, mod[0], ctx_row, g2[0], wo, None, w1[0], w2[0], mod_next=mod[1], g_next=gn)
    xx, hx = _post(xx, ox, mod[0], lat_row, g2[0], wo, None, w1[0], w2[0], mod_next=mod[1], g_next=gn)

    params = _s5_params(ssm_lam_re[0], ssm_lam_im[0], ssm_log_dt[0], ssm_b_re[0], ssm_b_im[0],
                        ssm_c_re[0], ssm_c_im[0])
    dsk = ssm_d[0].astype(F32).reshape(1, D_MODEL)
    s0p = jnp.zeros((N_GROUP_BLOCKS, 2, bp, STATE_LANES), F32)
    s0x = _state_to_blocks(state_ssm[:, 0].astype(F32))
    yp, sfin = _s5(hp, params, dsk, s0p, bp)
    yx, _ = _s5(hx, params, dsk, s0x, bx)
    new_state = _blocks_to_state(sfin)[:, None]

    wa = glu_w_a[0].astype(BF16)
    wb = glu_w_b[0].astype(BF16)
    fg = final_norm_g.reshape(1, D_MODEL)
    (yp_out,) = _post(xp, yp, mod[1], ctx_row, g2[1], wa, wb, w1[1], w2[1], final_g=fg)
    (yx_out,) = _post(xx, yx, mod[1], lat_row, g2[1], wa, wb, w1[1], w2[1], final_g=fg)

    new_k = kp.reshape(bp, 1, lp, N_KV_HEADS, HEAD_DIM)
    new_v = vp.reshape(bp, 1, lp, N_KV_HEADS, HEAD_DIM)
    return (yp_out.reshape(bp, lp, D_MODEL), yx_out.reshape(bx, lx, D_MODEL), new_k, new_v, new_state)
```

```python
import functools
import math

import numpy as np
import jax
import jax.numpy as jnp
from jax import lax
from jax.experimental import pallas as pl
from jax.experimental.pallas import tpu as pltpu

F32 = jnp.float32
BF16 = jnp.bfloat16

D_MODEL = 1024
N_HEADS = 16
N_KV_HEADS = 4
HEAD_DIM = 64
Q_PER_KV = N_HEADS // N_KV_HEADS
KV_DIM = N_KV_HEADS * HEAD_DIM
QKV_DIM = D_MODEL + 2 * KV_DIM
BLOCK = 128
GRID_W = 64
ROPE_BASE = 10000.0
ATTN_SCALE = HEAD_DIM ** -0.5
N_GROUPS = 64
GROUP_CH = 16
STATE_DIM = 64
D_FF = 4 * D_MODEL
N_MOD = 6
RMS_EPS = 1e-6
NEG_INF = -1e30

LANES = 128
SUBLANES = 8
GROUPS_PER_BLOCK = LANES // GROUP_CH
N_GROUP_BLOCKS = N_GROUPS // GROUPS_PER_BLOCK
STATE_LANES = GROUPS_PER_BLOCK * 2 * STATE_DIM
CHUNK = SUBLANES
TOKEN_TILE = 512
FF_TILE = 1024
VMEM_LIMIT = 56 * 1024 * 1024


def _cparams(semantics):
    return pltpu.CompilerParams(dimension_semantics=semantics, vmem_limit_bytes=VMEM_LIMIT)


def _rms(x):
    return x * lax.rsqrt(jnp.mean(x * x, axis=-1, keepdims=True) + RMS_EPS)


def _dot(a, b):
    return jnp.dot(a, b, preferred_element_type=F32)


def _dot_nt(a, b):
    return lax.dot_general(a, b, (((1,), (1,)), ((), ())), preferred_element_type=F32)


def _mod_kernel(cv_ref, w_ref, b_ref, o_ref):
    cv = cv_ref[...]
    s = (cv * jax.nn.sigmoid(cv)).astype(BF16)
    o_ref[0] = _dot(s, w_ref[0].astype(BF16)) + b_ref[0]


def _modulation(cvecs, w_mod, b_mod):
    depth = w_mod.shape[0]
    out = pl.pallas_call(
        _mod_kernel,
        grid=(depth, N_MOD),
        in_specs=[
            pl.BlockSpec((8, D_MODEL), lambda l, j: (0, 0)),
            pl.BlockSpec((1, D_MODEL, D_MODEL), lambda l, j: (l, 0, j)),
            pl.BlockSpec((1, 1, D_MODEL), lambda l, j: (l, 0, j)),
        ],
        out_specs=pl.BlockSpec((1, 8, D_MODEL), lambda l, j: (l, 0, j)),
        out_shape=jax.ShapeDtypeStruct((depth, 8, N_MOD * D_MODEL), F32),
        compiler_params=_cparams(("arbitrary", "arbitrary")),
        name="modulation",
    )(cvecs, w_mod, b_mod.reshape(depth, 1, N_MOD * D_MODEL))
    return out.reshape(depth, 8, N_MOD, D_MODEL)


def _qkv_kernel(x_ref, mod_ref, g_ref, w_ref, cos_ref, sin_ref, q_ref, k_ref, v_ref):
    h = _rms(x_ref[...]) * g_ref[...] * (1.0 + mod_ref[1:2, :]) + mod_ref[0:1, :]
    qkv = _dot(h.astype(BF16), w_ref[...])
    cos = cos_ref[...]
    sin = sin_ref[...]
    lane = lax.broadcasted_iota(jnp.int32, cos.shape, 1)
    first = (lane & 31) < 16
    n_rot = (D_MODEL + KV_DIM) // LANES
    for blk in range(n_rot):
        t = qkv[:, blk * LANES:(blk + 1) * LANES]
        partner = jnp.where(first, pltpu.roll(t, LANES - 16, 1), pltpu.roll(t, 16, 1))
        r = t * cos + partner * sin
        if blk < D_MODEL // LANES:
            q_ref[:, blk * LANES:(blk + 1) * LANES] = (r * ATTN_SCALE).astype(BF16)
        else:
            c0 = blk * LANES - D_MODEL
            k_ref[:, c0:c0 + LANES] = r
    v_ref[...] = qkv[:, D_MODEL + KV_DIM:]


def _qkv(x, mod, mod_row, g, w_qkv, cos_t, sin_t, rope_blk):
    ntok = x.shape[0]
    nt = ntok // TOKEN_TILE
    return pl.pallas_call(
        _qkv_kernel,
        grid=(nt,),
        in_specs=[
            pl.BlockSpec((TOKEN_TILE, D_MODEL), lambda i: (i, 0)),
            pl.BlockSpec((None, N_MOD, D_MODEL), lambda i: (mod_row(i), 0, 0)),
            pl.BlockSpec((1, D_MODEL), lambda i: (0, 0)),
            pl.BlockSpec((D_MODEL, QKV_DIM), lambda i: (0, 0)),
            pl.BlockSpec((TOKEN_TILE, LANES), lambda i: (rope_blk(i), 0)),
            pl.BlockSpec((TOKEN_TILE, LANES), lambda i: (rope_blk(i), 0)),
        ],
        out_specs=[
            pl.BlockSpec((TOKEN_TILE, D_MODEL), lambda i: (i, 0)),
            pl.BlockSpec((TOKEN_TILE, KV_DIM), lambda i: (i, 0)),
            pl.BlockSpec((TOKEN_TILE, KV_DIM), lambda i: (i, 0)),
        ],
        out_shape=[
            jax.ShapeDtypeStruct((ntok, D_MODEL), BF16),
            jax.ShapeDtypeStruct((ntok, KV_DIM), F32),
            jax.ShapeDtypeStruct((ntok, KV_DIM), F32),
        ],
        compiler_params=_cparams(("arbitrary",)),
        name="norm_qkv_rope",
    )(x, mod, g, w_qkv, cos_t, sin_t)


def _softmax_pv(parts, sink):
    m = sink
    for s, _ in parts:
        m = jnp.maximum(jnp.max(s, axis=-1, keepdims=True), m)
    den = jnp.exp(sink - m)
    acc = None
    for s, v in parts:
        p = jnp.exp(s - m)
        den = den + jnp.sum(p, axis=-1, keepdims=True)
        pv = _dot(p.astype(BF16), v)
        acc = pv if acc is None else acc + pv
    return acc / den


def _ctx_attn_kernel(sink_ref, q_ref, k_ref, v_ref, o_ref):
    for kv in range(N_KV_HEADS):
        c0 = kv * HEAD_DIM
        kh = k_ref[:, c0:c0 + HEAD_DIM].astype(BF16)
        vh = v_ref[:, c0:c0 + HEAD_DIM].astype(BF16)
        for g in range(Q_PER_KV):
            h = kv * Q_PER_KV + g
            qh = q_ref[:, h * HEAD_DIM:(h + 1) * HEAD_DIM]
            o = _softmax_pv([(_dot_nt(qh, kh), vh)], sink_ref[h])
            o_ref[:, h * HEAD_DIM:(h + 1) * HEAD_DIM] = o.astype(BF16)


def _ctx_attention(sink, q, k, v, n_batch, seq):
    return pl.pallas_call(
        _ctx_attn_kernel,
        grid=(n_batch,),
        in_specs=[
            pl.BlockSpec(memory_space=pltpu.SMEM),
            pl.BlockSpec((seq, D_MODEL), lambda b: (b, 0)),
            pl.BlockSpec((seq, KV_DIM), lambda b: (b, 0)),
            pl.BlockSpec((seq, KV_DIM), lambda b: (b, 0)),
        ],
        out_specs=pl.BlockSpec((seq, D_MODEL), lambda b: (b, 0)),
        out_shape=jax.ShapeDtypeStruct((n_batch * seq, D_MODEL), BF16),
        compiler_params=_cparams(("arbitrary",)),
        name="context_attention",
    )(sink, q, k, v)


def _lat_attn_kernel(seq, sink_ref, q_ref, k_ref, v_ref, ck_ref, cv_ref, o_ref):
    n = pl.program_id(1)
    win = 3 * BLOCK
    start = pl.multiple_of(jnp.clip((n - 1) * BLOCK, 0, seq - win), BLOCK)
    kw = k_ref[pl.ds(start, win), :].astype(BF16)
    vw = v_ref[pl.ds(start, win), :].astype(BF16)
    ck = ck_ref[...].astype(BF16)
    cv = cv_ref[...].astype(BF16)
    qpos = n * BLOCK + lax.broadcasted_iota(jnp.int32, (BLOCK, win), 0)
    kpos = start + lax.broadcasted_iota(jnp.int32, (BLOCK, win), 1)
    band = jnp.abs(kpos - qpos) <= BLOCK
    for kv in range(N_KV_HEADS):
        c0 = kv * HEAD_DIM
        kh = kw[:, c0:c0 + HEAD_DIM]
        vh = vw[:, c0:c0 + HEAD_DIM]
        ckh = ck[:, c0:c0 + HEAD_DIM]
        cvh = cv[:, c0:c0 + HEAD_DIM]
        for g in range(Q_PER_KV):
            h = kv * Q_PER_KV + g
            qh = q_ref[:, h * HEAD_DIM:(h + 1) * HEAD_DIM]
            s_win = jnp.where(band, _dot_nt(qh, kh), NEG_INF)
            s_ctx = _dot_nt(qh, ckh)
            o = _softmax_pv([(s_win, vh), (s_ctx, cvh)], sink_ref[h])
            o_ref[:, h * HEAD_DIM:(h + 1) * HEAD_DIM] = o.astype(BF16)


def _lat_attention(sink, q, k, v, ck, cv, n_batch, seq):
    nb = seq // BLOCK
    past = ck.shape[1]
    return pl.pallas_call(
        functools.partial(_lat_attn_kernel, seq),
        grid=(n_batch, nb),
        in_specs=[
            pl.BlockSpec(memory_space=pltpu.SMEM),
            pl.BlockSpec((BLOCK, D_MODEL), lambda b, n: (b * nb + n, 0)),
            pl.BlockSpec((seq, KV_DIM), lambda b, n: (b, 0)),
            pl.BlockSpec((seq, KV_DIM), lambda b, n: (b, 0)),
            pl.BlockSpec((None, past, KV_DIM), lambda b, n: (b, 0, 0)),
            pl.BlockSpec((None, past, KV_DIM), lambda b, n: (b, 0, 0)),
        ],
        out_specs=pl.BlockSpec((BLOCK, D_MODEL), lambda b, n: (b * nb + n, 0)),
        out_shape=jax.ShapeDtypeStruct((n_batch * seq, D_MODEL), BF16),
        compiler_params=_cparams(("arbitrary", "arbitrary")),
        name="latent_attention",
    )(sink, q, k, v, ck, cv)


def _gelu_tanh(x):
    c = math.sqrt(2.0 / math.pi)
    return x * (0.5 * (1.0 + jnp.tanh(c * (x + 0.044715 * (x * x * x)))))


def _post_kernel(is_attn, emit_next, final, vpt, *refs):
    rows_per_v = TOKEN_TILE // vpt
    refs = list(refs)
    x_ref, mix_ref, mod_ref, g2_ref, wa_ref = refs[:5]
    refs = refs[5:]
    wb_ref = None if is_attn else refs.pop(0)
    w1_ref, w2_ref = refs[:2]
    refs = refs[2:]
    modn_ref = gn_ref = fg_ref = hn_ref = None
    if emit_next:
        modn_ref, gn_ref = refs[:2]
        refs = refs[2:]
    if final:
        fg_ref = refs.pop(0)
    xo_ref = refs.pop(0)
    if emit_next:
        hn_ref = refs.pop(0)
    x1_scr, h2_scr, acc_scr = refs

    f = pl.program_id(1)

    @pl.when(f == 0)
    def _():
        if is_attn:
            mix = _dot(mix_ref[...], wa_ref[...])
        else:
            y = jnp.concatenate(
                [jnp.concatenate([mix_ref[g, :, s].reshape(rows_per_v, LANES) for s in range(vpt)], axis=0)
                 for g in range(N_GROUP_BLOCKS)], axis=1)
            yg = _gelu_tanh(y).astype(BF16)
            mix = _dot(yg, wa_ref[...]) * jax.nn.sigmoid(_dot(yg, wb_ref[...]))
        x1 = x_ref[...] + mod_ref[2:3, :] * mix
        x1_scr[...] = x1
        h2 = _rms(x1) * g2_ref[...] * (1.0 + mod_ref[4:5, :]) + mod_ref[3:4, :]
        h2_scr[...] = h2.astype(BF16)
        acc_scr[...] = jnp.zeros_like(acc_scr)

    a = jnp.maximum(_dot(h2_scr[...], w1_ref[...]), 0.0)
    acc_scr[...] += _dot((a * a).astype(BF16), w2_ref[...])

    @pl.when(f == pl.num_programs(1) - 1)
    def _():
        x2 = x1_scr[...] + mod_ref[5:6, :] * acc_scr[...]
        if emit_next:
            hn = _rms(x2) * gn_ref[...] * (1.0 + modn_ref[1:2, :]) + modn_ref[0:1, :]
            for g in range(N_GROUP_BLOCKS):
                for s in range(vpt):
                    blk = hn[s * rows_per_v:(s + 1) * rows_per_v, g * LANES:(g + 1) * LANES]
                    hn_ref[g, :, s] = blk.reshape(rows_per_v // CHUNK, CHUNK, LANES)
        if final:
            xo_ref[...] = _rms(x2) * fg_ref[...]
        else:
            xo_ref[...] = x2


def _post(x, mix, mod, mod_row, g2, w_a, w_b, w1, w2, vpt, mod_next=None, g_next=None, final_g=None):
    is_attn = w_b is None
    emit_next = mod_next is not None
    final = final_g is not None
    ntok = x.shape[0]
    nt = ntok // TOKEN_TILE
    nf = D_FF // FF_TILE
    cpv = TOKEN_TILE // (vpt * CHUNK)
    n_virt = nt * vpt
    tile = pl.BlockSpec((TOKEN_TILE, D_MODEL), lambda i, f: (i, 0))
    row = pl.BlockSpec((1, D_MODEL), lambda i, f: (0, 0))
    modspec = pl.BlockSpec((None, N_MOD, D_MODEL), lambda i, f: (mod_row(i), 0, 0))
    wsq = pl.BlockSpec((D_MODEL, D_MODEL), lambda i, f: (0, 0))
    gtile = pl.BlockSpec((N_GROUP_BLOCKS, cpv, vpt, CHUNK, LANES), lambda i, f: (0, 0, i, 0, 0))
    in_specs = [tile, tile if is_attn else gtile, modspec, row, wsq]
    args = [x, mix, mod, g2, w_a]
    if not is_attn:
        in_specs.append(wsq)
        args.append(w_b)
    in_specs += [pl.BlockSpec((D_MODEL, FF_TILE), lambda i, f: (0, f)),
                 pl.BlockSpec((FF_TILE, D_MODEL), lambda i, f: (f, 0))]
    args += [w1, w2]
    if emit_next:
        in_specs += [modspec, row]
        args += [mod_next, g_next]
    if final:
        in_specs.append(row)
        args.append(final_g)
    out_specs = [tile]
    out_shape = [jax.ShapeDtypeStruct((ntok, D_MODEL), F32)]
    if emit_next:
        out_specs.append(gtile)
        out_shape.append(jax.ShapeDtypeStruct((N_GROUP_BLOCKS, cpv, n_virt, CHUNK, LANES), F32))
    return pl.pallas_call(
        functools.partial(_post_kernel, is_attn, emit_next, final, vpt),
        grid=(nt, nf),
        in_specs=in_specs,
        out_specs=out_specs,
        out_shape=out_shape,
        scratch_shapes=[
            pltpu.VMEM((TOKEN_TILE, D_MODEL), F32),
            pltpu.VMEM((TOKEN_TILE, D_MODEL), BF16),
            pltpu.VMEM((TOKEN_TILE, D_MODEL), F32),
        ],
        compiler_params=_cparams(("arbitrary", "arbitrary")),
        name="attn_proj_mlp" if is_attn else "glu_mlp_final",
    )(*args)


def _swap(x):
    return pltpu.roll(x, LANES // 2, 1)


def _cmul(z, w_r, w_i):
    return z * w_r + _swap(z) * w_i


def _multiplier(z, lo):
    zs = _swap(z)
    return jnp.where(lo, z, zs), jnp.where(lo, -zs, z)


def _rep_rows(x):
    return jnp.concatenate(
        [jnp.broadcast_to(x[g:g + 1, :], (GROUP_CH, LANES)) for g in range(GROUPS_PER_BLOCK)], axis=0)


def _s5_kernel(n_virt, n_seg, h_ref, lamr_ref, lami_ref, ldt_ref, bt_ref, cp_ref, dsk_ref, s0_ref, *refs):
    if n_seg == 1:
        y_ref, sfin_ref = refs[:2]
        refs = refs[2:]
    else:
        y_ref, sfin_ref = refs[0], None
        refs = refs[1:]
    f_scr, e_scr, k_scr, sf_scr, sb_scr, swf_scr, swb_scr = refs
    ntok = h_ref.shape[0]
    nc = ntok // CHUNK
    cpv = nc // n_virt
    gpb = GROUPS_PER_BLOCK

    lo8 = lax.broadcasted_iota(jnp.int32, (gpb, LANES), 1) < STATE_DIM
    lo = lax.broadcasted_iota(jnp.int32, (LANES, LANES), 1) < STATE_DIM
    conj = jnp.where(lo, 1.0, -1.0)
    row_g = lax.broadcasted_iota(jnp.int32, (LANES, STATE_LANES), 0) >> 4
    col_g = lax.broadcasted_iota(jnp.int32, (LANES, STATE_LANES), 1) >> 7
    diag_wide = row_g == col_g
    diag = (lax.broadcasted_iota(jnp.int32, (LANES, LANES), 0) >> 4) == (
        lax.broadcasted_iota(jnp.int32, (LANES, LANES), 1) >> 4)

    def expand(w):
        return jnp.where(diag_wide, jnp.concatenate([w] * gpb, axis=1), jnp.zeros((), BF16))

    decay = []
    lag = []
    for d in range(2):
        lam_r = lamr_ref[d]
        lam_i = lami_ref[d]
        dt = jnp.exp(ldt_ref[d])
        mag = jnp.exp(lam_r * dt)
        ang = lam_i * dt
        a_r = mag * jnp.cos(ang)
        a_im = mag * jnp.sin(ang)
        a_i = jnp.where(lo8, -a_im, a_im)
        den = lam_r * lam_r + lam_i * lam_i
        num = jnp.where(lo8, a_r - 1.0, a_im)
        f = _cmul(num, lam_r / den, jnp.where(lo8, lam_i, -lam_i) / den)
        pw = [jnp.where(lo8, 1.0, 0.0)]
        for _ in range(CHUNK):
            pw.append(_cmul(pw[-1], a_r, a_i))
        decay.append(pw[CHUNK])
        pw = [_rep_rows(p) for p in pw]
        f_r, f_i = _multiplier(_rep_rows(f), lo)
        bb_r, bb_i = _multiplier(_cmul(bt_ref[d], f_r, f_i), lo)
        c_r, c_i = _multiplier(cp_ref[d], lo)
        cm = (cp_ref[d] * conj).astype(BF16)
        fpow = [_cmul(p, bb_r, bb_i).astype(BF16) for p in pw[:CHUNK]]
        for j in range(CHUNK):
            e = (CHUNK - 1 - j) if d == 0 else j
            f_scr[d, j * LANES:(j + 1) * LANES, :] = expand(fpow[e])
        for t in range(CHUNK):
            e = (t + 1) if d == 0 else (CHUNK - t)
            w = _cmul(pw[e], c_r, c_i) * conj
            e_scr[d, t * LANES:(t + 1) * LANES, :] = expand(w.astype(BF16))
        lag.append([jnp.where(diag, _dot_nt(fp, cm), 0.0) for fp in fpow])

    for j in range(CHUNK):
        for t in range(CHUNK):
            k = t - j
            tile = lag[0][k] if k > 0 else (lag[1][-k] if k < 0 else lag[0][0] + lag[1][0])
            k_scr[j * LANES:(j + 1) * LANES, t * LANES:(t + 1) * LANES] = tile.astype(BF16)

    xcat = jnp.concatenate(
        [h_ref[pl.ds(j, nc, stride=CHUNK), :].astype(BF16) for j in range(CHUNK)], axis=1)

    for d, scr, sw_scr in ((0, sf_scr, swf_scr), (1, sb_scr, swb_scr)):
        loc_all = _dot(xcat, f_scr[d])
        for k in range(gpb):
            loc = loc_all[:, k * LANES:(k + 1) * LANES]
            scr[k] = loc
            sw_scr[k] = _swap(loc)

    sgn8 = jnp.where(lo8, -1.0, 1.0)

    def dup(z):
        zs = _swap(z)
        return jnp.where(lo8, z, zs), jnp.where(lo8, zs, z)

    def scan(scr, sw_scr, d, reverse):
        a_re, a_im = dup(decay[d])
        a_sg = a_im * sgn8
        a_r = [a_re[k:k + 1, :] for k in range(gpb)]
        a_i = [a_sg[k:k + 1, :] for k in range(gpb)]

        def rows_of(i):
            c = (cpv - 1 - i) if reverse else i
            return pl.ds(pl.multiple_of(c * n_virt, n_virt), n_virt)

        def body(i, carry):
            st, sw = carry
            rows = rows_of(i)
            new_st, new_sw = [], []
            for k in range(gpb):
                loc = scr[k, rows, :]
                loc_sw = sw_scr[k, rows, :]
                scr[k, rows, :] = st[k]
                new_st.append(a_r[k] * st[k] + a_i[k] * sw[k] + loc)
                new_sw.append(a_r[k] * sw[k] - a_i[k] * st[k] + loc_sw)
            return tuple(new_st), tuple(new_sw)

        st0 = tuple(s0_ref[d, :, k * LANES:(k + 1) * LANES] for k in range(gpb))
        sw0 = tuple(_swap(s) for s in st0)
        fin, _ = lax.fori_loop(0, cpv, body, (st0, sw0))
        if n_seg == 1:
            for k in range(gpb):
                sfin_ref[d, :, k * LANES:(k + 1) * LANES] = fin[k]
            return

        p = decay[d]
        for _ in range(cpv.bit_length() - 1):
            p_re, p_im = dup(p)
            p = _cmul(p, p_re, p_im * sgn8)
        v_re, v_im = dup(p)
        v_sg = v_im * sgn8
        seg = lax.broadcasted_iota(jnp.int32, (n_virt, LANES), 0) & (n_seg - 1)
        has_pred = seg != ((n_seg - 1) if reverse else 0)
        shift = (n_virt - 1) if reverse else 1
        cin = []
        for k in range(gpb):
            ck = jnp.zeros((n_virt, LANES), F32)
            for _ in range(n_seg - 1):
                nxt = fin[k] + ck * v_re[k:k + 1, :] + _swap(ck) * v_sg[k:k + 1, :]
                ck = jnp.where(has_pred, pltpu.roll(nxt, shift, 0), 0.0)
            cin.append(ck)
        cin_sw = [_swap(x) for x in cin]

        def fix(i, carry):
            q_re, q_im = carry
            rows = rows_of(i)
            q_sg = q_im * sgn8
            for k in range(gpb):
                scr[k, rows, :] += cin[k] * q_re[k:k + 1, :] + cin_sw[k] * q_sg[k:k + 1, :]
            return q_re * a_re - q_im * a_im, q_re * a_im + q_im * a_re

        lax.fori_loop(0, cpv, fix, (jnp.ones((gpb, LANES), F32), jnp.zeros((gpb, LANES), F32)))

    scan(sf_scr, swf_scr, 0, False)
    scan(sb_scr, swb_scr, 1, True)

    s_f = jnp.concatenate([sf_scr[k].astype(BF16) for k in range(gpb)], axis=1)
    s_b = jnp.concatenate([sb_scr[k].astype(BF16) for k in range(gpb)], axis=1)
    yall = _dot(xcat, k_scr[...]) + _dot_nt(s_f, e_scr[0]) + _dot_nt(s_b, e_scr[1])
    dsk = dsk_ref[...]
    for t in range(CHUNK):
        rows = pl.ds(t, nc, stride=CHUNK)
        y_ref[rows, :] = yall[:, t * LANES:(t + 1) * LANES] + h_ref[rows, :] * dsk


def _s5_params(lam_re, lam_im, log_dt, b_re, b_im, c_re, c_im):
    lamr = jnp.concatenate([lam_re, lam_re], axis=-1).astype(F32)
    lami = jnp.concatenate([lam_im, lam_im], axis=-1).astype(F32)
    ldt = jnp.broadcast_to(log_dt.astype(F32)[..., None], lamr.shape)
    bt = jnp.concatenate([b_re.transpose(0, 1, 3, 2), b_im.transpose(0, 1, 3, 2)], axis=-1)
    cp = jnp.concatenate([c_re, c_im], axis=-1)
    return (lamr, lami, ldt, bt.reshape(2, D_MODEL, LANES).astype(F32), cp.reshape(2, D_MODEL, LANES).astype(F32))


def _s5(h, params, d_skip, s0, n_seg):
    lamr, lami, ldt, bt, cp = params
    _, cpv, n_virt, _, _ = h.shape
    assert cpv & (cpv - 1) == 0 and n_virt % SUBLANES == 0 and n_seg & (n_seg - 1) == 0
    nc = cpv * n_virt
    ntok = nc * CHUNK
    hspec = pl.BlockSpec((None, ntok, LANES), lambda g: (g, 0, 0))
    kdim = CHUNK * LANES
    gspec = pl.BlockSpec((2, GROUPS_PER_BLOCK, LANES), lambda g: (0, g, 0))
    rspec = pl.BlockSpec((2, LANES, LANES), lambda g: (0, g, 0))
    sspec = pl.BlockSpec((None, 2, n_virt, STATE_LANES), lambda g: (g, 0, 0, 0))
    state_scr = pltpu.VMEM((GROUPS_PER_BLOCK, nc, LANES), F32)
    out_specs = [hspec]
    out_shape = [jax.ShapeDtypeStruct((N_GROUP_BLOCKS, ntok, LANES), F32)]
    if n_seg == 1:
        out_specs.append(sspec)
        out_shape.append(jax.ShapeDtypeStruct((N_GROUP_BLOCKS, 2, n_virt, STATE_LANES), F32))
    outs = pl.pallas_call(
        functools.partial(_s5_kernel, n_virt, n_seg),
        grid=(N_GROUP_BLOCKS,),
        in_specs=[
            hspec,
            gspec, gspec, gspec, rspec, rspec,
            pl.BlockSpec((1, LANES), lambda g: (0, g)),
            sspec,
        ],
        out_specs=out_specs,
        out_shape=out_shape,
        scratch_shapes=[
            pltpu.VMEM((2, kdim, STATE_LANES), BF16),
            pltpu.VMEM((2, kdim, STATE_LANES), BF16),
            pltpu.VMEM((kdim, kdim), BF16),
            state_scr, state_scr, state_scr, state_scr,
        ],
        compiler_params=_cparams(("arbitrary",)),
        name="s5_chunked_scan",
    )(h.reshape(N_GROUP_BLOCKS, ntok, LANES), lamr, lami, ldt, bt, cp, d_skip, s0)
    y = outs[0].reshape(h.shape)
    return (y, outs[1]) if n_seg == 1 else (y, None)


def _state_to_blocks(s):
    b = s.shape[0]
    s = s.reshape(b, 2, 2, N_GROUP_BLOCKS, GROUPS_PER_BLOCK, STATE_DIM)
    return s.transpose(3, 1, 0, 4, 2, 5).reshape(N_GROUP_BLOCKS, 2, b, STATE_LANES)


def _blocks_to_state(s):
    b = s.shape[2]
    s = s.reshape(N_GROUP_BLOCKS, 2, b, GROUPS_PER_BLOCK, 2, STATE_DIM)
    return s.transpose(2, 1, 4, 0, 3, 5).reshape(b, 2, 2, N_GROUPS, STATE_DIM)


def _rope_tables(n_tokens):
    pos = np.arange(n_tokens)
    n_freq = HEAD_DIM // 4
    freqs = ROPE_BASE ** (-np.arange(n_freq, dtype=np.float64) / n_freq)
    ang_r = (pos // GRID_W)[:, None] * freqs
    ang_c = (pos % GRID_W)[:, None] * freqs
    cos_h = np.concatenate([np.cos(ang_r), np.cos(ang_r), np.cos(ang_c), np.cos(ang_c)], axis=1)
    sin_h = np.concatenate([-np.sin(ang_r), np.sin(ang_r), -np.sin(ang_c), np.sin(ang_c)], axis=1)
    cos_t = np.concatenate([np.ones((TOKEN_TILE, LANES)), np.tile(cos_h, (1, 2))], axis=0)
    sin_t = np.concatenate([np.zeros((TOKEN_TILE, LANES)), np.tile(sin_h, (1, 2))], axis=0)
    return jnp.asarray(cos_t, F32), jnp.asarray(sin_t, F32)


def kernel(x_prompt, x_sample, cache_k, cache_v, state_ssm, c, c_ctx, norm1_g, norm2_g, w_mod, b_mod,
           w_qkv, w_o, attn_sink, ssm_lam_re, ssm_lam_im, ssm_log_dt, ssm_b_re, ssm_b_im, ssm_c_re,
           ssm_c_im, ssm_d, glu_w_a, glu_w_b, mlp_w1, mlp_w2, final_norm_g):
    bp, lp, _ = x_prompt.shape
    bx, lx, _ = x_sample.shape
    assert lx % TOKEN_TILE == 0 and (bp * lp) % TOKEN_TILE == 0
    tiles_per_lat = lx // TOKEN_TILE

    xp = x_prompt.reshape(bp * lp, D_MODEL)
    xx = x_sample.reshape(bx * lx, D_MODEL)

    cvecs = jnp.zeros((8, D_MODEL), F32).at[0].set(c_ctx).at[1:1 + bx].set(c)
    mod = _modulation(cvecs, w_mod, b_mod)

    ctx_row = lambda i: 0
    lat_row = lambda i: 1 + i // tiles_per_lat
    ctx_rope = lambda i: 0
    lat_rope = lambda i: 1 + i % tiles_per_lat

    cos_t, sin_t = _rope_tables(lx)
    wqkv = w_qkv[0].astype(BF16)
    g1 = norm1_g[0].reshape(1, D_MODEL)
    sink = attn_sink[0].astype(F32)
    qp, kp, vp = _qkv(xp, mod[0], ctx_row, g1, wqkv, cos_t, sin_t, ctx_rope)
    qx, kx, vx = _qkv(xx, mod[0], lat_row, g1, wqkv, cos_t, sin_t, lat_rope)
    op = _ctx_attention(sink, qp, kp, vp, bp, lp)
    ck = cache_k[:, 0].reshape(bx, -1, KV_DIM)
    cv = cache_v[:, 0].reshape(bx, -1, KV_DIM)
    ox = _lat_attention(sink, qx, kx, vx, ck, cv, bx, lx)

    wo = w_o[0].astype(BF16)
    w1 = mlp_w1.astype(BF16)
    w2 = mlp_w2.astype(BF16)
    g2 = norm2_g.reshape(-1, 1, D_MODEL)
    gn = norm1_g[1].reshape(1, D_MODEL)
    vpt_p = TOKEN_TILE // lp
    vpt_x = 1
    n_seg_x = tiles_per_lat
    xp, hp = _post(xp, op, mod[0], ctx_row, g2[0], wo, None, w1[0], w2[0], vpt_p, mod_next=mod[1], g_next=gn)
    xx, hx = _post(xx, ox, mod[0], lat_row, g2[0], wo, None, w1[0], w2[0], vpt_x, mod_next=mod[1], g_next=gn)

    params = _s5_params(ssm_lam_re[0], ssm_lam_im[0], ssm_log_dt[0], ssm_b_re[0], ssm_b_im[0],
                        ssm_c_re[0], ssm_c_im[0])
    dsk = ssm_d[0].astype(F32).reshape(1, D_MODEL)
    s0p = jnp.zeros((N_GROUP_BLOCKS, 2, bp, STATE_LANES), F32)
    sx = _state_to_blocks(state_ssm[:, 0].astype(F32))
    s0x = jnp.zeros((N_GROUP_BLOCKS, 2, bx, n_seg_x, STATE_LANES), F32)
    s0x = s0x.at[:, 0, :, 0].set(sx[:, 0]).at[:, 1, :, n_seg_x - 1].set(sx[:, 1])
    s0x = s0x.reshape(N_GROUP_BLOCKS, 2, bx * n_seg_x, STATE_LANES)
    yp, sfin = _s5(hp, params, dsk, s0p, 1)
    yx, _ = _s5(hx, params, dsk, s0x, n_seg_x)
    new_state = _blocks_to_state(sfin)[:, None]

    wa = glu_w_a[0].astype(BF16)
    wb = glu_w_b[0].astype(BF16)
    fg = final_norm_g.reshape(1, D_MODEL)
    (yp_out,) = _post(xp, yp, mod[1], ctx_row, g2[1], wa, wb, w1[1], w2[1], vpt_p, final_g=fg)
    (yx_out,) = _post(xx, yx, mod[1], lat_row, g2[1], wa, wb, w1[1], w2[1], vpt_x, final_g=fg)

    new_k = kp.reshape(bp, 1, lp, N_KV_HEADS, HEAD_DIM)
    new_v = vp.reshape(bp, 1, lp, N_KV_HEADS, HEAD_DIM)
    return (yp_out.reshape(bp, lp, D_MODEL), yx_out.reshape(bx, lx, D_MODEL), new_k, new_v, new_state)
```

```python
import functools
import math

import numpy as np
import jax
import jax.numpy as jnp
from jax import lax
from jax.experimental import pallas as pl
from jax.experimental.pallas import tpu as pltpu

F32 = jnp.float32
BF16 = jnp.bfloat16

D_MODEL = 1024
N_HEADS = 16
N_KV_HEADS = 4
HEAD_DIM = 64
Q_PER_KV = N_HEADS // N_KV_HEADS
KV_DIM = N_KV_HEADS * HEAD_DIM
QKV_DIM = D_MODEL + 2 * KV_DIM
BLOCK = 128
GRID_W = 64
ROPE_BASE = 10000.0
ATTN_SCALE = HEAD_DIM ** -0.5
N_GROUPS = 64
GROUP_CH = 16
STATE_DIM = 64
D_FF = 4 * D_MODEL
N_MOD = 6
RMS_EPS = 1e-6
NEG_INF = -1e30

LANES = 128
SUBLANES = 8
GROUPS_PER_BLOCK = LANES // GROUP_CH
N_GROUP_BLOCKS = N_GROUPS // GROUPS_PER_BLOCK
STATE_LANES = GROUPS_PER_BLOCK * 2 * STATE_DIM
CHUNK = SUBLANES
TOKEN_TILE = 512
FF_TILE = 1024
VMEM_LIMIT = 56 * 1024 * 1024


def _cparams(semantics):
    return pltpu.CompilerParams(dimension_semantics=semantics, vmem_limit_bytes=VMEM_LIMIT)


def _rms(x):
    return x * lax.rsqrt(jnp.mean(x * x, axis=-1, keepdims=True) + RMS_EPS)


def _dot(a, b):
    return jnp.dot(a, b, preferred_element_type=F32)


def _dot_nt(a, b):
    return lax.dot_general(a, b, (((1,), (1,)), ((), ())), preferred_element_type=F32)


def _mod_kernel(cv_ref, w_ref, b_ref, o_ref):
    cv = cv_ref[...]
    s = (cv * jax.nn.sigmoid(cv)).astype(BF16)
    o_ref[0] = _dot(s, w_ref[0].astype(BF16)) + b_ref[0]


def _modulation(cvecs, w_mod, b_mod):
    depth = w_mod.shape[0]
    out = pl.pallas_call(
        _mod_kernel,
        grid=(depth, N_MOD),
        in_specs=[
            pl.BlockSpec((8, D_MODEL), lambda l, j: (0, 0)),
            pl.BlockSpec((1, D_MODEL, D_MODEL), lambda l, j: (l, 0, j)),
            pl.BlockSpec((1, 1, D_MODEL), lambda l, j: (l, 0, j)),
        ],
        out_specs=pl.BlockSpec((1, 8, D_MODEL), lambda l, j: (l, 0, j)),
        out_shape=jax.ShapeDtypeStruct((depth, 8, N_MOD * D_MODEL), F32),
        compiler_params=_cparams(("arbitrary", "arbitrary")),
        name="modulation",
    )(cvecs, w_mod, b_mod.reshape(depth, 1, N_MOD * D_MODEL))
    return out.reshape(depth, 8, N_MOD, D_MODEL)


def _head_replicated(blk, odd):
    lo = lax.broadcasted_iota(jnp.int32, blk.shape, 1) < HEAD_DIM
    other = pltpu.roll(blk, HEAD_DIM, 1)
    dup = (jnp.where(lo, other, blk) if odd else jnp.where(lo, blk, other)).astype(BF16)
    return jnp.concatenate([dup, dup], axis=1)


def _qkv_kernel(emit_kv, x_ref, mod_ref, g_ref, w_ref, cos_ref, sin_ref, q_ref, krep_ref, vrep_ref, *kv_refs):
    h = _rms(x_ref[...]) * g_ref[...] * (1.0 + mod_ref[1:2, :]) + mod_ref[0:1, :]
    qkv = _dot(h.astype(BF16), w_ref[...])
    cos = cos_ref[...]
    sin = sin_ref[...]
    lane = lax.broadcasted_iota(jnp.int32, cos.shape, 1)
    first = (lane & 31) < 16
    n_rot = (D_MODEL + KV_DIM) // LANES
    for blk in range(n_rot):
        t = qkv[:, blk * LANES:(blk + 1) * LANES]
        partner = jnp.where(first, pltpu.roll(t, LANES - 16, 1), pltpu.roll(t, 16, 1))
        r = t * cos + partner * sin
        if blk < D_MODEL // LANES:
            q_ref[:, blk * LANES:(blk + 1) * LANES] = (r * ATTN_SCALE).astype(BF16)
        else:
            c0 = blk * LANES - D_MODEL
            if emit_kv:
                kv_refs[0][:, c0:c0 + LANES] = r
            for half in range(2):
                krep_ref[c0 // HEAD_DIM + half] = _head_replicated(r, half)
    v = qkv[:, D_MODEL + KV_DIM:]
    if emit_kv:
        kv_refs[1][...] = v
    for kv in range(N_KV_HEADS):
        blk = v[:, (kv // 2) * LANES:(kv // 2 + 1) * LANES]
        vrep_ref[kv] = _head_replicated(blk, kv % 2)


def _qkv(x, mod, mod_row, g, w_qkv, cos_t, sin_t, rope_blk, emit_kv):
    ntok = x.shape[0]
    nt = ntok // TOKEN_TILE
    rep_spec = pl.BlockSpec((N_KV_HEADS, TOKEN_TILE, KV_DIM), lambda i: (0, i, 0))
    kv_spec = pl.BlockSpec((TOKEN_TILE, KV_DIM), lambda i: (i, 0))
    rep_shape = jax.ShapeDtypeStruct((N_KV_HEADS, ntok, KV_DIM), BF16)
    kv_shape = jax.ShapeDtypeStruct((ntok, KV_DIM), F32)
    return pl.pallas_call(
        functools.partial(_qkv_kernel, emit_kv),
        grid=(nt,),
        in_specs=[
            pl.BlockSpec((TOKEN_TILE, D_MODEL), lambda i: (i, 0)),
            pl.BlockSpec((None, N_MOD, D_MODEL), lambda i: (mod_row(i), 0, 0)),
            pl.BlockSpec((1, D_MODEL), lambda i: (0, 0)),
            pl.BlockSpec((D_MODEL, QKV_DIM), lambda i: (0, 0)),
            pl.BlockSpec((TOKEN_TILE, LANES), lambda i: (rope_blk(i), 0)),
            pl.BlockSpec((TOKEN_TILE, LANES), lambda i: (rope_blk(i), 0)),
        ],
        out_specs=[pl.BlockSpec((TOKEN_TILE, D_MODEL), lambda i: (i, 0)), rep_spec, rep_spec]
        + ([kv_spec, kv_spec] if emit_kv else []),
        out_shape=[jax.ShapeDtypeStruct((ntok, D_MODEL), BF16), rep_shape, rep_shape]
        + ([kv_shape, kv_shape] if emit_kv else []),
        compiler_params=_cparams(("arbitrary",)),
        name="norm_qkv_rope",
    )(x, mod, g, w_qkv, cos_t, sin_t)


def _group_scores(q_ref, kv, key_parts, bias):
    q_kv = q_ref[:, kv * KV_DIM:(kv + 1) * KV_DIM]
    nq = q_kv.shape[0]
    slot = lax.broadcasted_iota(jnp.int32, q_kv.shape, 1) >> 6
    q4 = jnp.concatenate(
        [jnp.where(slot == g, q_kv, jnp.zeros((), BF16)) for g in range(Q_PER_KV)], axis=0)
    parts = [_dot_nt(q4, keys) for keys in key_parts]
    if bias is not None:
        s0 = parts[0].reshape(Q_PER_KV, nq, -1) + bias[None]
        parts[0] = s0.reshape(Q_PER_KV * nq, -1)
    return parts


def _group_softmax(parts, sink_ref, kv):
    nq = parts[0].shape[0] // Q_PER_KV
    sink = jnp.concatenate(
        [jnp.full((nq, 1), sink_ref[kv * Q_PER_KV + g], F32) for g in range(Q_PER_KV)], axis=0)
    m = sink
    for s in parts:
        m = jnp.maximum(jnp.max(s, axis=-1, keepdims=True), m)
    den = jnp.exp(sink - m)
    probs = []
    for s in parts:
        p = jnp.exp(s - m)
        den = den + jnp.sum(p, axis=-1, keepdims=True)
        probs.append(p.astype(BF16))
    return probs, 1.0 / den


def _group_output(probs, inv_den, value_parts, o_ref, kv):
    nq = probs[0].shape[0] // Q_PER_KV
    r = _dot(probs[0], value_parts[0])
    for p, v in zip(probs[1:], value_parts[1:]):
        r = r + _dot(p, v)
    r = r * inv_den
    slot = lax.broadcasted_iota(jnp.int32, (nq, KV_DIM), 1) >> 6
    o = r[:nq]
    for g in range(1, Q_PER_KV):
        o = jnp.where(slot == g, r[g * nq:(g + 1) * nq], o)
    o_ref[:, kv * KV_DIM:(kv + 1) * KV_DIM] = o.astype(BF16)


def _attend(q_ref, sink_ref, o_ref, keys_of, values_of, bias):
    s_next = _group_scores(q_ref, 0, keys_of(0), bias)
    for kv in range(N_KV_HEADS):
        s = s_next
        if kv + 1 < N_KV_HEADS:
            s_next = _group_scores(q_ref, kv + 1, keys_of(kv + 1), bias)
        probs, inv_den = _group_softmax(s, sink_ref, kv)
        _group_output(probs, inv_den, values_of(kv), o_ref, kv)


def _ctx_attn_kernel(sink_ref, q_ref, k_ref, v_ref, o_ref):
    _attend(q_ref, sink_ref, o_ref, lambda kv: [k_ref[kv]], lambda kv: [v_ref[kv]], None)


def _ctx_attention(sink, q, krep, vrep, n_batch, seq):
    rep_spec = pl.BlockSpec((N_KV_HEADS, seq, KV_DIM), lambda b: (0, b, 0))
    return pl.pallas_call(
        _ctx_attn_kernel,
        grid=(n_batch,),
        in_specs=[
            pl.BlockSpec(memory_space=pltpu.SMEM),
            pl.BlockSpec((seq, D_MODEL), lambda b: (b, 0)),
            rep_spec, rep_spec,
        ],
        out_specs=pl.BlockSpec((seq, D_MODEL), lambda b: (b, 0)),
        out_shape=jax.ShapeDtypeStruct((n_batch * seq, D_MODEL), BF16),
        compiler_params=_cparams(("arbitrary",)),
        name="context_attention",
    )(sink, q, krep, vrep)


def _window_start(n, seq):
    return jnp.clip((n - 1) * BLOCK, 0, seq - 3 * BLOCK)


def _band_bias():
    r = np.arange(BLOCK)[:, None]
    j = np.arange(3 * BLOCK)[None, :]
    out = [np.where(np.abs(j - d * BLOCK - r) <= BLOCK, 0.0, NEG_INF) for d in range(3)]
    return jnp.asarray(np.stack(out), F32)


def _lat_attn_kernel(seq, sink_ref, q_ref, k_ref, v_ref, ck_ref, cv_ref, bias_ref, o_ref):
    win = 3 * BLOCK
    start = pl.multiple_of(_window_start(pl.program_id(1), seq), BLOCK)
    keys_of = lambda kv: [k_ref[kv, pl.ds(start, win), :], ck_ref[kv]]
    values_of = lambda kv: [v_ref[kv, pl.ds(start, win), :], cv_ref[kv]]
    _attend(q_ref, sink_ref, o_ref, keys_of, values_of, bias_ref[...])


def _lat_attention(sink, q, krep, vrep, ckrep, cvrep, n_batch, seq):
    nb = seq // BLOCK
    past = ckrep.shape[2]
    rep_spec = pl.BlockSpec((N_KV_HEADS, seq, KV_DIM), lambda b, n: (0, b, 0))
    crep_spec = pl.BlockSpec((None, N_KV_HEADS, past, KV_DIM), lambda b, n: (b, 0, 0, 0))
    return pl.pallas_call(
        functools.partial(_lat_attn_kernel, seq),
        grid=(n_batch, nb),
        in_specs=[
            pl.BlockSpec(memory_space=pltpu.SMEM),
            pl.BlockSpec((BLOCK, D_MODEL), lambda b, n: (b * nb + n, 0)),
            rep_spec, rep_spec, crep_spec, crep_spec,
            pl.BlockSpec((None, BLOCK, 3 * BLOCK), lambda b, n: (n - _window_start(n, seq) // BLOCK, 0, 0)),
        ],
        out_specs=pl.BlockSpec((BLOCK, D_MODEL), lambda b, n: (b * nb + n, 0)),
        out_shape=jax.ShapeDtypeStruct((n_batch * seq, D_MODEL), BF16),
        compiler_params=_cparams(("arbitrary", "arbitrary")),
        name="latent_attention",
    )(sink, q, krep, vrep, ckrep, cvrep, _band_bias())


def _gelu_tanh(x):
    c = math.sqrt(2.0 / math.pi)
    return x * (0.5 * (1.0 + jnp.tanh(c * (x + 0.044715 * (x * x * x)))))


def _post_kernel(is_attn, emit_next, final, vpt, *refs):
    rows_per_v = TOKEN_TILE // vpt
    refs = list(refs)
    x_ref, mix_ref, mod_ref, g2_ref, wa_ref = refs[:5]
    refs = refs[5:]
    wb_ref = None if is_attn else refs.pop(0)
    w1_ref, w2_ref = refs[:2]
    refs = refs[2:]
    modn_ref = gn_ref = fg_ref = hn_ref = None
    if emit_next:
        modn_ref, gn_ref = refs[:2]
        refs = refs[2:]
    if final:
        fg_ref = refs.pop(0)
    xo_ref = refs.pop(0)
    if emit_next:
        hn_ref = refs.pop(0)
    x1_scr, h2_scr, acc_scr = refs

    f = pl.program_id(1)

    @pl.when(f == 0)
    def _():
        if is_attn:
            mix = _dot(mix_ref[...], wa_ref[...])
        else:
            y = jnp.concatenate(
                [jnp.concatenate([mix_ref[g, :, s].reshape(rows_per_v, LANES) for s in range(vpt)], axis=0)
                 for g in range(N_GROUP_BLOCKS)], axis=1)
            yg = _gelu_tanh(y).astype(BF16)
            mix = _dot(yg, wa_ref[...]) * jax.nn.sigmoid(_dot(yg, wb_ref[...]))
        x1 = x_ref[...] + mod_ref[2:3, :] * mix
        x1_scr[...] = x1
        h2 = _rms(x1) * g2_ref[...] * (1.0 + mod_ref[4:5, :]) + mod_ref[3:4, :]
        h2_scr[...] = h2.astype(BF16)
        acc_scr[...] = jnp.zeros_like(acc_scr)

    a = jnp.maximum(_dot(h2_scr[...], w1_ref[...]), 0.0)
    acc_scr[...] += _dot((a * a).astype(BF16), w2_ref[...])

    @pl.when(f == pl.num_programs(1) - 1)
    def _():
        x2 = x1_scr[...] + mod_ref[5:6, :] * acc_scr[...]
        if emit_next:
            hn = _rms(x2) * gn_ref[...] * (1.0 + modn_ref[1:2, :]) + modn_ref[0:1, :]
            for g in range(N_GROUP_BLOCKS):
                for s in range(vpt):
                    blk = hn[s * rows_per_v:(s + 1) * rows_per_v, g * LANES:(g + 1) * LANES]
                    hn_ref[g, :, s] = blk.reshape(rows_per_v // CHUNK, CHUNK, LANES)
        if final:
            xo_ref[...] = _rms(x2) * fg_ref[...]
        else:
            xo_ref[...] = x2


def _post(x, mix, mod, mod_row, g2, w_a, w_b, w1, w2, vpt, mod_next=None, g_next=None, final_g=None):
    is_attn = w_b is None
    emit_next = mod_next is not None
    final = final_g is not None
    ntok = x.shape[0]
    nt = ntok // TOKEN_TILE
    nf = D_FF // FF_TILE
    cpv = TOKEN_TILE // (vpt * CHUNK)
    n_virt = nt * vpt
    tile = pl.BlockSpec((TOKEN_TILE, D_MODEL), lambda i, f: (i, 0))
    row = pl.BlockSpec((1, D_MODEL), lambda i, f: (0, 0))
    modspec = pl.BlockSpec((None, N_MOD, D_MODEL), lambda i, f: (mod_row(i), 0, 0))
    wsq = pl.BlockSpec((D_MODEL, D_MODEL), lambda i, f: (0, 0))
    gtile = pl.BlockSpec((N_GROUP_BLOCKS, cpv, vpt, CHUNK, LANES), lambda i, f: (0, 0, i, 0, 0))
    in_specs = [tile, tile if is_attn else gtile, modspec, row, wsq]
    args = [x, mix, mod, g2, w_a]
    if not is_attn:
        in_specs.append(wsq)
        args.append(w_b)
    in_specs += [pl.BlockSpec((D_MODEL, FF_TILE), lambda i, f: (0, f)),
                 pl.BlockSpec((FF_TILE, D_MODEL), lambda i, f: (f, 0))]
    args += [w1, w2]
    if emit_next:
        in_specs += [modspec, row]
        args += [mod_next, g_next]
    if final:
        in_specs.append(row)
        args.append(final_g)
    out_specs = [tile]
    out_shape = [jax.ShapeDtypeStruct((ntok, D_MODEL), F32)]
    if emit_next:
        out_specs.append(gtile)
        out_shape.append(jax.ShapeDtypeStruct((N_GROUP_BLOCKS, cpv, n_virt, CHUNK, LANES), F32))
    return pl.pallas_call(
        functools.partial(_post_kernel, is_attn, emit_next, final, vpt),
        grid=(nt, nf),
        in_specs=in_specs,
        out_specs=out_specs,
        out_shape=out_shape,
        scratch_shapes=[
            pltpu.VMEM((TOKEN_TILE, D_MODEL), F32),
            pltpu.VMEM((TOKEN_TILE, D_MODEL), BF16),
            pltpu.VMEM((TOKEN_TILE, D_MODEL), F32),
        ],
        compiler_params=_cparams(("arbitrary", "arbitrary")),
        name="attn_proj_mlp" if is_attn else "glu_mlp_final",
    )(*args)


def _swap(x):
    return pltpu.roll(x, LANES // 2, 1)


def _cmul(z, w_r, w_i):
    return z * w_r + _swap(z) * w_i


def _multiplier(z, lo):
    zs = _swap(z)
    return jnp.where(lo, z, zs), jnp.where(lo, -zs, z)


def _rep_rows(x):
    return jnp.concatenate(
        [jnp.broadcast_to(x[g:g + 1, :], (GROUP_CH, LANES)) for g in range(GROUPS_PER_BLOCK)], axis=0)


def _s5_kernel(n_virt, n_seg, h_ref, lamr_ref, lami_ref, ldt_ref, bt_ref, cp_ref, dsk_ref, s0_ref, *refs):
    if n_seg == 1:
        y_ref, sfin_ref = refs[:2]
        refs = refs[2:]
    else:
        y_ref, sfin_ref = refs[0], None
        refs = refs[1:]
    f_scr, e_scr, k_scr, sf_scr, sb_scr, swf_scr, swb_scr = refs
    ntok = h_ref.shape[0]
    nc = ntok // CHUNK
    cpv = nc // n_virt
    gpb = GROUPS_PER_BLOCK

    lo8 = lax.broadcasted_iota(jnp.int32, (gpb, LANES), 1) < STATE_DIM
    lo = lax.broadcasted_iota(jnp.int32, (LANES, LANES), 1) < STATE_DIM
    conj = jnp.where(lo, 1.0, -1.0)
    row_g = lax.broadcasted_iota(jnp.int32, (LANES, STATE_LANES), 0) >> 4
    col_g = lax.broadcasted_iota(jnp.int32, (LANES, STATE_LANES), 1) >> 7
    diag_wide = row_g == col_g
    diag = (lax.broadcasted_iota(jnp.int32, (LANES, LANES), 0) >> 4) == (
        lax.broadcasted_iota(jnp.int32, (LANES, LANES), 1) >> 4)

    def expand(w):
        return jnp.where(diag_wide, jnp.concatenate([w] * gpb, axis=1), jnp.zeros((), BF16))

    decay = []
    lag = []
    for d in range(2):
        lam_r = lamr_ref[d]
        lam_i = lami_ref[d]
        dt = jnp.exp(ldt_ref[d])
        mag = jnp.exp(lam_r * dt)
        ang = lam_i * dt
        a_r = mag * jnp.cos(ang)
        a_im = mag * jnp.sin(ang)
        a_i = jnp.where(lo8, -a_im, a_im)
        den = lam_r * lam_r + lam_i * lam_i
        num = jnp.where(lo8, a_r - 1.0, a_im)
        f = _cmul(num, lam_r / den, jnp.where(lo8, lam_i, -lam_i) / den)
        pw = [jnp.where(lo8, 1.0, 0.0)]
        for _ in range(CHUNK):
            pw.append(_cmul(pw[-1], a_r, a_i))
        decay.append(pw[CHUNK])
        pw = [_rep_rows(p) for p in pw]
        f_r, f_i = _multiplier(_rep_rows(f), lo)
        bb_r, bb_i = _multiplier(_cmul(bt_ref[d], f_r, f_i), lo)
        c_r, c_i = _multiplier(cp_ref[d], lo)
        cm = (cp_ref[d] * conj).astype(BF16)
        fpow = [_cmul(p, bb_r, bb_i).astype(BF16) for p in pw[:CHUNK]]
        for j in range(CHUNK):
            e = (CHUNK - 1 - j) if d == 0 else j
            f_scr[d, j * LANES:(j + 1) * LANES, :] = expand(fpow[e])
        for t in range(CHUNK):
            e = (t + 1) if d == 0 else (CHUNK - t)
            w = _cmul(pw[e], c_r, c_i) * conj
            e_scr[d, t * LANES:(t + 1) * LANES, :] = expand(w.astype(BF16))
        lag.append([jnp.where(diag, _dot_nt(fp, cm), 0.0) for fp in fpow])

    for j in range(CHUNK):
        for t in range(CHUNK):
            k = t - j
            tile = lag[0][k] if k > 0 else (lag[1][-k] if k < 0 else lag[0][0] + lag[1][0])
            k_scr[j * LANES:(j + 1) * LANES, t * LANES:(t + 1) * LANES] = tile.astype(BF16)

    xcat = jnp.concatenate(
        [h_ref[pl.ds(j, nc, stride=CHUNK), :].astype(BF16) for j in range(CHUNK)], axis=1)

    for d, scr, sw_scr in ((0, sf_scr, swf_scr), (1, sb_scr, swb_scr)):
        loc_all = _dot(xcat, f_scr[d])
        for k in range(gpb):
            loc = loc_all[:, k * LANES:(k + 1) * LANES]
            scr[k] = loc
            sw_scr[k] = _swap(loc)

    sgn8 = jnp.where(lo8, -1.0, 1.0)

    def dup(z):
        zs = _swap(z)
        return jnp.where(lo8, z, zs), jnp.where(lo8, zs, z)

    def scan(scr, sw_scr, d, reverse):
        a_re, a_im = dup(decay[d])
        a_sg = a_im * sgn8
        a_r = [a_re[k:k + 1, :] for k in range(gpb)]
        a_i = [a_sg[k:k + 1, :] for k in range(gpb)]

        def rows_of(i):
            c = (cpv - 1 - i) if reverse else i
            return pl.ds(pl.multiple_of(c * n_virt, n_virt), n_virt)

        def body(i, carry):
            st, sw = carry
            rows = rows_of(i)
            new_st, new_sw = [], []
            for k in range(gpb):
                loc = scr[k, rows, :]
                loc_sw = sw_scr[k, rows, :]
                scr[k, rows, :] = st[k]
                new_st.append(a_r[k] * st[k] + a_i[k] * sw[k] + loc)
                new_sw.append(a_r[k] * sw[k] - a_i[k] * st[k] + loc_sw)
            return tuple(new_st), tuple(new_sw)

        st0 = tuple(s0_ref[d, :, k * LANES:(k + 1) * LANES] for k in range(gpb))
        sw0 = tuple(_swap(s) for s in st0)
        fin, _ = lax.fori_loop(0, cpv, body, (st0, sw0))
        if n_seg == 1:
            for k in range(gpb):
                sfin_ref[d, :, k * LANES:(k + 1) * LANES] = fin[k]
            return

        p = decay[d]
        for _ in range(cpv.bit_length() - 1):
            p_re, p_im = dup(p)
            p = _cmul(p, p_re, p_im * sgn8)
        v_re, v_im = dup(p)
        v_sg = v_im * sgn8
        seg = lax.broadcasted_iota(jnp.int32, (n_virt, LANES), 0) & (n_seg - 1)
        has_pred = seg != ((n_seg - 1) if reverse else 0)
        shift = (n_virt - 1) if reverse else 1
        cin = []
        for k in range(gpb):
            ck = jnp.zeros((n_virt, LANES), F32)
            for _ in range(n_seg - 1):
                nxt = fin[k] + ck * v_re[k:k + 1, :] + _swap(ck) * v_sg[k:k + 1, :]
                ck = jnp.where(has_pred, pltpu.roll(nxt, shift, 0), 0.0)
            cin.append(ck)
        cin_sw = [_swap(x) for x in cin]

        def fix(i, carry):
            q_re, q_im = carry
            rows = rows_of(i)
            q_sg = q_im * sgn8
            for k in range(gpb):
                scr[k, rows, :] += cin[k] * q_re[k:k + 1, :] + cin_sw[k] * q_sg[k:k + 1, :]
            return q_re * a_re - q_im * a_im, q_re * a_im + q_im * a_re

        lax.fori_loop(0, cpv, fix, (jnp.ones((gpb, LANES), F32), jnp.zeros((gpb, LANES), F32)))

    scan(sf_scr, swf_scr, 0, False)
    scan(sb_scr, swb_scr, 1, True)

    s_f = jnp.concatenate([sf_scr[k].astype(BF16) for k in range(gpb)], axis=1)
    s_b = jnp.concatenate([sb_scr[k].astype(BF16) for k in range(gpb)], axis=1)
    yall = _dot(xcat, k_scr[...]) + _dot_nt(s_f, e_scr[0]) + _dot_nt(s_b, e_scr[1])
    dsk = dsk_ref[...]
    for t in range(CHUNK):
        rows = pl.ds(t, nc, stride=CHUNK)
        y_ref[rows, :] = yall[:, t * LANES:(t + 1) * LANES] + h_ref[rows, :] * dsk


def _s5_params(lam_re, lam_im, log_dt, b_re, b_im, c_re, c_im):
    lamr = jnp.concatenate([lam_re, lam_re], axis=-1).astype(F32)
    lami = jnp.concatenate([lam_im, lam_im], axis=-1).astype(F32)
    ldt = jnp.broadcast_to(log_dt.astype(F32)[..., None], lamr.shape)
    bt = jnp.concatenate([b_re.transpose(0, 1, 3, 2), b_im.transpose(0, 1, 3, 2)], axis=-1)
    cp = jnp.concatenate([c_re, c_im], axis=-1)
    return (lamr, lami, ldt, bt.reshape(2, D_MODEL, LANES).astype(F32), cp.reshape(2, D_MODEL, LANES).astype(F32))


def _s5(h, params, d_skip, s0, n_seg):
    lamr, lami, ldt, bt, cp = params
    _, cpv, n_virt, _, _ = h.shape
    assert cpv & (cpv - 1) == 0 and n_virt % SUBLANES == 0 and n_seg & (n_seg - 1) == 0
    nc = cpv * n_virt
    ntok = nc * CHUNK
    hspec = pl.BlockSpec((None, ntok, LANES), lambda g: (g, 0, 0))
    kdim = CHUNK * LANES
    gspec = pl.BlockSpec((2, GROUPS_PER_BLOCK, LANES), lambda g: (0, g, 0))
    rspec = pl.BlockSpec((2, LANES, LANES), lambda g: (0, g, 0))
    sspec = pl.BlockSpec((None, 2, n_virt, STATE_LANES), lambda g: (g, 0, 0, 0))
    state_scr = pltpu.VMEM((GROUPS_PER_BLOCK, nc, LANES), F32)
    out_specs = [hspec]
    out_shape = [jax.ShapeDtypeStruct((N_GROUP_BLOCKS, ntok, LANES), F32)]
    if n_seg == 1:
        out_specs.append(sspec)
        out_shape.append(jax.ShapeDtypeStruct((N_GROUP_BLOCKS, 2, n_virt, STATE_LANES), F32))
    outs = pl.pallas_call(
        functools.partial(_s5_kernel, n_virt, n_seg),
        grid=(N_GROUP_BLOCKS,),
        in_specs=[
            hspec,
            gspec, gspec, gspec, rspec, rspec,
            pl.BlockSpec((1, LANES), lambda g: (0, g)),
            sspec,
        ],
        out_specs=out_specs,
        out_shape=out_shape,
        scratch_shapes=[
            pltpu.VMEM((2, kdim, STATE_LANES), BF16),
            pltpu.VMEM((2, kdim, STATE_LANES), BF16),
            pltpu.VMEM((kdim, kdim), BF16),
            state_scr, state_scr, state_scr, state_scr,
        ],
        compiler_params=_cparams(("arbitrary",)),
        name="s5_chunked_scan",
    )(h.reshape(N_GROUP_BLOCKS, ntok, LANES), lamr, lami, ldt, bt, cp, d_skip, s0)
    y = outs[0].reshape(h.shape)
    return (y, outs[1]) if n_seg == 1 else (y, None)


def _state_to_blocks(s):
    b = s.shape[0]
    s = s.reshape(b, 2, 2, N_GROUP_BLOCKS, GROUPS_PER_BLOCK, STATE_DIM)
    return s.transpose(3, 1, 0, 4, 2, 5).reshape(N_GROUP_BLOCKS, 2, b, STATE_LANES)


def _blocks_to_state(s):
    b = s.shape[2]
    s = s.reshape(N_GROUP_BLOCKS, 2, b, GROUPS_PER_BLOCK, 2, STATE_DIM)
    return s.transpose(2, 1, 4, 0, 3, 5).reshape(b, 2, 2, N_GROUPS, STATE_DIM)


def _rope_tables(n_tokens):
    pos = np.arange(n_tokens)
    n_freq = HEAD_DIM // 4
    freqs = ROPE_BASE ** (-np.arange(n_freq, dtype=np.float64) / n_freq)
    ang_r = (pos // GRID_W)[:, None] * freqs
    ang_c = (pos % GRID_W)[:, None] * freqs
    cos_h = np.concatenate([np.cos(ang_r), np.cos(ang_r), np.cos(ang_c), np.cos(ang_c)], axis=1)
    sin_h = np.concatenate([-np.sin(ang_r), np.sin(ang_r), -np.sin(ang_c), np.sin(ang_c)], axis=1)
    cos_t = np.concatenate([np.ones((TOKEN_TILE, LANES)), np.tile(cos_h, (1, 2))], axis=0)
    sin_t = np.concatenate([np.zeros((TOKEN_TILE, LANES)), np.tile(sin_h, (1, 2))], axis=0)
    return jnp.asarray(cos_t, F32), jnp.asarray(sin_t, F32)


def kernel(x_prompt, x_sample, cache_k, cache_v, state_ssm, c, c_ctx, norm1_g, norm2_g, w_mod, b_mod,
           w_qkv, w_o, attn_sink, ssm_lam_re, ssm_lam_im, ssm_log_dt, ssm_b_re, ssm_b_im, ssm_c_re,
           ssm_c_im, ssm_d, glu_w_a, glu_w_b, mlp_w1, mlp_w2, final_norm_g):
    bp, lp, _ = x_prompt.shape
    bx, lx, _ = x_sample.shape
    assert lx % TOKEN_TILE == 0 and (bp * lp) % TOKEN_TILE == 0
    tiles_per_lat = lx // TOKEN_TILE

    xp = x_prompt.reshape(bp * lp, D_MODEL)
    xx = x_sample.reshape(bx * lx, D_MODEL)

    cvecs = jnp.zeros((8, D_MODEL), F32).at[0].set(c_ctx).at[1:1 + bx].set(c)
    mod = _modulation(cvecs, w_mod, b_mod)

    ctx_row = lambda i: 0
    lat_row = lambda i: 1 + i // tiles_per_lat
    ctx_rope = lambda i: 0
    lat_rope = lambda i: 1 + i % tiles_per_lat

    cos_t, sin_t = _rope_tables(lx)
    wqkv = w_qkv[0].astype(BF16)
    g1 = norm1_g[0].reshape(1, D_MODEL)
    sink = attn_sink[0].astype(F32)
    qp, krp, vrp, kp, vp = _qkv(xp, mod[0], ctx_row, g1, wqkv, cos_t, sin_t, ctx_rope, True)
    qx, krx, vrx = _qkv(xx, mod[0], lat_row, g1, wqkv, cos_t, sin_t, lat_rope, False)
    op = _ctx_attention(sink, qp, krp, vrp, bp, lp)
    rep = lambda t: jnp.tile(t[:, 0].transpose(0, 2, 1, 3), (1, 1, 1, Q_PER_KV)).astype(BF16)
    ox = _lat_attention(sink, qx, krx, vrx, rep(cache_k), rep(cache_v), bx, lx)

    wo = w_o[0].astype(BF16)
    w1 = mlp_w1.astype(BF16)
    w2 = mlp_w2.astype(BF16)
    g2 = norm2_g.reshape(-1, 1, D_MODEL)
    gn = norm1_g[1].reshape(1, D_MODEL)
    vpt_p = TOKEN_TILE // lp
    vpt_x = 1
    n_seg_x = tiles_per_lat
    xp, hp = _post(xp, op, mod[0], ctx_row, g2[0], wo, None, w1[0], w2[0], vpt_p, mod_next=mod[1], g_next=gn)
    xx, hx = _post(xx, ox, mod[0], lat_row, g2[0], wo, None, w1[0], w2[0], vpt_x, mod_next=mod[1], g_next=gn)

    params = _s5_params(ssm_lam_re[0], ssm_lam_im[0], ssm_log_dt[0], ssm_b_re[0], ssm_b_im[0],
                        ssm_c_re[0], ssm_c_im[0])
    dsk = ssm_d[0].astype(F32).reshape(1, D_MODEL)
    s0p = jnp.zeros((N_GROUP_BLOCKS, 2, bp, STATE_LANES), F32)
    sx = _state_to_blocks(state_ssm[:, 0].astype(F32))
    s0x = jnp.zeros((N_GROUP_BLOCKS, 2, bx, n_seg_x, STATE_LANES), F32)
    s0x = s0x.at[:, 0, :, 0].set(sx[:, 0]).at[:, 1, :, n_seg_x - 1].set(sx[:, 1])
    s0x = s0x.reshape(N_GROUP_BLOCKS, 2, bx * n_seg_x, STATE_LANES)
    yp, sfin = _s5(hp, params, dsk, s0p, 1)
    yx, _ = _s5(hx, params, dsk, s0x, n_seg_x)
    new_state = _blocks_to_state(sfin)[:, None]

    wa = glu_w_a[0].astype(BF16)
    wb = glu_w_b[0].astype(BF16)
    fg = final_norm_g.reshape(1, D_MODEL)
    (yp_out,) = _post(xp, yp, mod[1], ctx_row, g2[1], wa, wb, w1[1], w2[1], vpt_p, final_g=fg)
    (yx_out,) = _post(xx, yx, mod[1], lat_row, g2[1], wa, wb, w1[1], w2[1], vpt_x, final_g=fg)

    new_k = kp.reshape(bp, 1, lp, N_KV_HEADS, HEAD_DIM)
    new_v = vp.reshape(bp, 1, lp, N_KV_HEADS, HEAD_DIM)
    return (yp_out.reshape(bp, lp, D_MODEL), yx_out.reshape(bx, lx, D_MODEL), new_k, new_v, new_state)
```

```python
import functools
import math

import numpy as np
import jax
import jax.numpy as jnp
from jax import lax
from jax.experimental import pallas as pl
from jax.experimental.pallas import tpu as pltpu

F32 = jnp.float32
BF16 = jnp.bfloat16

D_MODEL = 1024
N_HEADS = 16
N_KV_HEADS = 4
HEAD_DIM = 64
Q_PER_KV = N_HEADS // N_KV_HEADS
KV_DIM = N_KV_HEADS * HEAD_DIM
QKV_DIM = D_MODEL + 2 * KV_DIM
BLOCK = 128
GRID_W = 64
ROPE_BASE = 10000.0
ATTN_SCALE = HEAD_DIM ** -0.5
N_GROUPS = 64
GROUP_CH = 16
STATE_DIM = 64
D_FF = 4 * D_MODEL
N_MOD = 6
RMS_EPS = 1e-6
NEG_INF = -1e30

LANES = 128
SUBLANES = 8
GROUPS_PER_BLOCK = LANES // GROUP_CH
N_GROUP_BLOCKS = N_GROUPS // GROUPS_PER_BLOCK
STATE_LANES = GROUPS_PER_BLOCK * 2 * STATE_DIM
CHUNK = SUBLANES
TOKEN_TILE = 512
FF_TILE = 2048
VMEM_LIMIT = 56 * 1024 * 1024


def _cparams(semantics):
    return pltpu.CompilerParams(dimension_semantics=semantics, vmem_limit_bytes=VMEM_LIMIT)


def _rms(x):
    return x * lax.rsqrt(jnp.mean(x * x, axis=-1, keepdims=True) + RMS_EPS)


def _dot(a, b):
    return jnp.dot(a, b, preferred_element_type=F32)


def _dot_nt(a, b):
    return lax.dot_general(a, b, (((1,), (1,)), ((), ())), preferred_element_type=F32)


def _mod_kernel(cv_ref, w_ref, b_ref, o_ref):
    cv = cv_ref[...]
    s = (cv * jax.nn.sigmoid(cv)).astype(BF16)
    o_ref[0] = _dot(s, w_ref[0].astype(BF16)) + b_ref[0]


def _modulation(cvecs, w_mod, b_mod):
    depth = w_mod.shape[0]
    out = pl.pallas_call(
        _mod_kernel,
        grid=(depth, N_MOD),
        in_specs=[
            pl.BlockSpec((8, D_MODEL), lambda l, j: (0, 0)),
            pl.BlockSpec((1, D_MODEL, D_MODEL), lambda l, j: (l, 0, j)),
            pl.BlockSpec((1, 1, D_MODEL), lambda l, j: (l, 0, j)),
        ],
        out_specs=pl.BlockSpec((1, 8, D_MODEL), lambda l, j: (l, 0, j)),
        out_shape=jax.ShapeDtypeStruct((depth, 8, N_MOD * D_MODEL), F32),
        compiler_params=_cparams(("arbitrary", "arbitrary")),
        name="modulation",
    )(cvecs, w_mod, b_mod.reshape(depth, 1, N_MOD * D_MODEL))
    return out.reshape(depth, 8, N_MOD, D_MODEL)


def _head_replicated(blk, odd):
    lo = lax.broadcasted_iota(jnp.int32, blk.shape, 1) < HEAD_DIM
    other = pltpu.roll(blk, HEAD_DIM, 1)
    dup = (jnp.where(lo, other, blk) if odd else jnp.where(lo, blk, other)).astype(BF16)
    return jnp.concatenate([dup, dup], axis=1)


def _qkv_kernel(emit_kv, x_ref, mod_ref, g_ref, w_ref, cos_ref, sin_ref, q_ref, krep_ref, vrep_ref, *kv_refs):
    h = _rms(x_ref[...]) * g_ref[...] * (1.0 + mod_ref[1:2, :]) + mod_ref[0:1, :]
    qkv = _dot(h.astype(BF16), w_ref[...])
    cos = cos_ref[...]
    sin = sin_ref[...]
    lane = lax.broadcasted_iota(jnp.int32, cos.shape, 1)
    first = (lane & 31) < 16
    n_rot = (D_MODEL + KV_DIM) // LANES
    for blk in range(n_rot):
        t = qkv[:, blk * LANES:(blk + 1) * LANES]
        partner = jnp.where(first, pltpu.roll(t, LANES - 16, 1), pltpu.roll(t, 16, 1))
        r = t * cos + partner * sin
        if blk < D_MODEL // LANES:
            q_ref[:, blk * LANES:(blk + 1) * LANES] = (r * ATTN_SCALE).astype(BF16)
        else:
            c0 = blk * LANES - D_MODEL
            if emit_kv:
                kv_refs[0][:, c0:c0 + LANES] = r
            for half in range(2):
                krep_ref[c0 // HEAD_DIM + half] = _head_replicated(r, half)
    v = qkv[:, D_MODEL + KV_DIM:]
    if emit_kv:
        kv_refs[1][...] = v
    for kv in range(N_KV_HEADS):
        blk = v[:, (kv // 2) * LANES:(kv // 2 + 1) * LANES]
        vrep_ref[kv] = _head_replicated(blk, kv % 2)


def _qkv(x, mod, mod_row, g, w_qkv, cos_t, sin_t, rope_blk, emit_kv):
    ntok = x.shape[0]
    nt = ntok // TOKEN_TILE
    rep_spec = pl.BlockSpec((N_KV_HEADS, TOKEN_TILE, KV_DIM), lambda i: (0, i, 0))
    kv_spec = pl.BlockSpec((TOKEN_TILE, KV_DIM), lambda i: (i, 0))
    rep_shape = jax.ShapeDtypeStruct((N_KV_HEADS, ntok, KV_DIM), BF16)
    kv_shape = jax.ShapeDtypeStruct((ntok, KV_DIM), F32)
    return pl.pallas_call(
        functools.partial(_qkv_kernel, emit_kv),
        grid=(nt,),
        in_specs=[
            pl.BlockSpec((TOKEN_TILE, D_MODEL), lambda i: (i, 0)),
            pl.BlockSpec((None, N_MOD, D_MODEL), lambda i: (mod_row(i), 0, 0)),
            pl.BlockSpec((1, D_MODEL), lambda i: (0, 0)),
            pl.BlockSpec((D_MODEL, QKV_DIM), lambda i: (0, 0)),
            pl.BlockSpec((TOKEN_TILE, LANES), lambda i: (rope_blk(i), 0)),
            pl.BlockSpec((TOKEN_TILE, LANES), lambda i: (rope_blk(i), 0)),
        ],
        out_specs=[pl.BlockSpec((TOKEN_TILE, D_MODEL), lambda i: (i, 0)), rep_spec, rep_spec]
        + ([kv_spec, kv_spec] if emit_kv else []),
        out_shape=[jax.ShapeDtypeStruct((ntok, D_MODEL), BF16), rep_shape, rep_shape]
        + ([kv_shape, kv_shape] if emit_kv else []),
        compiler_params=_cparams(("arbitrary",)),
        name="norm_qkv_rope",
    )(x, mod, g, w_qkv, cos_t, sin_t)


def _group_scores(q_ref, kv, key_parts, bias):
    q_kv = q_ref[:, kv * KV_DIM:(kv + 1) * KV_DIM]
    nq = q_kv.shape[0]
    slot = lax.broadcasted_iota(jnp.int32, q_kv.shape, 1) >> 6
    q4 = jnp.concatenate(
        [jnp.where(slot == g, q_kv, jnp.zeros((), BF16)) for g in range(Q_PER_KV)], axis=0)
    parts = [_dot_nt(q4, keys) for keys in key_parts]
    if bias is not None:
        s0 = parts[0].reshape(Q_PER_KV, nq, -1) + bias[None]
        parts[0] = s0.reshape(Q_PER_KV * nq, -1)
    return parts


def _group_softmax(parts, sink_ref, kv):
    nq = parts[0].shape[0] // Q_PER_KV
    sink = jnp.concatenate(
        [jnp.full((nq, 1), sink_ref[kv * Q_PER_KV + g], F32) for g in range(Q_PER_KV)], axis=0)
    m = sink
    for s in parts:
        m = jnp.maximum(jnp.max(s, axis=-1, keepdims=True), m)
    den = jnp.exp(sink - m)
    probs = []
    for s in parts:
        p = jnp.exp(s - m)
        den = den + jnp.sum(p, axis=-1, keepdims=True)
        probs.append(p.astype(BF16))
    return probs, 1.0 / den


def _group_output(probs, inv_den, value_parts, o_ref, kv):
    nq = probs[0].shape[0] // Q_PER_KV
    r = _dot(probs[0], value_parts[0])
    for p, v in zip(probs[1:], value_parts[1:]):
        r = r + _dot(p, v)
    r = r * inv_den
    slot = lax.broadcasted_iota(jnp.int32, (nq, KV_DIM), 1) >> 6
    o = r[:nq]
    for g in range(1, Q_PER_KV):
        o = jnp.where(slot == g, r[g * nq:(g + 1) * nq], o)
    o_ref[:, kv * KV_DIM:(kv + 1) * KV_DIM] = o.astype(BF16)


def _attend(q_ref, sink_ref, o_ref, keys_of, values_of, bias):
    s_next = _group_scores(q_ref, 0, keys_of(0), bias)
    for kv in range(N_KV_HEADS):
        s = s_next
        if kv + 1 < N_KV_HEADS:
            s_next = _group_scores(q_ref, kv + 1, keys_of(kv + 1), bias)
        probs, inv_den = _group_softmax(s, sink_ref, kv)
        _group_output(probs, inv_den, values_of(kv), o_ref, kv)


def _ctx_attn_kernel(sink_ref, q_ref, k_ref, v_ref, o_ref):
    _attend(q_ref, sink_ref, o_ref, lambda kv: [k_ref[kv]], lambda kv: [v_ref[kv]], None)


def _ctx_attention(sink, q, krep, vrep, n_batch, seq):
    rep_spec = pl.BlockSpec((N_KV_HEADS, seq, KV_DIM), lambda b: (0, b, 0))
    return pl.pallas_call(
        _ctx_attn_kernel,
        grid=(n_batch,),
        in_specs=[
            pl.BlockSpec(memory_space=pltpu.SMEM),
            pl.BlockSpec((seq, D_MODEL), lambda b: (b, 0)),
            rep_spec, rep_spec,
        ],
        out_specs=pl.BlockSpec((seq, D_MODEL), lambda b: (b, 0)),
        out_shape=jax.ShapeDtypeStruct((n_batch * seq, D_MODEL), BF16),
        compiler_params=_cparams(("arbitrary",)),
        name="context_attention",
    )(sink, q, krep, vrep)


def _window_start(n, seq):
    return jnp.clip((n - 1) * BLOCK, 0, seq - 3 * BLOCK)


def _band_bias():
    r = np.arange(BLOCK)[:, None]
    j = np.arange(3 * BLOCK)[None, :]
    out = [np.where(np.abs(j - d * BLOCK - r) <= BLOCK, 0.0, NEG_INF) for d in range(3)]
    return jnp.asarray(np.stack(out), F32)


def _lat_attn_kernel(seq, sink_ref, q_ref, k_ref, v_ref, ck_ref, cv_ref, bias_ref, o_ref):
    win = 3 * BLOCK
    start = pl.multiple_of(_window_start(pl.program_id(1), seq), BLOCK)
    keys_of = lambda kv: [k_ref[kv, pl.ds(start, win), :], ck_ref[kv]]
    values_of = lambda kv: [v_ref[kv, pl.ds(start, win), :], cv_ref[kv]]
    _attend(q_ref, sink_ref, o_ref, keys_of, values_of, bias_ref[...])


def _lat_attention(sink, q, krep, vrep, ckrep, cvrep, n_batch, seq):
    nb = seq // BLOCK
    past = ckrep.shape[2]
    rep_spec = pl.BlockSpec((N_KV_HEADS, seq, KV_DIM), lambda b, n: (0, b, 0))
    crep_spec = pl.BlockSpec((None, N_KV_HEADS, past, KV_DIM), lambda b, n: (b, 0, 0, 0))
    return pl.pallas_call(
        functools.partial(_lat_attn_kernel, seq),
        grid=(n_batch, nb),
        in_specs=[
            pl.BlockSpec(memory_space=pltpu.SMEM),
            pl.BlockSpec((BLOCK, D_MODEL), lambda b, n: (b * nb + n, 0)),
            rep_spec, rep_spec, crep_spec, crep_spec,
            pl.BlockSpec((None, BLOCK, 3 * BLOCK), lambda b, n: (n - _window_start(n, seq) // BLOCK, 0, 0)),
        ],
        out_specs=pl.BlockSpec((BLOCK, D_MODEL), lambda b, n: (b * nb + n, 0)),
        out_shape=jax.ShapeDtypeStruct((n_batch * seq, D_MODEL), BF16),
        compiler_params=_cparams(("arbitrary", "arbitrary")),
        name="latent_attention",
    )(sink, q, krep, vrep, ckrep, cvrep, _band_bias())


def _gelu_tanh(x):
    c = math.sqrt(2.0 / math.pi)
    return x * (0.5 * (1.0 + jnp.tanh(c * (x + 0.044715 * (x * x * x)))))


def _post_kernel(is_attn, emit_next, final, vpt, *refs):
    rows_per_v = TOKEN_TILE // vpt
    refs = list(refs)
    x_ref, mix_ref, mod_ref, g2_ref, wa_ref = refs[:5]
    refs = refs[5:]
    wb_ref = None if is_attn else refs.pop(0)
    w1_ref, w2_ref = refs[:2]
    refs = refs[2:]
    modn_ref = gn_ref = fg_ref = hn_ref = None
    if emit_next:
        modn_ref, gn_ref = refs[:2]
        refs = refs[2:]
    if final:
        fg_ref = refs.pop(0)
    xo_ref = refs.pop(0)
    if emit_next:
        hn_ref = refs.pop(0)
    x1_scr, h2_scr, acc_scr = refs

    f = pl.program_id(1)

    @pl.when(f == 0)
    def _():
        if is_attn:
            mix = _dot(mix_ref[...], wa_ref[...])
        else:
            y = jnp.concatenate(
                [jnp.concatenate([mix_ref[g, :, s].reshape(rows_per_v, LANES) for s in range(vpt)], axis=0)
                 for g in range(N_GROUP_BLOCKS)], axis=1)
            yg = _gelu_tanh(y).astype(BF16)
            mix = _dot(yg, wa_ref[...]) * jax.nn.sigmoid(_dot(yg, wb_ref[...]))
        x1 = x_ref[...] + mod_ref[2:3, :] * mix
        x1_scr[...] = x1
        h2 = _rms(x1) * g2_ref[...] * (1.0 + mod_ref[4:5, :]) + mod_ref[3:4, :]
        h2_scr[...] = h2.astype(BF16)
        acc_scr[...] = jnp.zeros_like(acc_scr)

    a = jnp.maximum(_dot(h2_scr[...], w1_ref[...]), 0.0)
    acc_scr[...] += _dot((a * a).astype(BF16), w2_ref[...])

    @pl.when(f == pl.num_programs(1) - 1)
    def _():
        x2 = x1_scr[...] + mod_ref[5:6, :] * acc_scr[...]
        if emit_next:
            hn = _rms(x2) * gn_ref[...] * (1.0 + modn_ref[1:2, :]) + modn_ref[0:1, :]
            for g in range(N_GROUP_BLOCKS):
                for s in range(vpt):
                    blk = hn[s * rows_per_v:(s + 1) * rows_per_v, g * LANES:(g + 1) * LANES]
                    hn_ref[g, :, s] = blk.reshape(rows_per_v // CHUNK, CHUNK, LANES)
        if final:
            xo_ref[...] = _rms(x2) * fg_ref[...]
        else:
            xo_ref[...] = x2


def _post(x, mix, mod, mod_row, g2, w_a, w_b, w1, w2, vpt, mod_next=None, g_next=None, final_g=None):
    is_attn = w_b is None
    emit_next = mod_next is not None
    final = final_g is not None
    ntok = x.shape[0]
    nt = ntok // TOKEN_TILE
    nf = D_FF // FF_TILE
    cpv = TOKEN_TILE // (vpt * CHUNK)
    n_virt = nt * vpt
    tile = pl.BlockSpec((TOKEN_TILE, D_MODEL), lambda i, f: (i, 0))
    row = pl.BlockSpec((1, D_MODEL), lambda i, f: (0, 0))
    modspec = pl.BlockSpec((None, N_MOD, D_MODEL), lambda i, f: (mod_row(i), 0, 0))
    wsq = pl.BlockSpec((D_MODEL, D_MODEL), lambda i, f: (0, 0))
    gtile = pl.BlockSpec((N_GROUP_BLOCKS, cpv, vpt, CHUNK, LANES), lambda i, f: (0, 0, i, 0, 0))
    in_specs = [tile, tile if is_attn else gtile, modspec, row, wsq]
    args = [x, mix, mod, g2, w_a]
    if not is_attn:
        in_specs.append(wsq)
        args.append(w_b)
    in_specs += [pl.BlockSpec((D_MODEL, FF_TILE), lambda i, f: (0, f)),
                 pl.BlockSpec((FF_TILE, D_MODEL), lambda i, f: (f, 0))]
    args += [w1, w2]
    if emit_next:
        in_specs += [modspec, row]
        args += [mod_next, g_next]
    if final:
        in_specs.append(row)
        args.append(final_g)
    out_specs = [tile]
    out_shape = [jax.ShapeDtypeStruct((ntok, D_MODEL), F32)]
    if emit_next:
        out_specs.append(gtile)
        out_shape.append(jax.ShapeDtypeStruct((N_GROUP_BLOCKS, cpv, n_virt, CHUNK, LANES), F32))
    return pl.pallas_call(
        functools.partial(_post_kernel, is_attn, emit_next, final, vpt),
        grid=(nt, nf),
        in_specs=in_specs,
        out_specs=out_specs,
        out_shape=out_shape,
        scratch_shapes=[
            pltpu.VMEM((TOKEN_TILE, D_MODEL), F32),
            pltpu.VMEM((TOKEN_TILE, D_MODEL), BF16),
            pltpu.VMEM((TOKEN_TILE, D_MODEL), F32),
        ],
        compiler_params=_cparams(("arbitrary", "arbitrary")),
        name="attn_proj_mlp" if is_attn else "glu_mlp_final",
    )(*args)


def _swap(x):
    return pltpu.roll(x, LANES // 2, 1)


def _cmul(z, w_r, w_i):
    return z * w_r + _swap(z) * w_i


def _multiplier(z, lo):
    zs = _swap(z)
    return jnp.where(lo, z, zs), jnp.where(lo, -zs, z)


def _rep_rows(x):
    return jnp.concatenate(
        [jnp.broadcast_to(x[g:g + 1, :], (GROUP_CH, LANES)) for g in range(GROUPS_PER_BLOCK)], axis=0)


def _s5_kernel(n_virt, n_seg, h_ref, lamr_ref, lami_ref, ldt_ref, bt_ref, cp_ref, dsk_ref, s0_ref, *refs):
    if n_seg == 1:
        y_ref, sfin_ref = refs[:2]
        refs = refs[2:]
    else:
        y_ref, sfin_ref = refs[0], None
        refs = refs[1:]
    f_scr, e_scr, k_scr, sf_scr, sb_scr, swf_scr, swb_scr = refs
    ntok = h_ref.shape[0]
    nc = ntok // CHUNK
    cpv = nc // n_virt
    gpb = GROUPS_PER_BLOCK

    lo8 = lax.broadcasted_iota(jnp.int32, (gpb, LANES), 1) < STATE_DIM
    lo = lax.broadcasted_iota(jnp.int32, (LANES, LANES), 1) < STATE_DIM
    conj = jnp.where(lo, 1.0, -1.0)
    row_g = lax.broadcasted_iota(jnp.int32, (LANES, STATE_LANES), 0) >> 4
    col_g = lax.broadcasted_iota(jnp.int32, (LANES, STATE_LANES), 1) >> 7
    diag_wide = row_g == col_g
    diag = (lax.broadcasted_iota(jnp.int32, (LANES, LANES), 0) >> 4) == (
        lax.broadcasted_iota(jnp.int32, (LANES, LANES), 1) >> 4)

    def expand(w):
        return jnp.where(diag_wide, jnp.concatenate([w] * gpb, axis=1), jnp.zeros((), BF16))

    decay = []
    lag = []
    for d in range(2):
        lam_r = lamr_ref[d]
        lam_i = lami_ref[d]
        dt = jnp.exp(ldt_ref[d])
        mag = jnp.exp(lam_r * dt)
        ang = lam_i * dt
        a_r = mag * jnp.cos(ang)
        a_im = mag * jnp.sin(ang)
        a_i = jnp.where(lo8, -a_im, a_im)
        den = lam_r * lam_r + lam_i * lam_i
        num = jnp.where(lo8, a_r - 1.0, a_im)
        f = _cmul(num, lam_r / den, jnp.where(lo8, lam_i, -lam_i) / den)
        pw = [jnp.where(lo8, 1.0, 0.0)]
        for _ in range(CHUNK):
            pw.append(_cmul(pw[-1], a_r, a_i))
        decay.append(pw[CHUNK])
        pw = [_rep_rows(p) for p in pw]
        f_r, f_i = _multiplier(_rep_rows(f), lo)
        bb_r, bb_i = _multiplier(_cmul(bt_ref[d], f_r, f_i), lo)
        c_r, c_i = _multiplier(cp_ref[d], lo)
        cm = (cp_ref[d] * conj).astype(BF16)
        fpow = [_cmul(p, bb_r, bb_i).astype(BF16) for p in pw[:CHUNK]]
        for j in range(CHUNK):
            e = (CHUNK - 1 - j) if d == 0 else j
            f_scr[d, j * LANES:(j + 1) * LANES, :] = expand(fpow[e])
        for t in range(CHUNK):
            e = (t + 1) if d == 0 else (CHUNK - t)
            w = _cmul(pw[e], c_r, c_i) * conj
            e_scr[d, t * LANES:(t + 1) * LANES, :] = expand(w.astype(BF16))
        lag.append([jnp.where(diag, _dot_nt(fp, cm), 0.0) for fp in fpow])

    for j in range(CHUNK):
        for t in range(CHUNK):
            k = t - j
            tile = lag[0][k] if k > 0 else (lag[1][-k] if k < 0 else lag[0][0] + lag[1][0])
            k_scr[j * LANES:(j + 1) * LANES, t * LANES:(t + 1) * LANES] = tile.astype(BF16)

    xcat = jnp.concatenate(
        [h_ref[pl.ds(j, nc, stride=CHUNK), :].astype(BF16) for j in range(CHUNK)], axis=1)

    for d, scr, sw_scr in ((0, sf_scr, swf_scr), (1, sb_scr, swb_scr)):
        loc_all = _dot(xcat, f_scr[d])
        for k in range(gpb):
            loc = loc_all[:, k * LANES:(k + 1) * LANES]
            scr[k] = loc
            sw_scr[k] = _swap(loc)

    sgn8 = jnp.where(lo8, -1.0, 1.0)

    def dup(z):
        zs = _swap(z)
        return jnp.where(lo8, z, zs), jnp.where(lo8, zs, z)

    def scan(scr, sw_scr, d, reverse):
        a_re, a_im = dup(decay[d])
        a_sg = a_im * sgn8
        a_r = [a_re[k:k + 1, :] for k in range(gpb)]
        a_i = [a_sg[k:k + 1, :] for k in range(gpb)]

        def rows_of(i):
            c = (cpv - 1 - i) if reverse else i
            return pl.ds(pl.multiple_of(c * n_virt, n_virt), n_virt)

        def body(i, carry):
            st, sw = carry
            rows = rows_of(i)
            new_st, new_sw = [], []
            for k in range(gpb):
                loc = scr[k, rows, :]
                loc_sw = sw_scr[k, rows, :]
                scr[k, rows, :] = st[k]
                new_st.append(a_r[k] * st[k] + a_i[k] * sw[k] + loc)
                new_sw.append(a_r[k] * sw[k] - a_i[k] * st[k] + loc_sw)
            return tuple(new_st), tuple(new_sw)

        st0 = tuple(s0_ref[d, :, k * LANES:(k + 1) * LANES] for k in range(gpb))
        sw0 = tuple(_swap(s) for s in st0)
        fin, _ = lax.fori_loop(0, cpv, body, (st0, sw0))
        if n_seg == 1:
            for k in range(gpb):
                sfin_ref[d, :, k * LANES:(k + 1) * LANES] = fin[k]
            return

        p = decay[d]
        for _ in range(cpv.bit_length() - 1):
            p_re, p_im = dup(p)
            p = _cmul(p, p_re, p_im * sgn8)
        v_re, v_im = dup(p)
        v_sg = v_im * sgn8
        seg = lax.broadcasted_iota(jnp.int32, (n_virt, LANES), 0) & (n_seg - 1)
        has_pred = seg != ((n_seg - 1) if reverse else 0)
        shift = (n_virt - 1) if reverse else 1
        cin = []
        for k in range(gpb):
            ck = jnp.zeros((n_virt, LANES), F32)
            for _ in range(n_seg - 1):
                nxt = fin[k] + ck * v_re[k:k + 1, :] + _swap(ck) * v_sg[k:k + 1, :]
                ck = jnp.where(has_pred, pltpu.roll(nxt, shift, 0), 0.0)
            cin.append(ck)
        cin_sw = [_swap(x) for x in cin]

        def fix(i, carry):
            q_re, q_im = carry
            rows = rows_of(i)
            q_sg = q_im * sgn8
            for k in range(gpb):
                scr[k, rows, :] += cin[k] * q_re[k:k + 1, :] + cin_sw[k] * q_sg[k:k + 1, :]
            return q_re * a_re - q_im * a_im, q_re * a_im + q_im * a_re

        lax.fori_loop(0, cpv, fix, (jnp.ones((gpb, LANES), F32), jnp.zeros((gpb, LANES), F32)))

    scan(sf_scr, swf_scr, 0, False)
    scan(sb_scr, swb_scr, 1, True)

    s_f = jnp.concatenate([sf_scr[k].astype(BF16) for k in range(gpb)], axis=1)
    s_b = jnp.concatenate([sb_scr[k].astype(BF16) for k in range(gpb)], axis=1)
    yall = _dot(xcat, k_scr[...]) + _dot_nt(s_f, e_scr[0]) + _dot_nt(s_b, e_scr[1])
    dsk = dsk_ref[...]
    for t in range(CHUNK):
        rows = pl.ds(t, nc, stride=CHUNK)
        y_ref[rows, :] = yall[:, t * LANES:(t + 1) * LANES] + h_ref[rows, :] * dsk


def _s5_params(lam_re, lam_im, log_dt, b_re, b_im, c_re, c_im):
    lamr = jnp.concatenate([lam_re, lam_re], axis=-1).astype(F32)
    lami = jnp.concatenate([lam_im, lam_im], axis=-1).astype(F32)
    ldt = jnp.broadcast_to(log_dt.astype(F32)[..., None], lamr.shape)
    bt = jnp.concatenate([b_re.transpose(0, 1, 3, 2), b_im.transpose(0, 1, 3, 2)], axis=-1)
    cp = jnp.concatenate([c_re, c_im], axis=-1)
    return (lamr, lami, ldt, bt.reshape(2, D_MODEL, LANES).astype(F32), cp.reshape(2, D_MODEL, LANES).astype(F32))


def _s5(h, params, d_skip, s0, n_seg):
    lamr, lami, ldt, bt, cp = params
    _, cpv, n_virt, _, _ = h.shape
    assert cpv & (cpv - 1) == 0 and n_virt % SUBLANES == 0 and n_seg & (n_seg - 1) == 0
    nc = cpv * n_virt
    ntok = nc * CHUNK
    hspec = pl.BlockSpec((None, ntok, LANES), lambda g: (g, 0, 0))
    kdim = CHUNK * LANES
    gspec = pl.BlockSpec((2, GROUPS_PER_BLOCK, LANES), lambda g: (0, g, 0))
    rspec = pl.BlockSpec((2, LANES, LANES), lambda g: (0, g, 0))
    sspec = pl.BlockSpec((None, 2, n_virt, STATE_LANES), lambda g: (g, 0, 0, 0))
    state_scr = pltpu.VMEM((GROUPS_PER_BLOCK, nc, LANES), F32)
    out_specs = [hspec]
    out_shape = [jax.ShapeDtypeStruct((N_GROUP_BLOCKS, ntok, LANES), F32)]
    if n_seg == 1:
        out_specs.append(sspec)
        out_shape.append(jax.ShapeDtypeStruct((N_GROUP_BLOCKS, 2, n_virt, STATE_LANES), F32))
    outs = pl.pallas_call(
        functools.partial(_s5_kernel, n_virt, n_seg),
        grid=(N_GROUP_BLOCKS,),
        in_specs=[
            hspec,
            gspec, gspec, gspec, rspec, rspec,
            pl.BlockSpec((1, LANES), lambda g: (0, g)),
            sspec,
        ],
        out_specs=out_specs,
        out_shape=out_shape,
        scratch_shapes=[
            pltpu.VMEM((2, kdim, STATE_LANES), BF16),
            pltpu.VMEM((2, kdim, STATE_LANES), BF16),
            pltpu.VMEM((kdim, kdim), BF16),
            state_scr, state_scr, state_scr, state_scr,
        ],
        compiler_params=_cparams(("arbitrary",)),
        name="s5_chunked_scan",
    )(h.reshape(N_GROUP_BLOCKS, ntok, LANES), lamr, lami, ldt, bt, cp, d_skip, s0)
    y = outs[0].reshape(h.shape)
    return (y, outs[1]) if n_seg == 1 else (y, None)


def _state_to_blocks(s):
    b = s.shape[0]
    s = s.reshape(b, 2, 2, N_GROUP_BLOCKS, GROUPS_PER_BLOCK, STATE_DIM)
    return s.transpose(3, 1, 0, 4, 2, 5).reshape(N_GROUP_BLOCKS, 2, b, STATE_LANES)


def _blocks_to_state(s):
    b = s.shape[2]
    s = s.reshape(N_GROUP_BLOCKS, 2, b, GROUPS_PER_BLOCK, 2, STATE_DIM)
    return s.transpose(2, 1, 4, 0, 3, 5).reshape(b, 2, 2, N_GROUPS, STATE_DIM)


def _rope_tables(n_tokens):
    pos = np.arange(n_tokens)
    n_freq = HEAD_DIM // 4
    freqs = ROPE_BASE ** (-np.arange(n_freq, dtype=np.float64) / n_freq)
    ang_r = (pos // GRID_W)[:, None] * freqs
    ang_c = (pos % GRID_W)[:, None] * freqs
    cos_h = np.concatenate([np.cos(ang_r), np.cos(ang_r), np.cos(ang_c), np.cos(ang_c)], axis=1)
    sin_h = np.concatenate([-np.sin(ang_r), np.sin(ang_r), -np.sin(ang_c), np.sin(ang_c)], axis=1)
    cos_t = np.concatenate([np.ones((TOKEN_TILE, LANES)), np.tile(cos_h, (1, 2))], axis=0)
    sin_t = np.concatenate([np.zeros((TOKEN_TILE, LANES)), np.tile(sin_h, (1, 2))], axis=0)
    return jnp.asarray(cos_t, F32), jnp.asarray(sin_t, F32)


def kernel(x_prompt, x_sample, cache_k, cache_v, state_ssm, c, c_ctx, norm1_g, norm2_g, w_mod, b_mod,
           w_qkv, w_o, attn_sink, ssm_lam_re, ssm_lam_im, ssm_log_dt, ssm_b_re, ssm_b_im, ssm_c_re,
           ssm_c_im, ssm_d, glu_w_a, glu_w_b, mlp_w1, mlp_w2, final_norm_g):
    bp, lp, _ = x_prompt.shape
    bx, lx, _ = x_sample.shape
    assert lx % TOKEN_TILE == 0 and (bp * lp) % TOKEN_TILE == 0
    tiles_per_lat = lx // TOKEN_TILE

    xp = x_prompt.reshape(bp * lp, D_MODEL)
    xx = x_sample.reshape(bx * lx, D_MODEL)

    cvecs = jnp.zeros((8, D_MODEL), F32).at[0].set(c_ctx).at[1:1 + bx].set(c)
    mod = _modulation(cvecs, w_mod, b_mod)

    ctx_row = lambda i: 0
    lat_row = lambda i: 1 + i // tiles_per_lat
    ctx_rope = lambda i: 0
    lat_rope = lambda i: 1 + i % tiles_per_lat

    cos_t, sin_t = _rope_tables(lx)
    wqkv = w_qkv[0].astype(BF16)
    g1 = norm1_g[0].reshape(1, D_MODEL)
    sink = attn_sink[0].astype(F32)
    qp, krp, vrp, kp, vp = _qkv(xp, mod[0], ctx_row, g1, wqkv, cos_t, sin_t, ctx_rope, True)
    qx, krx, vrx = _qkv(xx, mod[0], lat_row, g1, wqkv, cos_t, sin_t, lat_rope, False)
    op = _ctx_attention(sink, qp, krp, vrp, bp, lp)
    rep = lambda t: jnp.tile(t[:, 0].transpose(0, 2, 1, 3), (1, 1, 1, Q_PER_KV)).astype(BF16)
    ox = _lat_attention(sink, qx, krx, vrx, rep(cache_k), rep(cache_v), bx, lx)

    wo = w_o[0].astype(BF16)
    w1 = mlp_w1.astype(BF16)
    w2 = mlp_w2.astype(BF16)
    g2 = norm2_g.reshape(-1, 1, D_MODEL)
    gn = norm1_g[1].reshape(1, D_MODEL)
    vpt_p = TOKEN_TILE // lp
    vpt_x = 1
    n_seg_x = tiles_per_lat
    xp, hp = _post(xp, op, mod[0], ctx_row, g2[0], wo, None, w1[0], w2[0], vpt_p, mod_next=mod[1], g_next=gn)
    xx, hx = _post(xx, ox, mod[0], lat_row, g2[0], wo, None, w1[0], w2[0], vpt_x, mod_next=mod[1], g_next=gn)

    params = _s5_params(ssm_lam_re[0], ssm_lam_im[0], ssm_log_dt[0], ssm_b_re[0], ssm_b_im[0],
                        ssm_c_re[0], ssm_c_im[0])
    dsk = ssm_d[0].astype(F32).reshape(1, D_MODEL)
    s0p = jnp.zeros((N_GROUP_BLOCKS, 2, bp, STATE_LANES), F32)
    sx = _state_to_blocks(state_ssm[:, 0].astype(F32))
    s0x = jnp.zeros((N_GROUP_BLOCKS, 2, bx, n_seg_x, STATE_LANES), F32)
    s0x = s0x.at[:, 0, :, 0].set(sx[:, 0]).at[:, 1, :, n_seg_x - 1].set(sx[:, 1])
    s0x = s0x.reshape(N_GROUP_BLOCKS, 2, bx * n_seg_x, STATE_LANES)
    yp, sfin = _s5(hp, params, dsk, s0p, 1)
    yx, _ = _s5(hx, params, dsk, s0x, n_seg_x)
    new_state = _blocks_to_state(sfin)[:, None]

    wa = glu_w_a[0].astype(BF16)
    wb = glu_w_b[0].astype(BF16)
    fg = final_norm_g.reshape(1, D_MODEL)
    (yp_out,) = _post(xp, yp, mod[1], ctx_row, g2[1], wa, wb, w1[1], w2[1], vpt_p, final_g=fg)
    (yx_out,) = _post(xx, yx, mod[1], lat_row, g2[1], wa, wb, w1[1], w2[1], vpt_x, final_g=fg)

    new_k = kp.reshape(bp, 1, lp, N_KV_HEADS, HEAD_DIM)
    new_v = vp.reshape(bp, 1, lp, N_KV_HEADS, HEAD_DIM)
    return (yp_out.reshape(bp, lp, D_MODEL), yx_out.reshape(bx, lx, D_MODEL), new_k, new_v, new_state)
```

```python
import functools
import math

import numpy as np
import jax
import jax.numpy as jnp
from jax import lax
from jax.experimental import pallas as pl
from jax.experimental.pallas import tpu as pltpu

F32 = jnp.float32
BF16 = jnp.bfloat16

D_MODEL = 1024
N_HEADS = 16
N_KV_HEADS = 4
HEAD_DIM = 64
Q_PER_KV = N_HEADS // N_KV_HEADS
KV_DIM = N_KV_HEADS * HEAD_DIM
QKV_DIM = D_MODEL + 2 * KV_DIM
BLOCK = 128
GRID_W = 64
ROPE_BASE = 10000.0
ATTN_SCALE = HEAD_DIM ** -0.5
N_GROUPS = 64
GROUP_CH = 16
STATE_DIM = 64
D_FF = 4 * D_MODEL
N_MOD = 6
RMS_EPS = 1e-6
NEG_INF = -1e30

LANES = 128
SUBLANES = 8
GROUPS_PER_BLOCK = LANES // GROUP_CH
N_GROUP_BLOCKS = N_GROUPS // GROUPS_PER_BLOCK
STATE_LANES = GROUPS_PER_BLOCK * 2 * STATE_DIM
CHUNK = SUBLANES
TOKEN_TILE = 512
FF_TILE = 1024
POST_SUBTILES = 2
VMEM_LIMIT = 56 * 1024 * 1024


def _cparams(semantics):
    return pltpu.CompilerParams(dimension_semantics=semantics, vmem_limit_bytes=VMEM_LIMIT)


def _rms(x):
    return x * lax.rsqrt(jnp.mean(x * x, axis=-1, keepdims=True) + RMS_EPS)


def _dot(a, b):
    return jnp.dot(a, b, preferred_element_type=F32)


def _dot_nt(a, b):
    return lax.dot_general(a, b, (((1,), (1,)), ((), ())), preferred_element_type=F32)


def _mod_kernel(cv_ref, w_ref, b_ref, o_ref):
    cv = cv_ref[...]
    s = (cv * jax.nn.sigmoid(cv)).astype(BF16)
    o_ref[0] = _dot(s, w_ref[0].astype(BF16)) + b_ref[0]


def _modulation(cvecs, w_mod, b_mod):
    depth = w_mod.shape[0]
    out = pl.pallas_call(
        _mod_kernel,
        grid=(depth, N_MOD),
        in_specs=[
            pl.BlockSpec((8, D_MODEL), lambda l, j: (0, 0)),
            pl.BlockSpec((1, D_MODEL, D_MODEL), lambda l, j: (l, 0, j)),
            pl.BlockSpec((1, 1, D_MODEL), lambda l, j: (l, 0, j)),
        ],
        out_specs=pl.BlockSpec((1, 8, D_MODEL), lambda l, j: (l, 0, j)),
        out_shape=jax.ShapeDtypeStruct((depth, 8, N_MOD * D_MODEL), F32),
        compiler_params=_cparams(("arbitrary", "arbitrary")),
        name="modulation",
    )(cvecs, w_mod, b_mod.reshape(depth, 1, N_MOD * D_MODEL))
    return out.reshape(depth, 8, N_MOD, D_MODEL)


def _head_replicated(blk, odd):
    lo = lax.broadcasted_iota(jnp.int32, blk.shape, 1) < HEAD_DIM
    other = pltpu.roll(blk, HEAD_DIM, 1)
    dup = (jnp.where(lo, other, blk) if odd else jnp.where(lo, blk, other)).astype(BF16)
    return jnp.concatenate([dup, dup], axis=1)


def _qkv_kernel(emit_kv, x_ref, mod_ref, g_ref, w_ref, cos_ref, sin_ref, q_ref, krep_ref, vrep_ref, *kv_refs):
    h = _rms(x_ref[...]) * g_ref[...] * (1.0 + mod_ref[1:2, :]) + mod_ref[0:1, :]
    qkv = _dot(h.astype(BF16), w_ref[...])
    cos = cos_ref[...]
    sin = sin_ref[...]
    lane = lax.broadcasted_iota(jnp.int32, cos.shape, 1)
    first = (lane & 31) < 16
    n_rot = (D_MODEL + KV_DIM) // LANES
    for blk in range(n_rot):
        t = qkv[:, blk * LANES:(blk + 1) * LANES]
        partner = jnp.where(first, pltpu.roll(t, LANES - 16, 1), pltpu.roll(t, 16, 1))
        r = t * cos + partner * sin
        if blk < D_MODEL // LANES:
            q_ref[:, blk * LANES:(blk + 1) * LANES] = (r * ATTN_SCALE).astype(BF16)
        else:
            c0 = blk * LANES - D_MODEL
            if emit_kv:
                kv_refs[0][:, c0:c0 + LANES] = r
            for half in range(2):
                krep_ref[c0 // HEAD_DIM + half] = _head_replicated(r, half)
    v = qkv[:, D_MODEL + KV_DIM:]
    if emit_kv:
        kv_refs[1][...] = v
    for kv in range(N_KV_HEADS):
        blk = v[:, (kv // 2) * LANES:(kv // 2 + 1) * LANES]
        vrep_ref[kv] = _head_replicated(blk, kv % 2)


def _qkv(x, mod, mod_row, g, w_qkv, cos_t, sin_t, rope_blk, emit_kv):
    ntok = x.shape[0]
    nt = ntok // TOKEN_TILE
    rep_spec = pl.BlockSpec((N_KV_HEADS, TOKEN_TILE, KV_DIM), lambda i: (0, i, 0))
    kv_spec = pl.BlockSpec((TOKEN_TILE, KV_DIM), lambda i: (i, 0))
    rep_shape = jax.ShapeDtypeStruct((N_KV_HEADS, ntok, KV_DIM), BF16)
    kv_shape = jax.ShapeDtypeStruct((ntok, KV_DIM), F32)
    return pl.pallas_call(
        functools.partial(_qkv_kernel, emit_kv),
        grid=(nt,),
        in_specs=[
            pl.BlockSpec((TOKEN_TILE, D_MODEL), lambda i: (i, 0)),
            pl.BlockSpec((None, N_MOD, D_MODEL), lambda i: (mod_row(i), 0, 0)),
            pl.BlockSpec((1, D_MODEL), lambda i: (0, 0)),
            pl.BlockSpec((D_MODEL, QKV_DIM), lambda i: (0, 0)),
            pl.BlockSpec((TOKEN_TILE, LANES), lambda i: (rope_blk(i), 0)),
            pl.BlockSpec((TOKEN_TILE, LANES), lambda i: (rope_blk(i), 0)),
        ],
        out_specs=[pl.BlockSpec((TOKEN_TILE, D_MODEL), lambda i: (i, 0)), rep_spec, rep_spec]
        + ([kv_spec, kv_spec] if emit_kv else []),
        out_shape=[jax.ShapeDtypeStruct((ntok, D_MODEL), BF16), rep_shape, rep_shape]
        + ([kv_shape, kv_shape] if emit_kv else []),
        compiler_params=_cparams(("arbitrary",)),
        name="norm_qkv_rope",
    )(x, mod, g, w_qkv, cos_t, sin_t)


def _group_scores(q_ref, kv, key_parts, bias):
    q_kv = q_ref[:, kv * KV_DIM:(kv + 1) * KV_DIM]
    nq = q_kv.shape[0]
    slot = lax.broadcasted_iota(jnp.int32, q_kv.shape, 1) >> 6
    q4 = jnp.concatenate(
        [jnp.where(slot == g, q_kv, jnp.zeros((), BF16)) for g in range(Q_PER_KV)], axis=0)
    parts = [_dot_nt(q4, keys) for keys in key_parts]
    if bias is not None:
        s0 = parts[0].reshape(Q_PER_KV, nq, -1) + bias[None]
        parts[0] = s0.reshape(Q_PER_KV * nq, -1)
    return parts


def _group_softmax(parts, sink_ref, kv):
    nq = parts[0].shape[0] // Q_PER_KV
    sink = jnp.concatenate(
        [jnp.full((nq, 1), sink_ref[kv * Q_PER_KV + g], F32) for g in range(Q_PER_KV)], axis=0)
    m = sink
    for s in parts:
        m = jnp.maximum(jnp.max(s, axis=-1, keepdims=True), m)
    den = jnp.exp(sink - m)
    probs = []
    for s in parts:
        p = jnp.exp(s - m)
        den = den + jnp.sum(p, axis=-1, keepdims=True)
        probs.append(p.astype(BF16))
    return probs, 1.0 / den


def _group_output(probs, inv_den, value_parts, o_ref, kv):
    nq = probs[0].shape[0] // Q_PER_KV
    r = _dot(probs[0], value_parts[0])
    for p, v in zip(probs[1:], value_parts[1:]):
        r = r + _dot(p, v)
    r = r * inv_den
    slot = lax.broadcasted_iota(jnp.int32, (nq, KV_DIM), 1) >> 6
    o = r[:nq]
    for g in range(1, Q_PER_KV):
        o = jnp.where(slot == g, r[g * nq:(g + 1) * nq], o)
    o_ref[:, kv * KV_DIM:(kv + 1) * KV_DIM] = o.astype(BF16)


def _attend(q_ref, sink_ref, o_ref, keys_of, values_of, bias):
    s_next = _group_scores(q_ref, 0, keys_of(0), bias)
    for kv in range(N_KV_HEADS):
        s = s_next
        if kv + 1 < N_KV_HEADS:
            s_next = _group_scores(q_ref, kv + 1, keys_of(kv + 1), bias)
        probs, inv_den = _group_softmax(s, sink_ref, kv)
        _group_output(probs, inv_den, values_of(kv), o_ref, kv)


def _ctx_attn_kernel(sink_ref, q_ref, k_ref, v_ref, o_ref):
    _attend(q_ref, sink_ref, o_ref, lambda kv: [k_ref[kv]], lambda kv: [v_ref[kv]], None)


def _ctx_attention(sink, q, krep, vrep, n_batch, seq):
    rep_spec = pl.BlockSpec((N_KV_HEADS, seq, KV_DIM), lambda b: (0, b, 0))
    return pl.pallas_call(
        _ctx_attn_kernel,
        grid=(n_batch,),
        in_specs=[
            pl.BlockSpec(memory_space=pltpu.SMEM),
            pl.BlockSpec((seq, D_MODEL), lambda b: (b, 0)),
            rep_spec, rep_spec,
        ],
        out_specs=pl.BlockSpec((seq, D_MODEL), lambda b: (b, 0)),
        out_shape=jax.ShapeDtypeStruct((n_batch * seq, D_MODEL), BF16),
        compiler_params=_cparams(("arbitrary",)),
        name="context_attention",
    )(sink, q, krep, vrep)


def _window_start(n, seq):
    return jnp.clip((n - 1) * BLOCK, 0, seq - 3 * BLOCK)


def _band_bias():
    r = np.arange(BLOCK)[:, None]
    j = np.arange(3 * BLOCK)[None, :]
    out = [np.where(np.abs(j - d * BLOCK - r) <= BLOCK, 0.0, NEG_INF) for d in range(3)]
    return jnp.asarray(np.stack(out), F32)


def _lat_attn_kernel(seq, sink_ref, q_ref, k_ref, v_ref, ck_ref, cv_ref, bias_ref, o_ref):
    win = 3 * BLOCK
    start = pl.multiple_of(_window_start(pl.program_id(1), seq), BLOCK)
    keys_of = lambda kv: [k_ref[kv, pl.ds(start, win), :], ck_ref[kv]]
    values_of = lambda kv: [v_ref[kv, pl.ds(start, win), :], cv_ref[kv]]
    _attend(q_ref, sink_ref, o_ref, keys_of, values_of, bias_ref[...])


def _lat_attention(sink, q, krep, vrep, ckrep, cvrep, n_batch, seq):
    nb = seq // BLOCK
    past = ckrep.shape[2]
    rep_spec = pl.BlockSpec((N_KV_HEADS, seq, KV_DIM), lambda b, n: (0, b, 0))
    crep_spec = pl.BlockSpec((None, N_KV_HEADS, past, KV_DIM), lambda b, n: (b, 0, 0, 0))
    return pl.pallas_call(
        functools.partial(_lat_attn_kernel, seq),
        grid=(n_batch, nb),
        in_specs=[
            pl.BlockSpec(memory_space=pltpu.SMEM),
            pl.BlockSpec((BLOCK, D_MODEL), lambda b, n: (b * nb + n, 0)),
            rep_spec, rep_spec, crep_spec, crep_spec,
            pl.BlockSpec((None, BLOCK, 3 * BLOCK), lambda b, n: (n - _window_start(n, seq) // BLOCK, 0, 0)),
        ],
        out_specs=pl.BlockSpec((BLOCK, D_MODEL), lambda b, n: (b * nb + n, 0)),
        out_shape=jax.ShapeDtypeStruct((n_batch * seq, D_MODEL), BF16),
        compiler_params=_cparams(("arbitrary", "arbitrary")),
        name="latent_attention",
    )(sink, q, krep, vrep, ckrep, cvrep, _band_bias())


def _gelu_tanh(x):
    c = math.sqrt(2.0 / math.pi)
    return x * (0.5 * (1.0 + jnp.tanh(c * (x + 0.044715 * (x * x * x)))))


def _post_kernel(is_attn, emit_next, final, vpt, *refs):
    rows_per_v = TOKEN_TILE // vpt
    refs = list(refs)
    x_ref, mix_ref, mod_ref, g2_ref, wa_ref = refs[:5]
    refs = refs[5:]
    wb_ref = None if is_attn else refs.pop(0)
    w1_ref, w2_ref = refs[:2]
    refs = refs[2:]
    modn_ref = gn_ref = fg_ref = hn_ref = None
    if emit_next:
        modn_ref, gn_ref = refs[:2]
        refs = refs[2:]
    if final:
        fg_ref = refs.pop(0)
    xo_ref = refs.pop(0)
    if emit_next:
        hn_ref = refs.pop(0)
    assert not refs

    sub = TOKEN_TILE // POST_SUBTILES
    assert rows_per_v % sub == 0
    n_chunk = sub // CHUNK
    starts = [k * sub for k in range(POST_SUBTILES)]

    def s5_slot(r0):
        return r0 // rows_per_v, (r0 % rows_per_v) // CHUNK

    def project(r0):
        if is_attn:
            return _dot(mix_ref[r0:r0 + sub, :], wa_ref[...])
        s, c0 = s5_slot(r0)
        y = jnp.concatenate(
            [mix_ref[g, c0:c0 + n_chunk, s].reshape(sub, LANES) for g in range(N_GROUP_BLOCKS)], axis=1)
        yg = _gelu_tanh(y).astype(BF16)
        return _dot(yg, wa_ref[...]) * jax.nn.sigmoid(_dot(yg, wb_ref[...]))

    def prologue(r0, mix):
        x1 = x_ref[r0:r0 + sub, :] + mod_ref[2:3, :] * mix
        h2 = _rms(x1) * g2_ref[...] * (1.0 + mod_ref[4:5, :]) + mod_ref[3:4, :]
        return x1, h2.astype(BF16)

    def mlp(h2):
        acc = None
        for c in range(D_FF // FF_TILE):
            a = jnp.maximum(_dot(h2, w1_ref[:, c * FF_TILE:(c + 1) * FF_TILE]), 0.0)
            t = _dot((a * a).astype(BF16), w2_ref[c * FF_TILE:(c + 1) * FF_TILE, :])
            acc = t if acc is None else acc + t
        return acc

    def epilogue(r0, x1, acc):
        x2 = x1 + mod_ref[5:6, :] * acc
        if emit_next:
            hn = _rms(x2) * gn_ref[...] * (1.0 + modn_ref[1:2, :]) + modn_ref[0:1, :]
            s, c0 = s5_slot(r0)
            for g in range(N_GROUP_BLOCKS):
                blk = hn[:, g * LANES:(g + 1) * LANES]
                hn_ref[g, c0:c0 + n_chunk, s] = blk.reshape(n_chunk, CHUNK, LANES)
        xo_ref[r0:r0 + sub, :] = _rms(x2) * fg_ref[...] if final else x2

    mixes = [project(r0) for r0 in starts]
    pro = [prologue(r0, mix) for r0, mix in zip(starts, mixes)]
    accs = [mlp(h2) for _, h2 in pro]
    for r0, (x1, _), acc in zip(starts, pro, accs):
        epilogue(r0, x1, acc)


def _post(x, mix, mod, mod_row, g2, w_a, w_b, w1, w2, vpt, mod_next=None, g_next=None, final_g=None):
    is_attn = w_b is None
    emit_next = mod_next is not None
    final = final_g is not None
    ntok = x.shape[0]
    nt = ntok // TOKEN_TILE
    cpv = TOKEN_TILE // (vpt * CHUNK)
    n_virt = nt * vpt
    tile = pl.BlockSpec((TOKEN_TILE, D_MODEL), lambda i: (i, 0))
    row = pl.BlockSpec((1, D_MODEL), lambda i: (0, 0))
    modspec = pl.BlockSpec((None, N_MOD, D_MODEL), lambda i: (mod_row(i), 0, 0))
    resident = lambda shape: pl.BlockSpec(shape, lambda i: (0, 0), pipeline_mode=pl.Buffered(1))
    wsq = resident((D_MODEL, D_MODEL))
    gtile = pl.BlockSpec((N_GROUP_BLOCKS, cpv, vpt, CHUNK, LANES), lambda i: (0, 0, i, 0, 0))
    in_specs = [tile, tile if is_attn else gtile, modspec, row, wsq]
    args = [x, mix, mod, g2, w_a]
    if not is_attn:
        in_specs.append(wsq)
        args.append(w_b)
    in_specs += [resident((D_MODEL, D_FF)), resident((D_FF, D_MODEL))]
    args += [w1, w2]
    if emit_next:
        in_specs += [modspec, row]
        args += [mod_next, g_next]
    if final:
        in_specs.append(row)
        args.append(final_g)
    out_specs = [tile]
    out_shape = [jax.ShapeDtypeStruct((ntok, D_MODEL), F32)]
    if emit_next:
        out_specs.append(gtile)
        out_shape.append(jax.ShapeDtypeStruct((N_GROUP_BLOCKS, cpv, n_virt, CHUNK, LANES), F32))
    return pl.pallas_call(
        functools.partial(_post_kernel, is_attn, emit_next, final, vpt),
        grid=(nt,),
        in_specs=in_specs,
        out_specs=out_specs,
        out_shape=out_shape,
        compiler_params=_cparams(("arbitrary",)),
        name="attn_proj_mlp" if is_attn else "glu_mlp_final",
    )(*args)


def _swap(x):
    return pltpu.roll(x, LANES // 2, 1)


def _cmul(z, w_r, w_i):
    return z * w_r + _swap(z) * w_i


def _multiplier(z, lo):
    zs = _swap(z)
    return jnp.where(lo, z, zs), jnp.where(lo, -zs, z)


def _rep_rows(x):
    return jnp.concatenate(
        [jnp.broadcast_to(x[g:g + 1, :], (GROUP_CH, LANES)) for g in range(GROUPS_PER_BLOCK)], axis=0)


def _s5_kernel(n_virt, n_seg, h_ref, lamr_ref, lami_ref, ldt_ref, bt_ref, cp_ref, dsk_ref, s0_ref, *refs):
    if n_seg == 1:
        y_ref, sfin_ref = refs[:2]
        refs = refs[2:]
    else:
        y_ref, sfin_ref = refs[0], None
        refs = refs[1:]
    f_scr, e_scr, k_scr, sf_scr, sb_scr, swf_scr, swb_scr = refs
    ntok = h_ref.shape[0]
    nc = ntok // CHUNK
    cpv = nc // n_virt
    gpb = GROUPS_PER_BLOCK

    lo8 = lax.broadcasted_iota(jnp.int32, (gpb, LANES), 1) < STATE_DIM
    lo = lax.broadcasted_iota(jnp.int32, (LANES, LANES), 1) < STATE_DIM
    conj = jnp.where(lo, 1.0, -1.0)
    row_g = lax.broadcasted_iota(jnp.int32, (LANES, STATE_LANES), 0) >> 4
    col_g = lax.broadcasted_iota(jnp.int32, (LANES, STATE_LANES), 1) >> 7
    diag_wide = row_g == col_g
    diag = (lax.broadcasted_iota(jnp.int32, (LANES, LANES), 0) >> 4) == (
        lax.broadcasted_iota(jnp.int32, (LANES, LANES), 1) >> 4)

    def expand(w):
        return jnp.where(diag_wide, jnp.concatenate([w] * gpb, axis=1), jnp.zeros((), BF16))

    decay = []
    lag = []
    for d in range(2):
        lam_r = lamr_ref[d]
        lam_i = lami_ref[d]
        dt = jnp.exp(ldt_ref[d])
        mag = jnp.exp(lam_r * dt)
        ang = lam_i * dt
        a_r = mag * jnp.cos(ang)
        a_im = mag * jnp.sin(ang)
        a_i = jnp.where(lo8, -a_im, a_im)
        den = lam_r * lam_r + lam_i * lam_i
        num = jnp.where(lo8, a_r - 1.0, a_im)
        f = _cmul(num, lam_r / den, jnp.where(lo8, lam_i, -lam_i) / den)
        pw = [jnp.where(lo8, 1.0, 0.0)]
        for _ in range(CHUNK):
            pw.append(_cmul(pw[-1], a_r, a_i))
        decay.append(pw[CHUNK])
        pw = [_rep_rows(p) for p in pw]
        f_r, f_i = _multiplier(_rep_rows(f), lo)
        bb_r, bb_i = _multiplier(_cmul(bt_ref[d], f_r, f_i), lo)
        c_r, c_i = _multiplier(cp_ref[d], lo)
        cm = (cp_ref[d] * conj).astype(BF16)
        fpow = [_cmul(p, bb_r, bb_i).astype(BF16) for p in pw[:CHUNK]]
        for j in range(CHUNK):
            e = (CHUNK - 1 - j) if d == 0 else j
            f_scr[d, j * LANES:(j + 1) * LANES, :] = expand(fpow[e])
        for t in range(CHUNK):
            e = (t + 1) if d == 0 else (CHUNK - t)
            w = _cmul(pw[e], c_r, c_i) * conj
            e_scr[d, t * LANES:(t + 1) * LANES, :] = expand(w.astype(BF16))
        lag.append([jnp.where(diag, _dot_nt(fp, cm), 0.0) for fp in fpow])

    for j in range(CHUNK):
        for t in range(CHUNK):
            k = t - j
            tile = lag[0][k] if k > 0 else (lag[1][-k] if k < 0 else lag[0][0] + lag[1][0])
            k_scr[j * LANES:(j + 1) * LANES, t * LANES:(t + 1) * LANES] = tile.astype(BF16)

    xcat = jnp.concatenate(
        [h_ref[pl.ds(j, nc, stride=CHUNK), :].astype(BF16) for j in range(CHUNK)], axis=1)

    for d, scr, sw_scr in ((0, sf_scr, swf_scr), (1, sb_scr, swb_scr)):
        loc_all = _dot(xcat, f_scr[d])
        for k in range(gpb):
            loc = loc_all[:, k * LANES:(k + 1) * LANES]
            scr[k] = loc
            sw_scr[k] = _swap(loc)

    sgn8 = jnp.where(lo8, -1.0, 1.0)

    def dup(z):
        zs = _swap(z)
        return jnp.where(lo8, z, zs), jnp.where(lo8, zs, z)

    def scan(scr, sw_scr, d, reverse):
        a_re, a_im = dup(decay[d])
        a_sg = a_im * sgn8
        a_r = [a_re[k:k + 1, :] for k in range(gpb)]
        a_i = [a_sg[k:k + 1, :] for k in range(gpb)]

        def rows_of(i):
            c = (cpv - 1 - i) if reverse else i
            return pl.ds(pl.multiple_of(c * n_virt, n_virt), n_virt)

        def body(i, carry):
            st, sw = carry
            rows = rows_of(i)
            new_st, new_sw = [], []
            for k in range(gpb):
                loc = scr[k, rows, :]
                loc_sw = sw_scr[k, rows, :]
                scr[k, rows, :] = st[k]
                new_st.append(a_r[k] * st[k] + a_i[k] * sw[k] + loc)
                new_sw.append(a_r[k] * sw[k] - a_i[k] * st[k] + loc_sw)
            return tuple(new_st), tuple(new_sw)

        st0 = tuple(s0_ref[d, :, k * LANES:(k + 1) * LANES] for k in range(gpb))
        sw0 = tuple(_swap(s) for s in st0)
        fin, _ = lax.fori_loop(0, cpv, body, (st0, sw0))
        if n_seg == 1:
            for k in range(gpb):
                sfin_ref[d, :, k * LANES:(k + 1) * LANES] = fin[k]
            return

        p = decay[d]
        for _ in range(cpv.bit_length() - 1):
            p_re, p_im = dup(p)
            p = _cmul(p, p_re, p_im * sgn8)
        v_re, v_im = dup(p)
        v_sg = v_im * sgn8
        seg = lax.broadcasted_iota(jnp.int32, (n_virt, LANES), 0) & (n_seg - 1)
        has_pred = seg != ((n_seg - 1) if reverse else 0)
        shift = (n_virt - 1) if reverse else 1
        cin = []
        for k in range(gpb):
            ck = jnp.zeros((n_virt, LANES), F32)
            for _ in range(n_seg - 1):
                nxt = fin[k] + ck * v_re[k:k + 1, :] + _swap(ck) * v_sg[k:k + 1, :]
                ck = jnp.where(has_pred, pltpu.roll(nxt, shift, 0), 0.0)
            cin.append(ck)
        cin_sw = [_swap(x) for x in cin]

        def fix(i, carry):
            q_re, q_im = carry
            rows = rows_of(i)
            q_sg = q_im * sgn8
            for k in range(gpb):
                scr[k, rows, :] += cin[k] * q_re[k:k + 1, :] + cin_sw[k] * q_sg[k:k + 1, :]
            return q_re * a_re - q_im * a_im, q_re * a_im + q_im * a_re

        lax.fori_loop(0, cpv, fix, (jnp.ones((gpb, LANES), F32), jnp.zeros((gpb, LANES), F32)))

    scan(sf_scr, swf_scr, 0, False)
    scan(sb_scr, swb_scr, 1, True)

    s_f = jnp.concatenate([sf_scr[k].astype(BF16) for k in range(gpb)], axis=1)
    s_b = jnp.concatenate([sb_scr[k].astype(BF16) for k in range(gpb)], axis=1)
    yall = _dot(xcat, k_scr[...]) + _dot_nt(s_f, e_scr[0]) + _dot_nt(s_b, e_scr[1])
    dsk = dsk_ref[...]
    for t in range(CHUNK):
        rows = pl.ds(t, nc, stride=CHUNK)
        y_ref[rows, :] = yall[:, t * LANES:(t + 1) * LANES] + h_ref[rows, :] * dsk


def _s5_params(lam_re, lam_im, log_dt, b_re, b_im, c_re, c_im):
    lamr = jnp.concatenate([lam_re, lam_re], axis=-1).astype(F32)
    lami = jnp.concatenate([lam_im, lam_im], axis=-1).astype(F32)
    ldt = jnp.broadcast_to(log_dt.astype(F32)[..., None], lamr.shape)
    bt = jnp.concatenate([b_re.transpose(0, 1, 3, 2), b_im.transpose(0, 1, 3, 2)], axis=-1)
    cp = jnp.concatenate([c_re, c_im], axis=-1)
    return (lamr, lami, ldt, bt.reshape(2, D_MODEL, LANES).astype(F32), cp.reshape(2, D_MODEL, LANES).astype(F32))


def _s5(h, params, d_skip, s0, n_seg):
    lamr, lami, ldt, bt, cp = params
    _, cpv, n_virt, _, _ = h.shape
    assert cpv & (cpv - 1) == 0 and n_virt % SUBLANES == 0 and n_seg & (n_seg - 1) == 0
    nc = cpv * n_virt
    ntok = nc * CHUNK
    hspec = pl.BlockSpec((None, ntok, LANES), lambda g: (g, 0, 0))
    kdim = CHUNK * LANES
    gspec = pl.BlockSpec((2, GROUPS_PER_BLOCK, LANES), lambda g: (0, g, 0))
    rspec = pl.BlockSpec((2, LANES, LANES), lambda g: (0, g, 0))
    sspec = pl.BlockSpec((None, 2, n_virt, STATE_LANES), lambda g: (g, 0, 0, 0))
    state_scr = pltpu.VMEM((GROUPS_PER_BLOCK, nc, LANES), F32)
    out_specs = [hspec]
    out_shape = [jax.ShapeDtypeStruct((N_GROUP_BLOCKS, ntok, LANES), F32)]
    if n_seg == 1:
        out_specs.append(sspec)
        out_shape.append(jax.ShapeDtypeStruct((N_GROUP_BLOCKS, 2, n_virt, STATE_LANES), F32))
    outs = pl.pallas_call(
        functools.partial(_s5_kernel, n_virt, n_seg),
        grid=(N_GROUP_BLOCKS,),
        in_specs=[
            hspec,
            gspec, gspec, gspec, rspec, rspec,
            pl.BlockSpec((1, LANES), lambda g: (0, g)),
            sspec,
        ],
        out_specs=out_specs,
        out_shape=out_shape,
        scratch_shapes=[
            pltpu.VMEM((2, kdim, STATE_LANES), BF16),
            pltpu.VMEM((2, kdim, STATE_LANES), BF16),
            pltpu.VMEM((kdim, kdim), BF16),
            state_scr, state_scr, state_scr, state_scr,
        ],
        compiler_params=_cparams(("arbitrary",)),
        name="s5_chunked_scan",
    )(h.reshape(N_GROUP_BLOCKS, ntok, LANES), lamr, lami, ldt, bt, cp, d_skip, s0)
    y = outs[0].reshape(h.shape)
    return (y, outs[1]) if n_seg == 1 else (y, None)


def _state_to_blocks(s):
    b = s.shape[0]
    s = s.reshape(b, 2, 2, N_GROUP_BLOCKS, GROUPS_PER_BLOCK, STATE_DIM)
    return s.transpose(3, 1, 0, 4, 2, 5).reshape(N_GROUP_BLOCKS, 2, b, STATE_LANES)


def _blocks_to_state(s):
    b = s.shape[2]
    s = s.reshape(N_GROUP_BLOCKS, 2, b, GROUPS_PER_BLOCK, 2, STATE_DIM)
    return s.transpose(2, 1, 4, 0, 3, 5).reshape(b, 2, 2, N_GROUPS, STATE_DIM)


def _rope_tables(n_tokens):
    pos = np.arange(n_tokens)
    n_freq = HEAD_DIM // 4
    freqs = ROPE_BASE ** (-np.arange(n_freq, dtype=np.float64) / n_freq)
    ang_r = (pos // GRID_W)[:, None] * freqs
    ang_c = (pos % GRID_W)[:, None] * freqs
    cos_h = np.concatenate([np.cos(ang_r), np.cos(ang_r), np.cos(ang_c), np.cos(ang_c)], axis=1)
    sin_h = np.concatenate([-np.sin(ang_r), np.sin(ang_r), -np.sin(ang_c), np.sin(ang_c)], axis=1)
    cos_t = np.concatenate([np.ones((TOKEN_TILE, LANES)), np.tile(cos_h, (1, 2))], axis=0)
    sin_t = np.concatenate([np.zeros((TOKEN_TILE, LANES)), np.tile(sin_h, (1, 2))], axis=0)
    return jnp.asarray(cos_t, F32), jnp.asarray(sin_t, F32)


def kernel(x_prompt, x_sample, cache_k, cache_v, state_ssm, c, c_ctx, norm1_g, norm2_g, w_mod, b_mod,
           w_qkv, w_o, attn_sink, ssm_lam_re, ssm_lam_im, ssm_log_dt, ssm_b_re, ssm_b_im, ssm_c_re,
           ssm_c_im, ssm_d, glu_w_a, glu_w_b, mlp_w1, mlp_w2, final_norm_g):
    bp, lp, _ = x_prompt.shape
    bx, lx, _ = x_sample.shape
    assert lx % TOKEN_TILE == 0 and (bp * lp) % TOKEN_TILE == 0
    tiles_per_lat = lx // TOKEN_TILE

    xp = x_prompt.reshape(bp * lp, D_MODEL)
    xx = x_sample.reshape(bx * lx, D_MODEL)

    cvecs = jnp.zeros((8, D_MODEL), F32).at[0].set(c_ctx).at[1:1 + bx].set(c)
    mod = _modulation(cvecs, w_mod, b_mod)

    ctx_row = lambda i: 0
    lat_row = lambda i: 1 + i // tiles_per_lat
    ctx_rope = lambda i: 0
    lat_rope = lambda i: 1 + i % tiles_per_lat

    cos_t, sin_t = _rope_tables(lx)
    wqkv = w_qkv[0].astype(BF16)
    g1 = norm1_g[0].reshape(1, D_MODEL)
    sink = attn_sink[0].astype(F32)
    qp, krp, vrp, kp, vp = _qkv(xp, mod[0], ctx_row, g1, wqkv, cos_t, sin_t, ctx_rope, True)
    qx, krx, vrx = _qkv(xx, mod[0], lat_row, g1, wqkv, cos_t, sin_t, lat_rope, False)
    op = _ctx_attention(sink, qp, krp, vrp, bp, lp)
    rep = lambda t: jnp.tile(t[:, 0].transpose(0, 2, 1, 3), (1, 1, 1, Q_PER_KV)).astype(BF16)
    ox = _lat_attention(sink, qx, krx, vrx, rep(cache_k), rep(cache_v), bx, lx)

    wo = w_o[0].astype(BF16)
    w1 = mlp_w1.astype(BF16)
    w2 = mlp_w2.astype(BF16)
    g2 = norm2_g.reshape(-1, 1, D_MODEL)
    gn = norm1_g[1].reshape(1, D_MODEL)
    vpt_p = TOKEN_TILE // lp
    vpt_x = 1
    n_seg_x = tiles_per_lat
    xp, hp = _post(xp, op, mod[0], ctx_row, g2[0], wo, None, w1[0], w2[0], vpt_p, mod_next=mod[1], g_next=gn)
    xx, hx = _post(xx, ox, mod[0], lat_row, g2[0], wo, None, w1[0], w2[0], vpt_x, mod_next=mod[1], g_next=gn)

    params = _s5_params(ssm_lam_re[0], ssm_lam_im[0], ssm_log_dt[0], ssm_b_re[0], ssm_b_im[0],
                        ssm_c_re[0], ssm_c_im[0])
    dsk = ssm_d[0].astype(F32).reshape(1, D_MODEL)
    s0p = jnp.zeros((N_GROUP_BLOCKS, 2, bp, STATE_LANES), F32)
    sx = _state_to_blocks(state_ssm[:, 0].astype(F32))
    s0x = jnp.zeros((N_GROUP_BLOCKS, 2, bx, n_seg_x, STATE_LANES), F32)
    s0x = s0x.at[:, 0, :, 0].set(sx[:, 0]).at[:, 1, :, n_seg_x - 1].set(sx[:, 1])
    s0x = s0x.reshape(N_GROUP_BLOCKS, 2, bx * n_seg_x, STATE_LANES)
    yp, sfin = _s5(hp, params, dsk, s0p, 1)
    yx, _ = _s5(hx, params, dsk, s0x, n_seg_x)
    new_state = _blocks_to_state(sfin)[:, None]

    wa = glu_w_a[0].astype(BF16)
    wb = glu_w_b[0].astype(BF16)
    fg = final_norm_g.reshape(1, D_MODEL)
    (yp_out,) = _post(xp, yp, mod[1], ctx_row, g2[1], wa, wb, w1[1], w2[1], vpt_p, final_g=fg)
    (yx_out,) = _post(xx, yx, mod[1], lat_row, g2[1], wa, wb, w1[1], w2[1], vpt_x, final_g=fg)

    new_k = kp.reshape(bp, 1, lp, N_KV_HEADS, HEAD_DIM)
    new_v = vp.reshape(bp, 1, lp, N_KV_HEADS, HEAD_DIM)
    return (yp_out.reshape(bp, lp, D_MODEL), yx_out.reshape(bx, lx, D_MODEL), new_k, new_v, new_state)
```

```python
import functools
import math

import numpy as np
import jax
import jax.numpy as jnp
from jax import lax
from jax.experimental import pallas as pl
from jax.experimental.pallas import tpu as pltpu

F32 = jnp.float32
BF16 = jnp.bfloat16

D_MODEL = 1024
N_HEADS = 16
N_KV_HEADS = 4
HEAD_DIM = 64
Q_PER_KV = N_HEADS // N_KV_HEADS
KV_DIM = N_KV_HEADS * HEAD_DIM
QKV_DIM = D_MODEL + 2 * KV_DIM
BLOCK = 128
GRID_W = 64
ROPE_BASE = 10000.0
ATTN_SCALE = HEAD_DIM ** -0.5
N_GROUPS = 64
GROUP_CH = 16
STATE_DIM = 64
D_FF = 4 * D_MODEL
N_MOD = 6
RMS_EPS = 1e-6
NEG_INF = -1e30

LANES = 128
SUBLANES = 8
GROUPS_PER_BLOCK = LANES // GROUP_CH
N_GROUP_BLOCKS = N_GROUPS // GROUPS_PER_BLOCK
STATE_LANES = GROUPS_PER_BLOCK * 2 * STATE_DIM
CHUNK = SUBLANES
TOKEN_TILE = 512
FF_TILE = 1024
POST_SUBTILES = 2
VMEM_LIMIT = 56 * 1024 * 1024


def _cparams(semantics):
    return pltpu.CompilerParams(dimension_semantics=semantics, vmem_limit_bytes=VMEM_LIMIT)


def _rms(x):
    return x * lax.rsqrt(jnp.mean(x * x, axis=-1, keepdims=True) + RMS_EPS)


def _dot(a, b):
    return jnp.dot(a, b, preferred_element_type=F32)


def _dot_nt(a, b):
    return lax.dot_general(a, b, (((1,), (1,)), ((), ())), preferred_element_type=F32)


def _mod_kernel(cv_ref, w_ref, b_ref, o_ref):
    cv = cv_ref[...]
    s = (cv * jax.nn.sigmoid(cv)).astype(BF16)
    o_ref[0] = _dot(s, w_ref[0].astype(BF16)) + b_ref[0]


def _modulation(cvecs, w_mod, b_mod):
    depth = w_mod.shape[0]
    out = pl.pallas_call(
        _mod_kernel,
        grid=(depth, N_MOD),
        in_specs=[
            pl.BlockSpec((8, D_MODEL), lambda l, j: (0, 0)),
            pl.BlockSpec((1, D_MODEL, D_MODEL), lambda l, j: (l, 0, j)),
            pl.BlockSpec((1, 1, D_MODEL), lambda l, j: (l, 0, j)),
        ],
        out_specs=pl.BlockSpec((1, 8, D_MODEL), lambda l, j: (l, 0, j)),
        out_shape=jax.ShapeDtypeStruct((depth, 8, N_MOD * D_MODEL), F32),
        compiler_params=_cparams(("arbitrary", "arbitrary")),
        name="modulation",
    )(cvecs, w_mod, b_mod.reshape(depth, 1, N_MOD * D_MODEL))
    return out.reshape(depth, 8, N_MOD, D_MODEL)


def _head_replicated(blk, odd):
    lo = lax.broadcasted_iota(jnp.int32, blk.shape, 1) < HEAD_DIM
    other = pltpu.roll(blk, HEAD_DIM, 1)
    dup = (jnp.where(lo, other, blk) if odd else jnp.where(lo, blk, other)).astype(BF16)
    return jnp.concatenate([dup, dup], axis=1)


def _qkv_kernel(emit_kv, x_ref, mod_ref, g_ref, w_ref, cos_ref, sin_ref, q_ref, krep_ref, vrep_ref, *kv_refs):
    h = _rms(x_ref[...]) * g_ref[...] * (1.0 + mod_ref[1:2, :]) + mod_ref[0:1, :]
    qkv = _dot(h.astype(BF16), w_ref[...])
    cos = cos_ref[...]
    sin = sin_ref[...]
    lane = lax.broadcasted_iota(jnp.int32, cos.shape, 1)
    first = (lane & 31) < 16
    n_rot = (D_MODEL + KV_DIM) // LANES
    for blk in range(n_rot):
        t = qkv[:, blk * LANES:(blk + 1) * LANES]
        partner = jnp.where(first, pltpu.roll(t, LANES - 16, 1), pltpu.roll(t, 16, 1))
        r = t * cos + partner * sin
        if blk < D_MODEL // LANES:
            q_ref[:, blk * LANES:(blk + 1) * LANES] = (r * ATTN_SCALE).astype(BF16)
        else:
            c0 = blk * LANES - D_MODEL
            if emit_kv:
                kv_refs[0][:, c0:c0 + LANES] = r
            for half in range(2):
                krep_ref[c0 // HEAD_DIM + half] = _head_replicated(r, half)
    v = qkv[:, D_MODEL + KV_DIM:]
    if emit_kv:
        kv_refs[1][...] = v
    for kv in range(N_KV_HEADS):
        blk = v[:, (kv // 2) * LANES:(kv // 2 + 1) * LANES]
        vrep_ref[kv] = _head_replicated(blk, kv % 2)


def _qkv(x, mod, mod_row, g, w_qkv, cos_t, sin_t, rope_blk, emit_kv):
    ntok = x.shape[0]
    nt = ntok // TOKEN_TILE
    rep_spec = pl.BlockSpec((N_KV_HEADS, TOKEN_TILE, KV_DIM), lambda i: (0, i, 0))
    kv_spec = pl.BlockSpec((TOKEN_TILE, KV_DIM), lambda i: (i, 0))
    rep_shape = jax.ShapeDtypeStruct((N_KV_HEADS, ntok, KV_DIM), BF16)
    kv_shape = jax.ShapeDtypeStruct((ntok, KV_DIM), F32)
    return pl.pallas_call(
        functools.partial(_qkv_kernel, emit_kv),
        grid=(nt,),
        in_specs=[
            pl.BlockSpec((TOKEN_TILE, D_MODEL), lambda i: (i, 0)),
            pl.BlockSpec((None, N_MOD, D_MODEL), lambda i: (mod_row(i), 0, 0)),
            pl.BlockSpec((1, D_MODEL), lambda i: (0, 0)),
            pl.BlockSpec((D_MODEL, QKV_DIM), lambda i: (0, 0)),
            pl.BlockSpec((TOKEN_TILE, LANES), lambda i: (rope_blk(i), 0)),
            pl.BlockSpec((TOKEN_TILE, LANES), lambda i: (rope_blk(i), 0)),
        ],
        out_specs=[pl.BlockSpec((TOKEN_TILE, D_MODEL), lambda i: (i, 0)), rep_spec, rep_spec]
        + ([kv_spec, kv_spec] if emit_kv else []),
        out_shape=[jax.ShapeDtypeStruct((ntok, D_MODEL), BF16), rep_shape, rep_shape]
        + ([kv_shape, kv_shape] if emit_kv else []),
        compiler_params=_cparams(("arbitrary",)),
        name="norm_qkv_rope",
    )(x, mod, g, w_qkv, cos_t, sin_t)


def _group_scores(q_ref, kv, key_parts, bias):
    q_kv = q_ref[:, kv * KV_DIM:(kv + 1) * KV_DIM]
    nq = q_kv.shape[0]
    slot = lax.broadcasted_iota(jnp.int32, q_kv.shape, 1) >> 6
    q4 = jnp.concatenate(
        [jnp.where(slot == g, q_kv, jnp.zeros((), BF16)) for g in range(Q_PER_KV)], axis=0)
    parts = [_dot_nt(q4, keys) for keys in key_parts]
    if bias is not None:
        s0 = parts[0].reshape(Q_PER_KV, nq, -1) + bias[None]
        parts[0] = s0.reshape(Q_PER_KV * nq, -1)
    return parts


def _group_softmax(parts, sink_ref, kv):
    nq = parts[0].shape[0] // Q_PER_KV
    sink = jnp.concatenate(
        [jnp.full((nq, 1), sink_ref[kv * Q_PER_KV + g], F32) for g in range(Q_PER_KV)], axis=0)
    m = sink
    for s in parts:
        m = jnp.maximum(jnp.max(s, axis=-1, keepdims=True), m)
    den = jnp.exp(sink - m)
    probs = []
    for s in parts:
        p = jnp.exp(s - m)
        den = den + jnp.sum(p, axis=-1, keepdims=True)
        probs.append(p.astype(BF16))
    return probs, 1.0 / den


def _group_output(probs, inv_den, value_parts, o_ref, kv):
    nq = probs[0].shape[0] // Q_PER_KV
    r = _dot(probs[0], value_parts[0])
    for p, v in zip(probs[1:], value_parts[1:]):
        r = r + _dot(p, v)
    r = r * inv_den
    slot = lax.broadcasted_iota(jnp.int32, (nq, KV_DIM), 1) >> 6
    o = r[:nq]
    for g in range(1, Q_PER_KV):
        o = jnp.where(slot == g, r[g * nq:(g + 1) * nq], o)
    o_ref[:, kv * KV_DIM:(kv + 1) * KV_DIM] = o.astype(BF16)


def _attend(q_ref, sink_ref, o_ref, keys_of, values_of, bias):
    s_next = _group_scores(q_ref, 0, keys_of(0), bias)
    for kv in range(N_KV_HEADS):
        s = s_next
        if kv + 1 < N_KV_HEADS:
            s_next = _group_scores(q_ref, kv + 1, keys_of(kv + 1), bias)
        probs, inv_den = _group_softmax(s, sink_ref, kv)
        _group_output(probs, inv_den, values_of(kv), o_ref, kv)


def _ctx_attn_kernel(sink_ref, q_ref, k_ref, v_ref, o_ref):
    _attend(q_ref, sink_ref, o_ref, lambda kv: [k_ref[kv]], lambda kv: [v_ref[kv]], None)


def _ctx_attention(sink, q, krep, vrep, n_batch, seq):
    rep_spec = pl.BlockSpec((N_KV_HEADS, seq, KV_DIM), lambda b: (0, b, 0))
    return pl.pallas_call(
        _ctx_attn_kernel,
        grid=(n_batch,),
        in_specs=[
            pl.BlockSpec(memory_space=pltpu.SMEM),
            pl.BlockSpec((seq, D_MODEL), lambda b: (b, 0)),
            rep_spec, rep_spec,
        ],
        out_specs=pl.BlockSpec((seq, D_MODEL), lambda b: (b, 0)),
        out_shape=jax.ShapeDtypeStruct((n_batch * seq, D_MODEL), BF16),
        compiler_params=_cparams(("arbitrary",)),
        name="context_attention",
    )(sink, q, krep, vrep)


def _window_start(n, seq):
    return jnp.clip((n - 1) * BLOCK, 0, seq - 3 * BLOCK)


def _band_bias():
    r = np.arange(BLOCK)[:, None]
    j = np.arange(3 * BLOCK)[None, :]
    out = [np.where(np.abs(j - d * BLOCK - r) <= BLOCK, 0.0, NEG_INF) for d in range(3)]
    return jnp.asarray(np.stack(out), F32)


def _lat_attn_kernel(seq, sink_ref, q_ref, k_ref, v_ref, ck_ref, cv_ref, bias_ref, o_ref):
    win = 3 * BLOCK
    start = pl.multiple_of(_window_start(pl.program_id(1), seq), BLOCK)
    keys_of = lambda kv: [k_ref[kv, pl.ds(start, win), :], ck_ref[kv]]
    values_of = lambda kv: [v_ref[kv, pl.ds(start, win), :], cv_ref[kv]]
    _attend(q_ref, sink_ref, o_ref, keys_of, values_of, bias_ref[...])


def _lat_attention(sink, q, krep, vrep, ckrep, cvrep, n_batch, seq):
    nb = seq // BLOCK
    past = ckrep.shape[2]
    rep_spec = pl.BlockSpec((N_KV_HEADS, seq, KV_DIM), lambda b, n: (0, b, 0))
    crep_spec = pl.BlockSpec((None, N_KV_HEADS, past, KV_DIM), lambda b, n: (b, 0, 0, 0))
    return pl.pallas_call(
        functools.partial(_lat_attn_kernel, seq),
        grid=(n_batch, nb),
        in_specs=[
            pl.BlockSpec(memory_space=pltpu.SMEM),
            pl.BlockSpec((BLOCK, D_MODEL), lambda b, n: (b * nb + n, 0)),
            rep_spec, rep_spec, crep_spec, crep_spec,
            pl.BlockSpec((None, BLOCK, 3 * BLOCK), lambda b, n: (n - _window_start(n, seq) // BLOCK, 0, 0)),
        ],
        out_specs=pl.BlockSpec((BLOCK, D_MODEL), lambda b, n: (b * nb + n, 0)),
        out_shape=jax.ShapeDtypeStruct((n_batch * seq, D_MODEL), BF16),
        compiler_params=_cparams(("arbitrary", "arbitrary")),
        name="latent_attention",
    )(sink, q, krep, vrep, ckrep, cvrep, _band_bias())


def _gelu_tanh(x):
    c = math.sqrt(2.0 / math.pi)
    return x * (0.5 * (1.0 + jnp.tanh(c * (x + 0.044715 * (x * x * x)))))


def _post_kernel(is_attn, emit_next, final, vpt, *refs):
    rows_per_v = TOKEN_TILE // vpt
    refs = list(refs)
    x_ref, mix_ref, mod_ref, g2_ref, wa_ref = refs[:5]
    refs = refs[5:]
    wb_ref = None if is_attn else refs.pop(0)
    w1_ref, w2_ref = refs[:2]
    refs = refs[2:]
    modn_ref = gn_ref = fg_ref = hn_ref = None
    if emit_next:
        modn_ref, gn_ref = refs[:2]
        refs = refs[2:]
    if final:
        fg_ref = refs.pop(0)
    xo_ref = refs.pop(0)
    if emit_next:
        hn_ref = refs.pop(0)
    assert not refs

    sub = TOKEN_TILE // POST_SUBTILES
    assert rows_per_v % sub == 0
    n_chunk = sub // CHUNK
    starts = [k * sub for k in range(POST_SUBTILES)]

    def s5_slot(r0):
        return r0 // rows_per_v, (r0 % rows_per_v) // CHUNK

    def project(r0):
        if is_attn:
            return _dot(mix_ref[r0:r0 + sub, :], wa_ref[...])
        s, c0 = s5_slot(r0)
        y = jnp.concatenate(
            [mix_ref[g, c0:c0 + n_chunk, s].reshape(sub, LANES) for g in range(N_GROUP_BLOCKS)], axis=1)
        yg = _gelu_tanh(y).astype(BF16)
        return _dot(yg, wa_ref[...]) * jax.nn.sigmoid(_dot(yg, wb_ref[...]))

    def prologue(r0, mix):
        x1 = x_ref[r0:r0 + sub, :] + mod_ref[2:3, :] * mix
        h2 = _rms(x1) * g2_ref[...] * (1.0 + mod_ref[4:5, :]) + mod_ref[3:4, :]
        return x1, h2.astype(BF16)

    def mlp(h2):
        acc = None
        for c in range(D_FF // FF_TILE):
            a = jnp.maximum(_dot(h2, w1_ref[:, c * FF_TILE:(c + 1) * FF_TILE]), 0.0)
            t = _dot((a * a).astype(BF16), w2_ref[c * FF_TILE:(c + 1) * FF_TILE, :])
            acc = t if acc is None else acc + t
        return acc

    def epilogue(r0, x1, acc):
        x2 = x1 + mod_ref[5:6, :] * acc
        if emit_next:
            hn = _rms(x2) * gn_ref[...] * (1.0 + modn_ref[1:2, :]) + modn_ref[0:1, :]
            s, c0 = s5_slot(r0)
            for g in range(N_GROUP_BLOCKS):
                blk = hn[:, g * LANES:(g + 1) * LANES]
                hn_ref[g, c0:c0 + n_chunk, s] = blk.reshape(n_chunk, CHUNK, LANES)
        xo_ref[r0:r0 + sub, :] = _rms(x2) * fg_ref[...] if final else x2

    mixes = [project(r0) for r0 in starts]
    pro = [prologue(r0, mix) for r0, mix in zip(starts, mixes)]
    accs = [mlp(h2) for _, h2 in pro]
    for r0, (x1, _), acc in zip(starts, pro, accs):
        epilogue(r0, x1, acc)


def _post(x, mix, mod, mod_row, g2, w_a, w_b, w1, w2, vpt, mod_next=None, g_next=None, final_g=None):
    is_attn = w_b is None
    emit_next = mod_next is not None
    final = final_g is not None
    ntok = x.shape[0]
    nt = ntok // TOKEN_TILE
    cpv = TOKEN_TILE // (vpt * CHUNK)
    n_virt = nt * vpt
    tile = pl.BlockSpec((TOKEN_TILE, D_MODEL), lambda i: (i, 0))
    row = pl.BlockSpec((1, D_MODEL), lambda i: (0, 0))
    modspec = pl.BlockSpec((None, N_MOD, D_MODEL), lambda i: (mod_row(i), 0, 0))
    resident = lambda shape: pl.BlockSpec(shape, lambda i: (0, 0), pipeline_mode=pl.Buffered(1))
    wsq = resident((D_MODEL, D_MODEL))
    gtile = pl.BlockSpec((N_GROUP_BLOCKS, cpv, vpt, CHUNK, LANES), lambda i: (0, 0, i, 0, 0))
    in_specs = [tile, tile if is_attn else gtile, modspec, row, wsq]
    args = [x, mix, mod, g2, w_a]
    if not is_attn:
        in_specs.append(wsq)
        args.append(w_b)
    in_specs += [resident((D_MODEL, D_FF)), resident((D_FF, D_MODEL))]
    args += [w1, w2]
    if emit_next:
        in_specs += [modspec, row]
        args += [mod_next, g_next]
    if final:
        in_specs.append(row)
        args.append(final_g)
    out_specs = [tile]
    out_shape = [jax.ShapeDtypeStruct((ntok, D_MODEL), F32)]
    if emit_next:
        out_specs.append(gtile)
        out_shape.append(jax.ShapeDtypeStruct((N_GROUP_BLOCKS, cpv, n_virt, CHUNK, LANES), F32))
    return pl.pallas_call(
        functools.partial(_post_kernel, is_attn, emit_next, final, vpt),
        grid=(nt,),
        in_specs=in_specs,
        out_specs=out_specs,
        out_shape=out_shape,
        compiler_params=_cparams(("arbitrary",)),
        name="attn_proj_mlp" if is_attn else "glu_mlp_final",
    )(*args)


def _swap(x):
    return pltpu.roll(x, LANES // 2, 1)


def _cmul(z, w_r, w_i):
    return z * w_r + _swap(z) * w_i


def _multiplier(z, lo):
    zs = _swap(z)
    return jnp.where(lo, z, zs), jnp.where(lo, -zs, z)


def _rep_rows(x):
    return jnp.concatenate(
        [jnp.broadcast_to(x[g:g + 1, :], (GROUP_CH, LANES)) for g in range(GROUPS_PER_BLOCK)], axis=0)


def _s5_kernel(n_virt, n_seg, h_ref, lamr_ref, lami_ref, ldt_ref, bt_ref, cp_ref, dsk_ref, s0_ref, *refs):
    if n_seg == 1:
        y_ref, sfin_ref = refs[:2]
        refs = refs[2:]
    else:
        y_ref, sfin_ref = refs[0], None
        refs = refs[1:]
    f_scr, e_scr, k_scr, sf_scr, sb_scr, swf_scr, swb_scr = refs
    ntok = h_ref.shape[0]
    nc = ntok // CHUNK
    cpv = nc // n_virt
    gpb = GROUPS_PER_BLOCK

    lo8 = lax.broadcasted_iota(jnp.int32, (gpb, LANES), 1) < STATE_DIM
    lo = lax.broadcasted_iota(jnp.int32, (LANES, LANES), 1) < STATE_DIM
    conj = jnp.where(lo, 1.0, -1.0)
    row_g = lax.broadcasted_iota(jnp.int32, (LANES, STATE_LANES), 0) >> 4
    col_g = lax.broadcasted_iota(jnp.int32, (LANES, STATE_LANES), 1) >> 7
    diag_wide = row_g == col_g
    diag = (lax.broadcasted_iota(jnp.int32, (LANES, LANES), 0) >> 4) == (
        lax.broadcasted_iota(jnp.int32, (LANES, LANES), 1) >> 4)

    def expand(w):
        return jnp.where(diag_wide, jnp.concatenate([w] * gpb, axis=1), jnp.zeros((), BF16))

    decay = []
    lag = []
    for d in range(2):
        lam_r = lamr_ref[d]
        lam_i = lami_ref[d]
        dt = jnp.exp(ldt_ref[d])
        mag = jnp.exp(lam_r * dt)
        ang = lam_i * dt
        a_r = mag * jnp.cos(ang)
        a_im = mag * jnp.sin(ang)
        a_i = jnp.where(lo8, -a_im, a_im)
        den = lam_r * lam_r + lam_i * lam_i
        num = jnp.where(lo8, a_r - 1.0, a_im)
        f = _cmul(num, lam_r / den, jnp.where(lo8, lam_i, -lam_i) / den)
        pw = [jnp.where(lo8, 1.0, 0.0)]
        for _ in range(CHUNK):
            pw.append(_cmul(pw[-1], a_r, a_i))
        decay.append(pw[CHUNK])
        pw = [_rep_rows(p) for p in pw]
        f_r, f_i = _multiplier(_rep_rows(f), lo)
        bb_r, bb_i = _multiplier(_cmul(bt_ref[d], f_r, f_i), lo)
        c_r, c_i = _multiplier(cp_ref[d], lo)
        cm = (cp_ref[d] * conj).astype(BF16)
        fpow = [_cmul(p, bb_r, bb_i).astype(BF16) for p in pw[:CHUNK]]
        for j in range(CHUNK):
            e = (CHUNK - 1 - j) if d == 0 else j
            f_scr[d, j * LANES:(j + 1) * LANES, :] = expand(fpow[e])
        for t in range(CHUNK):
            e = (t + 1) if d == 0 else (CHUNK - t)
            w = _cmul(pw[e], c_r, c_i) * conj
            e_scr[d, t * LANES:(t + 1) * LANES, :] = expand(w.astype(BF16))
        lag.append([jnp.where(diag, _dot_nt(fp, cm), 0.0) for fp in fpow])

    for j in range(CHUNK):
        for t in range(CHUNK):
            k = t - j
            tile = lag[0][k] if k > 0 else (lag[1][-k] if k < 0 else lag[0][0] + lag[1][0])
            k_scr[j * LANES:(j + 1) * LANES, t * LANES:(t + 1) * LANES] = tile.astype(BF16)

    xcat = jnp.concatenate(
        [h_ref[pl.ds(j, nc, stride=CHUNK), :].astype(BF16) for j in range(CHUNK)], axis=1)

    for d, scr, sw_scr in ((0, sf_scr, swf_scr), (1, sb_scr, swb_scr)):
        loc_all = _dot(xcat, f_scr[d])
        for k in range(gpb):
            loc = loc_all[:, k * LANES:(k + 1) * LANES]
            scr[k] = loc
            sw_scr[k] = _swap(loc)
    y_within = _dot(xcat, k_scr[...])

    sgn8 = jnp.where(lo8, -1.0, 1.0)

    def dup(z):
        zs = _swap(z)
        return jnp.where(lo8, z, zs), jnp.where(lo8, zs, z)

    def scan(scr, sw_scr, d, reverse):
        a_re, a_im = dup(decay[d])
        a_sg = a_im * sgn8
        a_r = [a_re[k:k + 1, :] for k in range(gpb)]
        a_i = [a_sg[k:k + 1, :] for k in range(gpb)]

        def rows_of(i):
            c = (cpv - 1 - i) if reverse else i
            return pl.ds(c * n_virt, n_virt)

        def body(i, carry):
            st, sw = carry
            rows = rows_of(i)
            new_st, new_sw = [], []
            for k in range(gpb):
                loc = scr[k, rows, :]
                loc_sw = sw_scr[k, rows, :]
                scr[k, rows, :] = st[k]
                new_st.append(a_r[k] * st[k] + a_i[k] * sw[k] + loc)
                new_sw.append(a_r[k] * sw[k] - a_i[k] * st[k] + loc_sw)
            return tuple(new_st), tuple(new_sw)

        st0 = tuple(s0_ref[d, :, k * LANES:(k + 1) * LANES] for k in range(gpb))
        sw0 = tuple(_swap(s) for s in st0)
        carry = (st0, sw0)
        for i in range(cpv):
            carry = body(i, carry)
        fin = carry[0]
        if n_seg == 1:
            for k in range(gpb):
                sfin_ref[d, :, k * LANES:(k + 1) * LANES] = fin[k]
            return

        p = decay[d]
        for _ in range(cpv.bit_length() - 1):
            p_re, p_im = dup(p)
            p = _cmul(p, p_re, p_im * sgn8)
        v_re, v_im = dup(p)
        v_sg = v_im * sgn8
        seg = lax.broadcasted_iota(jnp.int32, (n_virt, LANES), 0) & (n_seg - 1)
        has_pred = seg != ((n_seg - 1) if reverse else 0)
        shift = (n_virt - 1) if reverse else 1
        cin = []
        for k in range(gpb):
            ck = jnp.zeros((n_virt, LANES), F32)
            for _ in range(n_seg - 1):
                nxt = fin[k] + ck * v_re[k:k + 1, :] + _swap(ck) * v_sg[k:k + 1, :]
                ck = jnp.where(has_pred, pltpu.roll(nxt, shift, 0), 0.0)
            cin.append(ck)
        cin_sw = [_swap(x) for x in cin]

        def fix(i, carry):
            q_re, q_im = carry
            rows = rows_of(i)
            q_sg = q_im * sgn8
            for k in range(gpb):
                scr[k, rows, :] += cin[k] * q_re[k:k + 1, :] + cin_sw[k] * q_sg[k:k + 1, :]
            return q_re * a_re - q_im * a_im, q_re * a_im + q_im * a_re

        q = (jnp.ones((gpb, LANES), F32), jnp.zeros((gpb, LANES), F32))
        for i in range(cpv):
            q = fix(i, q)

    scan(sf_scr, swf_scr, 0, False)
    scan(sb_scr, swb_scr, 1, True)

    s_f = jnp.concatenate([sf_scr[k].astype(BF16) for k in range(gpb)], axis=1)
    s_b = jnp.concatenate([sb_scr[k].astype(BF16) for k in range(gpb)], axis=1)
    yall = y_within + _dot_nt(s_f, e_scr[0]) + _dot_nt(s_b, e_scr[1])
    dsk = dsk_ref[...]
    for t in range(CHUNK):
        rows = pl.ds(t, nc, stride=CHUNK)
        y_ref[rows, :] = yall[:, t * LANES:(t + 1) * LANES] + h_ref[rows, :] * dsk


def _s5_params(lam_re, lam_im, log_dt, b_re, b_im, c_re, c_im):
    lamr = jnp.concatenate([lam_re, lam_re], axis=-1).astype(F32)
    lami = jnp.concatenate([lam_im, lam_im], axis=-1).astype(F32)
    ldt = jnp.broadcast_to(log_dt.astype(F32)[..., None], lamr.shape)
    bt = jnp.concatenate([b_re.transpose(0, 1, 3, 2), b_im.transpose(0, 1, 3, 2)], axis=-1)
    cp = jnp.concatenate([c_re, c_im], axis=-1)
    return (lamr, lami, ldt, bt.reshape(2, D_MODEL, LANES).astype(F32), cp.reshape(2, D_MODEL, LANES).astype(F32))


def _s5(h, params, d_skip, s0, n_seg):
    lamr, lami, ldt, bt, cp = params
    _, cpv, n_virt, _, _ = h.shape
    assert cpv & (cpv - 1) == 0 and n_virt % SUBLANES == 0 and n_seg & (n_seg - 1) == 0
    nc = cpv * n_virt
    ntok = nc * CHUNK
    hspec = pl.BlockSpec((None, ntok, LANES), lambda g: (g, 0, 0))
    kdim = CHUNK * LANES
    gspec = pl.BlockSpec((2, GROUPS_PER_BLOCK, LANES), lambda g: (0, g, 0))
    rspec = pl.BlockSpec((2, LANES, LANES), lambda g: (0, g, 0))
    sspec = pl.BlockSpec((None, 2, n_virt, STATE_LANES), lambda g: (g, 0, 0, 0))
    state_scr = pltpu.VMEM((GROUPS_PER_BLOCK, nc, LANES), F32)
    out_specs = [hspec]
    out_shape = [jax.ShapeDtypeStruct((N_GROUP_BLOCKS, ntok, LANES), F32)]
    if n_seg == 1:
        out_specs.append(sspec)
        out_shape.append(jax.ShapeDtypeStruct((N_GROUP_BLOCKS, 2, n_virt, STATE_LANES), F32))
    outs = pl.pallas_call(
        functools.partial(_s5_kernel, n_virt, n_seg),
        grid=(N_GROUP_BLOCKS,),
        in_specs=[
            hspec,
            gspec, gspec, gspec, rspec, rspec,
            pl.BlockSpec((1, LANES), lambda g: (0, g)),
            sspec,
        ],
        out_specs=out_specs,
        out_shape=out_shape,
        scratch_shapes=[
            pltpu.VMEM((2, kdim, STATE_LANES), BF16),
            pltpu.VMEM((2, kdim, STATE_LANES), BF16),
            pltpu.VMEM((kdim, kdim), BF16),
            state_scr, state_scr, state_scr, state_scr,
        ],
        compiler_params=_cparams(("arbitrary",)),
        name="s5_chunked_scan",
    )(h.reshape(N_GROUP_BLOCKS, ntok, LANES), lamr, lami, ldt, bt, cp, d_skip, s0)
    y = outs[0].reshape(h.shape)
    return (y, outs[1]) if n_seg == 1 else (y, None)


def _state_to_blocks(s):
    b = s.shape[0]
    s = s.reshape(b, 2, 2, N_GROUP_BLOCKS, GROUPS_PER_BLOCK, STATE_DIM)
    return s.transpose(3, 1, 0, 4, 2, 5).reshape(N_GROUP_BLOCKS, 2, b, STATE_LANES)


def _blocks_to_state(s):
    b = s.shape[2]
    s = s.reshape(N_GROUP_BLOCKS, 2, b, GROUPS_PER_BLOCK, 2, STATE_DIM)
    return s.transpose(2, 1, 4, 0, 3, 5).reshape(b, 2, 2, N_GROUPS, STATE_DIM)


def _rope_tables(n_tokens):
    pos = np.arange(n_tokens)
    n_freq = HEAD_DIM // 4
    freqs = ROPE_BASE ** (-np.arange(n_freq, dtype=np.float64) / n_freq)
    ang_r = (pos // GRID_W)[:, None] * freqs
    ang_c = (pos % GRID_W)[:, None] * freqs
    cos_h = np.concatenate([np.cos(ang_r), np.cos(ang_r), np.cos(ang_c), np.cos(ang_c)], axis=1)
    sin_h = np.concatenate([-np.sin(ang_r), np.sin(ang_r), -np.sin(ang_c), np.sin(ang_c)], axis=1)
    cos_t = np.concatenate([np.ones((TOKEN_TILE, LANES)), np.tile(cos_h, (1, 2))], axis=0)
    sin_t = np.concatenate([np.zeros((TOKEN_TILE, LANES)), np.tile(sin_h, (1, 2))], axis=0)
    return jnp.asarray(cos_t, F32), jnp.asarray(sin_t, F32)


def kernel(x_prompt, x_sample, cache_k, cache_v, state_ssm, c, c_ctx, norm1_g, norm2_g, w_mod, b_mod,
           w_qkv, w_o, attn_sink, ssm_lam_re, ssm_lam_im, ssm_log_dt, ssm_b_re, ssm_b_im, ssm_c_re,
           ssm_c_im, ssm_d, glu_w_a, glu_w_b, mlp_w1, mlp_w2, final_norm_g):
    bp, lp, _ = x_prompt.shape
    bx, lx, _ = x_sample.shape
    assert lx % TOKEN_TILE == 0 and (bp * lp) % TOKEN_TILE == 0
    tiles_per_lat = lx // TOKEN_TILE

    xp = x_prompt.reshape(bp * lp, D_MODEL)
    xx = x_sample.reshape(bx * lx, D_MODEL)

    cvecs = jnp.zeros((8, D_MODEL), F32).at[0].set(c_ctx).at[1:1 + bx].set(c)
    mod = _modulation(cvecs, w_mod, b_mod)

    ctx_row = lambda i: 0
    lat_row = lambda i: 1 + i // tiles_per_lat
    ctx_rope = lambda i: 0
    lat_rope = lambda i: 1 + i % tiles_per_lat

    cos_t, sin_t = _rope_tables(lx)
    wqkv = w_qkv[0].astype(BF16)
    g1 = norm1_g[0].reshape(1, D_MODEL)
    sink = attn_sink[0].astype(F32)
    qp, krp, vrp, kp, vp = _qkv(xp, mod[0], ctx_row, g1, wqkv, cos_t, sin_t, ctx_rope, True)
    qx, krx, vrx = _qkv(xx, mod[0], lat_row, g1, wqkv, cos_t, sin_t, lat_rope, False)
    op = _ctx_attention(sink, qp, krp, vrp, bp, lp)
    rep = lambda t: jnp.tile(t[:, 0].transpose(0, 2, 1, 3), (1, 1, 1, Q_PER_KV)).astype(BF16)
    ox = _lat_attention(sink, qx, krx, vrx, rep(cache_k), rep(cache_v), bx, lx)

    wo = w_o[0].astype(BF16)
    w1 = [mlp_w1[i].astype(BF16) for i in range(mlp_w1.shape[0])]
    w2 = [mlp_w2[i].astype(BF16) for i in range(mlp_w2.shape[0])]
    g2 = norm2_g.reshape(-1, 1, D_MODEL)
    gn = norm1_g[1].reshape(1, D_MODEL)
    vpt_p = TOKEN_TILE // lp
    vpt_x = 1
    n_seg_x = tiles_per_lat
    xp, hp = _post(xp, op, mod[0], ctx_row, g2[0], wo, None, w1[0], w2[0], vpt_p, mod_next=mod[1], g_next=gn)
    xx, hx = _post(xx, ox, mod[0], lat_row, g2[0], wo, None, w1[0], w2[0], vpt_x, mod_next=mod[1], g_next=gn)

    params = _s5_params(ssm_lam_re[0], ssm_lam_im[0], ssm_log_dt[0], ssm_b_re[0], ssm_b_im[0],
                        ssm_c_re[0], ssm_c_im[0])
    dsk = ssm_d[0].astype(F32).reshape(1, D_MODEL)
    s0p = jnp.zeros((N_GROUP_BLOCKS, 2, bp, STATE_LANES), F32)
    sx = _state_to_blocks(state_ssm[:, 0].astype(F32))
    s0x = jnp.zeros((N_GROUP_BLOCKS, 2, bx, n_seg_x, STATE_LANES), F32)
    s0x = s0x.at[:, 0, :, 0].set(sx[:, 0]).at[:, 1, :, n_seg_x - 1].set(sx[:, 1])
    s0x = s0x.reshape(N_GROUP_BLOCKS, 2, bx * n_seg_x, STATE_LANES)
    yp, sfin = _s5(hp, params, dsk, s0p, 1)
    yx, _ = _s5(hx, params, dsk, s0x, n_seg_x)
    new_state = _blocks_to_state(sfin)[:, None]

    wa = glu_w_a[0].astype(BF16)
    wb = glu_w_b[0].astype(BF16)
    fg = final_norm_g.reshape(1, D_MODEL)
    (yp_out,) = _post(xp, yp, mod[1], ctx_row, g2[1], wa, wb, w1[1], w2[1], vpt_p, final_g=fg)
    (yx_out,) = _post(xx, yx, mod[1], lat_row, g2[1], wa, wb, w1[1], w2[1], vpt_x, final_g=fg)

    new_k = kp.reshape(bp, 1, lp, N_KV_HEADS, HEAD_DIM)
    new_v = vp.reshape(bp, 1, lp, N_KV_HEADS, HEAD_DIM)
    return (yp_out.reshape(bp, lp, D_MODEL), yx_out.reshape(bx, lx, D_MODEL), new_k, new_v, new_state)
```

```python
import functools
import math

import numpy as np
import jax
import jax.numpy as jnp
from jax import lax
from jax.experimental import pallas as pl
from jax.experimental.pallas import tpu as pltpu

F32 = jnp.float32
BF16 = jnp.bfloat16

D_MODEL = 1024
N_HEADS = 16
N_KV_HEADS = 4
HEAD_DIM = 64
Q_PER_KV = N_HEADS // N_KV_HEADS
KV_DIM = N_KV_HEADS * HEAD_DIM
QKV_DIM = D_MODEL + 2 * KV_DIM
BLOCK = 128
GRID_W = 64
ROPE_BASE = 10000.0
ATTN_SCALE = HEAD_DIM ** -0.5
N_GROUPS = 64
GROUP_CH = 16
STATE_DIM = 64
D_FF = 4 * D_MODEL
N_MOD = 6
RMS_EPS = 1e-6
NEG_INF = -1e30

LANES = 128
SUBLANES = 8
GROUPS_PER_BLOCK = LANES // GROUP_CH
N_GROUP_BLOCKS = N_GROUPS // GROUPS_PER_BLOCK
STATE_LANES = GROUPS_PER_BLOCK * 2 * STATE_DIM
CHUNK = SUBLANES
TOKEN_TILE = 512
FF_TILE = 1024
POST_SUBTILES = 2
VMEM_LIMIT = 56 * 1024 * 1024


def _cparams(semantics):
    return pltpu.CompilerParams(dimension_semantics=semantics, vmem_limit_bytes=VMEM_LIMIT)


def _rms(x):
    return x * lax.rsqrt(jnp.mean(x * x, axis=-1, keepdims=True) + RMS_EPS)


def _dot(a, b):
    return jnp.dot(a, b, preferred_element_type=F32)


def _dot_nt(a, b):
    return lax.dot_general(a, b, (((1,), (1,)), ((), ())), preferred_element_type=F32)


def _mod_kernel(cv_ref, w_ref, b_ref, o_ref):
    cv = cv_ref[...]
    s = (cv * jax.nn.sigmoid(cv)).astype(BF16)
    o_ref[0] = _dot(s, w_ref[0].astype(BF16)) + b_ref[0]


def _modulation(cvecs, w_mod, b_mod):
    depth = w_mod.shape[0]
    out = pl.pallas_call(
        _mod_kernel,
        grid=(depth, N_MOD),
        in_specs=[
            pl.BlockSpec((8, D_MODEL), lambda l, j: (0, 0)),
            pl.BlockSpec((1, D_MODEL, D_MODEL), lambda l, j: (l, 0, j)),
            pl.BlockSpec((1, 1, D_MODEL), lambda l, j: (l, 0, j)),
        ],
        out_specs=pl.BlockSpec((1, 8, D_MODEL), lambda l, j: (l, 0, j)),
        out_shape=jax.ShapeDtypeStruct((depth, 8, N_MOD * D_MODEL), F32),
        compiler_params=_cparams(("arbitrary", "arbitrary")),
        name="modulation",
    )(cvecs, w_mod, b_mod.reshape(depth, 1, N_MOD * D_MODEL))
    return out.reshape(depth, 8, N_MOD, D_MODEL)


def _head_replicated(blk, odd):
    lo = lax.broadcasted_iota(jnp.int32, blk.shape, 1) < HEAD_DIM
    other = pltpu.roll(blk, HEAD_DIM, 1)
    dup = (jnp.where(lo, other, blk) if odd else jnp.where(lo, blk, other)).astype(BF16)
    return jnp.concatenate([dup, dup], axis=1)


def _qkv_kernel(cache_seq, x_ref, mod_ref, g_ref, w_ref, cos_ref, sin_ref, q_ref, krep_ref, vrep_ref, *kv_refs):
    def emit_cache(ref, blk, c0):
        for s in range(TOKEN_TILE // cache_seq):
            ref[s, c0:c0 + LANES, :] = blk[s * cache_seq:(s + 1) * cache_seq, :].T

    h = _rms(x_ref[...]) * g_ref[...] * (1.0 + mod_ref[1:2, :]) + mod_ref[0:1, :]
    qkv = _dot(h.astype(BF16), w_ref[...])
    cos = cos_ref[...]
    sin = sin_ref[...]
    lane = lax.broadcasted_iota(jnp.int32, cos.shape, 1)
    first = (lane & 31) < 16
    n_rot = (D_MODEL + KV_DIM) // LANES
    for blk in range(n_rot):
        t = qkv[:, blk * LANES:(blk + 1) * LANES]
        partner = jnp.where(first, pltpu.roll(t, LANES - 16, 1), pltpu.roll(t, 16, 1))
        r = t * cos + partner * sin
        if blk < D_MODEL // LANES:
            q_ref[:, blk * LANES:(blk + 1) * LANES] = (r * ATTN_SCALE).astype(BF16)
        else:
            c0 = blk * LANES - D_MODEL
            if cache_seq:
                emit_cache(kv_refs[0], r, c0)
            for half in range(2):
                krep_ref[c0 // HEAD_DIM + half] = _head_replicated(r, half)
    v = qkv[:, D_MODEL + KV_DIM:]
    for c0 in range(0, KV_DIM, LANES):
        blk = v[:, c0:c0 + LANES]
        if cache_seq:
            emit_cache(kv_refs[1], blk, c0)
        for half in range(2):
            vrep_ref[c0 // HEAD_DIM + half] = _head_replicated(blk, half)


def _qkv(x, mod, mod_row, g, w_qkv, cos_t, sin_t, rope_blk, cache_seq):
    ntok = x.shape[0]
    nt = ntok // TOKEN_TILE
    emit_kv = cache_seq > 0
    rep_spec = pl.BlockSpec((N_KV_HEADS, TOKEN_TILE, KV_DIM), lambda i: (0, i, 0))
    rep_shape = jax.ShapeDtypeStruct((N_KV_HEADS, ntok, KV_DIM), BF16)
    if emit_kv:
        spt = TOKEN_TILE // cache_seq
        kv_spec = pl.BlockSpec((spt, KV_DIM, cache_seq), lambda i: (i, 0, 0))
        kv_shape = jax.ShapeDtypeStruct((ntok // cache_seq, KV_DIM, cache_seq), F32)
    return pl.pallas_call(
        functools.partial(_qkv_kernel, cache_seq),
        grid=(nt,),
        in_specs=[
            pl.BlockSpec((TOKEN_TILE, D_MODEL), lambda i: (i, 0)),
            pl.BlockSpec((None, N_MOD, D_MODEL), lambda i: (mod_row(i), 0, 0)),
            pl.BlockSpec((1, D_MODEL), lambda i: (0, 0)),
            pl.BlockSpec((D_MODEL, QKV_DIM), lambda i: (0, 0)),
            pl.BlockSpec((TOKEN_TILE, LANES), lambda i: (rope_blk(i), 0)),
            pl.BlockSpec((TOKEN_TILE, LANES), lambda i: (rope_blk(i), 0)),
        ],
        out_specs=[pl.BlockSpec((TOKEN_TILE, D_MODEL), lambda i: (i, 0)), rep_spec, rep_spec]
        + ([kv_spec, kv_spec] if emit_kv else []),
        out_shape=[jax.ShapeDtypeStruct((ntok, D_MODEL), BF16), rep_shape, rep_shape]
        + ([kv_shape, kv_shape] if emit_kv else []),
        compiler_params=_cparams(("arbitrary",)),
        name="norm_qkv_rope",
    )(x, mod, g, w_qkv, cos_t, sin_t)


def _group_scores(q_ref, kv, key_parts, bias):
    q_kv = q_ref[:, kv * KV_DIM:(kv + 1) * KV_DIM]
    nq = q_kv.shape[0]
    slot = lax.broadcasted_iota(jnp.int32, q_kv.shape, 1) >> 6
    q4 = jnp.concatenate(
        [jnp.where(slot == g, q_kv, jnp.zeros((), BF16)) for g in range(Q_PER_KV)], axis=0)
    parts = [_dot_nt(q4, keys) for keys in key_parts]
    if bias is not None:
        s0 = parts[0].reshape(Q_PER_KV, nq, -1) + bias[None]
        parts[0] = s0.reshape(Q_PER_KV * nq, -1)
    return parts


def _group_softmax(parts, sink_ref, kv):
    nq = parts[0].shape[0] // Q_PER_KV
    sink = jnp.concatenate(
        [jnp.full((nq, 1), sink_ref[kv * Q_PER_KV + g], F32) for g in range(Q_PER_KV)], axis=0)
    m = sink
    for s in parts:
        m = jnp.maximum(jnp.max(s, axis=-1, keepdims=True), m)
    den = jnp.exp(sink - m)
    probs = []
    for s in parts:
        p = jnp.exp(s - m)
        den = den + jnp.sum(p, axis=-1, keepdims=True)
        probs.append(p.astype(BF16))
    return probs, 1.0 / den


def _group_output(probs, inv_den, value_parts, o_ref, kv):
    nq = probs[0].shape[0] // Q_PER_KV
    r = _dot(probs[0], value_parts[0])
    for p, v in zip(probs[1:], value_parts[1:]):
        r = r + _dot(p, v)
    r = r * inv_den
    slot = lax.broadcasted_iota(jnp.int32, (nq, KV_DIM), 1) >> 6
    o = r[:nq]
    for g in range(1, Q_PER_KV):
        o = jnp.where(slot == g, r[g * nq:(g + 1) * nq], o)
    o_ref[:, kv * KV_DIM:(kv + 1) * KV_DIM] = o.astype(BF16)


def _attend(q_ref, sink_ref, o_ref, keys_of, values_of, bias):
    s_next = _group_scores(q_ref, 0, keys_of(0), bias)
    for kv in range(N_KV_HEADS):
        s = s_next
        if kv + 1 < N_KV_HEADS:
            s_next = _group_scores(q_ref, kv + 1, keys_of(kv + 1), bias)
        probs, inv_den = _group_softmax(s, sink_ref, kv)
        _group_output(probs, inv_den, values_of(kv), o_ref, kv)


def _ctx_attn_kernel(sink_ref, q_ref, k_ref, v_ref, o_ref):
    _attend(q_ref, sink_ref, o_ref, lambda kv: [k_ref[kv]], lambda kv: [v_ref[kv]], None)


def _ctx_attention(sink, q, krep, vrep, n_batch, seq):
    rep_spec = pl.BlockSpec((N_KV_HEADS, seq, KV_DIM), lambda b: (0, b, 0))
    return pl.pallas_call(
        _ctx_attn_kernel,
        grid=(n_batch,),
        in_specs=[
            pl.BlockSpec(memory_space=pltpu.SMEM),
            pl.BlockSpec((seq, D_MODEL), lambda b: (b, 0)),
            rep_spec, rep_spec,
        ],
        out_specs=pl.BlockSpec((seq, D_MODEL), lambda b: (b, 0)),
        out_shape=jax.ShapeDtypeStruct((n_batch * seq, D_MODEL), BF16),
        compiler_params=_cparams(("arbitrary",)),
        name="context_attention",
    )(sink, q, krep, vrep)


def _window_start(n, seq):
    return jnp.clip((n - 1) * BLOCK, 0, seq - 3 * BLOCK)


def _band_bias():
    r = np.arange(BLOCK)[:, None]
    j = np.arange(3 * BLOCK)[None, :]
    out = [np.where(np.abs(j - d * BLOCK - r) <= BLOCK, 0.0, NEG_INF) for d in range(3)]
    return jnp.asarray(np.stack(out), F32)


def _lat_attn_kernel(seq, sink_ref, q_ref, k_ref, v_ref, ck_ref, cv_ref, bias_ref, o_ref):
    win = 3 * BLOCK
    start = pl.multiple_of(_window_start(pl.program_id(1), seq), BLOCK)
    keys_of = lambda kv: [k_ref[kv, pl.ds(start, win), :], ck_ref[kv]]
    values_of = lambda kv: [v_ref[kv, pl.ds(start, win), :], cv_ref[kv]]
    _attend(q_ref, sink_ref, o_ref, keys_of, values_of, bias_ref[...])


def _lat_attention(sink, q, krep, vrep, ckrep, cvrep, n_batch, seq):
    nb = seq // BLOCK
    past = ckrep.shape[2]
    rep_spec = pl.BlockSpec((N_KV_HEADS, seq, KV_DIM), lambda b, n: (0, b, 0))
    crep_spec = pl.BlockSpec((None, N_KV_HEADS, past, KV_DIM), lambda b, n: (b, 0, 0, 0))
    return pl.pallas_call(
        functools.partial(_lat_attn_kernel, seq),
        grid=(n_batch, nb),
        in_specs=[
            pl.BlockSpec(memory_space=pltpu.SMEM),
            pl.BlockSpec((BLOCK, D_MODEL), lambda b, n: (b * nb + n, 0)),
            rep_spec, rep_spec, crep_spec, crep_spec,
            pl.BlockSpec((None, BLOCK, 3 * BLOCK), lambda b, n: (n - _window_start(n, seq) // BLOCK, 0, 0)),
        ],
        out_specs=pl.BlockSpec((BLOCK, D_MODEL), lambda b, n: (b * nb + n, 0)),
        out_shape=jax.ShapeDtypeStruct((n_batch * seq, D_MODEL), BF16),
        compiler_params=_cparams(("arbitrary", "arbitrary")),
        name="latent_attention",
    )(sink, q, krep, vrep, ckrep, cvrep, _band_bias())


def _gelu_tanh(x):
    c = math.sqrt(2.0 / math.pi)
    return x * (0.5 * (1.0 + jnp.tanh(c * (x + 0.044715 * (x * x * x)))))


def _post_kernel(is_attn, emit_next, final, vpt, *refs):
    rows_per_v = TOKEN_TILE // vpt
    refs = list(refs)
    x_ref, mix_ref, mod_ref, g2_ref, wa_ref = refs[:5]
    refs = refs[5:]
    wb_ref = None if is_attn else refs.pop(0)
    w1_ref, w2_ref = refs[:2]
    refs = refs[2:]
    modn_ref = gn_ref = fg_ref = hn_ref = None
    if emit_next:
        modn_ref, gn_ref = refs[:2]
        refs = refs[2:]
    if final:
        fg_ref = refs.pop(0)
    xo_ref = refs.pop(0)
    if emit_next:
        hn_ref = refs.pop(0)
    assert not refs

    sub = TOKEN_TILE // POST_SUBTILES
    assert rows_per_v % sub == 0
    n_chunk = sub // CHUNK
    starts = [k * sub for k in range(POST_SUBTILES)]

    def s5_slot(r0):
        return r0 // rows_per_v, (r0 % rows_per_v) // CHUNK

    def project(r0):
        if is_attn:
            return _dot(mix_ref[r0:r0 + sub, :], wa_ref[...])
        s, c0 = s5_slot(r0)
        y = jnp.concatenate(
            [mix_ref[g, c0:c0 + n_chunk, s].reshape(sub, LANES) for g in range(N_GROUP_BLOCKS)], axis=1)
        yg = _gelu_tanh(y).astype(BF16)
        return _dot(yg, wa_ref[...]) * jax.nn.sigmoid(_dot(yg, wb_ref[...]))

    def prologue(r0, mix):
        x1 = x_ref[r0:r0 + sub, :] + mod_ref[2:3, :] * mix
        h2 = _rms(x1) * g2_ref[...] * (1.0 + mod_ref[4:5, :]) + mod_ref[3:4, :]
        return x1, h2.astype(BF16)

    def mlp(h2):
        acc = None
        for c in range(D_FF // FF_TILE):
            a = jnp.maximum(_dot(h2, w1_ref[:, c * FF_TILE:(c + 1) * FF_TILE]), 0.0)
            t = _dot((a * a).astype(BF16), w2_ref[c * FF_TILE:(c + 1) * FF_TILE, :])
            acc = t if acc is None else acc + t
        return acc

    def epilogue(r0, x1, acc):
        x2 = x1 + mod_ref[5:6, :] * acc
        if emit_next:
            hn = _rms(x2) * gn_ref[...] * (1.0 + modn_ref[1:2, :]) + modn_ref[0:1, :]
            s, c0 = s5_slot(r0)
            for g in range(N_GROUP_BLOCKS):
                blk = hn[:, g * LANES:(g + 1) * LANES]
                hn_ref[g, c0:c0 + n_chunk, s] = blk.reshape(n_chunk, CHUNK, LANES)
        xo_ref[r0:r0 + sub, :] = _rms(x2) * fg_ref[...] if final else x2

    mixes = [project(r0) for r0 in starts]
    pro = [prologue(r0, mix) for r0, mix in zip(starts, mixes)]
    accs = [mlp(h2) for _, h2 in pro]
    for r0, (x1, _), acc in zip(starts, pro, accs):
        epilogue(r0, x1, acc)


def _post(x, mix, mod, mod_row, g2, w_a, w_b, w1, w2, layer, vpt, mod_next=None, g_next=None, final_g=None):
    is_attn = w_b is None
    emit_next = mod_next is not None
    final = final_g is not None
    ntok = x.shape[0]
    nt = ntok // TOKEN_TILE
    cpv = TOKEN_TILE // (vpt * CHUNK)
    n_virt = nt * vpt
    tile = pl.BlockSpec((TOKEN_TILE, D_MODEL), lambda i: (i, 0))
    row = pl.BlockSpec((1, D_MODEL), lambda i: (0, 0))
    modspec = pl.BlockSpec((None, N_MOD, D_MODEL), lambda i: (mod_row(i), 0, 0))
    resident = lambda shape: pl.BlockSpec(shape, lambda i: (0, 0), pipeline_mode=pl.Buffered(1))
    wsq = resident((D_MODEL, D_MODEL))
    gtile = pl.BlockSpec((N_GROUP_BLOCKS, cpv, vpt, CHUNK, LANES), lambda i: (0, 0, i, 0, 0))
    in_specs = [tile, tile if is_attn else gtile, modspec, row, wsq]
    args = [x, mix, mod, g2, w_a]
    if not is_attn:
        in_specs.append(wsq)
        args.append(w_b)
    in_specs += [pl.BlockSpec((None, D_MODEL, D_FF), lambda i: (layer, 0, 0), pipeline_mode=pl.Buffered(1)),
                 pl.BlockSpec((None, D_FF, D_MODEL), lambda i: (layer, 0, 0), pipeline_mode=pl.Buffered(1))]
    args += [w1, w2]
    if emit_next:
        in_specs += [modspec, row]
        args += [mod_next, g_next]
    if final:
        in_specs.append(row)
        args.append(final_g)
    out_specs = [tile]
    out_shape = [jax.ShapeDtypeStruct((ntok, D_MODEL), F32)]
    if emit_next:
        out_specs.append(gtile)
        out_shape.append(jax.ShapeDtypeStruct((N_GROUP_BLOCKS, cpv, n_virt, CHUNK, LANES), F32))
    return pl.pallas_call(
        functools.partial(_post_kernel, is_attn, emit_next, final, vpt),
        grid=(nt,),
        in_specs=in_specs,
        out_specs=out_specs,
        out_shape=out_shape,
        compiler_params=_cparams(("arbitrary",)),
        name="attn_proj_mlp" if is_attn else "glu_mlp_final",
    )(*args)


def _swap(x):
    return pltpu.roll(x, LANES // 2, 1)


def _cmul(z, w_r, w_i):
    return z * w_r + _swap(z) * w_i


def _multiplier(z, lo):
    zs = _swap(z)
    return jnp.where(lo, z, zs), jnp.where(lo, -zs, z)


def _rep_rows(x):
    return jnp.concatenate(
        [jnp.broadcast_to(x[g:g + 1, :], (GROUP_CH, LANES)) for g in range(GROUPS_PER_BLOCK)], axis=0)


def _s5_kernel(n_virt, n_seg, h_ref, lamr_ref, lami_ref, ldt_ref, bt_ref, cp_ref, dsk_ref, s0_ref, *refs):
    if n_seg == 1:
        y_ref, sfin_ref = refs[:2]
        refs = refs[2:]
    else:
        y_ref, sfin_ref = refs[0], None
        refs = refs[1:]
    f_scr, e_scr, k_scr, sf_scr, sb_scr, swf_scr, swb_scr = refs
    ntok = h_ref.shape[0]
    nc = ntok // CHUNK
    cpv = nc // n_virt
    gpb = GROUPS_PER_BLOCK

    lo8 = lax.broadcasted_iota(jnp.int32, (gpb, LANES), 1) < STATE_DIM
    lo = lax.broadcasted_iota(jnp.int32, (LANES, LANES), 1) < STATE_DIM
    conj = jnp.where(lo, 1.0, -1.0)
    row_g = lax.broadcasted_iota(jnp.int32, (LANES, STATE_LANES), 0) >> 4
    col_g = lax.broadcasted_iota(jnp.int32, (LANES, STATE_LANES), 1) >> 7
    diag_wide = row_g == col_g
    diag = (lax.broadcasted_iota(jnp.int32, (LANES, LANES), 0) >> 4) == (
        lax.broadcasted_iota(jnp.int32, (LANES, LANES), 1) >> 4)

    def expand(w):
        return jnp.where(diag_wide, jnp.concatenate([w] * gpb, axis=1), jnp.zeros((), BF16))

    decay = []
    lag = []
    for d in range(2):
        lam_r = lamr_ref[d]
        lam_i = lami_ref[d]
        dt = jnp.exp(ldt_ref[d])
        mag = jnp.exp(lam_r * dt)
        ang = lam_i * dt
        a_r = mag * jnp.cos(ang)
        a_im = mag * jnp.sin(ang)
        a_i = jnp.where(lo8, -a_im, a_im)
        den = lam_r * lam_r + lam_i * lam_i
        num = jnp.where(lo8, a_r - 1.0, a_im)
        f = _cmul(num, lam_r / den, jnp.where(lo8, lam_i, -lam_i) / den)
        pw = [jnp.where(lo8, 1.0, 0.0)]
        for _ in range(CHUNK):
            pw.append(_cmul(pw[-1], a_r, a_i))
        decay.append(pw[CHUNK])
        pw = [_rep_rows(p) for p in pw]
        f_r, f_i = _multiplier(_rep_rows(f), lo)
        bb_r, bb_i = _multiplier(_cmul(bt_ref[d], f_r, f_i), lo)
        c_r, c_i = _multiplier(cp_ref[d], lo)
        cm = (cp_ref[d] * conj).astype(BF16)
        fpow = [_cmul(p, bb_r, bb_i).astype(BF16) for p in pw[:CHUNK]]
        for j in range(CHUNK):
            e = (CHUNK - 1 - j) if d == 0 else j
            f_scr[d, j * LANES:(j + 1) * LANES, :] = expand(fpow[e])
        for t in range(CHUNK):
            e = (t + 1) if d == 0 else (CHUNK - t)
            w = _cmul(pw[e], c_r, c_i) * conj
            e_scr[d, t * LANES:(t + 1) * LANES, :] = expand(w.astype(BF16))
        lag.append([jnp.where(diag, _dot_nt(fp, cm), 0.0) for fp in fpow])

    for j in range(CHUNK):
        for t in range(CHUNK):
            k = t - j
            tile = lag[0][k] if k > 0 else (lag[1][-k] if k < 0 else lag[0][0] + lag[1][0])
            k_scr[j * LANES:(j + 1) * LANES, t * LANES:(t + 1) * LANES] = tile.astype(BF16)

    xcat = jnp.concatenate(
        [h_ref[pl.ds(j, nc, stride=CHUNK), :].astype(BF16) for j in range(CHUNK)], axis=1)

    for d, scr, sw_scr in ((0, sf_scr, swf_scr), (1, sb_scr, swb_scr)):
        loc_all = _dot(xcat, f_scr[d])
        for k in range(gpb):
            loc = loc_all[:, k * LANES:(k + 1) * LANES]
            scr[k] = loc
            sw_scr[k] = _swap(loc)
    y_within = _dot(xcat, k_scr[...])

    sgn8 = jnp.where(lo8, -1.0, 1.0)

    def dup(z):
        zs = _swap(z)
        return jnp.where(lo8, z, zs), jnp.where(lo8, zs, z)

    def scan(scr, sw_scr, d, reverse):
        a_re, a_im = dup(decay[d])
        a_sg = a_im * sgn8
        a_r = [a_re[k:k + 1, :] for k in range(gpb)]
        a_i = [a_sg[k:k + 1, :] for k in range(gpb)]

        def rows_of(i):
            c = (cpv - 1 - i) if reverse else i
            return pl.ds(c * n_virt, n_virt)

        def body(i, carry):
            st, sw = carry
            rows = rows_of(i)
            new_st, new_sw = [], []
            for k in range(gpb):
                loc = scr[k, rows, :]
                loc_sw = sw_scr[k, rows, :]
                scr[k, rows, :] = st[k]
                new_st.append(a_r[k] * st[k] + a_i[k] * sw[k] + loc)
                new_sw.append(a_r[k] * sw[k] - a_i[k] * st[k] + loc_sw)
            return tuple(new_st), tuple(new_sw)

        st0 = tuple(s0_ref[d, :, k * LANES:(k + 1) * LANES] for k in range(gpb))
        sw0 = tuple(_swap(s) for s in st0)
        carry = (st0, sw0)
        for i in range(cpv):
            carry = body(i, carry)
        fin = carry[0]
        if n_seg == 1:
            for k in range(gpb):
                sfin_ref[d, :, k * LANES:(k + 1) * LANES] = fin[k]
            return

        p = decay[d]
        for _ in range(cpv.bit_length() - 1):
            p_re, p_im = dup(p)
            p = _cmul(p, p_re, p_im * sgn8)
        v_re, v_im = dup(p)
        v_sg = v_im * sgn8
        seg = lax.broadcasted_iota(jnp.int32, (n_virt, LANES), 0) & (n_seg - 1)
        has_pred = seg != ((n_seg - 1) if reverse else 0)
        shift = (n_virt - 1) if reverse else 1
        cin = []
        for k in range(gpb):
            ck = jnp.zeros((n_virt, LANES), F32)
            for _ in range(n_seg - 1):
                nxt = fin[k] + ck * v_re[k:k + 1, :] + _swap(ck) * v_sg[k:k + 1, :]
                ck = jnp.where(has_pred, pltpu.roll(nxt, shift, 0), 0.0)
            cin.append(ck)
        cin_sw = [_swap(x) for x in cin]

        def fix(i, carry):
            q_re, q_im = carry
            rows = rows_of(i)
            q_sg = q_im * sgn8
            for k in range(gpb):
                scr[k, rows, :] += cin[k] * q_re[k:k + 1, :] + cin_sw[k] * q_sg[k:k + 1, :]
            return q_re * a_re - q_im * a_im, q_re * a_im + q_im * a_re

        q = (jnp.ones((gpb, LANES), F32), jnp.zeros((gpb, LANES), F32))
        for i in range(cpv):
            q = fix(i, q)

    scan(sf_scr, swf_scr, 0, False)
    scan(sb_scr, swb_scr, 1, True)

    s_f = jnp.concatenate([sf_scr[k].astype(BF16) for k in range(gpb)], axis=1)
    s_b = jnp.concatenate([sb_scr[k].astype(BF16) for k in range(gpb)], axis=1)
    yall = y_within + _dot_nt(s_f, e_scr[0]) + _dot_nt(s_b, e_scr[1])
    dsk = dsk_ref[...]
    for t in range(CHUNK):
        rows = pl.ds(t, nc, stride=CHUNK)
        y_ref[rows, :] = yall[:, t * LANES:(t + 1) * LANES] + h_ref[rows, :] * dsk


def _s5_params(lam_re, lam_im, log_dt, b_re, b_im, c_re, c_im):
    lamr = jnp.concatenate([lam_re, lam_re], axis=-1).astype(F32)
    lami = jnp.concatenate([lam_im, lam_im], axis=-1).astype(F32)
    ldt = jnp.broadcast_to(log_dt.astype(F32)[..., None], lamr.shape)
    bt = jnp.concatenate([b_re.transpose(0, 1, 3, 2), b_im.transpose(0, 1, 3, 2)], axis=-1)
    cp = jnp.concatenate([c_re, c_im], axis=-1)
    return (lamr, lami, ldt, bt.reshape(2, D_MODEL, LANES).astype(F32), cp.reshape(2, D_MODEL, LANES).astype(F32))


def _s5(h, params, d_skip, s0, n_seg):
    lamr, lami, ldt, bt, cp = params
    _, cpv, n_virt, _, _ = h.shape
    assert cpv & (cpv - 1) == 0 and n_virt % SUBLANES == 0 and n_seg & (n_seg - 1) == 0
    nc = cpv * n_virt
    ntok = nc * CHUNK
    hspec = pl.BlockSpec((None, ntok, LANES), lambda g: (g, 0, 0))
    kdim = CHUNK * LANES
    gspec = pl.BlockSpec((2, GROUPS_PER_BLOCK, LANES), lambda g: (0, g, 0))
    rspec = pl.BlockSpec((2, LANES, LANES), lambda g: (0, g, 0))
    sspec = pl.BlockSpec((None, 2, n_virt, STATE_LANES), lambda g: (g, 0, 0, 0))
    state_scr = pltpu.VMEM((GROUPS_PER_BLOCK, nc, LANES), F32)
    out_specs = [hspec]
    out_shape = [jax.ShapeDtypeStruct((N_GROUP_BLOCKS, ntok, LANES), F32)]
    if n_seg == 1:
        out_specs.append(sspec)
        out_shape.append(jax.ShapeDtypeStruct((N_GROUP_BLOCKS, 2, n_virt, STATE_LANES), F32))
    outs = pl.pallas_call(
        functools.partial(_s5_kernel, n_virt, n_seg),
        grid=(N_GROUP_BLOCKS,),
        in_specs=[
            hspec,
            gspec, gspec, gspec, rspec, rspec,
            pl.BlockSpec((1, LANES), lambda g: (0, g)),
            sspec,
        ],
        out_specs=out_specs,
        out_shape=out_shape,
        scratch_shapes=[
            pltpu.VMEM((2, kdim, STATE_LANES), BF16),
            pltpu.VMEM((2, kdim, STATE_LANES), BF16),
            pltpu.VMEM((kdim, kdim), BF16),
            state_scr, state_scr, state_scr, state_scr,
        ],
        compiler_params=_cparams(("arbitrary",)),
        name="s5_chunked_scan",
    )(h.reshape(N_GROUP_BLOCKS, ntok, LANES), lamr, lami, ldt, bt, cp, d_skip, s0)
    y = outs[0].reshape(h.shape)
    return (y, outs[1]) if n_seg == 1 else (y, None)


def _state_to_blocks(s):
    b = s.shape[0]
    s = s.reshape(b, 2, 2, N_GROUP_BLOCKS, GROUPS_PER_BLOCK, STATE_DIM)
    return s.transpose(3, 1, 0, 4, 2, 5).reshape(N_GROUP_BLOCKS, 2, b, STATE_LANES)


def _blocks_to_state(s):
    b = s.shape[2]
    s = s.reshape(N_GROUP_BLOCKS, 2, b, GROUPS_PER_BLOCK, 2, STATE_DIM)
    return s.transpose(2, 1, 4, 0, 3, 5).reshape(b, 2, 2, N_GROUPS, STATE_DIM)


def _rope_tables(n_tokens):
    pos = np.arange(n_tokens)
    n_freq = HEAD_DIM // 4
    freqs = ROPE_BASE ** (-np.arange(n_freq, dtype=np.float64) / n_freq)
    ang_r = (pos // GRID_W)[:, None] * freqs
    ang_c = (pos % GRID_W)[:, None] * freqs
    cos_h = np.concatenate([np.cos(ang_r), np.cos(ang_r), np.cos(ang_c), np.cos(ang_c)], axis=1)
    sin_h = np.concatenate([-np.sin(ang_r), np.sin(ang_r), -np.sin(ang_c), np.sin(ang_c)], axis=1)
    cos_t = np.concatenate([np.ones((TOKEN_TILE, LANES)), np.tile(cos_h, (1, 2))], axis=0)
    sin_t = np.concatenate([np.zeros((TOKEN_TILE, LANES)), np.tile(sin_h, (1, 2))], axis=0)
    return jnp.asarray(cos_t, F32), jnp.asarray(sin_t, F32)


def kernel(x_prompt, x_sample, cache_k, cache_v, state_ssm, c, c_ctx, norm1_g, norm2_g, w_mod, b_mod,
           w_qkv, w_o, attn_sink, ssm_lam_re, ssm_lam_im, ssm_log_dt, ssm_b_re, ssm_b_im, ssm_c_re,
           ssm_c_im, ssm_d, glu_w_a, glu_w_b, mlp_w1, mlp_w2, final_norm_g):
    bp, lp, _ = x_prompt.shape
    bx, lx, _ = x_sample.shape
    assert lx % TOKEN_TILE == 0 and (bp * lp) % TOKEN_TILE == 0
    tiles_per_lat = lx // TOKEN_TILE

    xp = x_prompt.reshape(bp * lp, D_MODEL)
    xx = x_sample.reshape(bx * lx, D_MODEL)

    cvecs = jnp.zeros((8, D_MODEL), F32).at[0].set(c_ctx).at[1:1 + bx].set(c)
    mod = _modulation(cvecs, w_mod, b_mod)

    ctx_row = lambda i: 0
    lat_row = lambda i: 1 + i // tiles_per_lat
    ctx_rope = lambda i: 0
    lat_rope = lambda i: 1 + i % tiles_per_lat

    cos_t, sin_t = _rope_tables(lx)
    wqkv = w_qkv[0].astype(BF16)
    g1 = norm1_g[0].reshape(1, D_MODEL)
    sink = attn_sink[0].astype(F32)
    qp, krp, vrp, kp, vp = _qkv(xp, mod[0], ctx_row, g1, wqkv, cos_t, sin_t, ctx_rope, lp)
    qx, krx, vrx = _qkv(xx, mod[0], lat_row, g1, wqkv, cos_t, sin_t, lat_rope, 0)
    op = _ctx_attention(sink, qp, krp, vrp, bp, lp)
    rep = lambda t: jnp.tile(t[:, 0].transpose(0, 2, 1, 3), (1, 1, 1, Q_PER_KV)).astype(BF16)
    ox = _lat_attention(sink, qx, krx, vrx, rep(cache_k), rep(cache_v), bx, lx)

    wo = w_o[0].astype(BF16)
    w1 = mlp_w1.astype(BF16)
    w2 = mlp_w2.astype(BF16)
    g2 = norm2_g.reshape(-1, 1, D_MODEL)
    gn = norm1_g[1].reshape(1, D_MODEL)
    vpt_p = TOKEN_TILE // lp
    vpt_x = 1
    n_seg_x = tiles_per_lat
    xp, hp = _post(xp, op, mod[0], ctx_row, g2[0], wo, None, w1, w2, 0, vpt_p, mod_next=mod[1], g_next=gn)
    xx, hx = _post(xx, ox, mod[0], lat_row, g2[0], wo, None, w1, w2, 0, vpt_x, mod_next=mod[1], g_next=gn)

    params = _s5_params(ssm_lam_re[0], ssm_lam_im[0], ssm_log_dt[0], ssm_b_re[0], ssm_b_im[0],
                        ssm_c_re[0], ssm_c_im[0])
    dsk = ssm_d[0].astype(F32).reshape(1, D_MODEL)
    s0p = jnp.zeros((N_GROUP_BLOCKS, 2, bp, STATE_LANES), F32)
    sx = _state_to_blocks(state_ssm[:, 0].astype(F32))
    s0x = jnp.zeros((N_GROUP_BLOCKS, 2, bx, n_seg_x, STATE_LANES), F32)
    s0x = s0x.at[:, 0, :, 0].set(sx[:, 0]).at[:, 1, :, n_seg_x - 1].set(sx[:, 1])
    s0x = s0x.reshape(N_GROUP_BLOCKS, 2, bx * n_seg_x, STATE_LANES)
    yp, sfin = _s5(hp, params, dsk, s0p, 1)
    yx, _ = _s5(hx, params, dsk, s0x, n_seg_x)
    new_state = _blocks_to_state(sfin)[:, None]

    wa = glu_w_a[0].astype(BF16)
    wb = glu_w_b[0].astype(BF16)
    fg = final_norm_g.reshape(1, D_MODEL)
    (yp_out,) = _post(xp, yp, mod[1], ctx_row, g2[1], wa, wb, w1, w2, 1, vpt_p, final_g=fg)
    (yx_out,) = _post(xx, yx, mod[1], lat_row, g2[1], wa, wb, w1, w2, 1, vpt_x, final_g=fg)

    to_cache = lambda t: t.reshape(bp, N_KV_HEADS, HEAD_DIM, lp).transpose(0, 3, 1, 2)[:, None]
    new_k = to_cache(kp)
    new_v = to_cache(vp)
    return (yp_out.reshape(bp, lp, D_MODEL), yx_out.reshape(bx, lx, D_MODEL), new_k, new_v, new_state)
```

```python
import functools
import math

import numpy as np
import jax
import jax.numpy as jnp
from jax import lax
from jax.experimental import pallas as pl
from jax.experimental.pallas import tpu as pltpu

F32 = jnp.float32
BF16 = jnp.bfloat16

D_MODEL = 1024
N_HEADS = 16
N_KV_HEADS = 4
HEAD_DIM = 64
Q_PER_KV = N_HEADS // N_KV_HEADS
KV_DIM = N_KV_HEADS * HEAD_DIM
QKV_DIM = D_MODEL + 2 * KV_DIM
BLOCK = 128
GRID_W = 64
ROPE_BASE = 10000.0
ATTN_SCALE = HEAD_DIM ** -0.5
N_GROUPS = 64
GROUP_CH = 16
STATE_DIM = 64
D_FF = 4 * D_MODEL
N_MOD = 6
RMS_EPS = 1e-6
NEG_INF = -1e30

LANES = 128
SUBLANES = 8
GROUPS_PER_BLOCK = LANES // GROUP_CH
N_GROUP_BLOCKS = N_GROUPS // GROUPS_PER_BLOCK
STATE_LANES = GROUPS_PER_BLOCK * 2 * STATE_DIM
CHUNK = SUBLANES
TOKEN_TILE = 512
FF_TILE = 1024
POST_SUBTILES = 2
VMEM_LIMIT = 56 * 1024 * 1024


def _cparams(semantics):
    return pltpu.CompilerParams(dimension_semantics=semantics, vmem_limit_bytes=VMEM_LIMIT)


def _rms(x):
    return x * lax.rsqrt(jnp.mean(x * x, axis=-1, keepdims=True) + RMS_EPS)


def _dot(a, b):
    return jnp.dot(a, b, preferred_element_type=F32)


def _dot_nt(a, b):
    return lax.dot_general(a, b, (((1,), (1,)), ((), ())), preferred_element_type=F32)


def _mod_kernel(cv_ref, w_ref, b_ref, o_ref):
    cv = cv_ref[...]
    s = (cv * jax.nn.sigmoid(cv)).astype(BF16)
    o_ref[0] = _dot(s, w_ref[0].astype(BF16)) + b_ref[0]


def _modulation(cvecs, w_mod, b_mod):
    depth = w_mod.shape[0]
    out = pl.pallas_call(
        _mod_kernel,
        grid=(depth, N_MOD),
        in_specs=[
            pl.BlockSpec((8, D_MODEL), lambda l, j: (0, 0)),
            pl.BlockSpec((1, D_MODEL, D_MODEL), lambda l, j: (l, 0, j)),
            pl.BlockSpec((1, 1, D_MODEL), lambda l, j: (l, 0, j)),
        ],
        out_specs=pl.BlockSpec((1, 8, D_MODEL), lambda l, j: (l, 0, j)),
        out_shape=jax.ShapeDtypeStruct((depth, 8, N_MOD * D_MODEL), F32),
        compiler_params=_cparams(("arbitrary", "arbitrary")),
        name="modulation",
    )(cvecs, w_mod, b_mod.reshape(depth, 1, N_MOD * D_MODEL))
    return out.reshape(depth, 8, N_MOD, D_MODEL)


def _head_replicated(blk, odd):
    lo = lax.broadcasted_iota(jnp.int32, blk.shape, 1) < HEAD_DIM
    other = pltpu.roll(blk, HEAD_DIM, 1)
    dup = (jnp.where(lo, other, blk) if odd else jnp.where(lo, blk, other)).astype(BF16)
    return jnp.concatenate([dup, dup], axis=1)


def _qkv_kernel(cache_seq, x_ref, mod_ref, g_ref, w_ref, cos_ref, sin_ref, q_ref, krep_ref, vrep_ref, *kv_refs):
    def emit_cache(ref, blk, c0):
        for s in range(TOKEN_TILE // cache_seq):
            ref[s, c0:c0 + LANES, :] = blk[s * cache_seq:(s + 1) * cache_seq, :].T

    h = _rms(x_ref[...]) * g_ref[...] * (1.0 + mod_ref[1:2, :]) + mod_ref[0:1, :]
    qkv = _dot(h.astype(BF16), w_ref[...])
    cos = cos_ref[...]
    sin = sin_ref[...]
    lane = lax.broadcasted_iota(jnp.int32, cos.shape, 1)
    first = (lane & 31) < 16
    n_rot = (D_MODEL + KV_DIM) // LANES
    for blk in range(n_rot):
        t = qkv[:, blk * LANES:(blk + 1) * LANES]
        partner = jnp.where(first, pltpu.roll(t, LANES - 16, 1), pltpu.roll(t, 16, 1))
        r = t * cos + partner * sin
        if blk < D_MODEL // LANES:
            q_ref[:, blk * LANES:(blk + 1) * LANES] = (r * ATTN_SCALE).astype(BF16)
        else:
            c0 = blk * LANES - D_MODEL
            if cache_seq:
                emit_cache(kv_refs[0], r, c0)
            for half in range(2):
                krep_ref[c0 // HEAD_DIM + half] = _head_replicated(r, half)
    v = qkv[:, D_MODEL + KV_DIM:]
    for c0 in range(0, KV_DIM, LANES):
        blk = v[:, c0:c0 + LANES]
        if cache_seq:
            emit_cache(kv_refs[1], blk, c0)
        for half in range(2):
            vrep_ref[c0 // HEAD_DIM + half] = _head_replicated(blk, half)


def _qkv(x, mod, mod_row, g, w_qkv, cos_t, sin_t, rope_blk, cache_seq):
    ntok = x.shape[0]
    nt = ntok // TOKEN_TILE
    emit_kv = cache_seq > 0
    rep_spec = pl.BlockSpec((N_KV_HEADS, TOKEN_TILE, KV_DIM), lambda i: (0, i, 0))
    rep_shape = jax.ShapeDtypeStruct((N_KV_HEADS, ntok, KV_DIM), BF16)
    if emit_kv:
        spt = TOKEN_TILE // cache_seq
        kv_spec = pl.BlockSpec((spt, KV_DIM, cache_seq), lambda i: (i, 0, 0))
        kv_shape = jax.ShapeDtypeStruct((ntok // cache_seq, KV_DIM, cache_seq), F32)
    return pl.pallas_call(
        functools.partial(_qkv_kernel, cache_seq),
        grid=(nt,),
        in_specs=[
            pl.BlockSpec((TOKEN_TILE, D_MODEL), lambda i: (i, 0)),
            pl.BlockSpec((None, N_MOD, D_MODEL), lambda i: (mod_row(i), 0, 0)),
            pl.BlockSpec((1, D_MODEL), lambda i: (0, 0)),
            pl.BlockSpec((D_MODEL, QKV_DIM), lambda i: (0, 0)),
            pl.BlockSpec((TOKEN_TILE, LANES), lambda i: (rope_blk(i), 0)),
            pl.BlockSpec((TOKEN_TILE, LANES), lambda i: (rope_blk(i), 0)),
        ],
        out_specs=[pl.BlockSpec((TOKEN_TILE, D_MODEL), lambda i: (i, 0)), rep_spec, rep_spec]
        + ([kv_spec, kv_spec] if emit_kv else []),
        out_shape=[jax.ShapeDtypeStruct((ntok, D_MODEL), BF16), rep_shape, rep_shape]
        + ([kv_shape, kv_shape] if emit_kv else []),
        compiler_params=_cparams(("arbitrary",)),
        name="norm_qkv_rope",
    )(x, mod, g, w_qkv, cos_t, sin_t)


def _group_scores(q_ref, kv, key_parts, bias):
    q_kv = q_ref[:, kv * KV_DIM:(kv + 1) * KV_DIM]
    nq = q_kv.shape[0]
    slot = lax.broadcasted_iota(jnp.int32, q_kv.shape, 1) >> 6
    q4 = jnp.concatenate(
        [jnp.where(slot == g, q_kv, jnp.zeros((), BF16)) for g in range(Q_PER_KV)], axis=0)
    parts = [_dot_nt(q4, keys) for keys in key_parts]
    if bias is not None:
        s0 = parts[0].reshape(Q_PER_KV, nq, -1) + bias[None]
        parts[0] = s0.reshape(Q_PER_KV * nq, -1)
    return parts


def _group_softmax(parts, sink_ref, kv):
    nq = parts[0].shape[0] // Q_PER_KV
    sink = jnp.concatenate(
        [jnp.full((nq, LANES), sink_ref[kv * Q_PER_KV + g], F32) for g in range(Q_PER_KV)], axis=0)
    blocks = [[s[:, j:j + LANES] for j in range(0, s.shape[1], LANES)] for s in parts]
    fold = None
    for b in sum(blocks, []):
        fold = b if fold is None else jnp.maximum(fold, b)
    m = jnp.maximum(jnp.max(fold, axis=-1, keepdims=True), sink)
    probs = [[jnp.exp(b - m) for b in bs] for bs in blocks]
    fold = None
    for p in sum(probs, []):
        fold = p if fold is None else fold + p
    den = jnp.sum(fold, axis=-1, keepdims=True) + jnp.exp(sink - m)
    return [jnp.concatenate(ps, axis=1).astype(BF16) for ps in probs], 1.0 / den


def _group_output(probs, inv_den, value_parts, o_ref, kv):
    nq = probs[0].shape[0] // Q_PER_KV
    r = _dot(probs[0], value_parts[0])
    for p, v in zip(probs[1:], value_parts[1:]):
        r = r + _dot(p, v)
    r = r * jnp.concatenate([inv_den] * (KV_DIM // LANES), axis=1)
    slot = lax.broadcasted_iota(jnp.int32, (nq, KV_DIM), 1) >> 6
    o = r[:nq]
    for g in range(1, Q_PER_KV):
        o = jnp.where(slot == g, r[g * nq:(g + 1) * nq], o)
    o_ref[:, kv * KV_DIM:(kv + 1) * KV_DIM] = o.astype(BF16)


def _attend(q_ref, sink_ref, o_ref, keys_of, values_of, bias):
    s_next = _group_scores(q_ref, 0, keys_of(0), bias)
    for kv in range(N_KV_HEADS):
        s = s_next
        if kv + 1 < N_KV_HEADS:
            s_next = _group_scores(q_ref, kv + 1, keys_of(kv + 1), bias)
        probs, inv_den = _group_softmax(s, sink_ref, kv)
        _group_output(probs, inv_den, values_of(kv), o_ref, kv)


def _ctx_attn_kernel(sink_ref, q_ref, k_ref, v_ref, o_ref):
    _attend(q_ref, sink_ref, o_ref, lambda kv: [k_ref[kv]], lambda kv: [v_ref[kv]], None)


def _ctx_attention(sink, q, krep, vrep, n_batch, seq):
    rep_spec = pl.BlockSpec((N_KV_HEADS, seq, KV_DIM), lambda b: (0, b, 0))
    return pl.pallas_call(
        _ctx_attn_kernel,
        grid=(n_batch,),
        in_specs=[
            pl.BlockSpec(memory_space=pltpu.SMEM),
            pl.BlockSpec((seq, D_MODEL), lambda b: (b, 0)),
            rep_spec, rep_spec,
        ],
        out_specs=pl.BlockSpec((seq, D_MODEL), lambda b: (b, 0)),
        out_shape=jax.ShapeDtypeStruct((n_batch * seq, D_MODEL), BF16),
        compiler_params=_cparams(("arbitrary",)),
        name="context_attention",
    )(sink, q, krep, vrep)


def _window_start(n, seq):
    return jnp.clip((n - 1) * BLOCK, 0, seq - 3 * BLOCK)


def _band_bias():
    r = np.arange(BLOCK)[:, None]
    j = np.arange(3 * BLOCK)[None, :]
    out = [np.where(np.abs(j - d * BLOCK - r) <= BLOCK, 0.0, NEG_INF) for d in range(3)]
    return jnp.asarray(np.stack(out), F32)


def _lat_attn_kernel(seq, sink_ref, q_ref, k_ref, v_ref, ck_ref, cv_ref, bias_ref, o_ref):
    win = 3 * BLOCK
    start = pl.multiple_of(_window_start(pl.program_id(1), seq), BLOCK)
    keys_of = lambda kv: [k_ref[kv, pl.ds(start, win), :], ck_ref[kv]]
    values_of = lambda kv: [v_ref[kv, pl.ds(start, win), :], cv_ref[kv]]
    _attend(q_ref, sink_ref, o_ref, keys_of, values_of, bias_ref[...])


def _lat_attention(sink, q, krep, vrep, ckrep, cvrep, n_batch, seq):
    nb = seq // BLOCK
    past = ckrep.shape[2]
    rep_spec = pl.BlockSpec((N_KV_HEADS, seq, KV_DIM), lambda b, n: (0, b, 0))
    crep_spec = pl.BlockSpec((None, N_KV_HEADS, past, KV_DIM), lambda b, n: (b, 0, 0, 0))
    return pl.pallas_call(
        functools.partial(_lat_attn_kernel, seq),
        grid=(n_batch, nb),
        in_specs=[
            pl.BlockSpec(memory_space=pltpu.SMEM),
            pl.BlockSpec((BLOCK, D_MODEL), lambda b, n: (b * nb + n, 0)),
            rep_spec, rep_spec, crep_spec, crep_spec,
            pl.BlockSpec((None, BLOCK, 3 * BLOCK), lambda b, n: (n - _window_start(n, seq) // BLOCK, 0, 0)),
        ],
        out_specs=pl.BlockSpec((BLOCK, D_MODEL), lambda b, n: (b * nb + n, 0)),
        out_shape=jax.ShapeDtypeStruct((n_batch * seq, D_MODEL), BF16),
        compiler_params=_cparams(("arbitrary", "arbitrary")),
        name="latent_attention",
    )(sink, q, krep, vrep, ckrep, cvrep, _band_bias())


def _gelu_tanh(x):
    c = math.sqrt(2.0 / math.pi)
    return x * (0.5 * (1.0 + jnp.tanh(c * (x + 0.044715 * (x * x * x)))))


def _post_kernel(is_attn, emit_next, final, vpt, *refs):
    rows_per_v = TOKEN_TILE // vpt
    refs = list(refs)
    x_ref, mix_ref, mod_ref, g2_ref, wa_ref = refs[:5]
    refs = refs[5:]
    wb_ref = None if is_attn else refs.pop(0)
    w1_ref, w2_ref = refs[:2]
    refs = refs[2:]
    modn_ref = gn_ref = fg_ref = hn_ref = None
    if emit_next:
        modn_ref, gn_ref = refs[:2]
        refs = refs[2:]
    if final:
        fg_ref = refs.pop(0)
    xo_ref = refs.pop(0)
    if emit_next:
        hn_ref = refs.pop(0)
    assert not refs

    sub = TOKEN_TILE // POST_SUBTILES
    assert rows_per_v % sub == 0
    n_chunk = sub // CHUNK
    starts = [k * sub for k in range(POST_SUBTILES)]

    def s5_slot(r0):
        return r0 // rows_per_v, (r0 % rows_per_v) // CHUNK

    def project(r0):
        if is_attn:
            return _dot(mix_ref[r0:r0 + sub, :], wa_ref[...])
        s, c0 = s5_slot(r0)
        y = jnp.concatenate(
            [mix_ref[g, c0:c0 + n_chunk, s].reshape(sub, LANES) for g in range(N_GROUP_BLOCKS)], axis=1)
        yg = _gelu_tanh(y).astype(BF16)
        return _dot(yg, wa_ref[...]) * jax.nn.sigmoid(_dot(yg, wb_ref[...]))

    def prologue(r0, mix):
        x1 = x_ref[r0:r0 + sub, :] + mod_ref[2:3, :] * mix
        h2 = _rms(x1) * g2_ref[...] * (1.0 + mod_ref[4:5, :]) + mod_ref[3:4, :]
        return x1, h2.astype(BF16)

    def mlp(h2):
        acc = None
        for c in range(D_FF // FF_TILE):
            a = jnp.maximum(_dot(h2, w1_ref[:, c * FF_TILE:(c + 1) * FF_TILE]), 0.0)
            t = _dot((a * a).astype(BF16), w2_ref[c * FF_TILE:(c + 1) * FF_TILE, :])
            acc = t if acc is None else acc + t
        return acc

    def epilogue(r0, x1, acc):
        x2 = x1 + mod_ref[5:6, :] * acc
        if emit_next:
            hn = _rms(x2) * gn_ref[...] * (1.0 + modn_ref[1:2, :]) + modn_ref[0:1, :]
            s, c0 = s5_slot(r0)
            for g in range(N_GROUP_BLOCKS):
                blk = hn[:, g * LANES:(g + 1) * LANES]
                hn_ref[g, c0:c0 + n_chunk, s] = blk.reshape(n_chunk, CHUNK, LANES)
        xo_ref[r0:r0 + sub, :] = _rms(x2) * fg_ref[...] if final else x2

    mixes = [project(r0) for r0 in starts]
    pro = [prologue(r0, mix) for r0, mix in zip(starts, mixes)]
    accs = [mlp(h2) for _, h2 in pro]
    for r0, (x1, _), acc in zip(starts, pro, accs):
        epilogue(r0, x1, acc)


def _post(x, mix, mod, mod_row, g2, w_a, w_b, w1, w2, layer, vpt, mod_next=None, g_next=None, final_g=None):
    is_attn = w_b is None
    emit_next = mod_next is not None
    final = final_g is not None
    ntok = x.shape[0]
    nt = ntok // TOKEN_TILE
    cpv = TOKEN_TILE // (vpt * CHUNK)
    n_virt = nt * vpt
    tile = pl.BlockSpec((TOKEN_TILE, D_MODEL), lambda i: (i, 0))
    row = pl.BlockSpec((1, D_MODEL), lambda i: (0, 0))
    modspec = pl.BlockSpec((None, N_MOD, D_MODEL), lambda i: (mod_row(i), 0, 0))
    resident = lambda shape: pl.BlockSpec(shape, lambda i: (0, 0), pipeline_mode=pl.Buffered(1))
    wsq = resident((D_MODEL, D_MODEL))
    gtile = pl.BlockSpec((N_GROUP_BLOCKS, cpv, vpt, CHUNK, LANES), lambda i: (0, 0, i, 0, 0))
    in_specs = [tile, tile if is_attn else gtile, modspec, row, wsq]
    args = [x, mix, mod, g2, w_a]
    if not is_attn:
        in_specs.append(wsq)
        args.append(w_b)
    in_specs += [pl.BlockSpec((None, D_MODEL, D_FF), lambda i: (layer, 0, 0), pipeline_mode=pl.Buffered(1)),
                 pl.BlockSpec((None, D_FF, D_MODEL), lambda i: (layer, 0, 0), pipeline_mode=pl.Buffered(1))]
    args += [w1, w2]
    if emit_next:
        in_specs += [modspec, row]
        args += [mod_next, g_next]
    if final:
        in_specs.append(row)
        args.append(final_g)
    out_specs = [tile]
    out_shape = [jax.ShapeDtypeStruct((ntok, D_MODEL), F32)]
    if emit_next:
        out_specs.append(gtile)
        out_shape.append(jax.ShapeDtypeStruct((N_GROUP_BLOCKS, cpv, n_virt, CHUNK, LANES), F32))
    return pl.pallas_call(
        functools.partial(_post_kernel, is_attn, emit_next, final, vpt),
        grid=(nt,),
        in_specs=in_specs,
        out_specs=out_specs,
        out_shape=out_shape,
        compiler_params=_cparams(("arbitrary",)),
        name="attn_proj_mlp" if is_attn else "glu_mlp_final",
    )(*args)


def _swap(x):
    return pltpu.roll(x, LANES // 2, 1)


def _cmul(z, w_r, w_i):
    return z * w_r + _swap(z) * w_i


def _multiplier(z, lo):
    zs = _swap(z)
    return jnp.where(lo, z, zs), jnp.where(lo, -zs, z)


def _rep_rows(x):
    return jnp.concatenate(
        [jnp.broadcast_to(x[g:g + 1, :], (GROUP_CH, LANES)) for g in range(GROUPS_PER_BLOCK)], axis=0)


def _s5_kernel(n_virt, n_seg, h_ref, lamr_ref, lami_ref, ldt_ref, bt_ref, cp_ref, dsk_ref, s0_ref, *refs):
    if n_seg == 1:
        y_ref, sfin_ref = refs[:2]
        refs = refs[2:]
    else:
        y_ref, sfin_ref = refs[0], None
        refs = refs[1:]
    f_scr, e_scr, k_scr, sf_scr, sb_scr, swf_scr, swb_scr = refs
    ntok = h_ref.shape[0]
    nc = ntok // CHUNK
    cpv = nc // n_virt
    gpb = GROUPS_PER_BLOCK

    lo8 = lax.broadcasted_iota(jnp.int32, (gpb, LANES), 1) < STATE_DIM
    lo = lax.broadcasted_iota(jnp.int32, (LANES, LANES), 1) < STATE_DIM
    conj = jnp.where(lo, 1.0, -1.0)
    row_g = lax.broadcasted_iota(jnp.int32, (LANES, STATE_LANES), 0) >> 4
    col_g = lax.broadcasted_iota(jnp.int32, (LANES, STATE_LANES), 1) >> 7
    diag_wide = row_g == col_g
    diag = (lax.broadcasted_iota(jnp.int32, (LANES, LANES), 0) >> 4) == (
        lax.broadcasted_iota(jnp.int32, (LANES, LANES), 1) >> 4)

    def expand(w):
        return jnp.where(diag_wide, jnp.concatenate([w] * gpb, axis=1), jnp.zeros((), BF16))

    decay = []
    lag = []
    for d in range(2):
        lam_r = lamr_ref[d]
        lam_i = lami_ref[d]
        dt = jnp.exp(ldt_ref[d])
        mag = jnp.exp(lam_r * dt)
        ang = lam_i * dt
        a_r = mag * jnp.cos(ang)
        a_im = mag * jnp.sin(ang)
        a_i = jnp.where(lo8, -a_im, a_im)
        den = lam_r * lam_r + lam_i * lam_i
        num = jnp.where(lo8, a_r - 1.0, a_im)
        f = _cmul(num, lam_r / den, jnp.where(lo8, lam_i, -lam_i) / den)
        pw = [jnp.where(lo8, 1.0, 0.0)]
        for _ in range(CHUNK):
            pw.append(_cmul(pw[-1], a_r, a_i))
        decay.append(pw[CHUNK])
        pw = [_rep_rows(p) for p in pw]
        f_r, f_i = _multiplier(_rep_rows(f), lo)
        bb_r, bb_i = _multiplier(_cmul(bt_ref[d], f_r, f_i), lo)
        c_r, c_i = _multiplier(cp_ref[d], lo)
        cm = (cp_ref[d] * conj).astype(BF16)
        fpow = [_cmul(p, bb_r, bb_i).astype(BF16) for p in pw[:CHUNK]]
        for j in range(CHUNK):
            e = (CHUNK - 1 - j) if d == 0 else j
            f_scr[d, j * LANES:(j + 1) * LANES, :] = expand(fpow[e])
        for t in range(CHUNK):
            e = (t + 1) if d == 0 else (CHUNK - t)
            w = _cmul(pw[e], c_r, c_i) * conj
            e_scr[d, t * LANES:(t + 1) * LANES, :] = expand(w.astype(BF16))
        lag.append([jnp.where(diag, _dot_nt(fp, cm), 0.0) for fp in fpow])

    for j in range(CHUNK):
        for t in range(CHUNK):
            k = t - j
            tile = lag[0][k] if k > 0 else (lag[1][-k] if k < 0 else lag[0][0] + lag[1][0])
            k_scr[j * LANES:(j + 1) * LANES, t * LANES:(t + 1) * LANES] = tile.astype(BF16)

    xcat = jnp.concatenate(
        [h_ref[pl.ds(j, nc, stride=CHUNK), :].astype(BF16) for j in range(CHUNK)], axis=1)

    for d, scr, sw_scr in ((0, sf_scr, swf_scr), (1, sb_scr, swb_scr)):
        loc_all = _dot(xcat, f_scr[d])
        for k in range(gpb):
            loc = loc_all[:, k * LANES:(k + 1) * LANES]
            scr[k] = loc
            sw_scr[k] = _swap(loc)
    y_within = _dot(xcat, k_scr[...])

    sgn8 = jnp.where(lo8, -1.0, 1.0)

    def dup(z):
        zs = _swap(z)
        return jnp.where(lo8, z, zs), jnp.where(lo8, zs, z)

    def scan(scr, sw_scr, d, reverse):
        a_re, a_im = dup(decay[d])
        a_sg = a_im * sgn8
        a_r = [a_re[k:k + 1, :] for k in range(gpb)]
        a_i = [a_sg[k:k + 1, :] for k in range(gpb)]

        def rows_of(i):
            c = (cpv - 1 - i) if reverse else i
            return pl.ds(c * n_virt, n_virt)

        def body(i, carry):
            st, sw = carry
            rows = rows_of(i)
            new_st, new_sw = [], []
            for k in range(gpb):
                loc = scr[k, rows, :]
                loc_sw = sw_scr[k, rows, :]
                scr[k, rows, :] = st[k]
                new_st.append(a_r[k] * st[k] + a_i[k] * sw[k] + loc)
                new_sw.append(a_r[k] * sw[k] - a_i[k] * st[k] + loc_sw)
            return tuple(new_st), tuple(new_sw)

        st0 = tuple(s0_ref[d, :, k * LANES:(k + 1) * LANES] for k in range(gpb))
        sw0 = tuple(_swap(s) for s in st0)
        carry = (st0, sw0)
        for i in range(cpv):
            carry = body(i, carry)
        fin = carry[0]
        if n_seg == 1:
            for k in range(gpb):
                sfin_ref[d, :, k * LANES:(k + 1) * LANES] = fin[k]
            return

        p = decay[d]
        for _ in range(cpv.bit_length() - 1):
            p_re, p_im = dup(p)
            p = _cmul(p, p_re, p_im * sgn8)
        v_re, v_im = dup(p)
        v_sg = v_im * sgn8
        seg = lax.broadcasted_iota(jnp.int32, (n_virt, LANES), 0) & (n_seg - 1)
        has_pred = seg != ((n_seg - 1) if reverse else 0)
        shift = (n_virt - 1) if reverse else 1
        cin = []
        for k in range(gpb):
            ck = jnp.zeros((n_virt, LANES), F32)
            for _ in range(n_seg - 1):
                nxt = fin[k] + ck * v_re[k:k + 1, :] + _swap(ck) * v_sg[k:k + 1, :]
                ck = jnp.where(has_pred, pltpu.roll(nxt, shift, 0), 0.0)
            cin.append(ck)
        cin_sw = [_swap(x) for x in cin]

        def fix(i, carry):
            q_re, q_im = carry
            rows = rows_of(i)
            q_sg = q_im * sgn8
            for k in range(gpb):
                scr[k, rows, :] += cin[k] * q_re[k:k + 1, :] + cin_sw[k] * q_sg[k:k + 1, :]
            return q_re * a_re - q_im * a_im, q_re * a_im + q_im * a_re

        q = (jnp.ones((gpb, LANES), F32), jnp.zeros((gpb, LANES), F32))
        for i in range(cpv):
            q = fix(i, q)

    scan(sf_scr, swf_scr, 0, False)
    scan(sb_scr, swb_scr, 1, True)

    s_f = jnp.concatenate([sf_scr[k].astype(BF16) for k in range(gpb)], axis=1)
    s_b = jnp.concatenate([sb_scr[k].astype(BF16) for k in range(gpb)], axis=1)
    yall = y_within + _dot_nt(s_f, e_scr[0]) + _dot_nt(s_b, e_scr[1])
    dsk = dsk_ref[...]
    for t in range(CHUNK):
        rows = pl.ds(t, nc, stride=CHUNK)
        y_ref[rows, :] = yall[:, t * LANES:(t + 1) * LANES] + h_ref[rows, :] * dsk


def _s5_params(lam_re, lam_im, log_dt, b_re, b_im, c_re, c_im):
    lamr = jnp.concatenate([lam_re, lam_re], axis=-1).astype(F32)
    lami = jnp.concatenate([lam_im, lam_im], axis=-1).astype(F32)
    ldt = jnp.broadcast_to(log_dt.astype(F32)[..., None], lamr.shape)
    bt = jnp.concatenate([b_re.transpose(0, 1, 3, 2), b_im.transpose(0, 1, 3, 2)], axis=-1)
    cp = jnp.concatenate([c_re, c_im], axis=-1)
    return (lamr, lami, ldt, bt.reshape(2, D_MODEL, LANES).astype(F32), cp.reshape(2, D_MODEL, LANES).astype(F32))


def _s5(h, params, d_skip, s0, n_seg):
    lamr, lami, ldt, bt, cp = params
    _, cpv, n_virt, _, _ = h.shape
    assert cpv & (cpv - 1) == 0 and n_virt % SUBLANES == 0 and n_seg & (n_seg - 1) == 0
    nc = cpv * n_virt
    ntok = nc * CHUNK
    hspec = pl.BlockSpec((None, ntok, LANES), lambda g: (g, 0, 0))
    kdim = CHUNK * LANES
    gspec = pl.BlockSpec((2, GROUPS_PER_BLOCK, LANES), lambda g: (0, g, 0))
    rspec = pl.BlockSpec((2, LANES, LANES), lambda g: (0, g, 0))
    sspec = pl.BlockSpec((None, 2, n_virt, STATE_LANES), lambda g: (g, 0, 0, 0))
    state_scr = pltpu.VMEM((GROUPS_PER_BLOCK, nc, LANES), F32)
    out_specs = [hspec]
    out_shape = [jax.ShapeDtypeStruct((N_GROUP_BLOCKS, ntok, LANES), F32)]
    if n_seg == 1:
        out_specs.append(sspec)
        out_shape.append(jax.ShapeDtypeStruct((N_GROUP_BLOCKS, 2, n_virt, STATE_LANES), F32))
    outs = pl.pallas_call(
        functools.partial(_s5_kernel, n_virt, n_seg),
        grid=(N_GROUP_BLOCKS,),
        in_specs=[
            hspec,
            gspec, gspec, gspec, rspec, rspec,
            pl.BlockSpec((1, LANES), lambda g: (0, g)),
            sspec,
        ],
        out_specs=out_specs,
        out_shape=out_shape,
        scratch_shapes=[
            pltpu.VMEM((2, kdim, STATE_LANES), BF16),
            pltpu.VMEM((2, kdim, STATE_LANES), BF16),
            pltpu.VMEM((kdim, kdim), BF16),
            state_scr, state_scr, state_scr, state_scr,
        ],
        compiler_params=_cparams(("arbitrary",)),
        name="s5_chunked_scan",
    )(h.reshape(N_GROUP_BLOCKS, ntok, LANES), lamr, lami, ldt, bt, cp, d_skip, s0)
    y = outs[0].reshape(h.shape)
    return (y, outs[1]) if n_seg == 1 else (y, None)


def _state_to_blocks(s):
    b = s.shape[0]
    s = s.reshape(b, 2, 2, N_GROUP_BLOCKS, GROUPS_PER_BLOCK, STATE_DIM)
    return s.transpose(3, 1, 0, 4, 2, 5).reshape(N_GROUP_BLOCKS, 2, b, STATE_LANES)


def _blocks_to_state(s):
    b = s.shape[2]
    s = s.reshape(N_GROUP_BLOCKS, 2, b, GROUPS_PER_BLOCK, 2, STATE_DIM)
    return s.transpose(2, 1, 4, 0, 3, 5).reshape(b, 2, 2, N_GROUPS, STATE_DIM)


def _rope_tables(n_tokens):
    pos = np.arange(n_tokens)
    n_freq = HEAD_DIM // 4
    freqs = ROPE_BASE ** (-np.arange(n_freq, dtype=np.float64) / n_freq)
    ang_r = (pos // GRID_W)[:, None] * freqs
    ang_c = (pos % GRID_W)[:, None] * freqs
    cos_h = np.concatenate([np.cos(ang_r), np.cos(ang_r), np.cos(ang_c), np.cos(ang_c)], axis=1)
    sin_h = np.concatenate([-np.sin(ang_r), np.sin(ang_r), -np.sin(ang_c), np.sin(ang_c)], axis=1)
    cos_t = np.concatenate([np.ones((TOKEN_TILE, LANES)), np.tile(cos_h, (1, 2))], axis=0)
    sin_t = np.concatenate([np.zeros((TOKEN_TILE, LANES)), np.tile(sin_h, (1, 2))], axis=0)
    return jnp.asarray(cos_t, F32), jnp.asarray(sin_t, F32)


def kernel(x_prompt, x_sample, cache_k, cache_v, state_ssm, c, c_ctx, norm1_g, norm2_g, w_mod, b_mod,
           w_qkv, w_o, attn_sink, ssm_lam_re, ssm_lam_im, ssm_log_dt, ssm_b_re, ssm_b_im, ssm_c_re,
           ssm_c_im, ssm_d, glu_w_a, glu_w_b, mlp_w1, mlp_w2, final_norm_g):
    bp, lp, _ = x_prompt.shape
    bx, lx, _ = x_sample.shape
    assert lx % TOKEN_TILE == 0 and (bp * lp) % TOKEN_TILE == 0
    tiles_per_lat = lx // TOKEN_TILE

    xp = x_prompt.reshape(bp * lp, D_MODEL)
    xx = x_sample.reshape(bx * lx, D_MODEL)

    cvecs = jnp.zeros((8, D_MODEL), F32).at[0].set(c_ctx).at[1:1 + bx].set(c)
    mod = _modulation(cvecs, w_mod, b_mod)

    ctx_row = lambda i: 0
    lat_row = lambda i: 1 + i // tiles_per_lat
    ctx_rope = lambda i: 0
    lat_rope = lambda i: 1 + i % tiles_per_lat

    cos_t, sin_t = _rope_tables(lx)
    wqkv = w_qkv[0].astype(BF16)
    g1 = norm1_g[0].reshape(1, D_MODEL)
    sink = attn_sink[0].astype(F32)
    qp, krp, vrp, kp, vp = _qkv(xp, mod[0], ctx_row, g1, wqkv, cos_t, sin_t, ctx_rope, lp)
    qx, krx, vrx = _qkv(xx, mod[0], lat_row, g1, wqkv, cos_t, sin_t, lat_rope, 0)
    op = _ctx_attention(sink, qp, krp, vrp, bp, lp)
    rep = lambda t: jnp.tile(t[:, 0].transpose(0, 2, 1, 3), (1, 1, 1, Q_PER_KV)).astype(BF16)
    ox = _lat_attention(sink, qx, krx, vrx, rep(cache_k), rep(cache_v), bx, lx)

    wo = w_o[0].astype(BF16)
    w1 = mlp_w1.astype(BF16)
    w2 = mlp_w2.astype(BF16)
    g2 = norm2_g.reshape(-1, 1, D_MODEL)
    gn = norm1_g[1].reshape(1, D_MODEL)
    vpt_p = TOKEN_TILE // lp
    vpt_x = 1
    n_seg_x = tiles_per_lat
    xp, hp = _post(xp, op, mod[0], ctx_row, g2[0], wo, None, w1, w2, 0, vpt_p, mod_next=mod[1], g_next=gn)
    xx, hx = _post(xx, ox, mod[0], lat_row, g2[0], wo, None, w1, w2, 0, vpt_x, mod_next=mod[1], g_next=gn)

    params = _s5_params(ssm_lam_re[0], ssm_lam_im[0], ssm_log_dt[0], ssm_b_re[0], ssm_b_im[0],
                        ssm_c_re[0], ssm_c_im[0])
    dsk = ssm_d[0].astype(F32).reshape(1, D_MODEL)
    s0p = jnp.zeros((N_GROUP_BLOCKS, 2, bp, STATE_LANES), F32)
    sx = _state_to_blocks(state_ssm[:, 0].astype(F32))
    s0x = jnp.zeros((N_GROUP_BLOCKS, 2, bx, n_seg_x, STATE_LANES), F32)
    s0x = s0x.at[:, 0, :, 0].set(sx[:, 0]).at[:, 1, :, n_seg_x - 1].set(sx[:, 1])
    s0x = s0x.reshape(N_GROUP_BLOCKS, 2, bx * n_seg_x, STATE_LANES)
    yp, sfin = _s5(hp, params, dsk, s0p, 1)
    yx, _ = _s5(hx, params, dsk, s0x, n_seg_x)
    new_state = _blocks_to_state(sfin)[:, None]

    wa = glu_w_a[0].astype(BF16)
    wb = glu_w_b[0].astype(BF16)
    fg = final_norm_g.reshape(1, D_MODEL)
    (yp_out,) = _post(xp, yp, mod[1], ctx_row, g2[1], wa, wb, w1, w2, 1, vpt_p, final_g=fg)
    (yx_out,) = _post(xx, yx, mod[1], lat_row, g2[1], wa, wb, w1, w2, 1, vpt_x, final_g=fg)

    to_cache = lambda t: t.reshape(bp, N_KV_HEADS, HEAD_DIM, lp).transpose(0, 3, 1, 2)[:, None]
    new_k = to_cache(kp)
    new_v = to_cache(vp)
    return (yp_out.reshape(bp, lp, D_MODEL), yx_out.reshape(bx, lx, D_MODEL), new_k, new_v, new_state)
```

```python
import functools
import math

import numpy as np
import jax
import jax.numpy as jnp
from jax import lax
from jax.experimental import pallas as pl
from jax.experimental.pallas import tpu as pltpu

F32 = jnp.float32
BF16 = jnp.bfloat16

D_MODEL = 1024
N_HEADS = 16
N_KV_HEADS = 4
HEAD_DIM = 64
Q_PER_KV = N_HEADS // N_KV_HEADS
KV_DIM = N_KV_HEADS * HEAD_DIM
QKV_DIM = D_MODEL + 2 * KV_DIM
BLOCK = 128
GRID_W = 64
ROPE_BASE = 10000.0
ATTN_SCALE = HEAD_DIM ** -0.5
N_GROUPS = 64
GROUP_CH = 16
STATE_DIM = 64
D_FF = 4 * D_MODEL
N_MOD = 6
RMS_EPS = 1e-6
NEG_INF = -1e30

LANES = 128
SUBLANES = 8
GROUPS_PER_BLOCK = LANES // GROUP_CH
N_GROUP_BLOCKS = N_GROUPS // GROUPS_PER_BLOCK
STATE_LANES = GROUPS_PER_BLOCK * 2 * STATE_DIM
CHUNK = SUBLANES
TOKEN_TILE = 512
FF_TILE = 1024
POST_SUBTILES = 2
QKV_SUBTILES = 2
VMEM_LIMIT = 56 * 1024 * 1024


def _cparams(semantics):
    return pltpu.CompilerParams(dimension_semantics=semantics, vmem_limit_bytes=VMEM_LIMIT)


def _rms(x):
    return x * lax.rsqrt(jnp.mean(x * x, axis=-1, keepdims=True) + RMS_EPS)


def _dot(a, b):
    return jnp.dot(a, b, preferred_element_type=F32)


def _dot_nt(a, b):
    return lax.dot_general(a, b, (((1,), (1,)), ((), ())), preferred_element_type=F32)


def _mod_kernel(cv_ref, w_ref, b_ref, o_ref):
    cv = cv_ref[...]
    s = (cv * jax.nn.sigmoid(cv)).astype(BF16)
    o_ref[0] = _dot(s, w_ref[0].astype(BF16)) + b_ref[0]


def _modulation(cvecs, w_mod, b_mod):
    depth = w_mod.shape[0]
    out = pl.pallas_call(
        _mod_kernel,
        grid=(depth, N_MOD),
        in_specs=[
            pl.BlockSpec((8, D_MODEL), lambda l, j: (0, 0)),
            pl.BlockSpec((1, D_MODEL, D_MODEL), lambda l, j: (l, 0, j)),
            pl.BlockSpec((1, 1, D_MODEL), lambda l, j: (l, 0, j)),
        ],
        out_specs=pl.BlockSpec((1, 8, D_MODEL), lambda l, j: (l, 0, j)),
        out_shape=jax.ShapeDtypeStruct((depth, 8, N_MOD * D_MODEL), F32),
        compiler_params=_cparams(("arbitrary", "arbitrary")),
        name="modulation",
    )(cvecs, w_mod, b_mod.reshape(depth, 1, N_MOD * D_MODEL))
    return out.reshape(depth, 8, N_MOD, D_MODEL)


def _head_pair(blk, odd):
    lo = lax.broadcasted_iota(jnp.int32, blk.shape, 1) < HEAD_DIM
    other = pltpu.roll(blk, HEAD_DIM, 1)
    return (jnp.where(lo, other, blk) if odd else jnp.where(lo, blk, other)).astype(BF16)


def _qkv_kernel(cache_seq, rope, x_ref, mod_ref, g_ref, w_ref, *refs):
    refs = list(refs)
    cos_ref, sin_ref = (refs.pop(0), refs.pop(0)) if rope else (None, None)
    q_ref, krep_ref, vrep_ref = refs[:3]
    kv_refs = refs[3:]
    sub = TOKEN_TILE // QKV_SUBTILES
    starts = [k * sub for k in range(QKV_SUBTILES)]

    def project(r0):
        h = _rms(x_ref[r0:r0 + sub, :]) * g_ref[...] * (1.0 + mod_ref[1:2, :]) + mod_ref[0:1, :]
        return _dot(h.astype(BF16), w_ref[...])

    def emit_cache(ref, blk, r0, c0):
        for s in range(sub // cache_seq):
            ref[r0 // cache_seq + s, c0:c0 + LANES, :] = blk[s * cache_seq:(s + 1) * cache_seq, :].T

    def finish(r0, qkv):
        rows = slice(r0, r0 + sub)
        if rope:
            cos = cos_ref[rows, :]
            sin = sin_ref[rows, :]
            first = (lax.broadcasted_iota(jnp.int32, cos.shape, 1) & 31) < 16
        for blk in range((D_MODEL + KV_DIM) // LANES):
            r = qkv[:, blk * LANES:(blk + 1) * LANES]
            if rope:
                partner = jnp.where(first, pltpu.roll(r, LANES - 16, 1), pltpu.roll(r, 16, 1))
                r = r * cos + partner * sin
            if blk < D_MODEL // LANES:
                q_ref[rows, blk * LANES:(blk + 1) * LANES] = (r * ATTN_SCALE).astype(BF16)
            else:
                c0 = blk * LANES - D_MODEL
                if cache_seq:
                    emit_cache(kv_refs[0], r, r0, c0)
                for half in range(2):
                    krep_ref[c0 // HEAD_DIM + half, rows, :] = _head_pair(r, half)
        v = qkv[:, D_MODEL + KV_DIM:]
        for c0 in range(0, KV_DIM, LANES):
            blk = v[:, c0:c0 + LANES]
            if cache_seq:
                emit_cache(kv_refs[1], blk, r0, c0)
            for half in range(2):
                vrep_ref[c0 // HEAD_DIM + half, rows, :] = _head_pair(blk, half)

    for r0, qkv in zip(starts, [project(r0) for r0 in starts]):
        finish(r0, qkv)


def _qkv(x, mod, mod_row, g, w_qkv, rope, cache_seq):
    ntok = x.shape[0]
    nt = ntok // TOKEN_TILE
    emit_kv = cache_seq > 0
    assert not emit_kv or (TOKEN_TILE // QKV_SUBTILES) % cache_seq == 0
    rep_spec = pl.BlockSpec((N_KV_HEADS, TOKEN_TILE, LANES), lambda i: (0, i, 0))
    rep_shape = jax.ShapeDtypeStruct((N_KV_HEADS, ntok, LANES), BF16)
    in_specs = [
        pl.BlockSpec((TOKEN_TILE, D_MODEL), lambda i: (i, 0)),
        pl.BlockSpec((None, N_MOD, D_MODEL), lambda i: (mod_row(i), 0, 0)),
        pl.BlockSpec((1, D_MODEL), lambda i: (0, 0)),
        pl.BlockSpec((D_MODEL, QKV_DIM), lambda i: (0, 0)),
    ]
    args = [x, mod, g, w_qkv]
    if rope is not None:
        cos_t, sin_t, rope_blk = rope
        in_specs += [pl.BlockSpec((TOKEN_TILE, LANES), lambda i: (rope_blk(i), 0))] * 2
        args += [cos_t, sin_t]
    out_specs = [pl.BlockSpec((TOKEN_TILE, D_MODEL), lambda i: (i, 0)), rep_spec, rep_spec]
    out_shape = [jax.ShapeDtypeStruct((ntok, D_MODEL), BF16), rep_shape, rep_shape]
    if emit_kv:
        spt = TOKEN_TILE // cache_seq
        out_specs += [pl.BlockSpec((spt, KV_DIM, cache_seq), lambda i: (i, 0, 0))] * 2
        out_shape += [jax.ShapeDtypeStruct((ntok // cache_seq, KV_DIM, cache_seq), F32)] * 2
    return pl.pallas_call(
        functools.partial(_qkv_kernel, cache_seq, rope is not None),
        grid=(nt,),
        in_specs=in_specs,
        out_specs=out_specs,
        out_shape=out_shape,
        compiler_params=_cparams(("arbitrary",)),
        name="norm_qkv_rope",
    )(*args)


def _group_scores(q_ref, kv, key_parts, bias):
    nq = q_ref.shape[0]
    lo = lax.broadcasted_iota(jnp.int32, (nq, LANES), 1) < HEAD_DIM
    zero = jnp.zeros((), BF16)
    rows = []
    for b in range(KV_DIM // LANES):
        blk = q_ref[:, kv * KV_DIM + b * LANES:kv * KV_DIM + (b + 1) * LANES]
        rows += [jnp.where(lo, blk, zero), jnp.where(lo, zero, blk)]
    q4 = jnp.concatenate(rows, axis=0)
    parts = [_dot_nt(q4, keys) for keys in key_parts]
    if bias is not None:
        s0 = parts[0].reshape(Q_PER_KV, nq, -1) + bias[None]
        parts[0] = s0.reshape(Q_PER_KV * nq, -1)
    return parts


def _group_softmax(parts, sink_ref, kv):
    nq = parts[0].shape[0] // Q_PER_KV
    sink = jnp.concatenate(
        [jnp.full((nq, LANES), sink_ref[kv * Q_PER_KV + g], F32) for g in range(Q_PER_KV)], axis=0)
    blocks = [[s[:, j:j + LANES] for j in range(0, s.shape[1], LANES)] for s in parts]
    fold = None
    for b in sum(blocks, []):
        fold = b if fold is None else jnp.maximum(fold, b)
    m = jnp.maximum(jnp.max(fold, axis=-1, keepdims=True), sink)
    probs = [[jnp.exp(b - m) for b in bs] for bs in blocks]
    fold = None
    for p in sum(probs, []):
        fold = p if fold is None else fold + p
    den = jnp.sum(fold, axis=-1, keepdims=True) + jnp.exp(sink - m)
    return [jnp.concatenate(ps, axis=1).astype(BF16) for ps in probs], 1.0 / den


def _group_output(probs, inv_den, value_parts, o_ref, kv):
    nq = probs[0].shape[0] // Q_PER_KV
    r = _dot(probs[0], value_parts[0])
    for p, v in zip(probs[1:], value_parts[1:]):
        r = r + _dot(p, v)
    r = r * inv_den
    lo = lax.broadcasted_iota(jnp.int32, (nq, LANES), 1) < HEAD_DIM
    for b in range(KV_DIM // LANES):
        pair = jnp.where(lo, r[2 * b * nq:(2 * b + 1) * nq], r[(2 * b + 1) * nq:(2 * b + 2) * nq])
        o_ref[:, kv * KV_DIM + b * LANES:kv * KV_DIM + (b + 1) * LANES] = pair.astype(BF16)


def _attend(q_ref, sink_ref, o_ref, keys_of, values_of, bias):
    s_next = _group_scores(q_ref, 0, keys_of(0), bias)
    for kv in range(N_KV_HEADS):
        s = s_next
        if kv + 1 < N_KV_HEADS:
            s_next = _group_scores(q_ref, kv + 1, keys_of(kv + 1), bias)
        probs, inv_den = _group_softmax(s, sink_ref, kv)
        _group_output(probs, inv_den, values_of(kv), o_ref, kv)


def _ctx_attn_kernel(sink_ref, q_ref, k_ref, v_ref, o_ref):
    _attend(q_ref, sink_ref, o_ref, lambda kv: [k_ref[kv]], lambda kv: [v_ref[kv]], None)


def _ctx_attention(sink, q, krep, vrep, n_batch, seq):
    rep_spec = pl.BlockSpec((N_KV_HEADS, seq, LANES), lambda b: (0, b, 0))
    return pl.pallas_call(
        _ctx_attn_kernel,
        grid=(n_batch,),
        in_specs=[
            pl.BlockSpec(memory_space=pltpu.SMEM),
            pl.BlockSpec((seq, D_MODEL), lambda b: (b, 0)),
            rep_spec, rep_spec,
        ],
        out_specs=pl.BlockSpec((seq, D_MODEL), lambda b: (b, 0)),
        out_shape=jax.ShapeDtypeStruct((n_batch * seq, D_MODEL), BF16),
        compiler_params=_cparams(("arbitrary",)),
        name="context_attention",
    )(sink, q, krep, vrep)


def _window_start(n, seq):
    return jnp.clip((n - 1) * BLOCK, 0, seq - 3 * BLOCK)


def _band_bias():
    r = np.arange(BLOCK)[:, None]
    j = np.arange(3 * BLOCK)[None, :]
    out = [np.where(np.abs(j - d * BLOCK - r) <= BLOCK, 0.0, NEG_INF) for d in range(3)]
    return jnp.asarray(np.stack(out), F32)


def _lat_attn_kernel(seq, sink_ref, q_ref, k_ref, v_ref, ck_ref, cv_ref, bias_ref, o_ref):
    win = 3 * BLOCK
    start = pl.multiple_of(_window_start(pl.program_id(1), seq), BLOCK)
    keys_of = lambda kv: [k_ref[kv, pl.ds(start, win), :], ck_ref[kv]]
    values_of = lambda kv: [v_ref[kv, pl.ds(start, win), :], cv_ref[kv]]
    _attend(q_ref, sink_ref, o_ref, keys_of, values_of, bias_ref[...])


def _lat_attention(sink, q, krep, vrep, ckrep, cvrep, n_batch, seq):
    nb = seq // BLOCK
    past = ckrep.shape[2]
    rep_spec = pl.BlockSpec((N_KV_HEADS, seq, LANES), lambda b, n: (0, b, 0))
    crep_spec = pl.BlockSpec((None, N_KV_HEADS, past, LANES), lambda b, n: (b, 0, 0, 0))
    return pl.pallas_call(
        functools.partial(_lat_attn_kernel, seq),
        grid=(n_batch, nb),
        in_specs=[
            pl.BlockSpec(memory_space=pltpu.SMEM),
            pl.BlockSpec((BLOCK, D_MODEL), lambda b, n: (b * nb + n, 0)),
            rep_spec, rep_spec, crep_spec, crep_spec,
            pl.BlockSpec((None, BLOCK, 3 * BLOCK), lambda b, n: (n - _window_start(n, seq) // BLOCK, 0, 0)),
        ],
        out_specs=pl.BlockSpec((BLOCK, D_MODEL), lambda b, n: (b * nb + n, 0)),
        out_shape=jax.ShapeDtypeStruct((n_batch * seq, D_MODEL), BF16),
        compiler_params=_cparams(("arbitrary", "arbitrary")),
        name="latent_attention",
    )(sink, q, krep, vrep, ckrep, cvrep, _band_bias())


def _gelu_tanh(x):
    c = math.sqrt(2.0 / math.pi)
    return x * (0.5 * (1.0 + jnp.tanh(c * (x + 0.044715 * (x * x * x)))))


def _post_kernel(is_attn, emit_next, final, vpt, *refs):
    rows_per_v = TOKEN_TILE // vpt
    refs = list(refs)
    x_ref, mix_ref, mod_ref, g2_ref, wa_ref = refs[:5]
    refs = refs[5:]
    wb_ref = None if is_attn else refs.pop(0)
    w1_ref, w2_ref = refs[:2]
    refs = refs[2:]
    modn_ref = gn_ref = fg_ref = hn_ref = None
    if emit_next:
        modn_ref, gn_ref = refs[:2]
        refs = refs[2:]
    if final:
        fg_ref = refs.pop(0)
    xo_ref = refs.pop(0)
    if emit_next:
        hn_ref = refs.pop(0)
    assert not refs

    sub = TOKEN_TILE // POST_SUBTILES
    assert rows_per_v % sub == 0
    n_chunk = sub // CHUNK
    starts = [k * sub for k in range(POST_SUBTILES)]

    def s5_slot(r0):
        return r0 // rows_per_v, (r0 % rows_per_v) // CHUNK

    def project(r0):
        if is_attn:
            return _dot(mix_ref[r0:r0 + sub, :], wa_ref[...])
        s, c0 = s5_slot(r0)
        y = jnp.concatenate(
            [mix_ref[g, c0:c0 + n_chunk, s].reshape(sub, LANES) for g in range(N_GROUP_BLOCKS)], axis=1)
        yg = _gelu_tanh(y).astype(BF16)
        return _dot(yg, wa_ref[...]) * jax.nn.sigmoid(_dot(yg, wb_ref[...]))

    def prologue(r0, mix):
        x1 = x_ref[r0:r0 + sub, :] + mod_ref[2:3, :] * mix
        h2 = _rms(x1) * g2_ref[...] * (1.0 + mod_ref[4:5, :]) + mod_ref[3:4, :]
        return x1, h2.astype(BF16)

    def mlp(h2):
        acc = None
        for c in range(D_FF // FF_TILE):
            a = jnp.maximum(_dot(h2, w1_ref[:, c * FF_TILE:(c + 1) * FF_TILE]), 0.0)
            t = _dot((a * a).astype(BF16), w2_ref[c * FF_TILE:(c + 1) * FF_TILE, :])
            acc = t if acc is None else acc + t
        return acc

    def epilogue(r0, x1, acc):
        x2 = x1 + mod_ref[5:6, :] * acc
        if emit_next:
            hn = _rms(x2) * gn_ref[...] * (1.0 + modn_ref[1:2, :]) + modn_ref[0:1, :]
            s, c0 = s5_slot(r0)
            for g in range(N_GROUP_BLOCKS):
                blk = hn[:, g * LANES:(g + 1) * LANES]
                hn_ref[g, c0:c0 + n_chunk, s] = blk.reshape(n_chunk, CHUNK, LANES)
        xo_ref[r0:r0 + sub, :] = _rms(x2) * fg_ref[...] if final else x2

    mixes = [project(r0) for r0 in starts]
    pro = [prologue(r0, mix) for r0, mix in zip(starts, mixes)]
    accs = [mlp(h2) for _, h2 in pro]
    for r0, (x1, _), acc in zip(starts, pro, accs):
        epilogue(r0, x1, acc)


def _post(x, mix, mod, mod_row, g2, w_a, w_b, w1, w2, layer, vpt, mod_next=None, g_next=None, final_g=None):
    is_attn = w_b is None
    emit_next = mod_next is not None
    final = final_g is not None
    ntok = x.shape[0]
    nt = ntok // TOKEN_TILE
    cpv = TOKEN_TILE // (vpt * CHUNK)
    n_virt = nt * vpt
    tile = pl.BlockSpec((TOKEN_TILE, D_MODEL), lambda i: (i, 0))
    row = pl.BlockSpec((1, D_MODEL), lambda i: (0, 0))
    modspec = pl.BlockSpec((None, N_MOD, D_MODEL), lambda i: (mod_row(i), 0, 0))
    resident = lambda shape: pl.BlockSpec(shape, lambda i: (0, 0), pipeline_mode=pl.Buffered(1))
    wsq = resident((D_MODEL, D_MODEL))
    gtile = pl.BlockSpec((N_GROUP_BLOCKS, cpv, vpt, CHUNK, LANES), lambda i: (0, 0, i, 0, 0))
    in_specs = [tile, tile if is_attn else gtile, modspec, row, wsq]
    args = [x, mix, mod, g2, w_a]
    if not is_attn:
        in_specs.append(wsq)
        args.append(w_b)
    in_specs += [pl.BlockSpec((None, D_MODEL, D_FF), lambda i: (layer, 0, 0), pipeline_mode=pl.Buffered(1)),
                 pl.BlockSpec((None, D_FF, D_MODEL), lambda i: (layer, 0, 0), pipeline_mode=pl.Buffered(1))]
    args += [w1, w2]
    if emit_next:
        in_specs += [modspec, row]
        args += [mod_next, g_next]
    if final:
        in_specs.append(row)
        args.append(final_g)
    out_specs = [tile]
    out_shape = [jax.ShapeDtypeStruct((ntok, D_MODEL), F32)]
    if emit_next:
        out_specs.append(gtile)
        out_shape.append(jax.ShapeDtypeStruct((N_GROUP_BLOCKS, cpv, n_virt, CHUNK, LANES), F32))
    return pl.pallas_call(
        functools.partial(_post_kernel, is_attn, emit_next, final, vpt),
        grid=(nt,),
        in_specs=in_specs,
        out_specs=out_specs,
        out_shape=out_shape,
        compiler_params=_cparams(("arbitrary",)),
        name="attn_proj_mlp" if is_attn else "glu_mlp_final",
    )(*args)


def _swap(x):
    return pltpu.roll(x, LANES // 2, 1)


def _cmul(z, w_r, w_i):
    return z * w_r + _swap(z) * w_i


def _multiplier(z, lo):
    zs = _swap(z)
    return jnp.where(lo, z, zs), jnp.where(lo, -zs, z)


def _rep_rows(x):
    return jnp.concatenate(
        [jnp.broadcast_to(x[g:g + 1, :], (GROUP_CH, LANES)) for g in range(GROUPS_PER_BLOCK)], axis=0)


def _s5_kernel(n_virt, n_seg, h_ref, lamr_ref, lami_ref, ldt_ref, bt_ref, cp_ref, dsk_ref, s0_ref, *refs):
    if n_seg == 1:
        y_ref, sfin_ref = refs[:2]
        refs = refs[2:]
    else:
        y_ref, sfin_ref = refs[0], None
        refs = refs[1:]
    f_scr, e_scr, k_scr, sf_scr, sb_scr, swf_scr, swb_scr = refs
    ntok = h_ref.shape[0]
    nc = ntok // CHUNK
    cpv = nc // n_virt
    gpb = GROUPS_PER_BLOCK

    lo8 = lax.broadcasted_iota(jnp.int32, (gpb, LANES), 1) < STATE_DIM
    lo = lax.broadcasted_iota(jnp.int32, (LANES, LANES), 1) < STATE_DIM
    conj = jnp.where(lo, 1.0, -1.0)
    row_g = lax.broadcasted_iota(jnp.int32, (LANES, STATE_LANES), 0) >> 4
    col_g = lax.broadcasted_iota(jnp.int32, (LANES, STATE_LANES), 1) >> 7
    diag_wide = row_g == col_g
    diag = (lax.broadcasted_iota(jnp.int32, (LANES, LANES), 0) >> 4) == (
        lax.broadcasted_iota(jnp.int32, (LANES, LANES), 1) >> 4)

    def expand(w):
        return jnp.where(diag_wide, jnp.concatenate([w] * gpb, axis=1), jnp.zeros((), BF16))

    decay = []
    lag = []
    for d in range(2):
        lam_r = lamr_ref[d]
        lam_i = lami_ref[d]
        dt = jnp.exp(ldt_ref[d])
        mag = jnp.exp(lam_r * dt)
        ang = lam_i * dt
        a_r = mag * jnp.cos(ang)
        a_im = mag * jnp.sin(ang)
        a_i = jnp.where(lo8, -a_im, a_im)
        den = lam_r * lam_r + lam_i * lam_i
        num = jnp.where(lo8, a_r - 1.0, a_im)
        f = _cmul(num, lam_r / den, jnp.where(lo8, lam_i, -lam_i) / den)
        pw = [jnp.where(lo8, 1.0, 0.0)]
        for _ in range(CHUNK):
            pw.append(_cmul(pw[-1], a_r, a_i))
        decay.append(pw[CHUNK])
        pw = [_rep_rows(p) for p in pw]
        f_r, f_i = _multiplier(_rep_rows(f), lo)
        bb_r, bb_i = _multiplier(_cmul(bt_ref[d], f_r, f_i), lo)
        c_r, c_i = _multiplier(cp_ref[d], lo)
        cm = (cp_ref[d] * conj).astype(BF16)
        fpow = [_cmul(p, bb_r, bb_i).astype(BF16) for p in pw[:CHUNK]]
        for j in range(CHUNK):
            e = (CHUNK - 1 - j) if d == 0 else j
            f_scr[d, j * LANES:(j + 1) * LANES, :] = expand(fpow[e])
        for t in range(CHUNK):
            e = (t + 1) if d == 0 else (CHUNK - t)
            w = _cmul(pw[e], c_r, c_i) * conj
            e_scr[d, t * LANES:(t + 1) * LANES, :] = expand(w.astype(BF16))
        lag.append([jnp.where(diag, _dot_nt(fp, cm), 0.0) for fp in fpow])

    for j in range(CHUNK):
        for t in range(CHUNK):
            k = t - j
            tile = lag[0][k] if k > 0 else (lag[1][-k] if k < 0 else lag[0][0] + lag[1][0])
            k_scr[j * LANES:(j + 1) * LANES, t * LANES:(t + 1) * LANES] = tile.astype(BF16)

    xcat = jnp.concatenate(
        [h_ref[pl.ds(j, nc, stride=CHUNK), :].astype(BF16) for j in range(CHUNK)], axis=1)

    for d, scr, sw_scr in ((0, sf_scr, swf_scr), (1, sb_scr, swb_scr)):
        loc_all = _dot(xcat, f_scr[d])
        for k in range(gpb):
            loc = loc_all[:, k * LANES:(k + 1) * LANES]
            scr[k] = loc
            sw_scr[k] = _swap(loc)
    y_within = _dot(xcat, k_scr[...])

    sgn8 = jnp.where(lo8, -1.0, 1.0)

    def dup(z):
        zs = _swap(z)
        return jnp.where(lo8, z, zs), jnp.where(lo8, zs, z)

    def scan(scr, sw_scr, d, reverse):
        a_re, a_im = dup(decay[d])
        a_sg = a_im * sgn8
        a_r = [a_re[k:k + 1, :] for k in range(gpb)]
        a_i = [a_sg[k:k + 1, :] for k in range(gpb)]

        def rows_of(i):
            c = (cpv - 1 - i) if reverse else i
            return pl.ds(c * n_virt, n_virt)

        def body(i, carry):
            st, sw = carry
            rows = rows_of(i)
            new_st, new_sw = [], []
            for k in range(gpb):
                loc = scr[k, rows, :]
                loc_sw = sw_scr[k, rows, :]
                scr[k, rows, :] = st[k]
                new_st.append(a_r[k] * st[k] + a_i[k] * sw[k] + loc)
                new_sw.append(a_r[k] * sw[k] - a_i[k] * st[k] + loc_sw)
            return tuple(new_st), tuple(new_sw)

        st0 = tuple(s0_ref[d, :, k * LANES:(k + 1) * LANES] for k in range(gpb))
        sw0 = tuple(_swap(s) for s in st0)
        carry = (st0, sw0)
        for i in range(cpv):
            carry = body(i, carry)
        fin = carry[0]
        if n_seg == 1:
            for k in range(gpb):
                sfin_ref[d, :, k * LANES:(k + 1) * LANES] = fin[k]
            return

        p = decay[d]
        for _ in range(cpv.bit_length() - 1):
            p_re, p_im = dup(p)
            p = _cmul(p, p_re, p_im * sgn8)
        v_re, v_im = dup(p)
        v_sg = v_im * sgn8
        seg = lax.broadcasted_iota(jnp.int32, (n_virt, LANES), 0) & (n_seg - 1)
        has_pred = seg != ((n_seg - 1) if reverse else 0)
        shift = (n_virt - 1) if reverse else 1
        cin = []
        for k in range(gpb):
            ck = jnp.zeros((n_virt, LANES), F32)
            for _ in range(n_seg - 1):
                nxt = fin[k] + ck * v_re[k:k + 1, :] + _swap(ck) * v_sg[k:k + 1, :]
                ck = jnp.where(has_pred, pltpu.roll(nxt, shift, 0), 0.0)
            cin.append(ck)
        cin_sw = [_swap(x) for x in cin]

        def fix(i, carry):
            q_re, q_im = carry
            rows = rows_of(i)
            q_sg = q_im * sgn8
            for k in range(gpb):
                scr[k, rows, :] += cin[k] * q_re[k:k + 1, :] + cin_sw[k] * q_sg[k:k + 1, :]
            return q_re * a_re - q_im * a_im, q_re * a_im + q_im * a_re

        q = (jnp.ones((gpb, LANES), F32), jnp.zeros((gpb, LANES), F32))
        for i in range(cpv):
            q = fix(i, q)

    scan(sf_scr, swf_scr, 0, False)
    scan(sb_scr, swb_scr, 1, True)

    s_f = jnp.concatenate([sf_scr[k].astype(BF16) for k in range(gpb)], axis=1)
    s_b = jnp.concatenate([sb_scr[k].astype(BF16) for k in range(gpb)], axis=1)
    yall = y_within + _dot_nt(s_f, e_scr[0]) + _dot_nt(s_b, e_scr[1])
    dsk = dsk_ref[...]
    for t in range(CHUNK):
        rows = pl.ds(t, nc, stride=CHUNK)
        y_ref[rows, :] = yall[:, t * LANES:(t + 1) * LANES] + h_ref[rows, :] * dsk


def _s5_params(lam_re, lam_im, log_dt, b_re, b_im, c_re, c_im):
    lamr = jnp.concatenate([lam_re, lam_re], axis=-1).astype(F32)
    lami = jnp.concatenate([lam_im, lam_im], axis=-1).astype(F32)
    ldt = jnp.broadcast_to(log_dt.astype(F32)[..., None], lamr.shape)
    bt = jnp.concatenate([b_re.transpose(0, 1, 3, 2), b_im.transpose(0, 1, 3, 2)], axis=-1)
    cp = jnp.concatenate([c_re, c_im], axis=-1)
    return (lamr, lami, ldt, bt.reshape(2, D_MODEL, LANES).astype(F32), cp.reshape(2, D_MODEL, LANES).astype(F32))


def _s5(h, params, d_skip, s0, n_seg):
    lamr, lami, ldt, bt, cp = params
    _, cpv, n_virt, _, _ = h.shape
    assert cpv & (cpv - 1) == 0 and n_virt % SUBLANES == 0 and n_seg & (n_seg - 1) == 0
    nc = cpv * n_virt
    ntok = nc * CHUNK
    hspec = pl.BlockSpec((None, ntok, LANES), lambda g: (g, 0, 0))
    kdim = CHUNK * LANES
    gspec = pl.BlockSpec((2, GROUPS_PER_BLOCK, LANES), lambda g: (0, g, 0))
    rspec = pl.BlockSpec((2, LANES, LANES), lambda g: (0, g, 0))
    sspec = pl.BlockSpec((None, 2, n_virt, STATE_LANES), lambda g: (g, 0, 0, 0))
    state_scr = pltpu.VMEM((GROUPS_PER_BLOCK, nc, LANES), F32)
    out_specs = [hspec]
    out_shape = [jax.ShapeDtypeStruct((N_GROUP_BLOCKS, ntok, LANES), F32)]
    if n_seg == 1:
        out_specs.append(sspec)
        out_shape.append(jax.ShapeDtypeStruct((N_GROUP_BLOCKS, 2, n_virt, STATE_LANES), F32))
    outs = pl.pallas_call(
        functools.partial(_s5_kernel, n_virt, n_seg),
        grid=(N_GROUP_BLOCKS,),
        in_specs=[
            hspec,
            gspec, gspec, gspec, rspec, rspec,
            pl.BlockSpec((1, LANES), lambda g: (0, g)),
            sspec,
        ],
        out_specs=out_specs,
        out_shape=out_shape,
        scratch_shapes=[
            pltpu.VMEM((2, kdim, STATE_LANES), BF16),
            pltpu.VMEM((2, kdim, STATE_LANES), BF16),
            pltpu.VMEM((kdim, kdim), BF16),
            state_scr, state_scr, state_scr, state_scr,
        ],
        compiler_params=_cparams(("arbitrary",)),
        name="s5_chunked_scan",
    )(h.reshape(N_GROUP_BLOCKS, ntok, LANES), lamr, lami, ldt, bt, cp, d_skip, s0)
    y = outs[0].reshape(h.shape)
    return (y, outs[1]) if n_seg == 1 else (y, None)


def _state_to_blocks(s):
    b = s.shape[0]
    s = s.reshape(b, 2, 2, N_GROUP_BLOCKS, GROUPS_PER_BLOCK, STATE_DIM)
    return s.transpose(3, 1, 0, 4, 2, 5).reshape(N_GROUP_BLOCKS, 2, b, STATE_LANES)


def _blocks_to_state(s):
    b = s.shape[2]
    s = s.reshape(N_GROUP_BLOCKS, 2, b, GROUPS_PER_BLOCK, 2, STATE_DIM)
    return s.transpose(2, 1, 4, 0, 3, 5).reshape(b, 2, 2, N_GROUPS, STATE_DIM)


def _rope_tables(n_tokens):
    pos = np.arange(n_tokens)
    n_freq = HEAD_DIM // 4
    freqs = ROPE_BASE ** (-np.arange(n_freq, dtype=np.float64) / n_freq)
    ang_r = (pos // GRID_W)[:, None] * freqs
    ang_c = (pos % GRID_W)[:, None] * freqs
    cos_h = np.concatenate([np.cos(ang_r), np.cos(ang_r), np.cos(ang_c), np.cos(ang_c)], axis=1)
    sin_h = np.concatenate([-np.sin(ang_r), np.sin(ang_r), -np.sin(ang_c), np.sin(ang_c)], axis=1)
    return jnp.asarray(np.tile(cos_h, (1, 2)), F32), jnp.asarray(np.tile(sin_h, (1, 2)), F32)


def kernel(x_prompt, x_sample, cache_k, cache_v, state_ssm, c, c_ctx, norm1_g, norm2_g, w_mod, b_mod,
           w_qkv, w_o, attn_sink, ssm_lam_re, ssm_lam_im, ssm_log_dt, ssm_b_re, ssm_b_im, ssm_c_re,
           ssm_c_im, ssm_d, glu_w_a, glu_w_b, mlp_w1, mlp_w2, final_norm_g):
    bp, lp, _ = x_prompt.shape
    bx, lx, _ = x_sample.shape
    assert lx % TOKEN_TILE == 0 and (bp * lp) % TOKEN_TILE == 0
    tiles_per_lat = lx // TOKEN_TILE

    xp = x_prompt.reshape(bp * lp, D_MODEL)
    xx = x_sample.reshape(bx * lx, D_MODEL)

    cvecs = jnp.zeros((8, D_MODEL), F32).at[0].set(c_ctx).at[1:1 + bx].set(c)
    mod = _modulation(cvecs, w_mod, b_mod)

    ctx_row = lambda i: 0
    lat_row = lambda i: 1 + i // tiles_per_lat

    rope = _rope_tables(lx) + (lambda i: i % tiles_per_lat,)
    wqkv = w_qkv[0].astype(BF16)
    g1 = norm1_g[0].reshape(1, D_MODEL)
    sink = attn_sink[0].astype(F32)
    qp, krp, vrp, kp, vp = _qkv(xp, mod[0], ctx_row, g1, wqkv, None, lp)
    qx, krx, vrx = _qkv(xx, mod[0], lat_row, g1, wqkv, rope, 0)
    op = _ctx_attention(sink, qp, krp, vrp, bp, lp)
    rep = lambda t: jnp.tile(t[:, 0].transpose(0, 2, 1, 3), (1, 1, 1, LANES // HEAD_DIM)).astype(BF16)
    ox = _lat_attention(sink, qx, krx, vrx, rep(cache_k), rep(cache_v), bx, lx)

    wo = w_o[0].astype(BF16)
    w1 = mlp_w1.astype(BF16)
    w2 = mlp_w2.astype(BF16)
    g2 = norm2_g.reshape(-1, 1, D_MODEL)
    gn = norm1_g[1].reshape(1, D_MODEL)
    vpt_p = TOKEN_TILE // lp
    vpt_x = 1
    n_seg_x = tiles_per_lat
    xp, hp = _post(xp, op, mod[0], ctx_row, g2[0], wo, None, w1, w2, 0, vpt_p, mod_next=mod[1], g_next=gn)
    xx, hx = _post(xx, ox, mod[0], lat_row, g2[0], wo, None, w1, w2, 0, vpt_x, mod_next=mod[1], g_next=gn)

    params = _s5_params(ssm_lam_re[0], ssm_lam_im[0], ssm_log_dt[0], ssm_b_re[0], ssm_b_im[0],
                        ssm_c_re[0], ssm_c_im[0])
    dsk = ssm_d[0].astype(F32).reshape(1, D_MODEL)
    s0p = jnp.zeros((N_GROUP_BLOCKS, 2, bp, STATE_LANES), F32)
    sx = _state_to_blocks(state_ssm[:, 0].astype(F32))
    s0x = jnp.zeros((N_GROUP_BLOCKS, 2, bx, n_seg_x, STATE_LANES), F32)
    s0x = s0x.at[:, 0, :, 0].set(sx[:, 0]).at[:, 1, :, n_seg_x - 1].set(sx[:, 1])
    s0x = s0x.reshape(N_GROUP_BLOCKS, 2, bx * n_seg_x, STATE_LANES)
    yp, sfin = _s5(hp, params, dsk, s0p, 1)
    yx, _ = _s5(hx, params, dsk, s0x, n_seg_x)
    new_state = _blocks_to_state(sfin)[:, None]

    wa = glu_w_a[0].astype(BF16)
    wb = glu_w_b[0].astype(BF16)
    fg = final_norm_g.reshape(1, D_MODEL)
    (yp_out,) = _post(xp, yp, mod[1], ctx_row, g2[1], wa, wb, w1, w2, 1, vpt_p, final_g=fg)
    (yx_out,) = _post(xx, yx, mod[1], lat_row, g2[1], wa, wb, w1, w2, 1, vpt_x, final_g=fg)

    to_cache = lambda t: t.reshape(bp, N_KV_HEADS, HEAD_DIM, lp).transpose(0, 3, 1, 2)[:, None]
    new_k = to_cache(kp)
    new_v = to_cache(vp)
    return (yp_out.reshape(bp, lp, D_MODEL), yx_out.reshape(bx, lx, D_MODEL), new_k, new_v, new_state)
```

```python
import functools
import math

import numpy as np
import jax
import jax.numpy as jnp
from jax import lax
from jax.experimental import pallas as pl
from jax.experimental.pallas import tpu as pltpu

F32 = jnp.float32
BF16 = jnp.bfloat16

D_MODEL = 1024
N_HEADS = 16
N_KV_HEADS = 4
HEAD_DIM = 64
Q_PER_KV = N_HEADS // N_KV_HEADS
KV_DIM = N_KV_HEADS * HEAD_DIM
QKV_DIM = D_MODEL + 2 * KV_DIM
BLOCK = 128
GRID_W = 64
ROPE_BASE = 10000.0
ATTN_SCALE = HEAD_DIM ** -0.5
N_GROUPS = 64
GROUP_CH = 16
STATE_DIM = 64
D_FF = 4 * D_MODEL
N_MOD = 6
RMS_EPS = 1e-6
NEG_INF = -1e30

LANES = 128
SUBLANES = 8
GROUPS_PER_BLOCK = LANES // GROUP_CH
N_GROUP_BLOCKS = N_GROUPS // GROUPS_PER_BLOCK
STATE_LANES = GROUPS_PER_BLOCK * 2 * STATE_DIM
CHUNK = SUBLANES
TOKEN_TILE = 512
FF_TILE = 1024
POST_SUBTILES = 2
QKV_SUBTILES = 2
VMEM_LIMIT = 56 * 1024 * 1024


def _cparams(semantics):
    return pltpu.CompilerParams(dimension_semantics=semantics, vmem_limit_bytes=VMEM_LIMIT)


def _rms(x):
    return x * lax.rsqrt(jnp.mean(x * x, axis=-1, keepdims=True) + RMS_EPS)


def _dot(a, b):
    return jnp.dot(a, b, preferred_element_type=F32)


def _dot_nt(a, b):
    return lax.dot_general(a, b, (((1,), (1,)), ((), ())), preferred_element_type=F32)


def _mod_kernel(cv_ref, w_ref, b_ref, o_ref):
    cv = cv_ref[...]
    s = (cv * jax.nn.sigmoid(cv)).astype(BF16)
    o_ref[0] = _dot(s, w_ref[0].astype(BF16)) + b_ref[0]


def _modulation(cvecs, w_mod, b_mod):
    depth = w_mod.shape[0]
    out = pl.pallas_call(
        _mod_kernel,
        grid=(depth, N_MOD),
        in_specs=[
            pl.BlockSpec((8, D_MODEL), lambda l, j: (0, 0)),
            pl.BlockSpec((1, D_MODEL, D_MODEL), lambda l, j: (l, 0, j)),
            pl.BlockSpec((1, 1, D_MODEL), lambda l, j: (l, 0, j)),
        ],
        out_specs=pl.BlockSpec((1, 8, D_MODEL), lambda l, j: (l, 0, j)),
        out_shape=jax.ShapeDtypeStruct((depth, 8, N_MOD * D_MODEL), F32),
        compiler_params=_cparams(("arbitrary", "arbitrary")),
        name="modulation",
    )(cvecs, w_mod, b_mod.reshape(depth, 1, N_MOD * D_MODEL))
    return out.reshape(depth, 8, N_MOD, D_MODEL)


def _head_pair(blk, odd):
    lo = lax.broadcasted_iota(jnp.int32, blk.shape, 1) < HEAD_DIM
    other = pltpu.roll(blk, HEAD_DIM, 1)
    return (jnp.where(lo, other, blk) if odd else jnp.where(lo, blk, other)).astype(BF16)


def _qkv_kernel(cache_seq, rope, x_ref, mod_ref, g_ref, w_ref, *refs):
    refs = list(refs)
    cos_ref, sin_ref = (refs.pop(0), refs.pop(0)) if rope else (None, None)
    q_ref, krep_ref, vrep_ref = refs[:3]
    kv_refs = refs[3:]
    sub = TOKEN_TILE // QKV_SUBTILES
    starts = [k * sub for k in range(QKV_SUBTILES)]

    def project(r0):
        h = _rms(x_ref[r0:r0 + sub, :]) * g_ref[...] * (1.0 + mod_ref[1:2, :]) + mod_ref[0:1, :]
        return _dot(h.astype(BF16), w_ref[...])

    def emit_cache(ref, blk, r0, c0):
        for s in range(sub // cache_seq):
            ref[r0 // cache_seq + s, c0:c0 + LANES, :] = blk[s * cache_seq:(s + 1) * cache_seq, :].T

    def finish(r0, qkv):
        rows = slice(r0, r0 + sub)
        if rope:
            cos = cos_ref[rows, :]
            sin = sin_ref[rows, :]
            first = (lax.broadcasted_iota(jnp.int32, cos.shape, 1) & 31) < 16
        for blk in range((D_MODEL + KV_DIM) // LANES):
            r = qkv[:, blk * LANES:(blk + 1) * LANES]
            if rope:
                partner = jnp.where(first, pltpu.roll(r, LANES - 16, 1), pltpu.roll(r, 16, 1))
                r = r * cos + partner * sin
            if blk < D_MODEL // LANES:
                q_ref[rows, blk * LANES:(blk + 1) * LANES] = (r * ATTN_SCALE).astype(BF16)
            else:
                c0 = blk * LANES - D_MODEL
                if cache_seq:
                    emit_cache(kv_refs[0], r, r0, c0)
                for half in range(2):
                    krep_ref[c0 // HEAD_DIM + half, rows, :] = _head_pair(r, half)
        v = qkv[:, D_MODEL + KV_DIM:]
        for c0 in range(0, KV_DIM, LANES):
            blk = v[:, c0:c0 + LANES]
            if cache_seq:
                emit_cache(kv_refs[1], blk, r0, c0)
            for half in range(2):
                vrep_ref[c0 // HEAD_DIM + half, rows, :] = _head_pair(blk, half)

    for r0, qkv in zip(starts, [project(r0) for r0 in starts]):
        finish(r0, qkv)


def _qkv(x, mod, mod_row, g, w_qkv, rope, cache_seq):
    ntok = x.shape[0]
    nt = ntok // TOKEN_TILE
    emit_kv = cache_seq > 0
    assert not emit_kv or (TOKEN_TILE // QKV_SUBTILES) % cache_seq == 0
    rep_spec = pl.BlockSpec((N_KV_HEADS, TOKEN_TILE, LANES), lambda i: (0, i, 0))
    rep_shape = jax.ShapeDtypeStruct((N_KV_HEADS, ntok, LANES), BF16)
    in_specs = [
        pl.BlockSpec((TOKEN_TILE, D_MODEL), lambda i: (i, 0)),
        pl.BlockSpec((None, N_MOD, D_MODEL), lambda i: (mod_row(i), 0, 0)),
        pl.BlockSpec((1, D_MODEL), lambda i: (0, 0)),
        pl.BlockSpec((D_MODEL, QKV_DIM), lambda i: (0, 0)),
    ]
    args = [x, mod, g, w_qkv]
    if rope is not None:
        cos_t, sin_t, rope_blk = rope
        in_specs += [pl.BlockSpec((TOKEN_TILE, LANES), lambda i: (rope_blk(i), 0))] * 2
        args += [cos_t, sin_t]
    out_specs = [pl.BlockSpec((TOKEN_TILE, D_MODEL), lambda i: (i, 0)), rep_spec, rep_spec]
    out_shape = [jax.ShapeDtypeStruct((ntok, D_MODEL), BF16), rep_shape, rep_shape]
    if emit_kv:
        spt = TOKEN_TILE // cache_seq
        out_specs += [pl.BlockSpec((spt, KV_DIM, cache_seq), lambda i: (i, 0, 0))] * 2
        out_shape += [jax.ShapeDtypeStruct((ntok // cache_seq, KV_DIM, cache_seq), F32)] * 2
    return pl.pallas_call(
        functools.partial(_qkv_kernel, cache_seq, rope is not None),
        grid=(nt,),
        in_specs=in_specs,
        out_specs=out_specs,
        out_shape=out_shape,
        compiler_params=_cparams(("arbitrary",)),
        name="norm_qkv_rope",
    )(*args)


def _group_scores(q_ref, kv, key_parts, bias):
    nq = q_ref.shape[0]
    lo = lax.broadcasted_iota(jnp.int32, (nq, LANES), 1) < HEAD_DIM
    zero = jnp.zeros((), BF16)
    rows = []
    for b in range(KV_DIM // LANES):
        blk = q_ref[:, kv * KV_DIM + b * LANES:kv * KV_DIM + (b + 1) * LANES]
        rows += [jnp.where(lo, blk, zero), jnp.where(lo, zero, blk)]
    q4 = jnp.concatenate(rows, axis=0)
    parts = [_dot_nt(q4, keys) for keys in key_parts]
    if bias is not None:
        s0 = parts[0].reshape(Q_PER_KV, nq, -1) + bias[None]
        parts[0] = s0.reshape(Q_PER_KV * nq, -1)
    return parts


def _group_softmax(parts, sink_ref, kv):
    nq = parts[0].shape[0] // Q_PER_KV
    sink = jnp.concatenate(
        [jnp.full((nq, LANES), sink_ref[kv * Q_PER_KV + g], F32) for g in range(Q_PER_KV)], axis=0)
    blocks = [[s[:, j:j + LANES] for j in range(0, s.shape[1], LANES)] for s in parts]
    fold = None
    for b in sum(blocks, []):
        fold = b if fold is None else jnp.maximum(fold, b)
    m = jnp.maximum(jnp.max(fold, axis=-1, keepdims=True), sink)
    probs = [[jnp.exp(b - m) for b in bs] for bs in blocks]
    fold = None
    for p in sum(probs, []):
        fold = p if fold is None else fold + p
    den = jnp.sum(fold, axis=-1, keepdims=True) + jnp.exp(sink - m)
    return [jnp.concatenate(ps, axis=1).astype(BF16) for ps in probs], 1.0 / den


def _group_output(probs, inv_den, value_parts, o_ref, kv):
    nq = probs[0].shape[0] // Q_PER_KV
    r = _dot(probs[0], value_parts[0])
    for p, v in zip(probs[1:], value_parts[1:]):
        r = r + _dot(p, v)
    r = r * inv_den
    lo = lax.broadcasted_iota(jnp.int32, (nq, LANES), 1) < HEAD_DIM
    for b in range(KV_DIM // LANES):
        pair = jnp.where(lo, r[2 * b * nq:(2 * b + 1) * nq], r[(2 * b + 1) * nq:(2 * b + 2) * nq])
        o_ref[:, kv * KV_DIM + b * LANES:kv * KV_DIM + (b + 1) * LANES] = pair.astype(BF16)


def _attend(q_ref, sink_ref, o_ref, keys_of, values_of, bias):
    s_next = _group_scores(q_ref, 0, keys_of(0), bias)
    for kv in range(N_KV_HEADS):
        s = s_next
        if kv + 1 < N_KV_HEADS:
            s_next = _group_scores(q_ref, kv + 1, keys_of(kv + 1), bias)
        probs, inv_den = _group_softmax(s, sink_ref, kv)
        _group_output(probs, inv_den, values_of(kv), o_ref, kv)


def _ctx_attn_kernel(sink_ref, q_ref, k_ref, v_ref, o_ref):
    _attend(q_ref, sink_ref, o_ref, lambda kv: [k_ref[kv]], lambda kv: [v_ref[kv]], None)


def _ctx_attention(sink, q, krep, vrep, n_batch, seq):
    rep_spec = pl.BlockSpec((N_KV_HEADS, seq, LANES), lambda b: (0, b, 0))
    return pl.pallas_call(
        _ctx_attn_kernel,
        grid=(n_batch,),
        in_specs=[
            pl.BlockSpec(memory_space=pltpu.SMEM),
            pl.BlockSpec((seq, D_MODEL), lambda b: (b, 0)),
            rep_spec, rep_spec,
        ],
        out_specs=pl.BlockSpec((seq, D_MODEL), lambda b: (b, 0)),
        out_shape=jax.ShapeDtypeStruct((n_batch * seq, D_MODEL), BF16),
        compiler_params=_cparams(("arbitrary",)),
        name="context_attention",
    )(sink, q, krep, vrep)


def _window_start(n, seq):
    return jnp.clip((n - 1) * BLOCK, 0, seq - 3 * BLOCK)


def _band_bias():
    r = np.arange(BLOCK)[:, None]
    j = np.arange(3 * BLOCK)[None, :]
    out = [np.where(np.abs(j - d * BLOCK - r) <= BLOCK, 0.0, NEG_INF) for d in range(3)]
    return jnp.asarray(np.stack(out), F32)


def _lat_attn_kernel(seq, n_cast, sink_ref, q_ref, k_ref, v_ref, ck_ref, cv_ref, bias_ref, *refs):
    o_ref = refs[n_cast]
    for src, dst in zip(refs[:n_cast], refs[n_cast + 1:]):
        dst[...] = src[...].astype(BF16)
    win = 3 * BLOCK
    start = pl.multiple_of(_window_start(pl.program_id(1), seq), BLOCK)
    keys_of = lambda kv: [k_ref[kv, pl.ds(start, win), :], ck_ref[kv]]
    values_of = lambda kv: [v_ref[kv, pl.ds(start, win), :], cv_ref[kv]]
    _attend(q_ref, sink_ref, o_ref, keys_of, values_of, bias_ref[...])


def _lat_attention(sink, q, krep, vrep, ckrep, cvrep, n_batch, seq, cast_weights):
    nb = seq // BLOCK
    steps = n_batch * nb
    past = ckrep.shape[2]
    rep_spec = pl.BlockSpec((N_KV_HEADS, seq, LANES), lambda b, n: (0, b, 0))
    crep_spec = pl.BlockSpec((None, N_KV_HEADS, past, LANES), lambda b, n: (b, 0, 0, 0))
    flat = [w.reshape(-1, w.shape[-1]) for w in cast_weights]
    assert all(w.shape[0] % (steps * 2 * SUBLANES) == 0 for w in flat)
    slabs = [pl.BlockSpec((w.shape[0] // steps, w.shape[1]), lambda b, n: (b * nb + n, 0)) for w in flat]
    outs = pl.pallas_call(
        functools.partial(_lat_attn_kernel, seq, len(flat)),
        grid=(n_batch, nb),
        in_specs=[
            pl.BlockSpec(memory_space=pltpu.SMEM),
            pl.BlockSpec((BLOCK, D_MODEL), lambda b, n: (b * nb + n, 0)),
            rep_spec, rep_spec, crep_spec, crep_spec,
            pl.BlockSpec((None, BLOCK, 3 * BLOCK), lambda b, n: (n - _window_start(n, seq) // BLOCK, 0, 0)),
        ] + slabs,
        out_specs=[pl.BlockSpec((BLOCK, D_MODEL), lambda b, n: (b * nb + n, 0))] + slabs,
        out_shape=[jax.ShapeDtypeStruct((n_batch * seq, D_MODEL), BF16)]
        + [jax.ShapeDtypeStruct(w.shape, BF16) for w in flat],
        compiler_params=_cparams(("arbitrary", "arbitrary")),
        name="latent_attention",
    )(sink, q, krep, vrep, ckrep, cvrep, _band_bias(), *flat)
    return outs[0], [o.reshape(w.shape) for o, w in zip(outs[1:], cast_weights)]


def _gelu_tanh(x):
    c = math.sqrt(2.0 / math.pi)
    return x * (0.5 * (1.0 + jnp.tanh(c * (x + 0.044715 * (x * x * x)))))


def _post_kernel(is_attn, emit_next, final, vpt, *refs):
    rows_per_v = TOKEN_TILE // vpt
    refs = list(refs)
    x_ref, mix_ref, mod_ref, g2_ref, wa_ref = refs[:5]
    refs = refs[5:]
    wb_ref = None if is_attn else refs.pop(0)
    w1_ref, w2_ref = refs[:2]
    refs = refs[2:]
    modn_ref = gn_ref = fg_ref = hn_ref = None
    if emit_next:
        modn_ref, gn_ref = refs[:2]
        refs = refs[2:]
    if final:
        fg_ref = refs.pop(0)
    xo_ref = refs.pop(0)
    if emit_next:
        hn_ref = refs.pop(0)
    assert not refs

    sub = TOKEN_TILE // POST_SUBTILES
    assert rows_per_v % sub == 0
    n_chunk = sub // CHUNK
    starts = [k * sub for k in range(POST_SUBTILES)]

    def s5_slot(r0):
        return r0 // rows_per_v, (r0 % rows_per_v) // CHUNK

    def project(r0):
        if is_attn:
            return _dot(mix_ref[r0:r0 + sub, :], wa_ref[...])
        s, c0 = s5_slot(r0)
        y = jnp.concatenate(
            [mix_ref[g, c0:c0 + n_chunk, s].reshape(sub, LANES) for g in range(N_GROUP_BLOCKS)], axis=1)
        yg = _gelu_tanh(y).astype(BF16)
        return _dot(yg, wa_ref[...]) * jax.nn.sigmoid(_dot(yg, wb_ref[...]))

    def prologue(r0, mix):
        x1 = x_ref[r0:r0 + sub, :] + mod_ref[2:3, :] * mix
        h2 = _rms(x1) * g2_ref[...] * (1.0 + mod_ref[4:5, :]) + mod_ref[3:4, :]
        return x1, h2.astype(BF16)

    def mlp(h2):
        acc = None
        for c in range(D_FF // FF_TILE):
            a = jnp.maximum(_dot(h2, w1_ref[:, c * FF_TILE:(c + 1) * FF_TILE]), 0.0)
            t = _dot((a * a).astype(BF16), w2_ref[c * FF_TILE:(c + 1) * FF_TILE, :])
            acc = t if acc is None else acc + t
        return acc

    def epilogue(r0, x1, acc):
        x2 = x1 + mod_ref[5:6, :] * acc
        if emit_next:
            hn = _rms(x2) * gn_ref[...] * (1.0 + modn_ref[1:2, :]) + modn_ref[0:1, :]
            s, c0 = s5_slot(r0)
            for g in range(N_GROUP_BLOCKS):
                blk = hn[:, g * LANES:(g + 1) * LANES]
                hn_ref[g, c0:c0 + n_chunk, s] = blk.reshape(n_chunk, CHUNK, LANES)
        xo_ref[r0:r0 + sub, :] = _rms(x2) * fg_ref[...] if final else x2

    mixes = [project(r0) for r0 in starts]
    pro = [prologue(r0, mix) for r0, mix in zip(starts, mixes)]
    accs = [mlp(h2) for _, h2 in pro]
    for r0, (x1, _), acc in zip(starts, pro, accs):
        epilogue(r0, x1, acc)


def _post(x, mix, mod, mod_row, g2, w_a, w_b, w1, w2, layer, vpt, mod_next=None, g_next=None, final_g=None):
    is_attn = w_b is None
    emit_next = mod_next is not None
    final = final_g is not None
    ntok = x.shape[0]
    nt = ntok // TOKEN_TILE
    cpv = TOKEN_TILE // (vpt * CHUNK)
    n_virt = nt * vpt
    tile = pl.BlockSpec((TOKEN_TILE, D_MODEL), lambda i: (i, 0))
    row = pl.BlockSpec((1, D_MODEL), lambda i: (0, 0))
    modspec = pl.BlockSpec((None, N_MOD, D_MODEL), lambda i: (mod_row(i), 0, 0))
    resident = lambda shape: pl.BlockSpec(shape, lambda i: (0, 0), pipeline_mode=pl.Buffered(1))
    wsq = resident((D_MODEL, D_MODEL))
    gtile = pl.BlockSpec((N_GROUP_BLOCKS, cpv, vpt, CHUNK, LANES), lambda i: (0, 0, i, 0, 0))
    in_specs = [tile, tile if is_attn else gtile, modspec, row, wsq]
    args = [x, mix, mod, g2, w_a]
    if not is_attn:
        in_specs.append(wsq)
        args.append(w_b)
    in_specs += [pl.BlockSpec((None, D_MODEL, D_FF), lambda i: (layer, 0, 0), pipeline_mode=pl.Buffered(1)),
                 pl.BlockSpec((None, D_FF, D_MODEL), lambda i: (layer, 0, 0), pipeline_mode=pl.Buffered(1))]
    args += [w1, w2]
    if emit_next:
        in_specs += [modspec, row]
        args += [mod_next, g_next]
    if final:
        in_specs.append(row)
        args.append(final_g)
    out_specs = [tile]
    out_shape = [jax.ShapeDtypeStruct((ntok, D_MODEL), F32)]
    if emit_next:
        out_specs.append(gtile)
        out_shape.append(jax.ShapeDtypeStruct((N_GROUP_BLOCKS, cpv, n_virt, CHUNK, LANES), F32))
    return pl.pallas_call(
        functools.partial(_post_kernel, is_attn, emit_next, final, vpt),
        grid=(nt,),
        in_specs=in_specs,
        out_specs=out_specs,
        out_shape=out_shape,
        compiler_params=_cparams(("arbitrary",)),
        name="attn_proj_mlp" if is_attn else "glu_mlp_final",
    )(*args)


def _swap(x):
    return pltpu.roll(x, LANES // 2, 1)


def _cmul(z, w_r, w_i):
    return z * w_r + _swap(z) * w_i


def _multiplier(z, lo):
    zs = _swap(z)
    return jnp.where(lo, z, zs), jnp.where(lo, -zs, z)


def _rep_rows(x):
    return jnp.concatenate(
        [jnp.broadcast_to(x[g:g + 1, :], (GROUP_CH, LANES)) for g in range(GROUPS_PER_BLOCK)], axis=0)


def _s5_kernel(n_virt, n_seg, h_ref, lamr_ref, lami_ref, ldt_ref, bt_ref, cp_ref, dsk_ref, s0_ref, *refs):
    if n_seg == 1:
        y_ref, sfin_ref = refs[:2]
        refs = refs[2:]
    else:
        y_ref, sfin_ref = refs[0], None
        refs = refs[1:]
    f_scr, e_scr, k_scr, sf_scr, sb_scr, swf_scr, swb_scr = refs
    ntok = h_ref.shape[0]
    nc = ntok // CHUNK
    cpv = nc // n_virt
    gpb = GROUPS_PER_BLOCK

    lo8 = lax.broadcasted_iota(jnp.int32, (gpb, LANES), 1) < STATE_DIM
    lo = lax.broadcasted_iota(jnp.int32, (LANES, LANES), 1) < STATE_DIM
    conj = jnp.where(lo, 1.0, -1.0)
    row_g = lax.broadcasted_iota(jnp.int32, (LANES, STATE_LANES), 0) >> 4
    col_g = lax.broadcasted_iota(jnp.int32, (LANES, STATE_LANES), 1) >> 7
    diag_wide = row_g == col_g
    diag = (lax.broadcasted_iota(jnp.int32, (LANES, LANES), 0) >> 4) == (
        lax.broadcasted_iota(jnp.int32, (LANES, LANES), 1) >> 4)

    def expand(w):
        return jnp.where(diag_wide, jnp.concatenate([w] * gpb, axis=1), jnp.zeros((), BF16))

    decay = []
    lag = []
    for d in range(2):
        lam_r = lamr_ref[d]
        lam_i = lami_ref[d]
        dt = jnp.exp(ldt_ref[d])
        mag = jnp.exp(lam_r * dt)
        ang = lam_i * dt
        a_r = mag * jnp.cos(ang)
        a_im = mag * jnp.sin(ang)
        a_i = jnp.where(lo8, -a_im, a_im)
        den = lam_r * lam_r + lam_i * lam_i
        num = jnp.where(lo8, a_r - 1.0, a_im)
        f = _cmul(num, lam_r / den, jnp.where(lo8, lam_i, -lam_i) / den)
        pw = [jnp.where(lo8, 1.0, 0.0)]
        for _ in range(CHUNK):
            pw.append(_cmul(pw[-1], a_r, a_i))
        decay.append(pw[CHUNK])
        pw = [_rep_rows(p) for p in pw]
        f_r, f_i = _multiplier(_rep_rows(f), lo)
        bb_r, bb_i = _multiplier(_cmul(bt_ref[d], f_r, f_i), lo)
        c_r, c_i = _multiplier(cp_ref[d], lo)
        cm = (cp_ref[d] * conj).astype(BF16)
        fpow = [_cmul(p, bb_r, bb_i).astype(BF16) for p in pw[:CHUNK]]
        for j in range(CHUNK):
            e = (CHUNK - 1 - j) if d == 0 else j
            f_scr[d, j * LANES:(j + 1) * LANES, :] = expand(fpow[e])
        for t in range(CHUNK):
            e = (t + 1) if d == 0 else (CHUNK - t)
            w = _cmul(pw[e], c_r, c_i) * conj
            e_scr[d, t * LANES:(t + 1) * LANES, :] = expand(w.astype(BF16))
        lag.append([jnp.where(diag, _dot_nt(fp, cm), 0.0) for fp in fpow])

    for j in range(CHUNK):
        for t in range(CHUNK):
            k = t - j
            tile = lag[0][k] if k > 0 else (lag[1][-k] if k < 0 else lag[0][0] + lag[1][0])
            k_scr[j * LANES:(j + 1) * LANES, t * LANES:(t + 1) * LANES] = tile.astype(BF16)

    xcat = jnp.concatenate(
        [h_ref[pl.ds(j, nc, stride=CHUNK), :].astype(BF16) for j in range(CHUNK)], axis=1)

    for d, scr, sw_scr in ((0, sf_scr, swf_scr), (1, sb_scr, swb_scr)):
        loc_all = _dot(xcat, f_scr[d])
        for k in range(gpb):
            loc = loc_all[:, k * LANES:(k + 1) * LANES]
            scr[k] = loc
            sw_scr[k] = _swap(loc)
    y_within = _dot(xcat, k_scr[...])

    sgn8 = jnp.where(lo8, -1.0, 1.0)

    def dup(z):
        zs = _swap(z)
        return jnp.where(lo8, z, zs), jnp.where(lo8, zs, z)

    def scan(scr, sw_scr, d, reverse):
        a_re, a_im = dup(decay[d])
        a_sg = a_im * sgn8
        a_r = [a_re[k:k + 1, :] for k in range(gpb)]
        a_i = [a_sg[k:k + 1, :] for k in range(gpb)]

        def rows_of(i):
            c = (cpv - 1 - i) if reverse else i
            return pl.ds(c * n_virt, n_virt)

        def body(i, carry):
            st, sw = carry
            rows = rows_of(i)
            new_st, new_sw = [], []
            for k in range(gpb):
                loc = scr[k, rows, :]
                loc_sw = sw_scr[k, rows, :]
                scr[k, rows, :] = st[k]
                new_st.append(a_r[k] * st[k] + a_i[k] * sw[k] + loc)
                new_sw.append(a_r[k] * sw[k] - a_i[k] * st[k] + loc_sw)
            return tuple(new_st), tuple(new_sw)

        st0 = tuple(s0_ref[d, :, k * LANES:(k + 1) * LANES] for k in range(gpb))
        sw0 = tuple(_swap(s) for s in st0)
        carry = (st0, sw0)
        for i in range(cpv):
            carry = body(i, carry)
        fin = carry[0]
        if n_seg == 1:
            for k in range(gpb):
                sfin_ref[d, :, k * LANES:(k + 1) * LANES] = fin[k]
            return

        p = decay[d]
        for _ in range(cpv.bit_length() - 1):
            p_re, p_im = dup(p)
            p = _cmul(p, p_re, p_im * sgn8)
        v_re, v_im = dup(p)
        v_sg = v_im * sgn8
        seg = lax.broadcasted_iota(jnp.int32, (n_virt, LANES), 0) & (n_seg - 1)
        has_pred = seg != ((n_seg - 1) if reverse else 0)
        shift = (n_virt - 1) if reverse else 1
        cin = []
        for k in range(gpb):
            ck = jnp.zeros((n_virt, LANES), F32)
            for _ in range(n_seg - 1):
                nxt = fin[k] + ck * v_re[k:k + 1, :] + _swap(ck) * v_sg[k:k + 1, :]
                ck = jnp.where(has_pred, pltpu.roll(nxt, shift, 0), 0.0)
            cin.append(ck)
        cin_sw = [_swap(x) for x in cin]

        def fix(i, carry):
            q_re, q_im = carry
            rows = rows_of(i)
            q_sg = q_im * sgn8
            for k in range(gpb):
                scr[k, rows, :] += cin[k] * q_re[k:k + 1, :] + cin_sw[k] * q_sg[k:k + 1, :]
            return q_re * a_re - q_im * a_im, q_re * a_im + q_im * a_re

        q = (jnp.ones((gpb, LANES), F32), jnp.zeros((gpb, LANES), F32))
        for i in range(cpv):
            q = fix(i, q)

    scan(sf_scr, swf_scr, 0, False)
    scan(sb_scr, swb_scr, 1, True)

    s_f = jnp.concatenate([sf_scr[k].astype(BF16) for k in range(gpb)], axis=1)
    s_b = jnp.concatenate([sb_scr[k].astype(BF16) for k in range(gpb)], axis=1)
    yall = y_within + _dot_nt(s_f, e_scr[0]) + _dot_nt(s_b, e_scr[1])
    dsk = dsk_ref[...]
    for t in range(CHUNK):
        rows = pl.ds(t, nc, stride=CHUNK)
        y_ref[rows, :] = yall[:, t * LANES:(t + 1) * LANES] + h_ref[rows, :] * dsk


def _s5_params(lam_re, lam_im, log_dt, b_re, b_im, c_re, c_im):
    lamr = jnp.concatenate([lam_re, lam_re], axis=-1).astype(F32)
    lami = jnp.concatenate([lam_im, lam_im], axis=-1).astype(F32)
    ldt = jnp.broadcast_to(log_dt.astype(F32)[..., None], lamr.shape)
    bt = jnp.concatenate([b_re.transpose(0, 1, 3, 2), b_im.transpose(0, 1, 3, 2)], axis=-1)
    cp = jnp.concatenate([c_re, c_im], axis=-1)
    return (lamr, lami, ldt, bt.reshape(2, D_MODEL, LANES).astype(F32), cp.reshape(2, D_MODEL, LANES).astype(F32))


def _s5(h, params, d_skip, s0, n_seg):
    lamr, lami, ldt, bt, cp = params
    _, cpv, n_virt, _, _ = h.shape
    assert cpv & (cpv - 1) == 0 and n_virt % SUBLANES == 0 and n_seg & (n_seg - 1) == 0
    nc = cpv * n_virt
    ntok = nc * CHUNK
    hspec = pl.BlockSpec((None, ntok, LANES), lambda g: (g, 0, 0))
    kdim = CHUNK * LANES
    gspec = pl.BlockSpec((2, GROUPS_PER_BLOCK, LANES), lambda g: (0, g, 0))
    rspec = pl.BlockSpec((2, LANES, LANES), lambda g: (0, g, 0))
    sspec = pl.BlockSpec((None, 2, n_virt, STATE_LANES), lambda g: (g, 0, 0, 0))
    state_scr = pltpu.VMEM((GROUPS_PER_BLOCK, nc, LANES), F32)
    out_specs = [hspec]
    out_shape = [jax.ShapeDtypeStruct((N_GROUP_BLOCKS, ntok, LANES), F32)]
    if n_seg == 1:
        out_specs.append(sspec)
        out_shape.append(jax.ShapeDtypeStruct((N_GROUP_BLOCKS, 2, n_virt, STATE_LANES), F32))
    outs = pl.pallas_call(
        functools.partial(_s5_kernel, n_virt, n_seg),
        grid=(N_GROUP_BLOCKS,),
        in_specs=[
            hspec,
            gspec, gspec, gspec, rspec, rspec,
            pl.BlockSpec((1, LANES), lambda g: (0, g)),
            sspec,
        ],
        out_specs=out_specs,
        out_shape=out_shape,
        scratch_shapes=[
            pltpu.VMEM((2, kdim, STATE_LANES), BF16),
            pltpu.VMEM((2, kdim, STATE_LANES), BF16),
            pltpu.VMEM((kdim, kdim), BF16),
            state_scr, state_scr, state_scr, state_scr,
        ],
        compiler_params=_cparams(("arbitrary",)),
        name="s5_chunked_scan",
    )(h.reshape(N_GROUP_BLOCKS, ntok, LANES), lamr, lami, ldt, bt, cp, d_skip, s0)
    y = outs[0].reshape(h.shape)
    return (y, outs[1]) if n_seg == 1 else (y, None)


def _state_to_blocks(s):
    b = s.shape[0]
    s = s.reshape(b, 2, 2, N_GROUP_BLOCKS, GROUPS_PER_BLOCK, STATE_DIM)
    return s.transpose(3, 1, 0, 4, 2, 5).reshape(N_GROUP_BLOCKS, 2, b, STATE_LANES)


def _blocks_to_state(s):
    b = s.shape[2]
    s = s.reshape(N_GROUP_BLOCKS, 2, b, GROUPS_PER_BLOCK, 2, STATE_DIM)
    return s.transpose(2, 1, 4, 0, 3, 5).reshape(b, 2, 2, N_GROUPS, STATE_DIM)


def _rope_tables(n_tokens):
    pos = np.arange(n_tokens)
    n_freq = HEAD_DIM // 4
    freqs = ROPE_BASE ** (-np.arange(n_freq, dtype=np.float64) / n_freq)
    ang_r = (pos // GRID_W)[:, None] * freqs
    ang_c = (pos % GRID_W)[:, None] * freqs
    cos_h = np.concatenate([np.cos(ang_r), np.cos(ang_r), np.cos(ang_c), np.cos(ang_c)], axis=1)
    sin_h = np.concatenate([-np.sin(ang_r), np.sin(ang_r), -np.sin(ang_c), np.sin(ang_c)], axis=1)
    return jnp.asarray(np.tile(cos_h, (1, 2)), F32), jnp.asarray(np.tile(sin_h, (1, 2)), F32)


def kernel(x_prompt, x_sample, cache_k, cache_v, state_ssm, c, c_ctx, norm1_g, norm2_g, w_mod, b_mod,
           w_qkv, w_o, attn_sink, ssm_lam_re, ssm_lam_im, ssm_log_dt, ssm_b_re, ssm_b_im, ssm_c_re,
           ssm_c_im, ssm_d, glu_w_a, glu_w_b, mlp_w1, mlp_w2, final_norm_g):
    bp, lp, _ = x_prompt.shape
    bx, lx, _ = x_sample.shape
    assert lx % TOKEN_TILE == 0 and (bp * lp) % TOKEN_TILE == 0
    tiles_per_lat = lx // TOKEN_TILE

    xp = x_prompt.reshape(bp * lp, D_MODEL)
    xx = x_sample.reshape(bx * lx, D_MODEL)

    cvecs = jnp.zeros((8, D_MODEL), F32).at[0].set(c_ctx).at[1:1 + bx].set(c)
    mod = _modulation(cvecs, w_mod, b_mod)

    ctx_row = lambda i: 0
    lat_row = lambda i: 1 + i // tiles_per_lat

    rope = _rope_tables(lx) + (lambda i: i % tiles_per_lat,)
    wqkv = w_qkv[0].astype(BF16)
    g1 = norm1_g[0].reshape(1, D_MODEL)
    sink = attn_sink[0].astype(F32)
    qp, krp, vrp, kp, vp = _qkv(xp, mod[0], ctx_row, g1, wqkv, None, lp)
    qx, krx, vrx = _qkv(xx, mod[0], lat_row, g1, wqkv, rope, 0)
    op = _ctx_attention(sink, qp, krp, vrp, bp, lp)
    rep = lambda t: jnp.tile(t[:, 0].transpose(0, 2, 1, 3), (1, 1, 1, LANES // HEAD_DIM)).astype(BF16)
    ox, (w1, w2, wo, wa, wb) = _lat_attention(
        sink, qx, krx, vrx, rep(cache_k), rep(cache_v), bx, lx,
        [mlp_w1, mlp_w2, w_o[0], glu_w_a[0], glu_w_b[0]])
    g2 = norm2_g.reshape(-1, 1, D_MODEL)
    gn = norm1_g[1].reshape(1, D_MODEL)
    vpt_p = TOKEN_TILE // lp
    vpt_x = 1
    n_seg_x = tiles_per_lat
    xp, hp = _post(xp, op, mod[0], ctx_row, g2[0], wo, None, w1, w2, 0, vpt_p, mod_next=mod[1], g_next=gn)
    xx, hx = _post(xx, ox, mod[0], lat_row, g2[0], wo, None, w1, w2, 0, vpt_x, mod_next=mod[1], g_next=gn)

    params = _s5_params(ssm_lam_re[0], ssm_lam_im[0], ssm_log_dt[0], ssm_b_re[0], ssm_b_im[0],
                        ssm_c_re[0], ssm_c_im[0])
    dsk = ssm_d[0].astype(F32).reshape(1, D_MODEL)
    s0p = jnp.zeros((N_GROUP_BLOCKS, 2, bp, STATE_LANES), F32)
    sx = _state_to_blocks(state_ssm[:, 0].astype(F32))
    s0x = jnp.zeros((N_GROUP_BLOCKS, 2, bx, n_seg_x, STATE_LANES), F32)
    s0x = s0x.at[:, 0, :, 0].set(sx[:, 0]).at[:, 1, :, n_seg_x - 1].set(sx[:, 1])
    s0x = s0x.reshape(N_GROUP_BLOCKS, 2, bx * n_seg_x, STATE_LANES)
    yp, sfin = _s5(hp, params, dsk, s0p, 1)
    yx, _ = _s5(hx, params, dsk, s0x, n_seg_x)
    new_state = _blocks_to_state(sfin)[:, None]

    fg = final_norm_g.reshape(1, D_MODEL)
    (yp_out,) = _post(xp, yp, mod[1], ctx_row, g2[1], wa, wb, w1, w2, 1, vpt_p, final_g=fg)
    (yx_out,) = _post(xx, yx, mod[1], lat_row, g2[1], wa, wb, w1, w2, 1, vpt_x, final_g=fg)

    to_cache = lambda t: t.reshape(bp, N_KV_HEADS, HEAD_DIM, lp).transpose(0, 3, 1, 2)[:, None]
    new_k = to_cache(kp)
    new_v = to_cache(vp)
    return (yp_out.reshape(bp, lp, D_MODEL), yx_out.reshape(bx, lx, D_MODEL), new_k, new_v, new_state)
```

```python
import functools
import math

import numpy as np
import jax
import jax.numpy as jnp
from jax import lax
from jax.experimental import pallas as pl
from jax.experimental.pallas import tpu as pltpu

F32 = jnp.float32
BF16 = jnp.bfloat16

D_MODEL = 1024
N_HEADS = 16
N_KV_HEADS = 4
HEAD_DIM = 64
Q_PER_KV = N_HEADS // N_KV_HEADS
KV_DIM = N_KV_HEADS * HEAD_DIM
QKV_DIM = D_MODEL + 2 * KV_DIM
BLOCK = 128
GRID_W = 64
ROPE_BASE = 10000.0
ATTN_SCALE = HEAD_DIM ** -0.5
N_GROUPS = 64
GROUP_CH = 16
STATE_DIM = 64
D_FF = 4 * D_MODEL
N_MOD = 6
RMS_EPS = 1e-6
NEG_INF = -1e30

LANES = 128
SUBLANES = 8
GROUPS_PER_BLOCK = LANES // GROUP_CH
N_GROUP_BLOCKS = N_GROUPS // GROUPS_PER_BLOCK
STATE_LANES = GROUPS_PER_BLOCK * 2 * STATE_DIM
CHUNK = SUBLANES
TOKEN_TILE = 512
FF_TILE = 1024
POST_SUBTILES = 2
QKV_SUBTILES = 2
VMEM_LIMIT = 56 * 1024 * 1024


def _cparams(semantics):
    return pltpu.CompilerParams(dimension_semantics=semantics, vmem_limit_bytes=VMEM_LIMIT)


def _rms(x):
    return x * lax.rsqrt(jnp.mean(x * x, axis=-1, keepdims=True) + RMS_EPS)


def _dot(a, b):
    return jnp.dot(a, b, preferred_element_type=F32)


def _dot_nt(a, b):
    return lax.dot_general(a, b, (((1,), (1,)), ((), ())), preferred_element_type=F32)


def _mod_kernel(cv_ref, w_ref, b_ref, o_ref):
    cv = cv_ref[...]
    s = (cv * jax.nn.sigmoid(cv)).astype(BF16)
    o_ref[0] = _dot(s, w_ref[0].astype(BF16)) + b_ref[0]


def _modulation(cvecs, w_mod, b_mod):
    depth = w_mod.shape[0]
    out = pl.pallas_call(
        _mod_kernel,
        grid=(depth, N_MOD),
        in_specs=[
            pl.BlockSpec((8, D_MODEL), lambda l, j: (0, 0)),
            pl.BlockSpec((1, D_MODEL, D_MODEL), lambda l, j: (l, 0, j)),
            pl.BlockSpec((1, 1, D_MODEL), lambda l, j: (l, 0, j)),
        ],
        out_specs=pl.BlockSpec((1, 8, D_MODEL), lambda l, j: (l, 0, j)),
        out_shape=jax.ShapeDtypeStruct((depth, 8, N_MOD * D_MODEL), F32),
        compiler_params=_cparams(("arbitrary", "arbitrary")),
        name="modulation",
    )(cvecs, w_mod, b_mod.reshape(depth, 1, N_MOD * D_MODEL))
    return out.reshape(depth, 8, N_MOD, D_MODEL)


def _head_pair(blk, odd):
    lo = lax.broadcasted_iota(jnp.int32, blk.shape, 1) < HEAD_DIM
    other = pltpu.roll(blk, HEAD_DIM, 1)
    return (jnp.where(lo, other, blk) if odd else jnp.where(lo, blk, other)).astype(BF16)


def _qkv_kernel(cache_seq, rope, x_ref, mod_ref, g_ref, w_ref, *refs):
    refs = list(refs)
    cos_ref, sin_ref = (refs.pop(0), refs.pop(0)) if rope else (None, None)
    q_ref, krep_ref, vrep_ref = refs[:3]
    kv_refs = refs[3:]
    sub = TOKEN_TILE // QKV_SUBTILES
    starts = [k * sub for k in range(QKV_SUBTILES)]

    def project(r0):
        h = _rms(x_ref[r0:r0 + sub, :]) * g_ref[...] * (1.0 + mod_ref[1:2, :]) + mod_ref[0:1, :]
        return _dot(h.astype(BF16), w_ref[...])

    def emit_cache(ref, blk, r0, c0):
        for s in range(sub // cache_seq):
            ref[r0 // cache_seq + s, c0:c0 + LANES, :] = blk[s * cache_seq:(s + 1) * cache_seq, :].T

    def finish(r0, qkv):
        rows = slice(r0, r0 + sub)
        if rope:
            cos = cos_ref[rows, :]
            sin = sin_ref[rows, :]
            first = (lax.broadcasted_iota(jnp.int32, cos.shape, 1) & 31) < 16
        for blk in range((D_MODEL + KV_DIM) // LANES):
            r = qkv[:, blk * LANES:(blk + 1) * LANES]
            if rope:
                partner = jnp.where(first, pltpu.roll(r, LANES - 16, 1), pltpu.roll(r, 16, 1))
                r = r * cos + partner * sin
            if blk < D_MODEL // LANES:
                q_ref[rows, blk * LANES:(blk + 1) * LANES] = (r * ATTN_SCALE).astype(BF16)
            else:
                c0 = blk * LANES - D_MODEL
                if cache_seq:
                    emit_cache(kv_refs[0], r, r0, c0)
                for half in range(2):
                    krep_ref[c0 // HEAD_DIM + half, rows, :] = _head_pair(r, half)
        v = qkv[:, D_MODEL + KV_DIM:]
        for c0 in range(0, KV_DIM, LANES):
            blk = v[:, c0:c0 + LANES]
            if cache_seq:
                emit_cache(kv_refs[1], blk, r0, c0)
            for half in range(2):
                vrep_ref[c0 // HEAD_DIM + half, rows, :] = _head_pair(blk, half)

    for r0, qkv in zip(starts, [project(r0) for r0 in starts]):
        finish(r0, qkv)


def _qkv(x, mod, mod_row, g, w_qkv, rope, cache_seq):
    ntok = x.shape[0]
    nt = ntok // TOKEN_TILE
    emit_kv = cache_seq > 0
    assert not emit_kv or (TOKEN_TILE // QKV_SUBTILES) % cache_seq == 0
    rep_spec = pl.BlockSpec((N_KV_HEADS, TOKEN_TILE, LANES), lambda i: (0, i, 0))
    rep_shape = jax.ShapeDtypeStruct((N_KV_HEADS, ntok, LANES), BF16)
    in_specs = [
        pl.BlockSpec((TOKEN_TILE, D_MODEL), lambda i: (i, 0)),
        pl.BlockSpec((None, N_MOD, D_MODEL), lambda i: (mod_row(i), 0, 0)),
        pl.BlockSpec((1, D_MODEL), lambda i: (0, 0)),
        pl.BlockSpec((D_MODEL, QKV_DIM), lambda i: (0, 0)),
    ]
    args = [x, mod, g, w_qkv]
    if rope is not None:
        cos_t, sin_t, rope_blk = rope
        in_specs += [pl.BlockSpec((TOKEN_TILE, LANES), lambda i: (rope_blk(i), 0))] * 2
        args += [cos_t, sin_t]
    out_specs = [pl.BlockSpec((TOKEN_TILE, D_MODEL), lambda i: (i, 0)), rep_spec, rep_spec]
    out_shape = [jax.ShapeDtypeStruct((ntok, D_MODEL), BF16), rep_shape, rep_shape]
    if emit_kv:
        spt = TOKEN_TILE // cache_seq
        out_specs += [pl.BlockSpec((spt, KV_DIM, cache_seq), lambda i: (i, 0, 0))] * 2
        out_shape += [jax.ShapeDtypeStruct((ntok // cache_seq, KV_DIM, cache_seq), F32)] * 2
    return pl.pallas_call(
        functools.partial(_qkv_kernel, cache_seq, rope is not None),
        grid=(nt,),
        in_specs=in_specs,
        out_specs=out_specs,
        out_shape=out_shape,
        compiler_params=_cparams(("arbitrary",)),
        name="norm_qkv_rope",
    )(*args)


def _group_scores(q_ref, kv, key_parts, bias):
    nq = q_ref.shape[0]
    lo = lax.broadcasted_iota(jnp.int32, (nq, LANES), 1) < HEAD_DIM
    zero = jnp.zeros((), BF16)
    rows = []
    for b in range(KV_DIM // LANES):
        blk = q_ref[:, kv * KV_DIM + b * LANES:kv * KV_DIM + (b + 1) * LANES]
        rows += [jnp.where(lo, blk, zero), jnp.where(lo, zero, blk)]
    q4 = jnp.concatenate(rows, axis=0)
    parts = [_dot_nt(q4, keys) for keys in key_parts]
    if bias is not None:
        s0 = parts[0].reshape(Q_PER_KV, nq, -1) + bias[None]
        parts[0] = s0.reshape(Q_PER_KV * nq, -1)
    return parts


def _group_softmax(parts, sink_ref, kv):
    nq = parts[0].shape[0] // Q_PER_KV
    sink = jnp.concatenate(
        [jnp.full((nq, LANES), sink_ref[kv * Q_PER_KV + g], F32) for g in range(Q_PER_KV)], axis=0)
    blocks = [[s[:, j:j + LANES] for j in range(0, s.shape[1], LANES)] for s in parts]
    fold = None
    for b in sum(blocks, []):
        fold = b if fold is None else jnp.maximum(fold, b)
    m = jnp.maximum(jnp.max(fold, axis=-1, keepdims=True), sink)
    probs = [[jnp.exp(b - m) for b in bs] for bs in blocks]
    fold = None
    for p in sum(probs, []):
        fold = p if fold is None else fold + p
    den = jnp.sum(fold, axis=-1, keepdims=True) + jnp.exp(sink - m)
    return [jnp.concatenate(ps, axis=1).astype(BF16) for ps in probs], 1.0 / den


def _group_output(probs, inv_den, value_parts, o_ref, kv):
    nq = probs[0].shape[0] // Q_PER_KV
    r = _dot(probs[0], value_parts[0])
    for p, v in zip(probs[1:], value_parts[1:]):
        r = r + _dot(p, v)
    r = r * inv_den
    lo = lax.broadcasted_iota(jnp.int32, (nq, LANES), 1) < HEAD_DIM
    for b in range(KV_DIM // LANES):
        pair = jnp.where(lo, r[2 * b * nq:(2 * b + 1) * nq], r[(2 * b + 1) * nq:(2 * b + 2) * nq])
        o_ref[:, kv * KV_DIM + b * LANES:kv * KV_DIM + (b + 1) * LANES] = pair.astype(BF16)


def _attend(q_ref, sink_ref, o_ref, keys_of, values_of, bias):
    s_next = _group_scores(q_ref, 0, keys_of(0), bias)
    for kv in range(N_KV_HEADS):
        s = s_next
        if kv + 1 < N_KV_HEADS:
            s_next = _group_scores(q_ref, kv + 1, keys_of(kv + 1), bias)
        probs, inv_den = _group_softmax(s, sink_ref, kv)
        _group_output(probs, inv_den, values_of(kv), o_ref, kv)


def _ctx_attn_kernel(sink_ref, q_ref, k_ref, v_ref, o_ref):
    _attend(q_ref, sink_ref, o_ref, lambda kv: [k_ref[kv]], lambda kv: [v_ref[kv]], None)


def _ctx_attention(sink, q, krep, vrep, n_batch, seq):
    rep_spec = pl.BlockSpec((N_KV_HEADS, seq, LANES), lambda b: (0, b, 0))
    return pl.pallas_call(
        _ctx_attn_kernel,
        grid=(n_batch,),
        in_specs=[
            pl.BlockSpec(memory_space=pltpu.SMEM),
            pl.BlockSpec((seq, D_MODEL), lambda b: (b, 0)),
            rep_spec, rep_spec,
        ],
        out_specs=pl.BlockSpec((seq, D_MODEL), lambda b: (b, 0)),
        out_shape=jax.ShapeDtypeStruct((n_batch * seq, D_MODEL), BF16),
        compiler_params=_cparams(("arbitrary",)),
        name="context_attention",
    )(sink, q, krep, vrep)


def _window_start(n, seq):
    return jnp.clip((n - 1) * BLOCK, 0, seq - 3 * BLOCK)


def _band_bias():
    r = np.arange(BLOCK)[:, None]
    j = np.arange(3 * BLOCK)[None, :]
    out = [np.where(np.abs(j - d * BLOCK - r) <= BLOCK, 0.0, NEG_INF) for d in range(3)]
    return jnp.asarray(np.stack(out), F32)


def _lat_attn_kernel(seq, n_cast, sink_ref, q_ref, k_ref, v_ref, ck_ref, cv_ref, bias_ref, *refs):
    o_ref = refs[n_cast]
    for src, dst in zip(refs[:n_cast], refs[n_cast + 1:]):
        dst[...] = src[...].astype(BF16)
    win = 3 * BLOCK
    start = pl.multiple_of(_window_start(pl.program_id(1), seq), BLOCK)
    keys_of = lambda kv: [k_ref[kv, pl.ds(start, win), :], ck_ref[kv]]
    values_of = lambda kv: [v_ref[kv, pl.ds(start, win), :], cv_ref[kv]]
    _attend(q_ref, sink_ref, o_ref, keys_of, values_of, bias_ref[...])


def _lat_attention(sink, q, krep, vrep, ckrep, cvrep, n_batch, seq, cast_weights):
    nb = seq // BLOCK
    steps = n_batch * nb
    past = ckrep.shape[2]
    rep_spec = pl.BlockSpec((N_KV_HEADS, seq, LANES), lambda b, n: (0, b, 0))
    crep_spec = pl.BlockSpec((None, N_KV_HEADS, past, LANES), lambda b, n: (b, 0, 0, 0))
    flat = [w.reshape(-1, w.shape[-1]) for w in cast_weights]
    assert all(w.shape[0] % (steps * 2 * SUBLANES) == 0 for w in flat)
    slabs = [pl.BlockSpec((w.shape[0] // steps, w.shape[1]), lambda b, n: (b * nb + n, 0)) for w in flat]
    outs = pl.pallas_call(
        functools.partial(_lat_attn_kernel, seq, len(flat)),
        grid=(n_batch, nb),
        in_specs=[
            pl.BlockSpec(memory_space=pltpu.SMEM),
            pl.BlockSpec((BLOCK, D_MODEL), lambda b, n: (b * nb + n, 0)),
            rep_spec, rep_spec, crep_spec, crep_spec,
            pl.BlockSpec((None, BLOCK, 3 * BLOCK), lambda b, n: (n - _window_start(n, seq) // BLOCK, 0, 0)),
        ] + slabs,
        out_specs=[pl.BlockSpec((BLOCK, D_MODEL), lambda b, n: (b * nb + n, 0))] + slabs,
        out_shape=[jax.ShapeDtypeStruct((n_batch * seq, D_MODEL), BF16)]
        + [jax.ShapeDtypeStruct(w.shape, BF16) for w in flat],
        compiler_params=_cparams(("arbitrary", "arbitrary")),
        name="latent_attention",
    )(sink, q, krep, vrep, ckrep, cvrep, _band_bias(), *flat)
    return outs[0], [o.reshape(w.shape) for o, w in zip(outs[1:], cast_weights)]


def _gelu_tanh(x):
    c = math.sqrt(2.0 / math.pi)
    return x * (0.5 * (1.0 + jnp.tanh(c * (x + 0.044715 * (x * x * x)))))


def _post_kernel(is_attn, emit_next, final, vpt, layer, *refs):
    rows_per_v = TOKEN_TILE // vpt
    refs = list(refs)
    x_ref, mix_ref, mod_ref, g2_ref, wa_hbm = refs[:5]
    refs = refs[5:]
    wb_hbm = None if is_attn else refs.pop(0)
    w1_hbm, w2_hbm = refs[:2]
    refs = refs[2:]
    modn_ref = gn_ref = fg_ref = hn_ref = None
    if emit_next:
        modn_ref, gn_ref = refs[:2]
        refs = refs[2:]
    if final:
        fg_ref = refs.pop(0)
    xo_ref = refs.pop(0)
    if emit_next:
        hn_ref = refs.pop(0)
    wa_ref = refs.pop(0)
    wb_ref = None if is_attn else refs.pop(0)
    w1_ref, w2_ref, sem = refs
    n_ff = w1_ref.shape[0]

    proj_copies = [pltpu.make_async_copy(wa_hbm, wa_ref, sem.at[0])]
    if not is_attn:
        proj_copies.append(pltpu.make_async_copy(wb_hbm, wb_ref, sem.at[1]))
    n_proj = len(proj_copies)
    chunk_copies = []
    for c in range(n_ff):
        cols = pl.ds(c * FF_TILE, FF_TILE)
        chunk_copies.append((
            pltpu.make_async_copy(w1_hbm.at[layer, :, cols], w1_ref.at[c], sem.at[n_proj + 2 * c]),
            pltpu.make_async_copy(w2_hbm.at[layer, cols, :], w2_ref.at[c], sem.at[n_proj + 2 * c + 1])))

    sub = TOKEN_TILE // POST_SUBTILES
    assert rows_per_v % sub == 0
    n_chunk = sub // CHUNK
    starts = [k * sub for k in range(POST_SUBTILES)]

    def s5_slot(r0):
        return r0 // rows_per_v, (r0 % rows_per_v) // CHUNK

    def project(r0):
        if is_attn:
            return _dot(mix_ref[r0:r0 + sub, :], wa_ref[...])
        s, c0 = s5_slot(r0)
        y = jnp.concatenate(
            [mix_ref[g, c0:c0 + n_chunk, s].reshape(sub, LANES) for g in range(N_GROUP_BLOCKS)], axis=1)
        yg = _gelu_tanh(y).astype(BF16)
        return _dot(yg, wa_ref[...]) * jax.nn.sigmoid(_dot(yg, wb_ref[...]))

    def prologue(r0, mix):
        x1 = x_ref[r0:r0 + sub, :] + mod_ref[2:3, :] * mix
        h2 = _rms(x1) * g2_ref[...] * (1.0 + mod_ref[4:5, :]) + mod_ref[3:4, :]
        return x1, h2.astype(BF16)

    def mlp(h2, wait):
        acc = None
        for c in range(n_ff):
            if wait:
                for cp in chunk_copies[c]:
                    cp.wait()
            a = jnp.maximum(_dot(h2, w1_ref[c]), 0.0)
            t = _dot((a * a).astype(BF16), w2_ref[c])
            acc = t if acc is None else acc + t
        return acc

    def epilogue(r0, x1, acc):
        x2 = x1 + mod_ref[5:6, :] * acc
        if emit_next:
            hn = _rms(x2) * gn_ref[...] * (1.0 + modn_ref[1:2, :]) + modn_ref[0:1, :]
            s, c0 = s5_slot(r0)
            for g in range(N_GROUP_BLOCKS):
                blk = hn[:, g * LANES:(g + 1) * LANES]
                hn_ref[g, c0:c0 + n_chunk, s] = blk.reshape(n_chunk, CHUNK, LANES)
        xo_ref[r0:r0 + sub, :] = _rms(x2) * fg_ref[...] if final else x2

    def tile(first):
        if first:
            for cp in proj_copies + [cp for pair in chunk_copies for cp in pair]:
                cp.start()
            for cp in proj_copies:
                cp.wait()
        mixes = [project(r0) for r0 in starts]
        pro = [prologue(r0, mix) for r0, mix in zip(starts, mixes)]
        accs = [mlp(h2, first and k == 0) for k, (_, h2) in enumerate(pro)]
        for r0, (x1, _), acc in zip(starts, pro, accs):
            epilogue(r0, x1, acc)

    @pl.when(pl.program_id(0) == 0)
    def _():
        tile(True)

    @pl.when(pl.program_id(0) != 0)
    def _():
        tile(False)


def _post(x, mix, mod, mod_row, g2, w_a, w_b, w1, w2, layer, vpt, mod_next=None, g_next=None, final_g=None):
    is_attn = w_b is None
    emit_next = mod_next is not None
    final = final_g is not None
    ntok = x.shape[0]
    nt = ntok // TOKEN_TILE
    cpv = TOKEN_TILE // (vpt * CHUNK)
    n_virt = nt * vpt
    tile = pl.BlockSpec((TOKEN_TILE, D_MODEL), lambda i: (i, 0))
    row = pl.BlockSpec((1, D_MODEL), lambda i: (0, 0))
    modspec = pl.BlockSpec((None, N_MOD, D_MODEL), lambda i: (mod_row(i), 0, 0))
    hbm = pl.BlockSpec(memory_space=pl.ANY)
    n_ff = D_FF // FF_TILE
    gtile = pl.BlockSpec((N_GROUP_BLOCKS, cpv, vpt, CHUNK, LANES), lambda i: (0, 0, i, 0, 0))
    in_specs = [tile, tile if is_attn else gtile, modspec, row, hbm]
    args = [x, mix, mod, g2, w_a]
    scratch = [pltpu.VMEM((D_MODEL, D_MODEL), BF16)]
    if not is_attn:
        in_specs.append(hbm)
        args.append(w_b)
        scratch.append(pltpu.VMEM((D_MODEL, D_MODEL), BF16))
    in_specs += [hbm, hbm]
    args += [w1, w2]
    n_copies = len(scratch) + 2 * n_ff
    scratch += [pltpu.VMEM((n_ff, D_MODEL, FF_TILE), BF16), pltpu.VMEM((n_ff, FF_TILE, D_MODEL), BF16),
                pltpu.SemaphoreType.DMA((n_copies,))]
    if emit_next:
        in_specs += [modspec, row]
        args += [mod_next, g_next]
    if final:
        in_specs.append(row)
        args.append(final_g)
    out_specs = [tile]
    out_shape = [jax.ShapeDtypeStruct((ntok, D_MODEL), F32)]
    if emit_next:
        out_specs.append(gtile)
        out_shape.append(jax.ShapeDtypeStruct((N_GROUP_BLOCKS, cpv, n_virt, CHUNK, LANES), F32))
    return pl.pallas_call(
        functools.partial(_post_kernel, is_attn, emit_next, final, vpt, layer),
        grid=(nt,),
        in_specs=in_specs,
        out_specs=out_specs,
        out_shape=out_shape,
        scratch_shapes=scratch,
        compiler_params=_cparams(("arbitrary",)),
        name="attn_proj_mlp" if is_attn else "glu_mlp_final",
    )(*args)


def _swap(x):
    return pltpu.roll(x, LANES // 2, 1)


def _cmul(z, w_r, w_i):
    return z * w_r + _swap(z) * w_i


def _multiplier(z, lo):
    zs = _swap(z)
    return jnp.where(lo, z, zs), jnp.where(lo, -zs, z)


def _rep_rows(x):
    return jnp.concatenate(
        [jnp.broadcast_to(x[g:g + 1, :], (GROUP_CH, LANES)) for g in range(GROUPS_PER_BLOCK)], axis=0)


def _s5_kernel(n_virt, n_seg, h_ref, lamr_ref, lami_ref, ldt_ref, bt_ref, cp_ref, dsk_ref, s0_ref, *refs):
    if n_seg == 1:
        y_ref, sfin_ref = refs[:2]
        refs = refs[2:]
    else:
        y_ref, sfin_ref = refs[0], None
        refs = refs[1:]
    f_scr, e_scr, k_scr, sf_scr, sb_scr, swf_scr, swb_scr = refs
    ntok = h_ref.shape[0]
    nc = ntok // CHUNK
    cpv = nc // n_virt
    gpb = GROUPS_PER_BLOCK

    lo8 = lax.broadcasted_iota(jnp.int32, (gpb, LANES), 1) < STATE_DIM
    lo = lax.broadcasted_iota(jnp.int32, (LANES, LANES), 1) < STATE_DIM
    conj = jnp.where(lo, 1.0, -1.0)
    row_g = lax.broadcasted_iota(jnp.int32, (LANES, STATE_LANES), 0) >> 4
    col_g = lax.broadcasted_iota(jnp.int32, (LANES, STATE_LANES), 1) >> 7
    diag_wide = row_g == col_g
    diag = (lax.broadcasted_iota(jnp.int32, (LANES, LANES), 0) >> 4) == (
        lax.broadcasted_iota(jnp.int32, (LANES, LANES), 1) >> 4)

    def expand(w):
        return jnp.where(diag_wide, jnp.concatenate([w] * gpb, axis=1), jnp.zeros((), BF16))

    decay = []
    lag = []
    for d in range(2):
        lam_r = lamr_ref[d]
        lam_i = lami_ref[d]
        dt = jnp.exp(ldt_ref[d])
        mag = jnp.exp(lam_r * dt)
        ang = lam_i * dt
        a_r = mag * jnp.cos(ang)
        a_im = mag * jnp.sin(ang)
        a_i = jnp.where(lo8, -a_im, a_im)
        den = lam_r * lam_r + lam_i * lam_i
        num = jnp.where(lo8, a_r - 1.0, a_im)
        f = _cmul(num, lam_r / den, jnp.where(lo8, lam_i, -lam_i) / den)
        pw = [jnp.where(lo8, 1.0, 0.0)]
        for _ in range(CHUNK):
            pw.append(_cmul(pw[-1], a_r, a_i))
        decay.append(pw[CHUNK])
        pw = [_rep_rows(p) for p in pw]
        f_r, f_i = _multiplier(_rep_rows(f), lo)
        bb_r, bb_i = _multiplier(_cmul(bt_ref[d], f_r, f_i), lo)
        c_r, c_i = _multiplier(cp_ref[d], lo)
        cm = (cp_ref[d] * conj).astype(BF16)
        fpow = [_cmul(p, bb_r, bb_i).astype(BF16) for p in pw[:CHUNK]]
        for j in range(CHUNK):
            e = (CHUNK - 1 - j) if d == 0 else j
            f_scr[d, j * LANES:(j + 1) * LANES, :] = expand(fpow[e])
        for t in range(CHUNK):
            e = (t + 1) if d == 0 else (CHUNK - t)
            w = _cmul(pw[e], c_r, c_i) * conj
            e_scr[d, t * LANES:(t + 1) * LANES, :] = expand(w.astype(BF16))
        lag.append([jnp.where(diag, _dot_nt(fp, cm), 0.0) for fp in fpow])

    for j in range(CHUNK):
        for t in range(CHUNK):
            k = t - j
            tile = lag[0][k] if k > 0 else (lag[1][-k] if k < 0 else lag[0][0] + lag[1][0])
            k_scr[j * LANES:(j + 1) * LANES, t * LANES:(t + 1) * LANES] = tile.astype(BF16)

    xcat = jnp.concatenate(
        [h_ref[pl.ds(j, nc, stride=CHUNK), :].astype(BF16) for j in range(CHUNK)], axis=1)

    for d, scr, sw_scr in ((0, sf_scr, swf_scr), (1, sb_scr, swb_scr)):
        loc_all = _dot(xcat, f_scr[d])
        for k in range(gpb):
            loc = loc_all[:, k * LANES:(k + 1) * LANES]
            scr[k] = loc
            sw_scr[k] = _swap(loc)
    y_within = _dot(xcat, k_scr[...])

    sgn8 = jnp.where(lo8, -1.0, 1.0)

    def dup(z):
        zs = _swap(z)
        return jnp.where(lo8, z, zs), jnp.where(lo8, zs, z)

    def scan(scr, sw_scr, d, reverse):
        a_re, a_im = dup(decay[d])
        a_sg = a_im * sgn8
        a_r = [a_re[k:k + 1, :] for k in range(gpb)]
        a_i = [a_sg[k:k + 1, :] for k in range(gpb)]

        def rows_of(i):
            c = (cpv - 1 - i) if reverse else i
            return pl.ds(c * n_virt, n_virt)

        def body(i, carry):
            st, sw = carry
            rows = rows_of(i)
            new_st, new_sw = [], []
            for k in range(gpb):
                loc = scr[k, rows, :]
                loc_sw = sw_scr[k, rows, :]
                scr[k, rows, :] = st[k]
                new_st.append(a_r[k] * st[k] + a_i[k] * sw[k] + loc)
                new_sw.append(a_r[k] * sw[k] - a_i[k] * st[k] + loc_sw)
            return tuple(new_st), tuple(new_sw)

        st0 = tuple(s0_ref[d, :, k * LANES:(k + 1) * LANES] for k in range(gpb))
        sw0 = tuple(_swap(s) for s in st0)
        carry = (st0, sw0)
        for i in range(cpv):
            carry = body(i, carry)
        fin = carry[0]
        if n_seg == 1:
            for k in range(gpb):
                sfin_ref[d, :, k * LANES:(k + 1) * LANES] = fin[k]
            return

        p = decay[d]
        for _ in range(cpv.bit_length() - 1):
            p_re, p_im = dup(p)
            p = _cmul(p, p_re, p_im * sgn8)
        v_re, v_im = dup(p)
        v_sg = v_im * sgn8
        seg = lax.broadcasted_iota(jnp.int32, (n_virt, LANES), 0) & (n_seg - 1)
        has_pred = seg != ((n_seg - 1) if reverse else 0)
        shift = (n_virt - 1) if reverse else 1
        cin = []
        for k in range(gpb):
            ck = jnp.zeros((n_virt, LANES), F32)
            for _ in range(n_seg - 1):
                nxt = fin[k] + ck * v_re[k:k + 1, :] + _swap(ck) * v_sg[k:k + 1, :]
                ck = jnp.where(has_pred, pltpu.roll(nxt, shift, 0), 0.0)
            cin.append(ck)
        cin_sw = [_swap(x) for x in cin]

        def fix(i, carry):
            q_re, q_im = carry
            rows = rows_of(i)
            q_sg = q_im * sgn8
            for k in range(gpb):
                scr[k, rows, :] += cin[k] * q_re[k:k + 1, :] + cin_sw[k] * q_sg[k:k + 1, :]
            return q_re * a_re - q_im * a_im, q_re * a_im + q_im * a_re

        q = (jnp.ones((gpb, LANES), F32), jnp.zeros((gpb, LANES), F32))
        for i in range(cpv):
            q = fix(i, q)

    scan(sf_scr, swf_scr, 0, False)
    scan(sb_scr, swb_scr, 1, True)

    s_f = jnp.concatenate([sf_scr[k].astype(BF16) for k in range(gpb)], axis=1)
    s_b = jnp.concatenate([sb_scr[k].astype(BF16) for k in range(gpb)], axis=1)
    yall = y_within + _dot_nt(s_f, e_scr[0]) + _dot_nt(s_b, e_scr[1])
    dsk = dsk_ref[...]
    for t in range(CHUNK):
        rows = pl.ds(t, nc, stride=CHUNK)
        y_ref[rows, :] = yall[:, t * LANES:(t + 1) * LANES] + h_ref[rows, :] * dsk


def _s5_params(lam_re, lam_im, log_dt, b_re, b_im, c_re, c_im):
    lamr = jnp.concatenate([lam_re, lam_re], axis=-1).astype(F32)
    lami = jnp.concatenate([lam_im, lam_im], axis=-1).astype(F32)
    ldt = jnp.broadcast_to(log_dt.astype(F32)[..., None], lamr.shape)
    bt = jnp.concatenate([b_re.transpose(0, 1, 3, 2), b_im.transpose(0, 1, 3, 2)], axis=-1)
    cp = jnp.concatenate([c_re, c_im], axis=-1)
    return (lamr, lami, ldt, bt.reshape(2, D_MODEL, LANES).astype(F32), cp.reshape(2, D_MODEL, LANES).astype(F32))


def _s5(h, params, d_skip, s0, n_seg):
    lamr, lami, ldt, bt, cp = params
    _, cpv, n_virt, _, _ = h.shape
    assert cpv & (cpv - 1) == 0 and n_virt % SUBLANES == 0 and n_seg & (n_seg - 1) == 0
    nc = cpv * n_virt
    ntok = nc * CHUNK
    hspec = pl.BlockSpec((None, ntok, LANES), lambda g: (g, 0, 0))
    kdim = CHUNK * LANES
    gspec = pl.BlockSpec((2, GROUPS_PER_BLOCK, LANES), lambda g: (0, g, 0))
    rspec = pl.BlockSpec((2, LANES, LANES), lambda g: (0, g, 0))
    sspec = pl.BlockSpec((None, 2, n_virt, STATE_LANES), lambda g: (g, 0, 0, 0))
    state_scr = pltpu.VMEM((GROUPS_PER_BLOCK, nc, LANES), F32)
    out_specs = [hspec]
    out_shape = [jax.ShapeDtypeStruct((N_GROUP_BLOCKS, ntok, LANES), F32)]
    if n_seg == 1:
        out_specs.append(sspec)
        out_shape.append(jax.ShapeDtypeStruct((N_GROUP_BLOCKS, 2, n_virt, STATE_LANES), F32))
    outs = pl.pallas_call(
        functools.partial(_s5_kernel, n_virt, n_seg),
        grid=(N_GROUP_BLOCKS,),
        in_specs=[
            hspec,
            gspec, gspec, gspec, rspec, rspec,
            pl.BlockSpec((1, LANES), lambda g: (0, g)),
            sspec,
        ],
        out_specs=out_specs,
        out_shape=out_shape,
        scratch_shapes=[
            pltpu.VMEM((2, kdim, STATE_LANES), BF16),
            pltpu.VMEM((2, kdim, STATE_LANES), BF16),
            pltpu.VMEM((kdim, kdim), BF16),
            state_scr, state_scr, state_scr, state_scr,
        ],
        compiler_params=_cparams(("arbitrary",)),
        name="s5_chunked_scan",
    )(h.reshape(N_GROUP_BLOCKS, ntok, LANES), lamr, lami, ldt, bt, cp, d_skip, s0)
    y = outs[0].reshape(h.shape)
    return (y, outs[1]) if n_seg == 1 else (y, None)


def _state_to_blocks(s):
    b = s.shape[0]
    s = s.reshape(b, 2, 2, N_GROUP_BLOCKS, GROUPS_PER_BLOCK, STATE_DIM)
    return s.transpose(3, 1, 0, 4, 2, 5).reshape(N_GROUP_BLOCKS, 2, b, STATE_LANES)


def _blocks_to_state(s):
    b = s.shape[2]
    s = s.reshape(N_GROUP_BLOCKS, 2, b, GROUPS_PER_BLOCK, 2, STATE_DIM)
    return s.transpose(2, 1, 4, 0, 3, 5).reshape(b, 2, 2, N_GROUPS, STATE_DIM)


def _rope_tables(n_tokens):
    pos = np.arange(n_tokens)
    n_freq = HEAD_DIM // 4
    freqs = ROPE_BASE ** (-np.arange(n_freq, dtype=np.float64) / n_freq)
    ang_r = (pos // GRID_W)[:, None] * freqs
    ang_c = (pos % GRID_W)[:, None] * freqs
    cos_h = np.concatenate([np.cos(ang_r), np.cos(ang_r), np.cos(ang_c), np.cos(ang_c)], axis=1)
    sin_h = np.concatenate([-np.sin(ang_r), np.sin(ang_r), -np.sin(ang_c), np.sin(ang_c)], axis=1)
    return jnp.asarray(np.tile(cos_h, (1, 2)), F32), jnp.asarray(np.tile(sin_h, (1, 2)), F32)


def kernel(x_prompt, x_sample, cache_k, cache_v, state_ssm, c, c_ctx, norm1_g, norm2_g, w_mod, b_mod,
           w_qkv, w_o, attn_sink, ssm_lam_re, ssm_lam_im, ssm_log_dt, ssm_b_re, ssm_b_im, ssm_c_re,
           ssm_c_im, ssm_d, glu_w_a, glu_w_b, mlp_w1, mlp_w2, final_norm_g):
    bp, lp, _ = x_prompt.shape
    bx, lx, _ = x_sample.shape
    assert lx % TOKEN_TILE == 0 and (bp * lp) % TOKEN_TILE == 0
    tiles_per_lat = lx // TOKEN_TILE

    xp = x_prompt.reshape(bp * lp, D_MODEL)
    xx = x_sample.reshape(bx * lx, D_MODEL)

    cvecs = jnp.zeros((8, D_MODEL), F32).at[0].set(c_ctx).at[1:1 + bx].set(c)
    mod = _modulation(cvecs, w_mod, b_mod)

    ctx_row = lambda i: 0
    lat_row = lambda i: 1 + i // tiles_per_lat

    rope = _rope_tables(lx) + (lambda i: i % tiles_per_lat,)
    wqkv = w_qkv[0].astype(BF16)
    g1 = norm1_g[0].reshape(1, D_MODEL)
    sink = attn_sink[0].astype(F32)
    qp, krp, vrp, kp, vp = _qkv(xp, mod[0], ctx_row, g1, wqkv, None, lp)
    qx, krx, vrx = _qkv(xx, mod[0], lat_row, g1, wqkv, rope, 0)
    op = _ctx_attention(sink, qp, krp, vrp, bp, lp)
    rep = lambda t: jnp.tile(t[:, 0].transpose(0, 2, 1, 3), (1, 1, 1, LANES // HEAD_DIM)).astype(BF16)
    ox, (w1, w2, wo, wa, wb) = _lat_attention(
        sink, qx, krx, vrx, rep(cache_k), rep(cache_v), bx, lx,
        [mlp_w1, mlp_w2, w_o[0], glu_w_a[0], glu_w_b[0]])
    g2 = norm2_g.reshape(-1, 1, D_MODEL)
    gn = norm1_g[1].reshape(1, D_MODEL)
    vpt_p = TOKEN_TILE // lp
    vpt_x = 1
    n_seg_x = tiles_per_lat
    xp, hp = _post(xp, op, mod[0], ctx_row, g2[0], wo, None, w1, w2, 0, vpt_p, mod_next=mod[1], g_next=gn)
    xx, hx = _post(xx, ox, mod[0], lat_row, g2[0], wo, None, w1, w2, 0, vpt_x, mod_next=mod[1], g_next=gn)

    params = _s5_params(ssm_lam_re[0], ssm_lam_im[0], ssm_log_dt[0], ssm_b_re[0], ssm_b_im[0],
                        ssm_c_re[0], ssm_c_im[0])
    dsk = ssm_d[0].astype(F32).reshape(1, D_MODEL)
    s0p = jnp.zeros((N_GROUP_BLOCKS, 2, bp, STATE_LANES), F32)
    sx = _state_to_blocks(state_ssm[:, 0].astype(F32))
    s0x = jnp.zeros((N_GROUP_BLOCKS, 2, bx, n_seg_x, STATE_LANES), F32)
    s0x = s0x.at[:, 0, :, 0].set(sx[:, 0]).at[:, 1, :, n_seg_x - 1].set(sx[:, 1])
    s0x = s0x.reshape(N_GROUP_BLOCKS, 2, bx * n_seg_x, STATE_LANES)
    yp, sfin = _s5(hp, params, dsk, s0p, 1)
    yx, _ = _s5(hx, params, dsk, s0x, n_seg_x)
    new_state = _blocks_to_state(sfin)[:, None]

    fg = final_norm_g.reshape(1, D_MODEL)
    (yp_out,) = _post(xp, yp, mod[1], ctx_row, g2[1], wa, wb, w1, w2, 1, vpt_p, final_g=fg)
    (yx_out,) = _post(xx, yx, mod[1], lat_row, g2[1], wa, wb, w1, w2, 1, vpt_x, final_g=fg)

    to_cache = lambda t: t.reshape(bp, N_KV_HEADS, HEAD_DIM, lp).transpose(0, 3, 1, 2)[:, None]
    new_k = to_cache(kp)
    new_v = to_cache(vp)
    return (yp_out.reshape(bp, lp, D_MODEL), yx_out.reshape(bx, lx, D_MODEL), new_k, new_v, new_state)
```

```python
import functools
import math

import numpy as np
import jax
import jax.numpy as jnp
from jax import lax
from jax.experimental import pallas as pl
from jax.experimental.pallas import tpu as pltpu

F32 = jnp.float32
BF16 = jnp.bfloat16

D_MODEL = 1024
N_HEADS = 16
N_KV_HEADS = 4
HEAD_DIM = 64
Q_PER_KV = N_HEADS // N_KV_HEADS
KV_DIM = N_KV_HEADS * HEAD_DIM
QKV_DIM = D_MODEL + 2 * KV_DIM
BLOCK = 128
GRID_W = 64
ROPE_BASE = 10000.0
ATTN_SCALE = HEAD_DIM ** -0.5
N_GROUPS = 64
GROUP_CH = 16
STATE_DIM = 64
D_FF = 4 * D_MODEL
N_MOD = 6
RMS_EPS = 1e-6
NEG_INF = -1e30

LANES = 128
SUBLANES = 8
GROUPS_PER_BLOCK = LANES // GROUP_CH
N_GROUP_BLOCKS = N_GROUPS // GROUPS_PER_BLOCK
STATE_LANES = GROUPS_PER_BLOCK * 2 * STATE_DIM
CHUNK = SUBLANES
TOKEN_TILE = 512
FF_TILE = 1024
POST_SUBTILES = 2
QKV_TILE = 1024
QKV_SUBTILES = 4
VMEM_LIMIT = 56 * 1024 * 1024


def _cparams(semantics):
    return pltpu.CompilerParams(dimension_semantics=semantics, vmem_limit_bytes=VMEM_LIMIT)


def _rms(x):
    return x * lax.rsqrt(jnp.mean(x * x, axis=-1, keepdims=True) + RMS_EPS)


def _dot(a, b):
    return jnp.dot(a, b, preferred_element_type=F32)


def _dot_nt(a, b):
    return lax.dot_general(a, b, (((1,), (1,)), ((), ())), preferred_element_type=F32)


def _mod_kernel(cv_ref, w_ref, b_ref, o_ref):
    cv = cv_ref[...]
    s = (cv * jax.nn.sigmoid(cv)).astype(BF16)
    o_ref[0] = _dot(s, w_ref[0].astype(BF16)) + b_ref[0]


def _modulation(cvecs, w_mod, b_mod):
    depth = w_mod.shape[0]
    width = 2 * D_MODEL
    out = pl.pallas_call(
        _mod_kernel,
        grid=(depth, N_MOD * D_MODEL // width),
        in_specs=[
            pl.BlockSpec((8, D_MODEL), lambda l, j: (0, 0)),
            pl.BlockSpec((1, D_MODEL, width), lambda l, j: (l, 0, j)),
            pl.BlockSpec((1, 1, width), lambda l, j: (l, 0, j)),
        ],
        out_specs=pl.BlockSpec((1, 8, width), lambda l, j: (l, 0, j)),
        out_shape=jax.ShapeDtypeStruct((depth, 8, N_MOD * D_MODEL), F32),
        compiler_params=_cparams(("arbitrary", "arbitrary")),
        name="modulation",
    )(cvecs, w_mod, b_mod.reshape(depth, 1, N_MOD * D_MODEL))
    return out.reshape(depth, 8, N_MOD, D_MODEL)


def _head_pair(blk, odd):
    lo = lax.broadcasted_iota(jnp.int32, blk.shape, 1) < HEAD_DIM
    other = pltpu.roll(blk, HEAD_DIM, 1)
    return (jnp.where(lo, other, blk) if odd else jnp.where(lo, blk, other)).astype(BF16)


def _qkv_kernel(cache_seq, rope, x_ref, mod_ref, g_ref, w_ref, *refs):
    refs = list(refs)
    cos_ref, sin_ref = (refs.pop(0), refs.pop(0)) if rope else (None, None)
    q_ref, krep_ref, vrep_ref = refs[:3]
    kv_refs = refs[3:]
    sub = QKV_TILE // QKV_SUBTILES
    starts = [k * sub for k in range(QKV_SUBTILES)]

    def project(r0):
        h = _rms(x_ref[r0:r0 + sub, :]) * g_ref[...] * (1.0 + mod_ref[1:2, :]) + mod_ref[0:1, :]
        return _dot(h.astype(BF16), w_ref[...])

    def emit_cache(ref, blk, r0, c0):
        for s in range(sub // cache_seq):
            ref[r0 // cache_seq + s, c0:c0 + LANES, :] = blk[s * cache_seq:(s + 1) * cache_seq, :].T

    def finish(r0, qkv):
        rows = slice(r0, r0 + sub)
        if rope:
            cos = cos_ref[rows, :]
            sin = sin_ref[rows, :]
            first = (lax.broadcasted_iota(jnp.int32, cos.shape, 1) & 31) < 16
        for blk in range((D_MODEL + KV_DIM) // LANES):
            r = qkv[:, blk * LANES:(blk + 1) * LANES]
            if rope:
                partner = jnp.where(first, pltpu.roll(r, LANES - 16, 1), pltpu.roll(r, 16, 1))
                r = r * cos + partner * sin
            if blk < D_MODEL // LANES:
                q_ref[rows, blk * LANES:(blk + 1) * LANES] = (r * ATTN_SCALE).astype(BF16)
            else:
                c0 = blk * LANES - D_MODEL
                if cache_seq:
                    emit_cache(kv_refs[0], r, r0, c0)
                for half in range(2):
                    krep_ref[c0 // HEAD_DIM + half, rows, :] = _head_pair(r, half)
        v = qkv[:, D_MODEL + KV_DIM:]
        for c0 in range(0, KV_DIM, LANES):
            blk = v[:, c0:c0 + LANES]
            if cache_seq:
                emit_cache(kv_refs[1], blk, r0, c0)
            for half in range(2):
                vrep_ref[c0 // HEAD_DIM + half, rows, :] = _head_pair(blk, half)

    for r0, qkv in zip(starts, [project(r0) for r0 in starts]):
        finish(r0, qkv)


def _qkv(x, mod, mod_row, g, w_qkv, rope, cache_seq):
    ntok = x.shape[0]
    nt = ntok // QKV_TILE
    emit_kv = cache_seq > 0
    assert not emit_kv or (QKV_TILE // QKV_SUBTILES) % cache_seq == 0
    rep_spec = pl.BlockSpec((N_KV_HEADS, QKV_TILE, LANES), lambda i: (0, i, 0))
    rep_shape = jax.ShapeDtypeStruct((N_KV_HEADS, ntok, LANES), BF16)
    in_specs = [
        pl.BlockSpec((QKV_TILE, D_MODEL), lambda i: (i, 0)),
        pl.BlockSpec((None, N_MOD, D_MODEL), lambda i: (mod_row(i), 0, 0)),
        pl.BlockSpec((1, D_MODEL), lambda i: (0, 0)),
        pl.BlockSpec((D_MODEL, QKV_DIM), lambda i: (0, 0)),
    ]
    args = [x, mod, g, w_qkv]
    if rope is not None:
        cos_t, sin_t, rope_blk = rope
        in_specs += [pl.BlockSpec((QKV_TILE, LANES), lambda i: (rope_blk(i), 0))] * 2
        args += [cos_t, sin_t]
    out_specs = [pl.BlockSpec((QKV_TILE, D_MODEL), lambda i: (i, 0)), rep_spec, rep_spec]
    out_shape = [jax.ShapeDtypeStruct((ntok, D_MODEL), BF16), rep_shape, rep_shape]
    if emit_kv:
        spt = QKV_TILE // cache_seq
        out_specs += [pl.BlockSpec((spt, KV_DIM, cache_seq), lambda i: (i, 0, 0))] * 2
        out_shape += [jax.ShapeDtypeStruct((ntok // cache_seq, KV_DIM, cache_seq), F32)] * 2
    return pl.pallas_call(
        functools.partial(_qkv_kernel, cache_seq, rope is not None),
        grid=(nt,),
        in_specs=in_specs,
        out_specs=out_specs,
        out_shape=out_shape,
        compiler_params=_cparams(("arbitrary",)),
        name="norm_qkv_rope",
    )(*args)


def _group_scores(q_ref, kv, key_parts, bias):
    nq = q_ref.shape[0]
    lo = lax.broadcasted_iota(jnp.int32, (nq, LANES), 1) < HEAD_DIM
    zero = jnp.zeros((), BF16)
    rows = []
    for b in range(KV_DIM // LANES):
        blk = q_ref[:, kv * KV_DIM + b * LANES:kv * KV_DIM + (b + 1) * LANES]
        rows += [jnp.where(lo, blk, zero), jnp.where(lo, zero, blk)]
    q4 = jnp.concatenate(rows, axis=0)
    parts = [_dot_nt(q4, keys) for keys in key_parts]
    if bias is not None:
        s0 = parts[0].reshape(Q_PER_KV, nq, -1) + bias[None]
        parts[0] = s0.reshape(Q_PER_KV * nq, -1)
    return parts


def _group_softmax(parts, sink_ref, kv):
    nq = parts[0].shape[0] // Q_PER_KV
    sink = jnp.concatenate(
        [jnp.full((nq, LANES), sink_ref[kv * Q_PER_KV + g], F32) for g in range(Q_PER_KV)], axis=0)
    blocks = [[s[:, j:j + LANES] for j in range(0, s.shape[1], LANES)] for s in parts]
    fold = None
    for b in sum(blocks, []):
        fold = b if fold is None else jnp.maximum(fold, b)
    m = jnp.maximum(jnp.max(fold, axis=-1, keepdims=True), sink)
    probs = [[jnp.exp(b - m) for b in bs] for bs in blocks]
    fold = None
    for p in sum(probs, []):
        fold = p if fold is None else fold + p
    den = jnp.sum(fold, axis=-1, keepdims=True) + jnp.exp(sink - m)
    return [jnp.concatenate(ps, axis=1).astype(BF16) for ps in probs], 1.0 / den


def _group_output(probs, inv_den, value_parts, o_ref, kv):
    nq = probs[0].shape[0] // Q_PER_KV
    r = _dot(probs[0], value_parts[0])
    for p, v in zip(probs[1:], value_parts[1:]):
        r = r + _dot(p, v)
    r = r * inv_den
    lo = lax.broadcasted_iota(jnp.int32, (nq, LANES), 1) < HEAD_DIM
    for b in range(KV_DIM // LANES):
        pair = jnp.where(lo, r[2 * b * nq:(2 * b + 1) * nq], r[(2 * b + 1) * nq:(2 * b + 2) * nq])
        o_ref[:, kv * KV_DIM + b * LANES:kv * KV_DIM + (b + 1) * LANES] = pair.astype(BF16)


def _attend(q_ref, sink_ref, o_ref, keys_of, values_of, bias):
    s_next = _group_scores(q_ref, 0, keys_of(0), bias)
    for kv in range(N_KV_HEADS):
        s = s_next
        if kv + 1 < N_KV_HEADS:
            s_next = _group_scores(q_ref, kv + 1, keys_of(kv + 1), bias)
        probs, inv_den = _group_softmax(s, sink_ref, kv)
        _group_output(probs, inv_den, values_of(kv), o_ref, kv)


def _ctx_attn_kernel(sink_ref, q_ref, k_ref, v_ref, o_ref):
    _attend(q_ref, sink_ref, o_ref, lambda kv: [k_ref[kv]], lambda kv: [v_ref[kv]], None)


def _ctx_attention(sink, q, krep, vrep, n_batch, seq):
    rep_spec = pl.BlockSpec((N_KV_HEADS, seq, LANES), lambda b: (0, b, 0))
    return pl.pallas_call(
        _ctx_attn_kernel,
        grid=(n_batch,),
        in_specs=[
            pl.BlockSpec(memory_space=pltpu.SMEM),
            pl.BlockSpec((seq, D_MODEL), lambda b: (b, 0)),
            rep_spec, rep_spec,
        ],
        out_specs=pl.BlockSpec((seq, D_MODEL), lambda b: (b, 0)),
        out_shape=jax.ShapeDtypeStruct((n_batch * seq, D_MODEL), BF16),
        compiler_params=_cparams(("arbitrary",)),
        name="context_attention",
    )(sink, q, krep, vrep)


def _window_start(n, seq):
    return jnp.clip((n - 1) * BLOCK, 0, seq - 3 * BLOCK)


def _band_bias():
    r = np.arange(BLOCK)[:, None]
    j = np.arange(3 * BLOCK)[None, :]
    out = [np.where(np.abs(j - d * BLOCK - r) <= BLOCK, 0.0, NEG_INF) for d in range(3)]
    return jnp.asarray(np.stack(out), F32)


def _lat_attn_kernel(seq, n_cast, sink_ref, q_ref, k_ref, v_ref, ck_ref, cv_ref, bias_ref, *refs):
    o_ref = refs[n_cast]
    for src, dst in zip(refs[:n_cast], refs[n_cast + 1:]):
        dst[...] = src[...].astype(BF16)
    win = 3 * BLOCK
    start = pl.multiple_of(_window_start(pl.program_id(1), seq), BLOCK)
    keys_of = lambda kv: [k_ref[kv, pl.ds(start, win), :], ck_ref[kv]]
    values_of = lambda kv: [v_ref[kv, pl.ds(start, win), :], cv_ref[kv]]
    _attend(q_ref, sink_ref, o_ref, keys_of, values_of, bias_ref[...])


def _lat_attention(sink, q, krep, vrep, ckrep, cvrep, n_batch, seq, cast_weights):
    nb = seq // BLOCK
    steps = n_batch * nb
    past = ckrep.shape[2]
    rep_spec = pl.BlockSpec((N_KV_HEADS, seq, LANES), lambda b, n: (0, b, 0))
    crep_spec = pl.BlockSpec((None, N_KV_HEADS, past, LANES), lambda b, n: (b, 0, 0, 0))
    flat = [w.reshape(-1, w.shape[-1]) for w in cast_weights]
    assert all(w.shape[0] % (steps * 2 * SUBLANES) == 0 for w in flat)
    slabs = [pl.BlockSpec((w.shape[0] // steps, w.shape[1]), lambda b, n: (b * nb + n, 0)) for w in flat]
    outs = pl.pallas_call(
        functools.partial(_lat_attn_kernel, seq, len(flat)),
        grid=(n_batch, nb),
        in_specs=[
            pl.BlockSpec(memory_space=pltpu.SMEM),
            pl.BlockSpec((BLOCK, D_MODEL), lambda b, n: (b * nb + n, 0)),
            rep_spec, rep_spec, crep_spec, crep_spec,
            pl.BlockSpec((None, BLOCK, 3 * BLOCK), lambda b, n: (n - _window_start(n, seq) // BLOCK, 0, 0)),
        ] + slabs,
        out_specs=[pl.BlockSpec((BLOCK, D_MODEL), lambda b, n: (b * nb + n, 0))] + slabs,
        out_shape=[jax.ShapeDtypeStruct((n_batch * seq, D_MODEL), BF16)]
        + [jax.ShapeDtypeStruct(w.shape, BF16) for w in flat],
        compiler_params=_cparams(("arbitrary", "arbitrary")),
        name="latent_attention",
    )(sink, q, krep, vrep, ckrep, cvrep, _band_bias(), *flat)
    return outs[0], [o.reshape(w.shape) for o, w in zip(outs[1:], cast_weights)]


def _gelu_tanh(x):
    c = math.sqrt(2.0 / math.pi)
    return x * (0.5 * (1.0 + jnp.tanh(c * (x + 0.044715 * (x * x * x)))))


def _post_kernel(is_attn, emit_next, final, vpt, *refs):
    rows_per_v = TOKEN_TILE // vpt
    refs = list(refs)
    x_ref, mix_ref, mod_ref, g2_ref, wa_ref = refs[:5]
    refs = refs[5:]
    wb_ref = None if is_attn else refs.pop(0)
    w1_ref, w2_ref = refs[:2]
    refs = refs[2:]
    modn_ref = gn_ref = fg_ref = hn_ref = None
    if emit_next:
        modn_ref, gn_ref = refs[:2]
        refs = refs[2:]
    if final:
        fg_ref = refs.pop(0)
    xo_ref = refs.pop(0)
    if emit_next:
        hn_ref = refs.pop(0)
    assert not refs

    sub = TOKEN_TILE // POST_SUBTILES
    assert rows_per_v % sub == 0
    n_chunk = sub // CHUNK
    starts = [k * sub for k in range(POST_SUBTILES)]

    def s5_slot(r0):
        return r0 // rows_per_v, (r0 % rows_per_v) // CHUNK

    def project(r0):
        if is_attn:
            return _dot(mix_ref[r0:r0 + sub, :], wa_ref[...])
        s, c0 = s5_slot(r0)
        y = jnp.concatenate(
            [mix_ref[g, c0:c0 + n_chunk, s].reshape(sub, LANES) for g in range(N_GROUP_BLOCKS)], axis=1)
        yg = _gelu_tanh(y).astype(BF16)
        return _dot(yg, wa_ref[...]) * jax.nn.sigmoid(_dot(yg, wb_ref[...]))

    def prologue(r0, mix):
        x1 = x_ref[r0:r0 + sub, :] + mod_ref[2:3, :] * mix
        h2 = _rms(x1) * g2_ref[...] * (1.0 + mod_ref[4:5, :]) + mod_ref[3:4, :]
        return x1, h2.astype(BF16)

    def mlp(h2):
        acc = None
        for c in range(D_FF // FF_TILE):
            a = jnp.maximum(_dot(h2, w1_ref[:, c * FF_TILE:(c + 1) * FF_TILE]), 0.0)
            t = _dot((a * a).astype(BF16), w2_ref[c * FF_TILE:(c + 1) * FF_TILE, :])
            acc = t if acc is None else acc + t
        return acc

    def epilogue(r0, x1, acc):
        x2 = x1 + mod_ref[5:6, :] * acc
        if emit_next:
            hn = _rms(x2) * gn_ref[...] * (1.0 + modn_ref[1:2, :]) + modn_ref[0:1, :]
            s, c0 = s5_slot(r0)
            for g in range(N_GROUP_BLOCKS):
                blk = hn[:, g * LANES:(g + 1) * LANES]
                hn_ref[g, c0:c0 + n_chunk, s] = blk.reshape(n_chunk, CHUNK, LANES)
        xo_ref[r0:r0 + sub, :] = _rms(x2) * fg_ref[...] if final else x2

    mixes = [project(r0) for r0 in starts]
    pro = [prologue(r0, mix) for r0, mix in zip(starts, mixes)]
    accs = [mlp(h2) for _, h2 in pro]
    for r0, (x1, _), acc in zip(starts, pro, accs):
        epilogue(r0, x1, acc)


def _post(x, mix, mod, mod_row, g2, w_a, w_b, w1, w2, layer, vpt, mod_next=None, g_next=None, final_g=None):
    is_attn = w_b is None
    emit_next = mod_next is not None
    final = final_g is not None
    ntok = x.shape[0]
    nt = ntok // TOKEN_TILE
    cpv = TOKEN_TILE // (vpt * CHUNK)
    n_virt = nt * vpt
    tile = pl.BlockSpec((TOKEN_TILE, D_MODEL), lambda i: (i, 0))
    row = pl.BlockSpec((1, D_MODEL), lambda i: (0, 0))
    modspec = pl.BlockSpec((None, N_MOD, D_MODEL), lambda i: (mod_row(i), 0, 0))
    resident = lambda shape: pl.BlockSpec(shape, lambda i: (0, 0), pipeline_mode=pl.Buffered(1))
    wsq = resident((D_MODEL, D_MODEL))
    gtile = pl.BlockSpec((N_GROUP_BLOCKS, cpv, vpt, CHUNK, LANES), lambda i: (0, 0, i, 0, 0))
    in_specs = [tile, tile if is_attn else gtile, modspec, row, wsq]
    args = [x, mix, mod, g2, w_a]
    if not is_attn:
        in_specs.append(wsq)
        args.append(w_b)
    in_specs += [pl.BlockSpec((None, D_MODEL, D_FF), lambda i: (layer, 0, 0), pipeline_mode=pl.Buffered(1)),
                 pl.BlockSpec((None, D_FF, D_MODEL), lambda i: (layer, 0, 0), pipeline_mode=pl.Buffered(1))]
    args += [w1, w2]
    if emit_next:
        in_specs += [modspec, row]
        args += [mod_next, g_next]
    if final:
        in_specs.append(row)
        args.append(final_g)
    out_specs = [tile]
    out_shape = [jax.ShapeDtypeStruct((ntok, D_MODEL), F32)]
    if emit_next:
        out_specs.append(gtile)
        out_shape.append(jax.ShapeDtypeStruct((N_GROUP_BLOCKS, cpv, n_virt, CHUNK, LANES), F32))
    return pl.pallas_call(
        functools.partial(_post_kernel, is_attn, emit_next, final, vpt),
        grid=(nt,),
        in_specs=in_specs,
        out_specs=out_specs,
        out_shape=out_shape,
        compiler_params=_cparams(("arbitrary",)),
        name="attn_proj_mlp" if is_attn else "glu_mlp_final",
    )(*args)


def _swap(x):
    return pltpu.roll(x, LANES // 2, 1)


def _cmul(z, w_r, w_i):
    return z * w_r + _swap(z) * w_i


def _multiplier(z, lo):
    zs = _swap(z)
    return jnp.where(lo, z, zs), jnp.where(lo, -zs, z)


def _rep_rows(x):
    return jnp.concatenate(
        [jnp.broadcast_to(x[g:g + 1, :], (GROUP_CH, LANES)) for g in range(GROUPS_PER_BLOCK)], axis=0)


def _s5_kernel(n_virt, n_seg, h_ref, lamr_ref, lami_ref, ldt_ref, bt_ref, cp_ref, dsk_ref, s0_ref, *refs):
    if n_seg == 1:
        y_ref, sfin_ref = refs[:2]
        refs = refs[2:]
    else:
        y_ref, sfin_ref = refs[0], None
        refs = refs[1:]
    f_scr, e_scr, k_scr, sf_scr, sb_scr, swf_scr, swb_scr = refs
    ntok = h_ref.shape[0]
    nc = ntok // CHUNK
    cpv = nc // n_virt
    gpb = GROUPS_PER_BLOCK

    lo8 = lax.broadcasted_iota(jnp.int32, (gpb, LANES), 1) < STATE_DIM
    lo = lax.broadcasted_iota(jnp.int32, (LANES, LANES), 1) < STATE_DIM
    conj = jnp.where(lo, 1.0, -1.0)
    row_g = lax.broadcasted_iota(jnp.int32, (LANES, STATE_LANES), 0) >> 4
    col_g = lax.broadcasted_iota(jnp.int32, (LANES, STATE_LANES), 1) >> 7
    diag_wide = row_g == col_g
    diag = (lax.broadcasted_iota(jnp.int32, (LANES, LANES), 0) >> 4) == (
        lax.broadcasted_iota(jnp.int32, (LANES, LANES), 1) >> 4)

    def expand(w):
        return jnp.where(diag_wide, jnp.concatenate([w] * gpb, axis=1), jnp.zeros((), BF16))

    decay = []
    lag = []
    for d in range(2):
        lam_r = lamr_ref[d]
        lam_i = lami_ref[d]
        dt = jnp.exp(ldt_ref[d])
        mag = jnp.exp(lam_r * dt)
        ang = lam_i * dt
        a_r = mag * jnp.cos(ang)
        a_im = mag * jnp.sin(ang)
        a_i = jnp.where(lo8, -a_im, a_im)
        den = lam_r * lam_r + lam_i * lam_i
        num = jnp.where(lo8, a_r - 1.0, a_im)
        f = _cmul(num, lam_r / den, jnp.where(lo8, lam_i, -lam_i) / den)
        pw = [jnp.where(lo8, 1.0, 0.0)]
        for _ in range(CHUNK):
            pw.append(_cmul(pw[-1], a_r, a_i))
        decay.append(pw[CHUNK])
        pw = [_rep_rows(p) for p in pw]
        f_r, f_i = _multiplier(_rep_rows(f), lo)
        bb_r, bb_i = _multiplier(_cmul(bt_ref[d], f_r, f_i), lo)
        c_r, c_i = _multiplier(cp_ref[d], lo)
        cm = (cp_ref[d] * conj).astype(BF16)
        fpow = [_cmul(p, bb_r, bb_i).astype(BF16) for p in pw[:CHUNK]]
        for j in range(CHUNK):
            e = (CHUNK - 1 - j) if d == 0 else j
            f_scr[d, j * LANES:(j + 1) * LANES, :] = expand(fpow[e])
        for t in range(CHUNK):
            e = (t + 1) if d == 0 else (CHUNK - t)
            w = _cmul(pw[e], c_r, c_i) * conj
            e_scr[d, t * LANES:(t + 1) * LANES, :] = expand(w.astype(BF16))
        lag.append([jnp.where(diag, _dot_nt(fp, cm), 0.0) for fp in fpow])

    for j in range(CHUNK):
        for t in range(CHUNK):
            k = t - j
            tile = lag[0][k] if k > 0 else (lag[1][-k] if k < 0 else lag[0][0] + lag[1][0])
            k_scr[j * LANES:(j + 1) * LANES, t * LANES:(t + 1) * LANES] = tile.astype(BF16)

    xcat = jnp.concatenate(
        [h_ref[pl.ds(j, nc, stride=CHUNK), :].astype(BF16) for j in range(CHUNK)], axis=1)

    for d, scr, sw_scr in ((0, sf_scr, swf_scr), (1, sb_scr, swb_scr)):
        loc_all = _dot(xcat, f_scr[d])
        for k in range(gpb):
            loc = loc_all[:, k * LANES:(k + 1) * LANES]
            scr[k] = loc
            sw_scr[k] = _swap(loc)
    y_within = _dot(xcat, k_scr[...])

    sgn8 = jnp.where(lo8, -1.0, 1.0)

    def dup(z):
        zs = _swap(z)
        return jnp.where(lo8, z, zs), jnp.where(lo8, zs, z)

    def scan(scr, sw_scr, d, reverse):
        a_re, a_im = dup(decay[d])
        a_sg = a_im * sgn8
        a_r = [a_re[k:k + 1, :] for k in range(gpb)]
        a_i = [a_sg[k:k + 1, :] for k in range(gpb)]

        def rows_of(i):
            c = (cpv - 1 - i) if reverse else i
            return pl.ds(c * n_virt, n_virt)

        def body(i, carry):
            st, sw = carry
            rows = rows_of(i)
            new_st, new_sw = [], []
            for k in range(gpb):
                loc = scr[k, rows, :]
                loc_sw = sw_scr[k, rows, :]
                scr[k, rows, :] = st[k]
                new_st.append(a_r[k] * st[k] + a_i[k] * sw[k] + loc)
                new_sw.append(a_r[k] * sw[k] - a_i[k] * st[k] + loc_sw)
            return tuple(new_st), tuple(new_sw)

        st0 = tuple(s0_ref[d, :, k * LANES:(k + 1) * LANES] for k in range(gpb))
        sw0 = tuple(_swap(s) for s in st0)
        carry = (st0, sw0)
        for i in range(cpv):
            carry = body(i, carry)
        fin = carry[0]
        if n_seg == 1:
            for k in range(gpb):
                sfin_ref[d, :, k * LANES:(k + 1) * LANES] = fin[k]
            return

        p = decay[d]
        for _ in range(cpv.bit_length() - 1):
            p_re, p_im = dup(p)
            p = _cmul(p, p_re, p_im * sgn8)
        v_re, v_im = dup(p)
        v_sg = v_im * sgn8
        seg = lax.broadcasted_iota(jnp.int32, (n_virt, LANES), 0) & (n_seg - 1)
        has_pred = seg != ((n_seg - 1) if reverse else 0)
        shift = (n_virt - 1) if reverse else 1
        cin = []
        for k in range(gpb):
            ck = jnp.zeros((n_virt, LANES), F32)
            for _ in range(n_seg - 1):
                nxt = fin[k] + ck * v_re[k:k + 1, :] + _swap(ck) * v_sg[k:k + 1, :]
                ck = jnp.where(has_pred, pltpu.roll(nxt, shift, 0), 0.0)
            cin.append(ck)
        cin_sw = [_swap(x) for x in cin]

        def fix(i, carry):
            q_re, q_im = carry
            rows = rows_of(i)
            q_sg = q_im * sgn8
            for k in range(gpb):
                scr[k, rows, :] += cin[k] * q_re[k:k + 1, :] + cin_sw[k] * q_sg[k:k + 1, :]
            return q_re * a_re - q_im * a_im, q_re * a_im + q_im * a_re

        q = (jnp.ones((gpb, LANES), F32), jnp.zeros((gpb, LANES), F32))
        for i in range(cpv):
            q = fix(i, q)

    scan(sf_scr, swf_scr, 0, False)
    scan(sb_scr, swb_scr, 1, True)

    s_f = jnp.concatenate([sf_scr[k].astype(BF16) for k in range(gpb)], axis=1)
    s_b = jnp.concatenate([sb_scr[k].astype(BF16) for k in range(gpb)], axis=1)
    yall = y_within + _dot_nt(s_f, e_scr[0]) + _dot_nt(s_b, e_scr[1])
    dsk = dsk_ref[...]
    for t in range(CHUNK):
        rows = pl.ds(t, nc, stride=CHUNK)
        y_ref[rows, :] = yall[:, t * LANES:(t + 1) * LANES] + h_ref[rows, :] * dsk


def _s5_params(lam_re, lam_im, log_dt, b_re, b_im, c_re, c_im):
    lamr = jnp.concatenate([lam_re, lam_re], axis=-1).astype(F32)
    lami = jnp.concatenate([lam_im, lam_im], axis=-1).astype(F32)
    ldt = jnp.broadcast_to(log_dt.astype(F32)[..., None], lamr.shape)
    bt = jnp.concatenate([b_re.transpose(0, 1, 3, 2), b_im.transpose(0, 1, 3, 2)], axis=-1)
    cp = jnp.concatenate([c_re, c_im], axis=-1)
    return (lamr, lami, ldt, bt.reshape(2, D_MODEL, LANES).astype(F32), cp.reshape(2, D_MODEL, LANES).astype(F32))


def _s5(h, params, d_skip, s0, n_seg):
    lamr, lami, ldt, bt, cp = params
    _, cpv, n_virt, _, _ = h.shape
    assert cpv & (cpv - 1) == 0 and n_virt % SUBLANES == 0 and n_seg & (n_seg - 1) == 0
    nc = cpv * n_virt
    ntok = nc * CHUNK
    hspec = pl.BlockSpec((None, ntok, LANES), lambda g: (g, 0, 0))
    kdim = CHUNK * LANES
    gspec = pl.BlockSpec((2, GROUPS_PER_BLOCK, LANES), lambda g: (0, g, 0))
    rspec = pl.BlockSpec((2, LANES, LANES), lambda g: (0, g, 0))
    sspec = pl.BlockSpec((None, 2, n_virt, STATE_LANES), lambda g: (g, 0, 0, 0))
    state_scr = pltpu.VMEM((GROUPS_PER_BLOCK, nc, LANES), F32)
    out_specs = [hspec]
    out_shape = [jax.ShapeDtypeStruct((N_GROUP_BLOCKS, ntok, LANES), F32)]
    if n_seg == 1:
        out_specs.append(sspec)
        out_shape.append(jax.ShapeDtypeStruct((N_GROUP_BLOCKS, 2, n_virt, STATE_LANES), F32))
    outs = pl.pallas_call(
        functools.partial(_s5_kernel, n_virt, n_seg),
        grid=(N_GROUP_BLOCKS,),
        in_specs=[
            hspec,
            gspec, gspec, gspec, rspec, rspec,
            pl.BlockSpec((1, LANES), lambda g: (0, g)),
            sspec,
        ],
        out_specs=out_specs,
        out_shape=out_shape,
        scratch_shapes=[
            pltpu.VMEM((2, kdim, STATE_LANES), BF16),
            pltpu.VMEM((2, kdim, STATE_LANES), BF16),
            pltpu.VMEM((kdim, kdim), BF16),
            state_scr, state_scr, state_scr, state_scr,
        ],
        compiler_params=_cparams(("arbitrary",)),
        name="s5_chunked_scan",
    )(h.reshape(N_GROUP_BLOCKS, ntok, LANES), lamr, lami, ldt, bt, cp, d_skip, s0)
    y = outs[0].reshape(h.shape)
    return (y, outs[1]) if n_seg == 1 else (y, None)


def _state_to_blocks(s):
    b = s.shape[0]
    s = s.reshape(b, 2, 2, N_GROUP_BLOCKS, GROUPS_PER_BLOCK, STATE_DIM)
    return s.transpose(3, 1, 0, 4, 2, 5).reshape(N_GROUP_BLOCKS, 2, b, STATE_LANES)


def _blocks_to_state(s):
    b = s.shape[2]
    s = s.reshape(N_GROUP_BLOCKS, 2, b, GROUPS_PER_BLOCK, 2, STATE_DIM)
    return s.transpose(2, 1, 4, 0, 3, 5).reshape(b, 2, 2, N_GROUPS, STATE_DIM)


def _rope_tables(n_tokens):
    pos = np.arange(n_tokens)
    n_freq = HEAD_DIM // 4
    freqs = ROPE_BASE ** (-np.arange(n_freq, dtype=np.float64) / n_freq)
    ang_r = (pos // GRID_W)[:, None] * freqs
    ang_c = (pos % GRID_W)[:, None] * freqs
    cos_h = np.concatenate([np.cos(ang_r), np.cos(ang_r), np.cos(ang_c), np.cos(ang_c)], axis=1)
    sin_h = np.concatenate([-np.sin(ang_r), np.sin(ang_r), -np.sin(ang_c), np.sin(ang_c)], axis=1)
    return jnp.asarray(np.tile(cos_h, (1, 2)), F32), jnp.asarray(np.tile(sin_h, (1, 2)), F32)


def kernel(x_prompt, x_sample, cache_k, cache_v, state_ssm, c, c_ctx, norm1_g, norm2_g, w_mod, b_mod,
           w_qkv, w_o, attn_sink, ssm_lam_re, ssm_lam_im, ssm_log_dt, ssm_b_re, ssm_b_im, ssm_c_re,
           ssm_c_im, ssm_d, glu_w_a, glu_w_b, mlp_w1, mlp_w2, final_norm_g):
    bp, lp, _ = x_prompt.shape
    bx, lx, _ = x_sample.shape
    assert lx % TOKEN_TILE == 0 and (bp * lp) % TOKEN_TILE == 0
    tiles_per_lat = lx // TOKEN_TILE

    xp = x_prompt.reshape(bp * lp, D_MODEL)
    xx = x_sample.reshape(bx * lx, D_MODEL)

    cvecs = jnp.zeros((8, D_MODEL), F32).at[0].set(c_ctx).at[1:1 + bx].set(c)
    mod = _modulation(cvecs, w_mod, b_mod)

    ctx_row = lambda i: 0
    lat_row = lambda i: 1 + i // tiles_per_lat

    assert lx % QKV_TILE == 0 and (bp * lp) % QKV_TILE == 0
    qkv_tiles_per_lat = lx // QKV_TILE
    rope = _rope_tables(lx) + (lambda i: i % qkv_tiles_per_lat,)
    wqkv = w_qkv[0].astype(BF16)
    g1 = norm1_g[0].reshape(1, D_MODEL)
    sink = attn_sink[0].astype(F32)
    qp, krp, vrp, kp, vp = _qkv(xp, mod[0], ctx_row, g1, wqkv, None, lp)
    qx, krx, vrx = _qkv(xx, mod[0], lambda i: 1 + i // qkv_tiles_per_lat, g1, wqkv, rope, 0)
    op = _ctx_attention(sink, qp, krp, vrp, bp, lp)
    rep = lambda t: jnp.tile(t[:, 0].transpose(0, 2, 1, 3), (1, 1, 1, LANES // HEAD_DIM)).astype(BF16)
    ox, (w1, w2, wo, wa, wb) = _lat_attention(
        sink, qx, krx, vrx, rep(cache_k), rep(cache_v), bx, lx,
        [mlp_w1, mlp_w2, w_o[0], glu_w_a[0], glu_w_b[0]])
    g2 = norm2_g.reshape(-1, 1, D_MODEL)
    gn = norm1_g[1].reshape(1, D_MODEL)
    vpt_p = TOKEN_TILE // lp
    vpt_x = 1
    n_seg_x = tiles_per_lat
    xp, hp = _post(xp, op, mod[0], ctx_row, g2[0], wo, None, w1, w2, 0, vpt_p, mod_next=mod[1], g_next=gn)
    xx, hx = _post(xx, ox, mod[0], lat_row, g2[0], wo, None, w1, w2, 0, vpt_x, mod_next=mod[1], g_next=gn)

    params = _s5_params(ssm_lam_re[0], ssm_lam_im[0], ssm_log_dt[0], ssm_b_re[0], ssm_b_im[0],
                        ssm_c_re[0], ssm_c_im[0])
    dsk = ssm_d[0].astype(F32).reshape(1, D_MODEL)
    s0p = jnp.zeros((N_GROUP_BLOCKS, 2, bp, STATE_LANES), F32)
    sx = _state_to_blocks(state_ssm[:, 0].astype(F32))
    s0x = jnp.zeros((N_GROUP_BLOCKS, 2, bx, n_seg_x, STATE_LANES), F32)
    s0x = s0x.at[:, 0, :, 0].set(sx[:, 0]).at[:, 1, :, n_seg_x - 1].set(sx[:, 1])
    s0x = s0x.reshape(N_GROUP_BLOCKS, 2, bx * n_seg_x, STATE_LANES)
    yp, sfin = _s5(hp, params, dsk, s0p, 1)
    yx, _ = _s5(hx, params, dsk, s0x, n_seg_x)
    new_state = _blocks_to_state(sfin)[:, None]

    fg = final_norm_g.reshape(1, D_MODEL)
    (yp_out,) = _post(xp, yp, mod[1], ctx_row, g2[1], wa, wb, w1, w2, 1, vpt_p, final_g=fg)
    (yx_out,) = _post(xx, yx, mod[1], lat_row, g2[1], wa, wb, w1, w2, 1, vpt_x, final_g=fg)

    to_cache = lambda t: t.reshape(bp, N_KV_HEADS, HEAD_DIM, lp).transpose(0, 3, 1, 2)[:, None]
    new_k = to_cache(kp)
    new_v = to_cache(vp)
    return (yp_out.reshape(bp, lp, D_MODEL), yx_out.reshape(bx, lx, D_MODEL), new_k, new_v, new_state)
```

```python
import functools
import math

import numpy as np
import jax
import jax.numpy as jnp
from jax import lax
from jax.experimental import pallas as pl
from jax.experimental.pallas import tpu as pltpu

F32 = jnp.float32
BF16 = jnp.bfloat16

D_MODEL = 1024
N_HEADS = 16
N_KV_HEADS = 4
HEAD_DIM = 64
Q_PER_KV = N_HEADS // N_KV_HEADS
KV_DIM = N_KV_HEADS * HEAD_DIM
QKV_DIM = D_MODEL + 2 * KV_DIM
BLOCK = 128
GRID_W = 64
ROPE_BASE = 10000.0
ATTN_SCALE = HEAD_DIM ** -0.5
N_GROUPS = 64
GROUP_CH = 16
STATE_DIM = 64
D_FF = 4 * D_MODEL
N_MOD = 6
RMS_EPS = 1e-6
NEG_INF = -1e30

LANES = 128
SUBLANES = 8
GROUPS_PER_BLOCK = LANES // GROUP_CH
N_GROUP_BLOCKS = N_GROUPS // GROUPS_PER_BLOCK
STATE_LANES = GROUPS_PER_BLOCK * 2 * STATE_DIM
CHUNK = SUBLANES
TOKEN_TILE = 512
FF_TILE = 1024
POST_SUBTILES = 2
QKV_TILE = 1024
QKV_SUBTILES = 4
VMEM_LIMIT = 56 * 1024 * 1024


def _cparams(semantics):
    return pltpu.CompilerParams(dimension_semantics=semantics, vmem_limit_bytes=VMEM_LIMIT)


def _rms(x):
    return x * lax.rsqrt(jnp.mean(x * x, axis=-1, keepdims=True) + RMS_EPS)


def _dot(a, b):
    return jnp.dot(a, b, preferred_element_type=F32)


def _dot_nt(a, b):
    return lax.dot_general(a, b, (((1,), (1,)), ((), ())), preferred_element_type=F32)


def _mod_kernel(cv_ref, w_ref, b_ref, o_ref):
    cv = cv_ref[...]
    s = (cv * jax.nn.sigmoid(cv)).astype(BF16)
    o_ref[0] = _dot(s, w_ref[0].astype(BF16)) + b_ref[0]


def _modulation(cvecs, w_mod, b_mod):
    depth = w_mod.shape[0]
    width = 2 * D_MODEL
    out = pl.pallas_call(
        _mod_kernel,
        grid=(depth, N_MOD * D_MODEL // width),
        in_specs=[
            pl.BlockSpec((8, D_MODEL), lambda l, j: (0, 0)),
            pl.BlockSpec((1, D_MODEL, width), lambda l, j: (l, 0, j)),
            pl.BlockSpec((1, 1, width), lambda l, j: (l, 0, j)),
        ],
        out_specs=pl.BlockSpec((1, 8, width), lambda l, j: (l, 0, j)),
        out_shape=jax.ShapeDtypeStruct((depth, 8, N_MOD * D_MODEL), F32),
        compiler_params=_cparams(("arbitrary", "arbitrary")),
        name="modulation",
    )(cvecs, w_mod, b_mod.reshape(depth, 1, N_MOD * D_MODEL))
    return out.reshape(depth, 8, N_MOD, D_MODEL)


def _head_pair(blk, odd):
    lo = lax.broadcasted_iota(jnp.int32, blk.shape, 1) < HEAD_DIM
    other = pltpu.roll(blk, HEAD_DIM, 1)
    return (jnp.where(lo, other, blk) if odd else jnp.where(lo, blk, other)).astype(BF16)


def _qkv_kernel(cache_seq, rope, x_ref, mod_ref, g_ref, w_ref, *refs):
    refs = list(refs)
    cos_ref, sin_ref = (refs.pop(0), refs.pop(0)) if rope else (None, None)
    q_ref, krep_ref, vrep_ref = refs[:3]
    kv_refs = refs[3:]
    sub = QKV_TILE // QKV_SUBTILES
    starts = [k * sub for k in range(QKV_SUBTILES)]

    def project(r0):
        h = _rms(x_ref[r0:r0 + sub, :]) * g_ref[...] * (1.0 + mod_ref[1:2, :]) + mod_ref[0:1, :]
        return _dot(h.astype(BF16), w_ref[...])

    def emit_cache(ref, blk, r0, c0):
        for s in range(sub // cache_seq):
            ref[r0 // cache_seq + s, c0:c0 + LANES, :] = blk[s * cache_seq:(s + 1) * cache_seq, :].T

    def finish(r0, qkv):
        rows = slice(r0, r0 + sub)
        if rope:
            cos = cos_ref[rows, :]
            sin = sin_ref[rows, :]
            first = (lax.broadcasted_iota(jnp.int32, cos.shape, 1) & 31) < 16
        for blk in range((D_MODEL + KV_DIM) // LANES):
            r = qkv[:, blk * LANES:(blk + 1) * LANES]
            if rope:
                partner = jnp.where(first, pltpu.roll(r, LANES - 16, 1), pltpu.roll(r, 16, 1))
                r = r * cos + partner * sin
            if blk < D_MODEL // LANES:
                q_ref[rows, blk * LANES:(blk + 1) * LANES] = (r * ATTN_SCALE).astype(BF16)
            else:
                c0 = blk * LANES - D_MODEL
                if cache_seq:
                    emit_cache(kv_refs[0], r, r0, c0)
                for half in range(2):
                    krep_ref[c0 // HEAD_DIM + half, rows, :] = _head_pair(r, half)
        v = qkv[:, D_MODEL + KV_DIM:]
        for c0 in range(0, KV_DIM, LANES):
            blk = v[:, c0:c0 + LANES]
            if cache_seq:
                emit_cache(kv_refs[1], blk, r0, c0)
            for half in range(2):
                vrep_ref[c0 // HEAD_DIM + half, rows, :] = _head_pair(blk, half)

    for r0, qkv in zip(starts, [project(r0) for r0 in starts]):
        finish(r0, qkv)


def _qkv(x, mod, mod_row, g, w_qkv, rope, cache_seq):
    ntok = x.shape[0]
    nt = ntok // QKV_TILE
    emit_kv = cache_seq > 0
    assert not emit_kv or (QKV_TILE // QKV_SUBTILES) % cache_seq == 0
    rep_spec = pl.BlockSpec((N_KV_HEADS, QKV_TILE, LANES), lambda i: (0, i, 0))
    rep_shape = jax.ShapeDtypeStruct((N_KV_HEADS, ntok, LANES), BF16)
    in_specs = [
        pl.BlockSpec((QKV_TILE, D_MODEL), lambda i: (i, 0)),
        pl.BlockSpec((None, N_MOD, D_MODEL), lambda i: (mod_row(i), 0, 0)),
        pl.BlockSpec((1, D_MODEL), lambda i: (0, 0)),
        pl.BlockSpec((D_MODEL, QKV_DIM), lambda i: (0, 0)),
    ]
    args = [x, mod, g, w_qkv]
    if rope is not None:
        cos_t, sin_t, rope_blk = rope
        in_specs += [pl.BlockSpec((QKV_TILE, LANES), lambda i: (rope_blk(i), 0))] * 2
        args += [cos_t, sin_t]
    out_specs = [pl.BlockSpec((QKV_TILE, D_MODEL), lambda i: (i, 0)), rep_spec, rep_spec]
    out_shape = [jax.ShapeDtypeStruct((ntok, D_MODEL), BF16), rep_shape, rep_shape]
    if emit_kv:
        spt = QKV_TILE // cache_seq
        out_specs += [pl.BlockSpec((spt, KV_DIM, cache_seq), lambda i: (i, 0, 0))] * 2
        out_shape += [jax.ShapeDtypeStruct((ntok // cache_seq, KV_DIM, cache_seq), F32)] * 2
    return pl.pallas_call(
        functools.partial(_qkv_kernel, cache_seq, rope is not None),
        grid=(nt,),
        in_specs=in_specs,
        out_specs=out_specs,
        out_shape=out_shape,
        compiler_params=_cparams(("arbitrary",)),
        name="norm_qkv_rope",
    )(*args)


def _group_scores(q_ref, kv, key_parts, bias):
    nq = q_ref.shape[0]
    lo = lax.broadcasted_iota(jnp.int32, (nq, LANES), 1) < HEAD_DIM
    zero = jnp.zeros((), BF16)
    rows = []
    for b in range(KV_DIM // LANES):
        blk = q_ref[:, kv * KV_DIM + b * LANES:kv * KV_DIM + (b + 1) * LANES]
        rows += [jnp.where(lo, blk, zero), jnp.where(lo, zero, blk)]
    q4 = jnp.concatenate(rows, axis=0)
    parts = [_dot_nt(q4, keys) for keys in key_parts]
    if bias is not None:
        s0 = parts[0].reshape(Q_PER_KV, nq, -1) + bias[None]
        parts[0] = s0.reshape(Q_PER_KV * nq, -1)
    return parts


def _group_softmax(parts, sink_ref, kv):
    nq = parts[0].shape[0] // Q_PER_KV
    sink = jnp.concatenate(
        [jnp.full((nq, LANES), sink_ref[kv * Q_PER_KV + g], F32) for g in range(Q_PER_KV)], axis=0)
    blocks = [[s[:, j:j + LANES] for j in range(0, s.shape[1], LANES)] for s in parts]
    fold = None
    for b in sum(blocks, []):
        fold = b if fold is None else jnp.maximum(fold, b)
    m = jnp.maximum(jnp.max(fold, axis=-1, keepdims=True), sink)
    probs = [[jnp.exp(b - m) for b in bs] for bs in blocks]
    fold = None
    for p in sum(probs, []):
        fold = p if fold is None else fold + p
    den = jnp.sum(fold, axis=-1, keepdims=True) + jnp.exp(sink - m)
    return [jnp.concatenate(ps, axis=1).astype(BF16) for ps in probs], 1.0 / den


def _group_output(probs, inv_den, value_parts, o_ref, kv):
    nq = probs[0].shape[0] // Q_PER_KV
    r = _dot(probs[0], value_parts[0])
    for p, v in zip(probs[1:], value_parts[1:]):
        r = r + _dot(p, v)
    r = r * inv_den
    lo = lax.broadcasted_iota(jnp.int32, (nq, LANES), 1) < HEAD_DIM
    for b in range(KV_DIM // LANES):
        pair = jnp.where(lo, r[2 * b * nq:(2 * b + 1) * nq], r[(2 * b + 1) * nq:(2 * b + 2) * nq])
        o_ref[:, kv * KV_DIM + b * LANES:kv * KV_DIM + (b + 1) * LANES] = pair.astype(BF16)


def _attend(q_ref, sink_ref, o_ref, keys_of, values_of, bias):
    s_next = _group_scores(q_ref, 0, keys_of(0), bias)
    for kv in range(N_KV_HEADS):
        s = s_next
        if kv + 1 < N_KV_HEADS:
            s_next = _group_scores(q_ref, kv + 1, keys_of(kv + 1), bias)
        probs, inv_den = _group_softmax(s, sink_ref, kv)
        _group_output(probs, inv_den, values_of(kv), o_ref, kv)


def _ctx_attn_kernel(sink_ref, q_ref, k_ref, v_ref, o_ref):
    _attend(q_ref, sink_ref, o_ref, lambda kv: [k_ref[kv]], lambda kv: [v_ref[kv]], None)


def _ctx_attention(sink, q, krep, vrep, n_batch, seq):
    rep_spec = pl.BlockSpec((N_KV_HEADS, seq, LANES), lambda b: (0, b, 0))
    return pl.pallas_call(
        _ctx_attn_kernel,
        grid=(n_batch,),
        in_specs=[
            pl.BlockSpec(memory_space=pltpu.SMEM),
            pl.BlockSpec((seq, D_MODEL), lambda b: (b, 0)),
            rep_spec, rep_spec,
        ],
        out_specs=pl.BlockSpec((seq, D_MODEL), lambda b: (b, 0)),
        out_shape=jax.ShapeDtypeStruct((n_batch * seq, D_MODEL), BF16),
        compiler_params=_cparams(("arbitrary",)),
        name="context_attention",
    )(sink, q, krep, vrep)


def _window_start(n, seq):
    return jnp.clip((n - 1) * BLOCK, 0, seq - 3 * BLOCK)


def _band_bias():
    r = np.arange(BLOCK)[:, None]
    j = np.arange(3 * BLOCK)[None, :]
    out = [np.where(np.abs(j - d * BLOCK - r) <= BLOCK, 0.0, NEG_INF) for d in range(3)]
    return jnp.asarray(np.stack(out), F32)


def _lat_attn_kernel(seq, n_cast, sink_ref, q_ref, k_ref, v_ref, ck_ref, cv_ref, bias_ref, *refs):
    o_ref = refs[n_cast]
    for src, dst in zip(refs[:n_cast], refs[n_cast + 1:]):
        dst[...] = src[...].astype(BF16)
    win = 3 * BLOCK
    start = pl.multiple_of(_window_start(pl.program_id(1), seq), BLOCK)
    ck_scr, cv_scr = refs[2 * n_cast + 1:]

    @pl.when(pl.program_id(1) == 0)
    def _():
        for src, dst in ((ck_ref, ck_scr), (cv_ref, cv_scr)):
            for kv in range(N_KV_HEADS):
                dst[kv] = _head_pair(src[:, (kv // 2) * LANES:(kv // 2 + 1) * LANES], kv % 2)

    keys_of = lambda kv: [k_ref[kv, pl.ds(start, win), :], ck_scr[kv]]
    values_of = lambda kv: [v_ref[kv, pl.ds(start, win), :], cv_scr[kv]]
    _attend(q_ref, sink_ref, o_ref, keys_of, values_of, bias_ref[...])


def _lat_attention(sink, q, krep, vrep, ck, cv, n_batch, seq, cast_weights):
    nb = seq // BLOCK
    steps = n_batch * nb
    past = ck.shape[1]
    rep_spec = pl.BlockSpec((N_KV_HEADS, seq, LANES), lambda b, n: (0, b, 0))
    crep_spec = pl.BlockSpec((None, past, KV_DIM), lambda b, n: (b, 0, 0))
    flat = [w.reshape(-1, w.shape[-1]) for w in cast_weights]
    assert all(w.shape[0] % (steps * 2 * SUBLANES) == 0 for w in flat)
    slabs = [pl.BlockSpec((w.shape[0] // steps, w.shape[1]), lambda b, n: (b * nb + n, 0)) for w in flat]
    outs = pl.pallas_call(
        functools.partial(_lat_attn_kernel, seq, len(flat)),
        grid=(n_batch, nb),
        in_specs=[
            pl.BlockSpec(memory_space=pltpu.SMEM),
            pl.BlockSpec((BLOCK, D_MODEL), lambda b, n: (b * nb + n, 0)),
            rep_spec, rep_spec, crep_spec, crep_spec,
            pl.BlockSpec((None, BLOCK, 3 * BLOCK), lambda b, n: (n - _window_start(n, seq) // BLOCK, 0, 0)),
        ] + slabs,
        out_specs=[pl.BlockSpec((BLOCK, D_MODEL), lambda b, n: (b * nb + n, 0))] + slabs,
        out_shape=[jax.ShapeDtypeStruct((n_batch * seq, D_MODEL), BF16)]
        + [jax.ShapeDtypeStruct(w.shape, BF16) for w in flat],
        scratch_shapes=[pltpu.VMEM((N_KV_HEADS, past, LANES), BF16)] * 2,
        compiler_params=_cparams(("arbitrary", "arbitrary")),
        name="latent_attention",
    )(sink, q, krep, vrep, ck, cv, _band_bias(), *flat)
    return outs[0], [o.reshape(w.shape) for o, w in zip(outs[1:], cast_weights)]


def _gelu_tanh(x):
    c = math.sqrt(2.0 / math.pi)
    return x * (0.5 * (1.0 + jnp.tanh(c * (x + 0.044715 * (x * x * x)))))


def _post_kernel(is_attn, emit_next, final, vpt, *refs):
    rows_per_v = TOKEN_TILE // vpt
    refs = list(refs)
    x_ref, mix_ref, mod_ref, g2_ref, wa_ref = refs[:5]
    refs = refs[5:]
    wb_ref = None if is_attn else refs.pop(0)
    w1_ref, w2_ref = refs[:2]
    refs = refs[2:]
    modn_ref = gn_ref = fg_ref = hn_ref = None
    if emit_next:
        modn_ref, gn_ref = refs[:2]
        refs = refs[2:]
    if final:
        fg_ref = refs.pop(0)
    xo_ref = refs.pop(0)
    if emit_next:
        hn_ref = refs.pop(0)
    assert not refs

    sub = TOKEN_TILE // POST_SUBTILES
    assert rows_per_v % sub == 0
    n_chunk = sub // CHUNK
    starts = [k * sub for k in range(POST_SUBTILES)]

    def s5_slot(r0):
        return r0 // rows_per_v, (r0 % rows_per_v) // CHUNK

    def project(r0):
        if is_attn:
            return _dot(mix_ref[r0:r0 + sub, :], wa_ref[...])
        s, c0 = s5_slot(r0)
        y = jnp.concatenate(
            [mix_ref[g, c0:c0 + n_chunk, s].reshape(sub, LANES) for g in range(N_GROUP_BLOCKS)], axis=1)
        yg = _gelu_tanh(y).astype(BF16)
        return _dot(yg, wa_ref[...]) * jax.nn.sigmoid(_dot(yg, wb_ref[...]))

    def prologue(r0, mix):
        x1 = x_ref[r0:r0 + sub, :] + mod_ref[2:3, :] * mix
        h2 = _rms(x1) * g2_ref[...] * (1.0 + mod_ref[4:5, :]) + mod_ref[3:4, :]
        return x1, h2.astype(BF16)

    def mlp(h2):
        acc = None
        for c in range(D_FF // FF_TILE):
            a = jnp.maximum(_dot(h2, w1_ref[:, c * FF_TILE:(c + 1) * FF_TILE]), 0.0)
            t = _dot((a * a).astype(BF16), w2_ref[c * FF_TILE:(c + 1) * FF_TILE, :])
            acc = t if acc is None else acc + t
        return acc

    def epilogue(r0, x1, acc):
        x2 = x1 + mod_ref[5:6, :] * acc
        if emit_next:
            hn = _rms(x2) * gn_ref[...] * (1.0 + modn_ref[1:2, :]) + modn_ref[0:1, :]
            s, c0 = s5_slot(r0)
            for g in range(N_GROUP_BLOCKS):
                blk = hn[:, g * LANES:(g + 1) * LANES]
                hn_ref[g, c0:c0 + n_chunk, s] = blk.reshape(n_chunk, CHUNK, LANES)
        xo_ref[r0:r0 + sub, :] = _rms(x2) * fg_ref[...] if final else x2

    mixes = [project(r0) for r0 in starts]
    pro = [prologue(r0, mix) for r0, mix in zip(starts, mixes)]
    accs = [mlp(h2) for _, h2 in pro]
    for r0, (x1, _), acc in zip(starts, pro, accs):
        epilogue(r0, x1, acc)


def _post(x, mix, mod, mod_row, g2, w_a, w_b, w1, w2, layer, vpt, mod_next=None, g_next=None, final_g=None):
    is_attn = w_b is None
    emit_next = mod_next is not None
    final = final_g is not None
    ntok = x.shape[0]
    nt = ntok // TOKEN_TILE
    cpv = TOKEN_TILE // (vpt * CHUNK)
    n_virt = nt * vpt
    tile = pl.BlockSpec((TOKEN_TILE, D_MODEL), lambda i: (i, 0))
    row = pl.BlockSpec((1, D_MODEL), lambda i: (0, 0))
    modspec = pl.BlockSpec((None, N_MOD, D_MODEL), lambda i: (mod_row(i), 0, 0))
    resident = lambda shape: pl.BlockSpec(shape, lambda i: (0, 0), pipeline_mode=pl.Buffered(1))
    wsq = resident((D_MODEL, D_MODEL))
    gtile = pl.BlockSpec((N_GROUP_BLOCKS, cpv, vpt, CHUNK, LANES), lambda i: (0, 0, i, 0, 0))
    in_specs = [tile, tile if is_attn else gtile, modspec, row, wsq]
    args = [x, mix, mod, g2, w_a]
    if not is_attn:
        in_specs.append(wsq)
        args.append(w_b)
    in_specs += [pl.BlockSpec((None, D_MODEL, D_FF), lambda i: (layer, 0, 0), pipeline_mode=pl.Buffered(1)),
                 pl.BlockSpec((None, D_FF, D_MODEL), lambda i: (layer, 0, 0), pipeline_mode=pl.Buffered(1))]
    args += [w1, w2]
    if emit_next:
        in_specs += [modspec, row]
        args += [mod_next, g_next]
    if final:
        in_specs.append(row)
        args.append(final_g)
    out_specs = [tile]
    out_shape = [jax.ShapeDtypeStruct((ntok, D_MODEL), F32)]
    if emit_next:
        out_specs.append(gtile)
        out_shape.append(jax.ShapeDtypeStruct((N_GROUP_BLOCKS, cpv, n_virt, CHUNK, LANES), F32))
    return pl.pallas_call(
        functools.partial(_post_kernel, is_attn, emit_next, final, vpt),
        grid=(nt,),
        in_specs=in_specs,
        out_specs=out_specs,
        out_shape=out_shape,
        compiler_params=_cparams(("arbitrary",)),
        name="attn_proj_mlp" if is_attn else "glu_mlp_final",
    )(*args)


def _swap(x):
    return pltpu.roll(x, LANES // 2, 1)


def _cmul(z, w_r, w_i):
    return z * w_r + _swap(z) * w_i


def _multiplier(z, lo):
    zs = _swap(z)
    return jnp.where(lo, z, zs), jnp.where(lo, -zs, z)


def _rep_rows(x):
    return jnp.concatenate(
        [jnp.broadcast_to(x[g:g + 1, :], (GROUP_CH, LANES)) for g in range(GROUPS_PER_BLOCK)], axis=0)


def _s5_kernel(n_virt, n_seg, zero_init, h_ref, lamr_ref, lami_ref, ldt_ref, bt_ref, cp_ref, dsk_ref, *refs):
    s0_ref = None
    if not zero_init:
        s0_ref, refs = refs[0], refs[1:]
    if n_seg == 1:
        y_ref, sfin_ref = refs[:2]
        refs = refs[2:]
    else:
        y_ref, sfin_ref = refs[0], None
        refs = refs[1:]
    f_scr, e_scr, k_scr, sf_scr, sb_scr, swf_scr, swb_scr = refs
    ntok = h_ref.shape[0]
    nc = ntok // CHUNK
    cpv = nc // n_virt
    gpb = GROUPS_PER_BLOCK

    lo8 = lax.broadcasted_iota(jnp.int32, (gpb, LANES), 1) < STATE_DIM
    lo = lax.broadcasted_iota(jnp.int32, (LANES, LANES), 1) < STATE_DIM
    conj = jnp.where(lo, 1.0, -1.0)
    row_g = lax.broadcasted_iota(jnp.int32, (LANES, STATE_LANES), 0) >> 4
    col_g = lax.broadcasted_iota(jnp.int32, (LANES, STATE_LANES), 1) >> 7
    diag_wide = row_g == col_g
    diag = (lax.broadcasted_iota(jnp.int32, (LANES, LANES), 0) >> 4) == (
        lax.broadcasted_iota(jnp.int32, (LANES, LANES), 1) >> 4)

    def expand(w):
        return jnp.where(diag_wide, jnp.concatenate([w] * gpb, axis=1), jnp.zeros((), BF16))

    decay = []
    lag = []
    for d in range(2):
        lam_r = lamr_ref[d]
        lam_i = lami_ref[d]
        dt = jnp.exp(ldt_ref[d])
        mag = jnp.exp(lam_r * dt)
        ang = lam_i * dt
        a_r = mag * jnp.cos(ang)
        a_im = mag * jnp.sin(ang)
        a_i = jnp.where(lo8, -a_im, a_im)
        den = lam_r * lam_r + lam_i * lam_i
        num = jnp.where(lo8, a_r - 1.0, a_im)
        f = _cmul(num, lam_r / den, jnp.where(lo8, lam_i, -lam_i) / den)
        pw = [jnp.where(lo8, 1.0, 0.0)]
        for _ in range(CHUNK):
            pw.append(_cmul(pw[-1], a_r, a_i))
        decay.append(pw[CHUNK])
        pw = [_rep_rows(p) for p in pw]
        f_r, f_i = _multiplier(_rep_rows(f), lo)
        bb_r, bb_i = _multiplier(_cmul(bt_ref[d], f_r, f_i), lo)
        c_r, c_i = _multiplier(cp_ref[d], lo)
        cm = (cp_ref[d] * conj).astype(BF16)
        fpow = [_cmul(p, bb_r, bb_i).astype(BF16) for p in pw[:CHUNK]]
        for j in range(CHUNK):
            e = (CHUNK - 1 - j) if d == 0 else j
            f_scr[d, j * LANES:(j + 1) * LANES, :] = expand(fpow[e])
        for t in range(CHUNK):
            e = (t + 1) if d == 0 else (CHUNK - t)
            w = _cmul(pw[e], c_r, c_i) * conj
            e_scr[d, t * LANES:(t + 1) * LANES, :] = expand(w.astype(BF16))
        lag.append([jnp.where(diag, _dot_nt(fp, cm), 0.0) for fp in fpow])

    for j in range(CHUNK):
        for t in range(CHUNK):
            k = t - j
            tile = lag[0][k] if k > 0 else (lag[1][-k] if k < 0 else lag[0][0] + lag[1][0])
            k_scr[j * LANES:(j + 1) * LANES, t * LANES:(t + 1) * LANES] = tile.astype(BF16)

    xcat = jnp.concatenate(
        [h_ref[pl.ds(j, nc, stride=CHUNK), :].astype(BF16) for j in range(CHUNK)], axis=1)

    for d, scr, sw_scr in ((0, sf_scr, swf_scr), (1, sb_scr, swb_scr)):
        loc_all = _dot(xcat, f_scr[d])
        for k in range(gpb):
            loc = loc_all[:, k * LANES:(k + 1) * LANES]
            scr[k] = loc
            sw_scr[k] = _swap(loc)
    y_within = _dot(xcat, k_scr[...])

    sgn8 = jnp.where(lo8, -1.0, 1.0)

    def dup(z):
        zs = _swap(z)
        return jnp.where(lo8, z, zs), jnp.where(lo8, zs, z)

    def scan(scr, sw_scr, d, reverse):
        a_re, a_im = dup(decay[d])
        a_sg = a_im * sgn8
        a_r = [a_re[k:k + 1, :] for k in range(gpb)]
        a_i = [a_sg[k:k + 1, :] for k in range(gpb)]

        def rows_of(i):
            c = (cpv - 1 - i) if reverse else i
            return pl.ds(c * n_virt, n_virt)

        def body(i, carry):
            st, sw = carry
            rows = rows_of(i)
            new_st, new_sw = [], []
            for k in range(gpb):
                loc = scr[k, rows, :]
                loc_sw = sw_scr[k, rows, :]
                scr[k, rows, :] = st[k]
                new_st.append(a_r[k] * st[k] + a_i[k] * sw[k] + loc)
                new_sw.append(a_r[k] * sw[k] - a_i[k] * st[k] + loc_sw)
            return tuple(new_st), tuple(new_sw)

        if zero_init:
            st0 = tuple(jnp.zeros((n_virt, LANES), F32) for _ in range(gpb))
        else:
            st0 = tuple(s0_ref[d, :, k * LANES:(k + 1) * LANES] for k in range(gpb))
        sw0 = tuple(_swap(s) for s in st0)
        carry = (st0, sw0)
        for i in range(cpv):
            carry = body(i, carry)
        fin = carry[0]
        if n_seg == 1:
            for k in range(gpb):
                sfin_ref[d, :, k * LANES:(k + 1) * LANES] = fin[k]
            return

        p = decay[d]
        for _ in range(cpv.bit_length() - 1):
            p_re, p_im = dup(p)
            p = _cmul(p, p_re, p_im * sgn8)
        v_re, v_im = dup(p)
        v_sg = v_im * sgn8
        seg = lax.broadcasted_iota(jnp.int32, (n_virt, LANES), 0) & (n_seg - 1)
        has_pred = seg != ((n_seg - 1) if reverse else 0)
        shift = (n_virt - 1) if reverse else 1
        cin = []
        for k in range(gpb):
            ck = jnp.zeros((n_virt, LANES), F32)
            for _ in range(n_seg - 1):
                nxt = fin[k] + ck * v_re[k:k + 1, :] + _swap(ck) * v_sg[k:k + 1, :]
                ck = jnp.where(has_pred, pltpu.roll(nxt, shift, 0), 0.0)
            cin.append(ck)
        cin_sw = [_swap(x) for x in cin]

        def fix(i, carry):
            q_re, q_im = carry
            rows = rows_of(i)
            q_sg = q_im * sgn8
            for k in range(gpb):
                scr[k, rows, :] += cin[k] * q_re[k:k + 1, :] + cin_sw[k] * q_sg[k:k + 1, :]
            return q_re * a_re - q_im * a_im, q_re * a_im + q_im * a_re

        q = (jnp.ones((gpb, LANES), F32), jnp.zeros((gpb, LANES), F32))
        for i in range(cpv):
            q = fix(i, q)

    scan(sf_scr, swf_scr, 0, False)
    scan(sb_scr, swb_scr, 1, True)

    s_f = jnp.concatenate([sf_scr[k].astype(BF16) for k in range(gpb)], axis=1)
    s_b = jnp.concatenate([sb_scr[k].astype(BF16) for k in range(gpb)], axis=1)
    yall = y_within + _dot_nt(s_f, e_scr[0]) + _dot_nt(s_b, e_scr[1])
    dsk = dsk_ref[...]
    for t in range(CHUNK):
        rows = pl.ds(t, nc, stride=CHUNK)
        y_ref[rows, :] = yall[:, t * LANES:(t + 1) * LANES] + h_ref[rows, :] * dsk


def _s5_params(lam_re, lam_im, log_dt, b_re, b_im, c_re, c_im):
    lamr = jnp.concatenate([lam_re, lam_re], axis=-1).astype(F32)
    lami = jnp.concatenate([lam_im, lam_im], axis=-1).astype(F32)
    ldt = jnp.broadcast_to(log_dt.astype(F32)[..., None], lamr.shape)
    bt = jnp.concatenate([b_re.transpose(0, 1, 3, 2), b_im.transpose(0, 1, 3, 2)], axis=-1)
    cp = jnp.concatenate([c_re, c_im], axis=-1)
    return (lamr, lami, ldt, bt.reshape(2, D_MODEL, LANES).astype(F32), cp.reshape(2, D_MODEL, LANES).astype(F32))


def _s5(h, params, d_skip, s0, n_seg):
    lamr, lami, ldt, bt, cp = params
    _, cpv, n_virt, _, _ = h.shape
    assert cpv & (cpv - 1) == 0 and n_virt % SUBLANES == 0 and n_seg & (n_seg - 1) == 0
    nc = cpv * n_virt
    ntok = nc * CHUNK
    hspec = pl.BlockSpec((None, ntok, LANES), lambda g: (g, 0, 0))
    kdim = CHUNK * LANES
    gspec = pl.BlockSpec((2, GROUPS_PER_BLOCK, LANES), lambda g: (0, g, 0))
    rspec = pl.BlockSpec((2, LANES, LANES), lambda g: (0, g, 0))
    sspec = pl.BlockSpec((None, 2, n_virt, STATE_LANES), lambda g: (g, 0, 0, 0))
    state_scr = pltpu.VMEM((GROUPS_PER_BLOCK, nc, LANES), F32)
    out_specs = [hspec]
    out_shape = [jax.ShapeDtypeStruct((N_GROUP_BLOCKS, ntok, LANES), F32)]
    if n_seg == 1:
        out_specs.append(sspec)
        out_shape.append(jax.ShapeDtypeStruct((N_GROUP_BLOCKS, 2, n_virt, STATE_LANES), F32))
    zero_init = s0 is None
    outs = pl.pallas_call(
        functools.partial(_s5_kernel, n_virt, n_seg, zero_init),
        grid=(N_GROUP_BLOCKS,),
        in_specs=[
            hspec,
            gspec, gspec, gspec, rspec, rspec,
            pl.BlockSpec((1, LANES), lambda g: (0, g)),
        ] + ([] if zero_init else [sspec]),
        out_specs=out_specs,
        out_shape=out_shape,
        scratch_shapes=[
            pltpu.VMEM((2, kdim, STATE_LANES), BF16),
            pltpu.VMEM((2, kdim, STATE_LANES), BF16),
            pltpu.VMEM((kdim, kdim), BF16),
            state_scr, state_scr, state_scr, state_scr,
        ],
        compiler_params=_cparams(("arbitrary",)),
        name="s5_chunked_scan",
    )(h.reshape(N_GROUP_BLOCKS, ntok, LANES), lamr, lami, ldt, bt, cp, d_skip, *([] if zero_init else [s0]))
    y = outs[0].reshape(h.shape)
    return (y, outs[1]) if n_seg == 1 else (y, None)


def _state_to_blocks(s):
    b = s.shape[0]
    s = s.reshape(b, 2, 2, N_GROUP_BLOCKS, GROUPS_PER_BLOCK, STATE_DIM)
    return s.transpose(3, 1, 0, 4, 2, 5).reshape(N_GROUP_BLOCKS, 2, b, STATE_LANES)


def _blocks_to_state(s):
    b = s.shape[2]
    s = s.reshape(N_GROUP_BLOCKS, 2, b, GROUPS_PER_BLOCK, 2, STATE_DIM)
    return s.transpose(2, 1, 4, 0, 3, 5).reshape(b, 2, 2, N_GROUPS, STATE_DIM)


def _rope_tables(n_tokens):
    pos = np.arange(n_tokens)
    n_freq = HEAD_DIM // 4
    freqs = ROPE_BASE ** (-np.arange(n_freq, dtype=np.float64) / n_freq)
    ang_r = (pos // GRID_W)[:, None] * freqs
    ang_c = (pos % GRID_W)[:, None] * freqs
    cos_h = np.concatenate([np.cos(ang_r), np.cos(ang_r), np.cos(ang_c), np.cos(ang_c)], axis=1)
    sin_h = np.concatenate([-np.sin(ang_r), np.sin(ang_r), -np.sin(ang_c), np.sin(ang_c)], axis=1)
    return jnp.asarray(np.tile(cos_h, (1, 2)), F32), jnp.asarray(np.tile(sin_h, (1, 2)), F32)


def kernel(x_prompt, x_sample, cache_k, cache_v, state_ssm, c, c_ctx, norm1_g, norm2_g, w_mod, b_mod,
           w_qkv, w_o, attn_sink, ssm_lam_re, ssm_lam_im, ssm_log_dt, ssm_b_re, ssm_b_im, ssm_c_re,
           ssm_c_im, ssm_d, glu_w_a, glu_w_b, mlp_w1, mlp_w2, final_norm_g):
    bp, lp, _ = x_prompt.shape
    bx, lx, _ = x_sample.shape
    assert lx % TOKEN_TILE == 0 and (bp * lp) % TOKEN_TILE == 0
    tiles_per_lat = lx // TOKEN_TILE

    xp = x_prompt.reshape(bp * lp, D_MODEL)
    xx = x_sample.reshape(bx * lx, D_MODEL)

    cvecs = jnp.concatenate([c_ctx[None], c, jnp.zeros((8 - 1 - bx, D_MODEL), F32)], axis=0).astype(F32)
    mod = _modulation(cvecs, w_mod, b_mod)

    ctx_row = lambda i: 0
    lat_row = lambda i: 1 + i // tiles_per_lat

    assert lx % QKV_TILE == 0 and (bp * lp) % QKV_TILE == 0
    qkv_tiles_per_lat = lx // QKV_TILE
    rope = _rope_tables(lx) + (lambda i: i % qkv_tiles_per_lat,)
    wqkv = w_qkv[0].astype(BF16)
    g1 = norm1_g[0].reshape(1, D_MODEL)
    sink = attn_sink[0].astype(F32)
    qp, krp, vrp, kp, vp = _qkv(xp, mod[0], ctx_row, g1, wqkv, None, lp)
    qx, krx, vrx = _qkv(xx, mod[0], lambda i: 1 + i // qkv_tiles_per_lat, g1, wqkv, rope, 0)
    op = _ctx_attention(sink, qp, krp, vrp, bp, lp)
    flat_cache = lambda t: t[:, 0].reshape(bx, -1, KV_DIM).astype(F32)
    ox, (w1, w2, wo, wa, wb) = _lat_attention(
        sink, qx, krx, vrx, flat_cache(cache_k), flat_cache(cache_v), bx, lx,
        [mlp_w1, mlp_w2, w_o[0], glu_w_a[0], glu_w_b[0]])
    g2 = norm2_g.reshape(-1, 1, D_MODEL)
    gn = norm1_g[1].reshape(1, D_MODEL)
    vpt_p = TOKEN_TILE // lp
    vpt_x = 1
    n_seg_x = tiles_per_lat
    xp, hp = _post(xp, op, mod[0], ctx_row, g2[0], wo, None, w1, w2, 0, vpt_p, mod_next=mod[1], g_next=gn)
    xx, hx = _post(xx, ox, mod[0], lat_row, g2[0], wo, None, w1, w2, 0, vpt_x, mod_next=mod[1], g_next=gn)

    params = _s5_params(ssm_lam_re[0], ssm_lam_im[0], ssm_log_dt[0], ssm_b_re[0], ssm_b_im[0],
                        ssm_c_re[0], ssm_c_im[0])
    dsk = ssm_d[0].astype(F32).reshape(1, D_MODEL)
    sx = _state_to_blocks(state_ssm[:, 0].astype(F32))[:, :, :, None]
    pad = jnp.zeros((N_GROUP_BLOCKS, 1, bx, n_seg_x - 1, STATE_LANES), F32)
    s0x = jnp.concatenate([jnp.concatenate([sx[:, 0:1], pad], axis=3),
                           jnp.concatenate([pad, sx[:, 1:2]], axis=3)], axis=1)
    s0x = s0x.reshape(N_GROUP_BLOCKS, 2, bx * n_seg_x, STATE_LANES)
    yp, sfin = _s5(hp, params, dsk, None, 1)
    yx, _ = _s5(hx, params, dsk, s0x, n_seg_x)
    new_state = _blocks_to_state(sfin)[:, None]

    fg = final_norm_g.reshape(1, D_MODEL)
    (yp_out,) = _post(xp, yp, mod[1], ctx_row, g2[1], wa, wb, w1, w2, 1, vpt_p, final_g=fg)
    (yx_out,) = _post(xx, yx, mod[1], lat_row, g2[1], wa, wb, w1, w2, 1, vpt_x, final_g=fg)

    to_cache = lambda t: t.reshape(bp, N_KV_HEADS, HEAD_DIM, lp).transpose(0, 3, 1, 2)[:, None]
    new_k = to_cache(kp)
    new_v = to_cache(vp)
    return (yp_out.reshape(bp, lp, D_MODEL), yx_out.reshape(bx, lx, D_MODEL), new_k, new_v, new_state)
```

```python
import functools
import math

import numpy as np
import jax
import jax.numpy as jnp
from jax import lax
from jax.experimental import pallas as pl
from jax.experimental.pallas import tpu as pltpu

F32 = jnp.float32
BF16 = jnp.bfloat16

D_MODEL = 1024
N_HEADS = 16
N_KV_HEADS = 4
HEAD_DIM = 64
Q_PER_KV = N_HEADS // N_KV_HEADS
KV_DIM = N_KV_HEADS * HEAD_DIM
QKV_DIM = D_MODEL + 2 * KV_DIM
BLOCK = 128
GRID_W = 64
ROPE_BASE = 10000.0
ROT_HALF = HEAD_DIM // 4
ATTN_SCALE = HEAD_DIM ** -0.5
N_GROUPS = 64
GROUP_CH = 16
STATE_DIM = 64
D_FF = 4 * D_MODEL
N_MOD = 6
RMS_EPS = 1e-6
NEG_INF = -1e30

LANES = 128
SUBLANES = 8
GROUPS_PER_BLOCK = LANES // GROUP_CH
N_GROUP_BLOCKS = N_GROUPS // GROUPS_PER_BLOCK
STATE_LANES = GROUPS_PER_BLOCK * 2 * STATE_DIM
CHUNK = SUBLANES
TOKEN_TILE = 512
FF_TILE = 1024
POST_SUBTILES = 2
QKV_TILE = 1024
QKV_SUBTILES = 4
VMEM_LIMIT = 56 * 1024 * 1024


def _cparams(semantics):
    return pltpu.CompilerParams(dimension_semantics=semantics, vmem_limit_bytes=VMEM_LIMIT)


def _rms(x):
    return x * lax.rsqrt(jnp.mean(x * x, axis=-1, keepdims=True) + RMS_EPS)


def _dot(a, b):
    return jnp.dot(a, b, preferred_element_type=F32)


def _dot_nt(a, b):
    return lax.dot_general(a, b, (((1,), (1,)), ((), ())), preferred_element_type=F32)


def _mod_kernel(cv_ref, w_ref, b_ref, o_ref):
    cv = cv_ref[...]
    s = (cv * jax.nn.sigmoid(cv)).astype(BF16)
    o_ref[0] = _dot(s, w_ref[0].astype(BF16)) + b_ref[0]


def _modulation(cvecs, w_mod, b_mod):
    depth = w_mod.shape[0]
    width = 2 * D_MODEL
    out = pl.pallas_call(
        _mod_kernel,
        grid=(depth, N_MOD * D_MODEL // width),
        in_specs=[
            pl.BlockSpec((8, D_MODEL), lambda l, j: (0, 0)),
            pl.BlockSpec((1, D_MODEL, width), lambda l, j: (l, 0, j)),
            pl.BlockSpec((1, 1, width), lambda l, j: (l, 0, j)),
        ],
        out_specs=pl.BlockSpec((1, 8, width), lambda l, j: (l, 0, j)),
        out_shape=jax.ShapeDtypeStruct((depth, 8, N_MOD * D_MODEL), F32),
        compiler_params=_cparams(("arbitrary", "arbitrary")),
        name="modulation",
    )(cvecs, w_mod, b_mod.reshape(depth, 1, N_MOD * D_MODEL))
    return out.reshape(depth, 8, N_MOD, D_MODEL)


def _head_pair(blk, odd):
    lo = lax.broadcasted_iota(jnp.int32, blk.shape, 1) < HEAD_DIM
    other = pltpu.roll(blk, HEAD_DIM, 1)
    return (jnp.where(lo, other, blk) if odd else jnp.where(lo, blk, other)).astype(BF16)


def _qkv_kernel(cache_seq, rope, x_ref, mod_ref, g_ref, w_ref, *refs):
    refs = list(refs)
    cos_ref, sin_ref = (refs.pop(0), refs.pop(0)) if rope else (None, None)
    q_ref, krep_ref, vrep_ref = refs[:3]
    kv_refs = refs[3:]
    sub = QKV_TILE // QKV_SUBTILES
    starts = [k * sub for k in range(QKV_SUBTILES)]

    def project(r0):
        h = _rms(x_ref[r0:r0 + sub, :]) * g_ref[...] * (1.0 + mod_ref[1:2, :]) + mod_ref[0:1, :]
        return _dot(h.astype(BF16), w_ref[...])

    def emit_cache(ref, blk, r0, c0):
        for s in range(sub // cache_seq):
            ref[r0 // cache_seq + s, c0:c0 + LANES, :] = blk[s * cache_seq:(s + 1) * cache_seq, :].T

    def finish(r0, qkv):
        rows = slice(r0, r0 + sub)
        if rope:
            cos = cos_ref[rows, :]
            sin = sin_ref[rows, :]
            lane = lax.broadcasted_iota(jnp.int32, cos.shape, 1)
            first = (lane & (2 * ROT_HALF - 1)) < ROT_HALF
        for blk in range((D_MODEL + KV_DIM) // LANES):
            r = qkv[:, blk * LANES:(blk + 1) * LANES]
            if rope:
                partner = jnp.where(first, pltpu.roll(r, LANES - ROT_HALF, 1), pltpu.roll(r, ROT_HALF, 1))
                r = r * cos + partner * sin
            if blk < D_MODEL // LANES:
                q_ref[rows, blk * LANES:(blk + 1) * LANES] = (r * ATTN_SCALE).astype(BF16)
            else:
                c0 = blk * LANES - D_MODEL
                if cache_seq:
                    emit_cache(kv_refs[0], r, r0, c0)
                for half in range(2):
                    krep_ref[c0 // HEAD_DIM + half, rows, :] = _head_pair(r, half)
        v = qkv[:, D_MODEL + KV_DIM:]
        for c0 in range(0, KV_DIM, LANES):
            blk = v[:, c0:c0 + LANES]
            if cache_seq:
                emit_cache(kv_refs[1], blk, r0, c0)
            for half in range(2):
                vrep_ref[c0 // HEAD_DIM + half, rows, :] = _head_pair(blk, half)

    for r0, qkv in zip(starts, [project(r0) for r0 in starts]):
        finish(r0, qkv)


def _qkv(x, mod, mod_row, g, w_qkv, rope, cache_seq):
    ntok = x.shape[0]
    nt = ntok // QKV_TILE
    emit_kv = cache_seq > 0
    assert not emit_kv or (QKV_TILE // QKV_SUBTILES) % cache_seq == 0
    rep_spec = pl.BlockSpec((N_KV_HEADS, QKV_TILE, LANES), lambda i: (0, i, 0))
    rep_shape = jax.ShapeDtypeStruct((N_KV_HEADS, ntok, LANES), BF16)
    in_specs = [
        pl.BlockSpec((QKV_TILE, D_MODEL), lambda i: (i, 0)),
        pl.BlockSpec((None, N_MOD, D_MODEL), lambda i: (mod_row(i), 0, 0)),
        pl.BlockSpec((1, D_MODEL), lambda i: (0, 0)),
        pl.BlockSpec((D_MODEL, QKV_DIM), lambda i: (0, 0)),
    ]
    args = [x, mod, g, w_qkv]
    if rope is not None:
        cos_t, sin_t, rope_blk = rope
        in_specs += [pl.BlockSpec((QKV_TILE, LANES), lambda i: (rope_blk(i), 0))] * 2
        args += [cos_t, sin_t]
    out_specs = [pl.BlockSpec((QKV_TILE, D_MODEL), lambda i: (i, 0)), rep_spec, rep_spec]
    out_shape = [jax.ShapeDtypeStruct((ntok, D_MODEL), BF16), rep_shape, rep_shape]
    if emit_kv:
        spt = QKV_TILE // cache_seq
        out_specs += [pl.BlockSpec((spt, KV_DIM, cache_seq), lambda i: (i, 0, 0))] * 2
        out_shape += [jax.ShapeDtypeStruct((ntok // cache_seq, KV_DIM, cache_seq), F32)] * 2
    return pl.pallas_call(
        functools.partial(_qkv_kernel, cache_seq, rope is not None),
        grid=(nt,),
        in_specs=in_specs,
        out_specs=out_specs,
        out_shape=out_shape,
        compiler_params=_cparams(("arbitrary",)),
        name="norm_qkv_rope",
    )(*args)


def _group_scores(q_ref, kv, key_parts, bias):
    nq = q_ref.shape[0]
    lo = lax.broadcasted_iota(jnp.int32, (nq, LANES), 1) < HEAD_DIM
    zero = jnp.zeros((), BF16)
    rows = []
    for b in range(KV_DIM // LANES):
        blk = q_ref[:, kv * KV_DIM + b * LANES:kv * KV_DIM + (b + 1) * LANES]
        rows += [jnp.where(lo, blk, zero), jnp.where(lo, zero, blk)]
    q4 = jnp.concatenate(rows, axis=0)
    parts = [_dot_nt(q4, keys) for keys in key_parts]
    if bias is not None:
        s0 = parts[0].reshape(Q_PER_KV, nq, -1) + bias[None]
        parts[0] = s0.reshape(Q_PER_KV * nq, -1)
    return parts


def _group_softmax(parts, sink_ref, kv):
    nq = parts[0].shape[0] // Q_PER_KV
    sink = jnp.concatenate(
        [jnp.full((nq, LANES), sink_ref[kv * Q_PER_KV + g], F32) for g in range(Q_PER_KV)], axis=0)
    blocks = [[s[:, j:j + LANES] for j in range(0, s.shape[1], LANES)] for s in parts]
    fold = None
    for b in sum(blocks, []):
        fold = b if fold is None else jnp.maximum(fold, b)
    m = jnp.maximum(jnp.max(fold, axis=-1, keepdims=True), sink)
    probs = [[jnp.exp(b - m) for b in bs] for bs in blocks]
    fold = None
    for p in sum(probs, []):
        fold = p if fold is None else fold + p
    den = jnp.sum(fold, axis=-1, keepdims=True) + jnp.exp(sink - m)
    return [jnp.concatenate(ps, axis=1).astype(BF16) for ps in probs], 1.0 / den


def _group_output(probs, inv_den, value_parts, o_ref, kv):
    nq = probs[0].shape[0] // Q_PER_KV
    r = _dot(probs[0], value_parts[0])
    for p, v in zip(probs[1:], value_parts[1:]):
        r = r + _dot(p, v)
    r = r * inv_den
    lo = lax.broadcasted_iota(jnp.int32, (nq, LANES), 1) < HEAD_DIM
    for b in range(KV_DIM // LANES):
        pair = jnp.where(lo, r[2 * b * nq:(2 * b + 1) * nq], r[(2 * b + 1) * nq:(2 * b + 2) * nq])
        o_ref[:, kv * KV_DIM + b * LANES:kv * KV_DIM + (b + 1) * LANES] = pair.astype(BF16)


def _attend(q_ref, sink_ref, o_ref, keys_of, values_of, bias):
    s_next = _group_scores(q_ref, 0, keys_of(0), bias)
    for kv in range(N_KV_HEADS):
        s = s_next
        if kv + 1 < N_KV_HEADS:
            s_next = _group_scores(q_ref, kv + 1, keys_of(kv + 1), bias)
        probs, inv_den = _group_softmax(s, sink_ref, kv)
        _group_output(probs, inv_den, values_of(kv), o_ref, kv)


def _ctx_attn_kernel(sink_ref, q_ref, k_ref, v_ref, o_ref):
    _attend(q_ref, sink_ref, o_ref, lambda kv: [k_ref[kv]], lambda kv: [v_ref[kv]], None)


def _ctx_attention(sink, q, krep, vrep, n_batch, seq):
    rep_spec = pl.BlockSpec((N_KV_HEADS, seq, LANES), lambda b: (0, b, 0))
    return pl.pallas_call(
        _ctx_attn_kernel,
        grid=(n_batch,),
        in_specs=[
            pl.BlockSpec(memory_space=pltpu.SMEM),
            pl.BlockSpec((seq, D_MODEL), lambda b: (b, 0)),
            rep_spec, rep_spec,
        ],
        out_specs=pl.BlockSpec((seq, D_MODEL), lambda b: (b, 0)),
        out_shape=jax.ShapeDtypeStruct((n_batch * seq, D_MODEL), BF16),
        compiler_params=_cparams(("arbitrary",)),
        name="context_attention",
    )(sink, q, krep, vrep)


def _window_start(n, seq):
    return jnp.clip((n - 1) * BLOCK, 0, seq - 3 * BLOCK)


def _band_bias():
    r = np.arange(BLOCK)[:, None]
    j = np.arange(3 * BLOCK)[None, :]
    out = [np.where(np.abs(j - d * BLOCK - r) <= BLOCK, 0.0, NEG_INF) for d in range(3)]
    return jnp.asarray(np.stack(out), F32)


def _lat_attn_kernel(seq, n_cast, sink_ref, q_ref, k_ref, v_ref, ck_ref, cv_ref, bias_ref, *refs):
    o_ref = refs[n_cast]
    for src, dst in zip(refs[:n_cast], refs[n_cast + 1:]):
        dst[...] = src[...].astype(BF16)
    win = 3 * BLOCK
    start = pl.multiple_of(_window_start(pl.program_id(1), seq), BLOCK)
    keys_of = lambda kv: [k_ref[kv, pl.ds(start, win), :], ck_ref[kv]]
    values_of = lambda kv: [v_ref[kv, pl.ds(start, win), :], cv_ref[kv]]
    _attend(q_ref, sink_ref, o_ref, keys_of, values_of, bias_ref[...])


def _lat_attention(sink, q, krep, vrep, ckrep, cvrep, n_batch, seq, cast_weights):
    nb = seq // BLOCK
    steps = n_batch * nb
    past = ckrep.shape[2]
    rep_spec = pl.BlockSpec((N_KV_HEADS, seq, LANES), lambda b, n: (0, b, 0))
    crep_spec = pl.BlockSpec((None, N_KV_HEADS, past, LANES), lambda b, n: (b, 0, 0, 0))
    flat = [w.reshape(-1, w.shape[-1]) for w in cast_weights]
    assert all(w.shape[0] % (steps * 2 * SUBLANES) == 0 for w in flat)
    slabs = [pl.BlockSpec((w.shape[0] // steps, w.shape[1]), lambda b, n: (b * nb + n, 0)) for w in flat]
    outs = pl.pallas_call(
        functools.partial(_lat_attn_kernel, seq, len(flat)),
        grid=(n_batch, nb),
        in_specs=[
            pl.BlockSpec(memory_space=pltpu.SMEM),
            pl.BlockSpec((BLOCK, D_MODEL), lambda b, n: (b * nb + n, 0)),
            rep_spec, rep_spec, crep_spec, crep_spec,
            pl.BlockSpec((None, BLOCK, 3 * BLOCK), lambda b, n: (n - _window_start(n, seq) // BLOCK, 0, 0)),
        ] + slabs,
        out_specs=[pl.BlockSpec((BLOCK, D_MODEL), lambda b, n: (b * nb + n, 0))] + slabs,
        out_shape=[jax.ShapeDtypeStruct((n_batch * seq, D_MODEL), BF16)]
        + [jax.ShapeDtypeStruct(w.shape, BF16) for w in flat],
        compiler_params=_cparams(("arbitrary", "arbitrary")),
        name="latent_attention",
    )(sink, q, krep, vrep, ckrep, cvrep, _band_bias(), *flat)
    return outs[0], [o.reshape(w.shape) for o, w in zip(outs[1:], cast_weights)]


def _gelu_tanh(x):
    c = math.sqrt(2.0 / math.pi)
    return x * (0.5 * (1.0 + jnp.tanh(c * (x + 0.044715 * (x * x * x)))))


def _post_kernel(is_attn, emit_next, final, vpt, *refs):
    rows_per_v = TOKEN_TILE // vpt
    refs = list(refs)
    x_ref, mix_ref, mod_ref, g2_ref, wa_ref = refs[:5]
    refs = refs[5:]
    wb_ref = None if is_attn else refs.pop(0)
    w1_ref, w2_ref = refs[:2]
    refs = refs[2:]
    modn_ref = gn_ref = fg_ref = hn_ref = None
    if emit_next:
        modn_ref, gn_ref = refs[:2]
        refs = refs[2:]
    if final:
        fg_ref = refs.pop(0)
    xo_ref = refs.pop(0)
    if emit_next:
        hn_ref = refs.pop(0)
    assert not refs

    sub = TOKEN_TILE // POST_SUBTILES
    assert rows_per_v % sub == 0
    n_chunk = sub // CHUNK
    starts = [k * sub for k in range(POST_SUBTILES)]

    def s5_slot(r0):
        return r0 // rows_per_v, (r0 % rows_per_v) // CHUNK

    def project(r0):
        if is_attn:
            return _dot(mix_ref[r0:r0 + sub, :], wa_ref[...])
        s, c0 = s5_slot(r0)
        y = jnp.concatenate(
            [mix_ref[g, c0:c0 + n_chunk, s].reshape(sub, LANES) for g in range(N_GROUP_BLOCKS)], axis=1)
        yg = _gelu_tanh(y).astype(BF16)
        return _dot(yg, wa_ref[...]) * jax.nn.sigmoid(_dot(yg, wb_ref[...]))

    def prologue(r0, mix):
        x1 = x_ref[r0:r0 + sub, :] + mod_ref[2:3, :] * mix
        h2 = _rms(x1) * g2_ref[...] * (1.0 + mod_ref[4:5, :]) + mod_ref[3:4, :]
        return x1, h2.astype(BF16)

    def mlp(h2):
        acc = None
        for c in range(D_FF // FF_TILE):
            a = jnp.maximum(_dot(h2, w1_ref[:, c * FF_TILE:(c + 1) * FF_TILE]), 0.0)
            t = _dot((a * a).astype(BF16), w2_ref[c * FF_TILE:(c + 1) * FF_TILE, :])
            acc = t if acc is None else acc + t
        return acc

    def epilogue(r0, x1, acc):
        x2 = x1 + mod_ref[5:6, :] * acc
        if emit_next:
            hn = _rms(x2) * gn_ref[...] * (1.0 + modn_ref[1:2, :]) + modn_ref[0:1, :]
            s, c0 = s5_slot(r0)
            for g in range(N_GROUP_BLOCKS):
                blk = hn[:, g * LANES:(g + 1) * LANES]
                hn_ref[g, c0:c0 + n_chunk, s] = blk.reshape(n_chunk, CHUNK, LANES)
        xo_ref[r0:r0 + sub, :] = _rms(x2) * fg_ref[...] if final else x2

    mixes = [project(r0) for r0 in starts]
    pro = [prologue(r0, mix) for r0, mix in zip(starts, mixes)]
    accs = [mlp(h2) for _, h2 in pro]
    for r0, (x1, _), acc in zip(starts, pro, accs):
        epilogue(r0, x1, acc)


def _post(x, mix, mod, mod_row, g2, w_a, w_b, w1, w2, layer, vpt, mod_next=None, g_next=None, final_g=None):
    is_attn = w_b is None
    emit_next = mod_next is not None
    final = final_g is not None
    ntok = x.shape[0]
    nt = ntok // TOKEN_TILE
    cpv = TOKEN_TILE // (vpt * CHUNK)
    n_virt = nt * vpt
    tile = pl.BlockSpec((TOKEN_TILE, D_MODEL), lambda i: (i, 0))
    row = pl.BlockSpec((1, D_MODEL), lambda i: (0, 0))
    modspec = pl.BlockSpec((None, N_MOD, D_MODEL), lambda i: (mod_row(i), 0, 0))
    resident = lambda shape: pl.BlockSpec(shape, lambda i: (0, 0), pipeline_mode=pl.Buffered(1))
    wsq = resident((D_MODEL, D_MODEL))
    gtile = pl.BlockSpec((N_GROUP_BLOCKS, cpv, vpt, CHUNK, LANES), lambda i: (0, 0, i, 0, 0))
    in_specs = [tile, tile if is_attn else gtile, modspec, row, wsq]
    args = [x, mix, mod, g2, w_a]
    if not is_attn:
        in_specs.append(wsq)
        args.append(w_b)
    in_specs += [pl.BlockSpec((None, D_MODEL, D_FF), lambda i: (layer, 0, 0), pipeline_mode=pl.Buffered(1)),
                 pl.BlockSpec((None, D_FF, D_MODEL), lambda i: (layer, 0, 0), pipeline_mode=pl.Buffered(1))]
    args += [w1, w2]
    if emit_next:
        in_specs += [modspec, row]
        args += [mod_next, g_next]
    if final:
        in_specs.append(row)
        args.append(final_g)
    out_specs = [tile]
    out_shape = [jax.ShapeDtypeStruct((ntok, D_MODEL), F32)]
    if emit_next:
        out_specs.append(gtile)
        out_shape.append(jax.ShapeDtypeStruct((N_GROUP_BLOCKS, cpv, n_virt, CHUNK, LANES), F32))
    return pl.pallas_call(
        functools.partial(_post_kernel, is_attn, emit_next, final, vpt),
        grid=(nt,),
        in_specs=in_specs,
        out_specs=out_specs,
        out_shape=out_shape,
        compiler_params=_cparams(("arbitrary",)),
        name="attn_proj_mlp" if is_attn else "glu_mlp_final",
    )(*args)


def _swap(x):
    return pltpu.roll(x, LANES // 2, 1)


def _cmul(z, w_r, w_i):
    return z * w_r + _swap(z) * w_i


def _multiplier(z, lo):
    zs = _swap(z)
    return jnp.where(lo, z, zs), jnp.where(lo, -zs, z)


def _rep_rows(x):
    return jnp.concatenate(
        [jnp.broadcast_to(x[g:g + 1, :], (GROUP_CH, LANES)) for g in range(GROUPS_PER_BLOCK)], axis=0)


def _s5_kernel(n_virt, n_seg, h_ref, lamr_ref, lami_ref, ldt_ref, bt_ref, cp_ref, dsk_ref, s0_ref, *refs):
    if n_seg == 1:
        y_ref, sfin_ref = refs[:2]
        refs = refs[2:]
    else:
        y_ref, sfin_ref = refs[0], None
        refs = refs[1:]
    f_scr, e_scr, k_scr, sf_scr, sb_scr, swf_scr, swb_scr = refs
    ntok = h_ref.shape[0]
    nc = ntok // CHUNK
    cpv = nc // n_virt
    gpb = GROUPS_PER_BLOCK

    lo8 = lax.broadcasted_iota(jnp.int32, (gpb, LANES), 1) < STATE_DIM
    lo = lax.broadcasted_iota(jnp.int32, (LANES, LANES), 1) < STATE_DIM
    conj = jnp.where(lo, 1.0, -1.0)
    row_g = lax.broadcasted_iota(jnp.int32, (LANES, STATE_LANES), 0) // GROUP_CH
    col_g = lax.broadcasted_iota(jnp.int32, (LANES, STATE_LANES), 1) // LANES
    diag_wide = row_g == col_g
    diag = (lax.broadcasted_iota(jnp.int32, (LANES, LANES), 0) // GROUP_CH) == (
        lax.broadcasted_iota(jnp.int32, (LANES, LANES), 1) // GROUP_CH)

    def expand(w):
        return jnp.where(diag_wide, jnp.concatenate([w] * gpb, axis=1), jnp.zeros((), BF16))

    decay = []
    lag = []
    for d in range(2):
        lam_r = lamr_ref[d]
        lam_i = lami_ref[d]
        dt = jnp.exp(ldt_ref[d])
        mag = jnp.exp(lam_r * dt)
        ang = lam_i * dt
        a_r = mag * jnp.cos(ang)
        a_im = mag * jnp.sin(ang)
        a_i = jnp.where(lo8, -a_im, a_im)
        den = lam_r * lam_r + lam_i * lam_i
        num = jnp.where(lo8, a_r - 1.0, a_im)
        f = _cmul(num, lam_r / den, jnp.where(lo8, lam_i, -lam_i) / den)
        pw = [jnp.where(lo8, 1.0, 0.0)]
        for _ in range(CHUNK):
            pw.append(_cmul(pw[-1], a_r, a_i))
        decay.append(pw[CHUNK])
        pw = [_rep_rows(p) for p in pw]
        f_r, f_i = _multiplier(_rep_rows(f), lo)
        bb_r, bb_i = _multiplier(_cmul(bt_ref[d], f_r, f_i), lo)
        c_r, c_i = _multiplier(cp_ref[d], lo)
        cm = (cp_ref[d] * conj).astype(BF16)
        fpow = [_cmul(p, bb_r, bb_i).astype(BF16) for p in pw[:CHUNK]]
        for j in range(CHUNK):
            e = (CHUNK - 1 - j) if d == 0 else j
            f_scr[d, j * LANES:(j + 1) * LANES, :] = expand(fpow[e])
        for t in range(CHUNK):
            e = (t + 1) if d == 0 else (CHUNK - t)
            w = _cmul(pw[e], c_r, c_i) * conj
            e_scr[d, t * LANES:(t + 1) * LANES, :] = expand(w.astype(BF16))
        lag.append([jnp.where(diag, _dot_nt(fp, cm), 0.0) for fp in fpow])

    for j in range(CHUNK):
        for t in range(CHUNK):
            k = t - j
            tile = lag[0][k] if k > 0 else (lag[1][-k] if k < 0 else lag[0][0] + lag[1][0])
            k_scr[j * LANES:(j + 1) * LANES, t * LANES:(t + 1) * LANES] = tile.astype(BF16)

    xcat = jnp.concatenate(
        [h_ref[pl.ds(j, nc, stride=CHUNK), :].astype(BF16) for j in range(CHUNK)], axis=1)

    for d, scr, sw_scr in ((0, sf_scr, swf_scr), (1, sb_scr, swb_scr)):
        loc_all = _dot(xcat, f_scr[d])
        for k in range(gpb):
            loc = loc_all[:, k * LANES:(k + 1) * LANES]
            scr[k] = loc
            sw_scr[k] = _swap(loc)
    y_within = _dot(xcat, k_scr[...])

    sgn8 = jnp.where(lo8, -1.0, 1.0)

    def dup(z):
        zs = _swap(z)
        return jnp.where(lo8, z, zs), jnp.where(lo8, zs, z)

    def scan(scr, sw_scr, d, reverse):
        a_re, a_im = dup(decay[d])
        a_sg = a_im * sgn8
        a_r = [a_re[k:k + 1, :] for k in range(gpb)]
        a_i = [a_sg[k:k + 1, :] for k in range(gpb)]

        def rows_of(i):
            c = (cpv - 1 - i) if reverse else i
            return pl.ds(c * n_virt, n_virt)

        def body(i, carry):
            st, sw = carry
            rows = rows_of(i)
            new_st, new_sw = [], []
            for k in range(gpb):
                loc = scr[k, rows, :]
                loc_sw = sw_scr[k, rows, :]
                scr[k, rows, :] = st[k]
                new_st.append(a_r[k] * st[k] + a_i[k] * sw[k] + loc)
                new_sw.append(a_r[k] * sw[k] - a_i[k] * st[k] + loc_sw)
            return tuple(new_st), tuple(new_sw)

        st0 = tuple(s0_ref[d, :, k * LANES:(k + 1) * LANES] for k in range(gpb))
        sw0 = tuple(_swap(s) for s in st0)
        carry = (st0, sw0)
        for i in range(cpv):
            carry = body(i, carry)
        fin = carry[0]
        if n_seg == 1:
            for k in range(gpb):
                sfin_ref[d, :, k * LANES:(k + 1) * LANES] = fin[k]
            return

        p = decay[d]
        for _ in range(cpv.bit_length() - 1):
            p_re, p_im = dup(p)
            p = _cmul(p, p_re, p_im * sgn8)
        v_re, v_im = dup(p)
        v_sg = v_im * sgn8
        seg = lax.broadcasted_iota(jnp.int32, (n_virt, LANES), 0) & (n_seg - 1)
        has_pred = seg != ((n_seg - 1) if reverse else 0)
        shift = (n_virt - 1) if reverse else 1
        cin = []
        for k in range(gpb):
            ck = jnp.zeros((n_virt, LANES), F32)
            for _ in range(n_seg - 1):
                nxt = fin[k] + ck * v_re[k:k + 1, :] + _swap(ck) * v_sg[k:k + 1, :]
                ck = jnp.where(has_pred, pltpu.roll(nxt, shift, 0), 0.0)
            cin.append(ck)
        cin_sw = [_swap(x) for x in cin]

        def fix(i, carry):
            q_re, q_im = carry
            rows = rows_of(i)
            q_sg = q_im * sgn8
            for k in range(gpb):
                scr[k, rows, :] += cin[k] * q_re[k:k + 1, :] + cin_sw[k] * q_sg[k:k + 1, :]
            return q_re * a_re - q_im * a_im, q_re * a_im + q_im * a_re

        q = (jnp.ones((gpb, LANES), F32), jnp.zeros((gpb, LANES), F32))
        for i in range(cpv):
            q = fix(i, q)

    scan(sf_scr, swf_scr, 0, False)
    scan(sb_scr, swb_scr, 1, True)

    s_f = jnp.concatenate([sf_scr[k].astype(BF16) for k in range(gpb)], axis=1)
    s_b = jnp.concatenate([sb_scr[k].astype(BF16) for k in range(gpb)], axis=1)
    yall = y_within + _dot_nt(s_f, e_scr[0]) + _dot_nt(s_b, e_scr[1])
    dsk = dsk_ref[...]
    for t in range(CHUNK):
        rows = pl.ds(t, nc, stride=CHUNK)
        y_ref[rows, :] = yall[:, t * LANES:(t + 1) * LANES] + h_ref[rows, :] * dsk


def _s5_params(lam_re, lam_im, log_dt, b_re, b_im, c_re, c_im):
    lamr = jnp.concatenate([lam_re, lam_re], axis=-1).astype(F32)
    lami = jnp.concatenate([lam_im, lam_im], axis=-1).astype(F32)
    ldt = jnp.broadcast_to(log_dt.astype(F32)[..., None], lamr.shape)
    bt = jnp.concatenate([b_re.transpose(0, 1, 3, 2), b_im.transpose(0, 1, 3, 2)], axis=-1)
    cp = jnp.concatenate([c_re, c_im], axis=-1)
    return (lamr, lami, ldt, bt.reshape(2, D_MODEL, LANES).astype(F32), cp.reshape(2, D_MODEL, LANES).astype(F32))


def _s5(h, params, d_skip, s0, n_seg):
    lamr, lami, ldt, bt, cp = params
    _, cpv, n_virt, _, _ = h.shape
    assert cpv & (cpv - 1) == 0 and n_virt % SUBLANES == 0 and n_seg & (n_seg - 1) == 0
    nc = cpv * n_virt
    ntok = nc * CHUNK
    hspec = pl.BlockSpec((None, ntok, LANES), lambda g: (g, 0, 0))
    kdim = CHUNK * LANES
    gspec = pl.BlockSpec((2, GROUPS_PER_BLOCK, LANES), lambda g: (0, g, 0))
    rspec = pl.BlockSpec((2, LANES, LANES), lambda g: (0, g, 0))
    sspec = pl.BlockSpec((None, 2, n_virt, STATE_LANES), lambda g: (g, 0, 0, 0))
    state_scr = pltpu.VMEM((GROUPS_PER_BLOCK, nc, LANES), F32)
    out_specs = [hspec]
    out_shape = [jax.ShapeDtypeStruct((N_GROUP_BLOCKS, ntok, LANES), F32)]
    if n_seg == 1:
        out_specs.append(sspec)
        out_shape.append(jax.ShapeDtypeStruct((N_GROUP_BLOCKS, 2, n_virt, STATE_LANES), F32))
    outs = pl.pallas_call(
        functools.partial(_s5_kernel, n_virt, n_seg),
        grid=(N_GROUP_BLOCKS,),
        in_specs=[
            hspec,
            gspec, gspec, gspec, rspec, rspec,
            pl.BlockSpec((1, LANES), lambda g: (0, g)),
            sspec,
        ],
        out_specs=out_specs,
        out_shape=out_shape,
        scratch_shapes=[
            pltpu.VMEM((2, kdim, STATE_LANES), BF16),
            pltpu.VMEM((2, kdim, STATE_LANES), BF16),
            pltpu.VMEM((kdim, kdim), BF16),
            state_scr, state_scr, state_scr, state_scr,
        ],
        compiler_params=_cparams(("arbitrary",)),
        name="s5_chunked_scan",
    )(h.reshape(N_GROUP_BLOCKS, ntok, LANES), lamr, lami, ldt, bt, cp, d_skip, s0)
    y = outs[0].reshape(h.shape)
    return (y, outs[1]) if n_seg == 1 else (y, None)


def _state_to_blocks(s):
    b = s.shape[0]
    s = s.reshape(b, 2, 2, N_GROUP_BLOCKS, GROUPS_PER_BLOCK, STATE_DIM)
    return s.transpose(3, 1, 0, 4, 2, 5).reshape(N_GROUP_BLOCKS, 2, b, STATE_LANES)


def _blocks_to_state(s):
    b = s.shape[2]
    s = s.reshape(N_GROUP_BLOCKS, 2, b, GROUPS_PER_BLOCK, 2, STATE_DIM)
    return s.transpose(2, 1, 4, 0, 3, 5).reshape(b, 2, 2, N_GROUPS, STATE_DIM)


def _rope_tables(n_tokens):
    pos = np.arange(n_tokens)
    n_freq = HEAD_DIM // 4
    freqs = ROPE_BASE ** (-np.arange(n_freq, dtype=np.float64) / n_freq)
    ang_r = (pos // GRID_W)[:, None] * freqs
    ang_c = (pos % GRID_W)[:, None] * freqs
    cos_h = np.concatenate([np.cos(ang_r), np.cos(ang_r), np.cos(ang_c), np.cos(ang_c)], axis=1)
    sin_h = np.concatenate([-np.sin(ang_r), np.sin(ang_r), -np.sin(ang_c), np.sin(ang_c)], axis=1)
    return jnp.asarray(np.tile(cos_h, (1, 2)), F32), jnp.asarray(np.tile(sin_h, (1, 2)), F32)


def kernel(x_prompt, x_sample, cache_k, cache_v, state_ssm, c, c_ctx, norm1_g, norm2_g, w_mod, b_mod,
           w_qkv, w_o, attn_sink, ssm_lam_re, ssm_lam_im, ssm_log_dt, ssm_b_re, ssm_b_im, ssm_c_re,
           ssm_c_im, ssm_d, glu_w_a, glu_w_b, mlp_w1, mlp_w2, final_norm_g):
    bp, lp, _ = x_prompt.shape
    bx, lx, _ = x_sample.shape
    assert lx % TOKEN_TILE == 0 and (bp * lp) % TOKEN_TILE == 0
    tiles_per_lat = lx // TOKEN_TILE

    xp = x_prompt.reshape(bp * lp, D_MODEL)
    xx = x_sample.reshape(bx * lx, D_MODEL)

    cvecs = jnp.zeros((8, D_MODEL), F32).at[0].set(c_ctx).at[1:1 + bx].set(c)
    mod = _modulation(cvecs, w_mod, b_mod)

    ctx_row = lambda i: 0
    lat_row = lambda i: 1 + i // tiles_per_lat

    assert lx % QKV_TILE == 0 and (bp * lp) % QKV_TILE == 0
    qkv_tiles_per_lat = lx // QKV_TILE
    rope = _rope_tables(lx) + (lambda i: i % qkv_tiles_per_lat,)
    wqkv = w_qkv[0].astype(BF16)
    g1 = norm1_g[0].reshape(1, D_MODEL)
    sink = attn_sink[0].astype(F32)
    qp, krp, vrp, kp, vp = _qkv(xp, mod[0], ctx_row, g1, wqkv, None, lp)
    qx, krx, vrx = _qkv(xx, mod[0], lambda i: 1 + i // qkv_tiles_per_lat, g1, wqkv, rope, 0)
    op = _ctx_attention(sink, qp, krp, vrp, bp, lp)
    rep = lambda t: jnp.tile(t[:, 0].transpose(0, 2, 1, 3), (1, 1, 1, LANES // HEAD_DIM)).astype(BF16)
    ox, (w1, w2, wo, wa, wb) = _lat_attention(
        sink, qx, krx, vrx, rep(cache_k), rep(cache_v), bx, lx,
        [mlp_w1, mlp_w2, w_o[0], glu_w_a[0], glu_w_b[0]])
    g2 = norm2_g.reshape(-1, 1, D_MODEL)
    gn = norm1_g[1].reshape(1, D_MODEL)
    vpt_p = TOKEN_TILE // lp
    vpt_x = 1
    n_seg_x = tiles_per_lat
    xp, hp = _post(xp, op, mod[0], ctx_row, g2[0], wo, None, w1, w2, 0, vpt_p, mod_next=mod[1], g_next=gn)
    xx, hx = _post(xx, ox, mod[0], lat_row, g2[0], wo, None, w1, w2, 0, vpt_x, mod_next=mod[1], g_next=gn)

    params = _s5_params(ssm_lam_re[0], ssm_lam_im[0], ssm_log_dt[0], ssm_b_re[0], ssm_b_im[0],
                        ssm_c_re[0], ssm_c_im[0])
    dsk = ssm_d[0].astype(F32).reshape(1, D_MODEL)
    s0p = jnp.zeros((N_GROUP_BLOCKS, 2, bp, STATE_LANES), F32)
    sx = _state_to_blocks(state_ssm[:, 0].astype(F32))
    s0x = jnp.zeros((N_GROUP_BLOCKS, 2, bx, n_seg_x, STATE_LANES), F32)
    s0x = s0x.at[:, 0, :, 0].set(sx[:, 0]).at[:, 1, :, n_seg_x - 1].set(sx[:, 1])
    s0x = s0x.reshape(N_GROUP_BLOCKS, 2, bx * n_seg_x, STATE_LANES)
    yp, sfin = _s5(hp, params, dsk, s0p, 1)
    yx, _ = _s5(hx, params, dsk, s0x, n_seg_x)
    new_state = _blocks_to_state(sfin)[:, None]

    fg = final_norm_g.reshape(1, D_MODEL)
    (yp_out,) = _post(xp, yp, mod[1], ctx_row, g2[1], wa, wb, w1, w2, 1, vpt_p, final_g=fg)
    (yx_out,) = _post(xx, yx, mod[1], lat_row, g2[1], wa, wb, w1, w2, 1, vpt_x, final_g=fg)

    to_cache = lambda t: t.reshape(bp, N_KV_HEADS, HEAD_DIM, lp).transpose(0, 3, 1, 2)[:, None]
    new_k = to_cache(kp)
    new_v = to_cache(vp)
    return (yp_out.reshape(bp, lp, D_MODEL), yx_out.reshape(bx, lx, D_MODEL), new_k, new_v, new_state)
```

```python
import functools
import math

import numpy as np
import jax
import jax.numpy as jnp
from jax import lax
from jax.experimental import pallas as pl
from jax.experimental.pallas import tpu as pltpu

F32 = jnp.float32
BF16 = jnp.bfloat16

D_MODEL = 1024
N_HEADS = 16
N_KV_HEADS = 4
HEAD_DIM = 64
Q_PER_KV = N_HEADS // N_KV_HEADS
KV_DIM = N_KV_HEADS * HEAD_DIM
QKV_DIM = D_MODEL + 2 * KV_DIM
BLOCK = 128
GRID_W = 64
ROPE_BASE = 10000.0
ROT_HALF = HEAD_DIM // 4
ATTN_SCALE = HEAD_DIM ** -0.5
N_GROUPS = 64
GROUP_CH = 16
STATE_DIM = 64
D_FF = 4 * D_MODEL
N_MOD = 6
RMS_EPS = 1e-6
NEG_INF = -1e30

LANES = 128
SUBLANES = 8
GROUPS_PER_BLOCK = LANES // GROUP_CH
N_GROUP_BLOCKS = N_GROUPS // GROUPS_PER_BLOCK
STATE_LANES = GROUPS_PER_BLOCK * 2 * STATE_DIM
CHUNK = SUBLANES
TOKEN_TILE = 512
FF_TILE = 1024
POST_SUBTILES = 2
MOD_K_BLOCK = 256
QKV_TILE = 1024
QKV_SUBTILES = 4
VMEM_LIMIT = 56 * 1024 * 1024


def _cparams(semantics):
    return pltpu.CompilerParams(dimension_semantics=semantics, vmem_limit_bytes=VMEM_LIMIT)


def _rms(x):
    return x * lax.rsqrt(jnp.mean(x * x, axis=-1, keepdims=True) + RMS_EPS)


def _dot(a, b):
    return jnp.dot(a, b, preferred_element_type=F32)


def _dot_nt(a, b):
    return lax.dot_general(a, b, (((1,), (1,)), ((), ())), preferred_element_type=F32)


def _mod_kernel(cv_ref, w_ref, b_ref, o_ref):
    @pl.when(pl.program_id(1) == 0)
    def _():
        o_ref[0] = jnp.broadcast_to(b_ref[0], o_ref.shape[1:])

    cv = cv_ref[...]
    s = (cv * jax.nn.sigmoid(cv)).astype(BF16)
    o_ref[0] += _dot(s, w_ref[0].astype(BF16))


def _modulation(cvecs, w_mod, b_mod):
    depth = w_mod.shape[0]
    width = N_MOD * D_MODEL
    k_blk = MOD_K_BLOCK
    out = pl.pallas_call(
        _mod_kernel,
        grid=(depth, D_MODEL // k_blk),
        in_specs=[
            pl.BlockSpec((8, k_blk), lambda l, k: (0, k)),
            pl.BlockSpec((1, k_blk, width), lambda l, k: (l, k, 0)),
            pl.BlockSpec((1, 1, width), lambda l, k: (l, 0, 0)),
        ],
        out_specs=pl.BlockSpec((1, 8, width), lambda l, k: (l, 0, 0)),
        out_shape=jax.ShapeDtypeStruct((depth, 8, N_MOD * D_MODEL), F32),
        compiler_params=_cparams(("arbitrary", "arbitrary")),
        name="modulation",
    )(cvecs, w_mod, b_mod.reshape(depth, 1, N_MOD * D_MODEL))
    return out.reshape(depth, 8, N_MOD, D_MODEL)


def _head_pair(blk, odd):
    lo = lax.broadcasted_iota(jnp.int32, blk.shape, 1) < HEAD_DIM
    other = pltpu.roll(blk, HEAD_DIM, 1)
    return (jnp.where(lo, other, blk) if odd else jnp.where(lo, blk, other)).astype(BF16)


def _qkv_kernel(cache_seq, rope, x_ref, mod_ref, g_ref, w_ref, *refs):
    refs = list(refs)
    cos_ref, sin_ref = (refs.pop(0), refs.pop(0)) if rope else (None, None)
    q_ref, krep_ref, vrep_ref = refs[:3]
    kv_refs = refs[3:]
    sub = QKV_TILE // QKV_SUBTILES
    starts = [k * sub for k in range(QKV_SUBTILES)]

    def project(r0):
        h = _rms(x_ref[r0:r0 + sub, :]) * g_ref[...] * (1.0 + mod_ref[1:2, :]) + mod_ref[0:1, :]
        return _dot(h.astype(BF16), w_ref[...])

    def emit_cache(ref, blk, r0, c0):
        for s in range(sub // cache_seq):
            ref[r0 // cache_seq + s, c0:c0 + LANES, :] = blk[s * cache_seq:(s + 1) * cache_seq, :].T

    def finish(r0, qkv):
        rows = slice(r0, r0 + sub)
        if rope:
            cos = cos_ref[rows, :]
            sin = sin_ref[rows, :]
            lane = lax.broadcasted_iota(jnp.int32, cos.shape, 1)
            first = (lane & (2 * ROT_HALF - 1)) < ROT_HALF
        for blk in range((D_MODEL + KV_DIM) // LANES):
            r = qkv[:, blk * LANES:(blk + 1) * LANES]
            if rope:
                partner = jnp.where(first, pltpu.roll(r, LANES - ROT_HALF, 1), pltpu.roll(r, ROT_HALF, 1))
                r = r * cos + partner * sin
            if blk < D_MODEL // LANES:
                q_ref[rows, blk * LANES:(blk + 1) * LANES] = (r * ATTN_SCALE).astype(BF16)
            else:
                c0 = blk * LANES - D_MODEL
                if cache_seq:
                    emit_cache(kv_refs[0], r, r0, c0)
                for half in range(2):
                    krep_ref[c0 // HEAD_DIM + half, rows, :] = _head_pair(r, half)
        v = qkv[:, D_MODEL + KV_DIM:]
        for c0 in range(0, KV_DIM, LANES):
            blk = v[:, c0:c0 + LANES]
            if cache_seq:
                emit_cache(kv_refs[1], blk, r0, c0)
            for half in range(2):
                vrep_ref[c0 // HEAD_DIM + half, rows, :] = _head_pair(blk, half)

    for r0, qkv in zip(starts, [project(r0) for r0 in starts]):
        finish(r0, qkv)


def _qkv(x, mod, mod_row, g, w_qkv, rope, cache_seq):
    ntok = x.shape[0]
    nt = ntok // QKV_TILE
    emit_kv = cache_seq > 0
    assert not emit_kv or (QKV_TILE // QKV_SUBTILES) % cache_seq == 0
    rep_spec = pl.BlockSpec((N_KV_HEADS, QKV_TILE, LANES), lambda i: (0, i, 0))
    rep_shape = jax.ShapeDtypeStruct((N_KV_HEADS, ntok, LANES), BF16)
    in_specs = [
        pl.BlockSpec((QKV_TILE, D_MODEL), lambda i: (i, 0)),
        pl.BlockSpec((None, N_MOD, D_MODEL), lambda i: (mod_row(i), 0, 0)),
        pl.BlockSpec((1, D_MODEL), lambda i: (0, 0)),
        pl.BlockSpec((D_MODEL, QKV_DIM), lambda i: (0, 0)),
    ]
    args = [x, mod, g, w_qkv]
    if rope is not None:
        cos_t, sin_t, rope_blk = rope
        in_specs += [pl.BlockSpec((QKV_TILE, LANES), lambda i: (rope_blk(i), 0))] * 2
        args += [cos_t, sin_t]
    out_specs = [pl.BlockSpec((QKV_TILE, D_MODEL), lambda i: (i, 0)), rep_spec, rep_spec]
    out_shape = [jax.ShapeDtypeStruct((ntok, D_MODEL), BF16), rep_shape, rep_shape]
    if emit_kv:
        spt = QKV_TILE // cache_seq
        out_specs += [pl.BlockSpec((spt, KV_DIM, cache_seq), lambda i: (i, 0, 0))] * 2
        out_shape += [jax.ShapeDtypeStruct((ntok // cache_seq, KV_DIM, cache_seq), F32)] * 2
    return pl.pallas_call(
        functools.partial(_qkv_kernel, cache_seq, rope is not None),
        grid=(nt,),
        in_specs=in_specs,
        out_specs=out_specs,
        out_shape=out_shape,
        compiler_params=_cparams(("arbitrary",)),
        name="norm_qkv_rope",
    )(*args)


def _group_scores(q_ref, kv, key_parts, bias):
    nq = q_ref.shape[0]
    lo = lax.broadcasted_iota(jnp.int32, (nq, LANES), 1) < HEAD_DIM
    zero = jnp.zeros((), BF16)
    rows = []
    for b in range(KV_DIM // LANES):
        blk = q_ref[:, kv * KV_DIM + b * LANES:kv * KV_DIM + (b + 1) * LANES]
        rows += [jnp.where(lo, blk, zero), jnp.where(lo, zero, blk)]
    q4 = jnp.concatenate(rows, axis=0)
    parts = [_dot_nt(q4, keys) for keys in key_parts]
    if bias is not None:
        s0 = parts[0].reshape(Q_PER_KV, nq, -1) + bias[None]
        parts[0] = s0.reshape(Q_PER_KV * nq, -1)
    return parts


def _group_softmax(parts, sink_ref, kv):
    nq = parts[0].shape[0] // Q_PER_KV
    sink = jnp.concatenate(
        [jnp.full((nq, LANES), sink_ref[kv * Q_PER_KV + g], F32) for g in range(Q_PER_KV)], axis=0)
    blocks = [[s[:, j:j + LANES] for j in range(0, s.shape[1], LANES)] for s in parts]
    fold = None
    for b in sum(blocks, []):
        fold = b if fold is None else jnp.maximum(fold, b)
    m = jnp.maximum(jnp.max(fold, axis=-1, keepdims=True), sink)
    probs = [[jnp.exp(b - m) for b in bs] for bs in blocks]
    fold = None
    for p in sum(probs, []):
        fold = p if fold is None else fold + p
    den = jnp.sum(fold, axis=-1, keepdims=True) + jnp.exp(sink - m)
    return [jnp.concatenate(ps, axis=1).astype(BF16) for ps in probs], 1.0 / den


def _group_output(probs, inv_den, value_parts, o_ref, kv):
    nq = probs[0].shape[0] // Q_PER_KV
    r = _dot(probs[0], value_parts[0])
    for p, v in zip(probs[1:], value_parts[1:]):
        r = r + _dot(p, v)
    r = r * inv_den
    lo = lax.broadcasted_iota(jnp.int32, (nq, LANES), 1) < HEAD_DIM
    for b in range(KV_DIM // LANES):
        pair = jnp.where(lo, r[2 * b * nq:(2 * b + 1) * nq], r[(2 * b + 1) * nq:(2 * b + 2) * nq])
        o_ref[:, kv * KV_DIM + b * LANES:kv * KV_DIM + (b + 1) * LANES] = pair.astype(BF16)


def _attend(q_ref, sink_ref, o_ref, keys_of, values_of, bias):
    s_next = _group_scores(q_ref, 0, keys_of(0), bias)
    for kv in range(N_KV_HEADS):
        s = s_next
        if kv + 1 < N_KV_HEADS:
            s_next = _group_scores(q_ref, kv + 1, keys_of(kv + 1), bias)
        probs, inv_den = _group_softmax(s, sink_ref, kv)
        _group_output(probs, inv_den, values_of(kv), o_ref, kv)


def _ctx_attn_kernel(sink_ref, q_ref, k_ref, v_ref, o_ref):
    _attend(q_ref, sink_ref, o_ref, lambda kv: [k_ref[kv]], lambda kv: [v_ref[kv]], None)


def _ctx_attention(sink, q, krep, vrep, n_batch, seq):
    rep_spec = pl.BlockSpec((N_KV_HEADS, seq, LANES), lambda b: (0, b, 0))
    return pl.pallas_call(
        _ctx_attn_kernel,
        grid=(n_batch,),
        in_specs=[
            pl.BlockSpec(memory_space=pltpu.SMEM),
            pl.BlockSpec((seq, D_MODEL), lambda b: (b, 0)),
            rep_spec, rep_spec,
        ],
        out_specs=pl.BlockSpec((seq, D_MODEL), lambda b: (b, 0)),
        out_shape=jax.ShapeDtypeStruct((n_batch * seq, D_MODEL), BF16),
        compiler_params=_cparams(("arbitrary",)),
        name="context_attention",
    )(sink, q, krep, vrep)


def _window_start(n, seq):
    return jnp.clip((n - 1) * BLOCK, 0, seq - 3 * BLOCK)


def _band_bias():
    r = np.arange(BLOCK)[:, None]
    j = np.arange(3 * BLOCK)[None, :]
    out = [np.where(np.abs(j - d * BLOCK - r) <= BLOCK, 0.0, NEG_INF) for d in range(3)]
    return jnp.asarray(np.stack(out), F32)


def _lat_attn_kernel(seq, n_cast, sink_ref, q_ref, k_ref, v_ref, ck_ref, cv_ref, bias_ref, *refs):
    o_ref = refs[n_cast]
    for src, dst in zip(refs[:n_cast], refs[n_cast + 1:]):
        dst[...] = src[...].astype(BF16)
    win = 3 * BLOCK
    start = pl.multiple_of(_window_start(pl.program_id(1), seq), BLOCK)
    keys_of = lambda kv: [k_ref[kv, pl.ds(start, win), :], ck_ref[kv]]
    values_of = lambda kv: [v_ref[kv, pl.ds(start, win), :], cv_ref[kv]]
    _attend(q_ref, sink_ref, o_ref, keys_of, values_of, bias_ref[...])


def _lat_attention(sink, q, krep, vrep, ckrep, cvrep, n_batch, seq, cast_weights):
    nb = seq // BLOCK
    steps = n_batch * nb
    past = ckrep.shape[2]
    rep_spec = pl.BlockSpec((N_KV_HEADS, seq, LANES), lambda b, n: (0, b, 0))
    crep_spec = pl.BlockSpec((None, N_KV_HEADS, past, LANES), lambda b, n: (b, 0, 0, 0))
    flat = [w.reshape(-1, w.shape[-1]) for w in cast_weights]
    assert all(w.shape[0] % (steps * 2 * SUBLANES) == 0 for w in flat)
    slabs = [pl.BlockSpec((w.shape[0] // steps, w.shape[1]), lambda b, n: (b * nb + n, 0)) for w in flat]
    outs = pl.pallas_call(
        functools.partial(_lat_attn_kernel, seq, len(flat)),
        grid=(n_batch, nb),
        in_specs=[
            pl.BlockSpec(memory_space=pltpu.SMEM),
            pl.BlockSpec((BLOCK, D_MODEL), lambda b, n: (b * nb + n, 0)),
            rep_spec, rep_spec, crep_spec, crep_spec,
            pl.BlockSpec((None, BLOCK, 3 * BLOCK), lambda b, n: (n - _window_start(n, seq) // BLOCK, 0, 0)),
        ] + slabs,
        out_specs=[pl.BlockSpec((BLOCK, D_MODEL), lambda b, n: (b * nb + n, 0))] + slabs,
        out_shape=[jax.ShapeDtypeStruct((n_batch * seq, D_MODEL), BF16)]
        + [jax.ShapeDtypeStruct(w.shape, BF16) for w in flat],
        compiler_params=_cparams(("arbitrary", "arbitrary")),
        name="latent_attention",
    )(sink, q, krep, vrep, ckrep, cvrep, _band_bias(), *flat)
    return outs[0], [o.reshape(w.shape) for o, w in zip(outs[1:], cast_weights)]


def _gelu_tanh(x):
    c = math.sqrt(2.0 / math.pi)
    return x * (0.5 * (1.0 + jnp.tanh(c * (x + 0.044715 * (x * x * x)))))


def _post_kernel(is_attn, emit_next, final, vpt, *refs):
    rows_per_v = TOKEN_TILE // vpt
    refs = list(refs)
    x_ref, mix_ref, mod_ref, g2_ref, wa_ref = refs[:5]
    refs = refs[5:]
    wb_ref = None if is_attn else refs.pop(0)
    w1_ref, w2_ref = refs[:2]
    refs = refs[2:]
    modn_ref = gn_ref = fg_ref = hn_ref = None
    if emit_next:
        modn_ref, gn_ref = refs[:2]
        refs = refs[2:]
    if final:
        fg_ref = refs.pop(0)
    xo_ref = refs.pop(0)
    if emit_next:
        hn_ref = refs.pop(0)
    assert not refs

    sub = TOKEN_TILE // POST_SUBTILES
    assert rows_per_v % sub == 0
    n_chunk = sub // CHUNK
    starts = [k * sub for k in range(POST_SUBTILES)]

    def s5_slot(r0):
        return r0 // rows_per_v, (r0 % rows_per_v) // CHUNK

    def project(r0):
        if is_attn:
            return _dot(mix_ref[r0:r0 + sub, :], wa_ref[...])
        s, c0 = s5_slot(r0)
        y = jnp.concatenate(
            [mix_ref[g, c0:c0 + n_chunk, s].reshape(sub, LANES) for g in range(N_GROUP_BLOCKS)], axis=1)
        yg = _gelu_tanh(y).astype(BF16)
        return _dot(yg, wa_ref[...]) * jax.nn.sigmoid(_dot(yg, wb_ref[...]))

    def prologue(r0, mix):
        x1 = x_ref[r0:r0 + sub, :] + mod_ref[2:3, :] * mix
        h2 = _rms(x1) * g2_ref[...] * (1.0 + mod_ref[4:5, :]) + mod_ref[3:4, :]
        return x1, h2.astype(BF16)

    def mlp(h2):
        acc = None
        for c in range(D_FF // FF_TILE):
            a = jnp.maximum(_dot(h2, w1_ref[:, c * FF_TILE:(c + 1) * FF_TILE]), 0.0)
            t = _dot((a * a).astype(BF16), w2_ref[c * FF_TILE:(c + 1) * FF_TILE, :])
            acc = t if acc is None else acc + t
        return acc

    def epilogue(r0, x1, acc):
        x2 = x1 + mod_ref[5:6, :] * acc
        if emit_next:
            hn = _rms(x2) * gn_ref[...] * (1.0 + modn_ref[1:2, :]) + modn_ref[0:1, :]
            s, c0 = s5_slot(r0)
            for g in range(N_GROUP_BLOCKS):
                blk = hn[:, g * LANES:(g + 1) * LANES]
                hn_ref[g, c0:c0 + n_chunk, s] = blk.reshape(n_chunk, CHUNK, LANES)
        xo_ref[r0:r0 + sub, :] = _rms(x2) * fg_ref[...] if final else x2

    mixes = [project(r0) for r0 in starts]
    pro = [prologue(r0, mix) for r0, mix in zip(starts, mixes)]
    accs = [mlp(h2) for _, h2 in pro]
    for r0, (x1, _), acc in zip(starts, pro, accs):
        epilogue(r0, x1, acc)


def _post(x, mix, mod, mod_row, g2, w_a, w_b, w1, w2, layer, vpt, mod_next=None, g_next=None, final_g=None):
    is_attn = w_b is None
    emit_next = mod_next is not None
    final = final_g is not None
    ntok = x.shape[0]
    nt = ntok // TOKEN_TILE
    cpv = TOKEN_TILE // (vpt * CHUNK)
    n_virt = nt * vpt
    tile = pl.BlockSpec((TOKEN_TILE, D_MODEL), lambda i: (i, 0))
    row = pl.BlockSpec((1, D_MODEL), lambda i: (0, 0))
    modspec = pl.BlockSpec((None, N_MOD, D_MODEL), lambda i: (mod_row(i), 0, 0))
    resident = lambda shape: pl.BlockSpec(shape, lambda i: (0, 0), pipeline_mode=pl.Buffered(1))
    wsq = resident((D_MODEL, D_MODEL))
    gtile = pl.BlockSpec((N_GROUP_BLOCKS, cpv, vpt, CHUNK, LANES), lambda i: (0, 0, i, 0, 0))
    in_specs = [tile, tile if is_attn else gtile, modspec, row, wsq]
    args = [x, mix, mod, g2, w_a]
    if not is_attn:
        in_specs.append(wsq)
        args.append(w_b)
    in_specs += [pl.BlockSpec((None, D_MODEL, D_FF), lambda i: (layer, 0, 0), pipeline_mode=pl.Buffered(1)),
                 pl.BlockSpec((None, D_FF, D_MODEL), lambda i: (layer, 0, 0), pipeline_mode=pl.Buffered(1))]
    args += [w1, w2]
    if emit_next:
        in_specs += [modspec, row]
        args += [mod_next, g_next]
    if final:
        in_specs.append(row)
        args.append(final_g)
    out_specs = [tile]
    out_shape = [jax.ShapeDtypeStruct((ntok, D_MODEL), F32)]
    if emit_next:
        out_specs.append(gtile)
        out_shape.append(jax.ShapeDtypeStruct((N_GROUP_BLOCKS, cpv, n_virt, CHUNK, LANES), F32))
    return pl.pallas_call(
        functools.partial(_post_kernel, is_attn, emit_next, final, vpt),
        grid=(nt,),
        in_specs=in_specs,
        out_specs=out_specs,
        out_shape=out_shape,
        compiler_params=_cparams(("arbitrary",)),
        name="attn_proj_mlp" if is_attn else "glu_mlp_final",
    )(*args)


def _swap(x):
    return pltpu.roll(x, LANES // 2, 1)


def _cmul(z, w_r, w_i):
    return z * w_r + _swap(z) * w_i


def _multiplier(z, lo):
    zs = _swap(z)
    return jnp.where(lo, z, zs), jnp.where(lo, -zs, z)


def _rep_rows(x):
    return jnp.concatenate(
        [jnp.broadcast_to(x[g:g + 1, :], (GROUP_CH, LANES)) for g in range(GROUPS_PER_BLOCK)], axis=0)


def _s5_kernel(n_virt, n_seg, h_ref, lamr_ref, lami_ref, ldt_ref, bt_ref, cp_ref, dsk_ref, s0_ref, *refs):
    if n_seg == 1:
        y_ref, sfin_ref = refs[:2]
        refs = refs[2:]
    else:
        y_ref, sfin_ref = refs[0], None
        refs = refs[1:]
    f_scr, e_scr, k_scr, sf_scr, sb_scr, swf_scr, swb_scr = refs
    ntok = h_ref.shape[0]
    nc = ntok // CHUNK
    cpv = nc // n_virt
    gpb = GROUPS_PER_BLOCK

    lo8 = lax.broadcasted_iota(jnp.int32, (gpb, LANES), 1) < STATE_DIM
    lo = lax.broadcasted_iota(jnp.int32, (LANES, LANES), 1) < STATE_DIM
    conj = jnp.where(lo, 1.0, -1.0)
    row_g = lax.broadcasted_iota(jnp.int32, (LANES, STATE_LANES), 0) // GROUP_CH
    col_g = lax.broadcasted_iota(jnp.int32, (LANES, STATE_LANES), 1) // LANES
    diag_wide = row_g == col_g
    diag = (lax.broadcasted_iota(jnp.int32, (LANES, LANES), 0) // GROUP_CH) == (
        lax.broadcasted_iota(jnp.int32, (LANES, LANES), 1) // GROUP_CH)

    def expand(w):
        return jnp.where(diag_wide, jnp.concatenate([w] * gpb, axis=1), jnp.zeros((), BF16))

    decay = []
    lag = []
    for d in range(2):
        lam_r = lamr_ref[d]
        lam_i = lami_ref[d]
        dt = jnp.exp(ldt_ref[d])
        mag = jnp.exp(lam_r * dt)
        ang = lam_i * dt
        a_r = mag * jnp.cos(ang)
        a_im = mag * jnp.sin(ang)
        a_i = jnp.where(lo8, -a_im, a_im)
        den = lam_r * lam_r + lam_i * lam_i
        num = jnp.where(lo8, a_r - 1.0, a_im)
        f = _cmul(num, lam_r / den, jnp.where(lo8, lam_i, -lam_i) / den)
        pw = [jnp.where(lo8, 1.0, 0.0)]
        for _ in range(CHUNK):
            pw.append(_cmul(pw[-1], a_r, a_i))
        decay.append(pw[CHUNK])
        pw = [_rep_rows(p) for p in pw]
        f_r, f_i = _multiplier(_rep_rows(f), lo)
        bb_r, bb_i = _multiplier(_cmul(bt_ref[d], f_r, f_i), lo)
        c_r, c_i = _multiplier(cp_ref[d], lo)
        cm = (cp_ref[d] * conj).astype(BF16)
        fpow = [_cmul(p, bb_r, bb_i).astype(BF16) for p in pw[:CHUNK]]
        for j in range(CHUNK):
            e = (CHUNK - 1 - j) if d == 0 else j
            f_scr[d, j * LANES:(j + 1) * LANES, :] = expand(fpow[e])
        for t in range(CHUNK):
            e = (t + 1) if d == 0 else (CHUNK - t)
            w = _cmul(pw[e], c_r, c_i) * conj
            e_scr[d, t * LANES:(t + 1) * LANES, :] = expand(w.astype(BF16))
        lag.append([jnp.where(diag, _dot_nt(fp, cm), 0.0) for fp in fpow])

    for j in range(CHUNK):
        for t in range(CHUNK):
            k = t - j
            tile = lag[0][k] if k > 0 else (lag[1][-k] if k < 0 else lag[0][0] + lag[1][0])
            k_scr[j * LANES:(j + 1) * LANES, t * LANES:(t + 1) * LANES] = tile.astype(BF16)

    xcat = jnp.concatenate(
        [h_ref[pl.ds(j, nc, stride=CHUNK), :].astype(BF16) for j in range(CHUNK)], axis=1)

    for d, scr, sw_scr in ((0, sf_scr, swf_scr), (1, sb_scr, swb_scr)):
        loc_all = _dot(xcat, f_scr[d])
        for k in range(gpb):
            loc = loc_all[:, k * LANES:(k + 1) * LANES]
            scr[k] = loc
            sw_scr[k] = _swap(loc)
    y_within = _dot(xcat, k_scr[...])

    sgn8 = jnp.where(lo8, -1.0, 1.0)

    def dup(z):
        zs = _swap(z)
        return jnp.where(lo8, z, zs), jnp.where(lo8, zs, z)

    def scan(scr, sw_scr, d, reverse):
        a_re, a_im = dup(decay[d])
        a_sg = a_im * sgn8
        a_r = [a_re[k:k + 1, :] for k in range(gpb)]
        a_i = [a_sg[k:k + 1, :] for k in range(gpb)]

        def rows_of(i):
            c = (cpv - 1 - i) if reverse else i
            return pl.ds(c * n_virt, n_virt)

        def body(i, carry):
            st, sw = carry
            rows = rows_of(i)
            new_st, new_sw = [], []
            for k in range(gpb):
                loc = scr[k, rows, :]
                loc_sw = sw_scr[k, rows, :]
                scr[k, rows, :] = st[k]
                new_st.append(a_r[k] * st[k] + a_i[k] * sw[k] + loc)
                new_sw.append(a_r[k] * sw[k] - a_i[k] * st[k] + loc_sw)
            return tuple(new_st), tuple(new_sw)

        st0 = tuple(s0_ref[d, :, k * LANES:(k + 1) * LANES] for k in range(gpb))
        sw0 = tuple(_swap(s) for s in st0)
        carry = (st0, sw0)
        for i in range(cpv):
            carry = body(i, carry)
        fin = carry[0]
        if n_seg == 1:
            for k in range(gpb):
                sfin_ref[d, :, k * LANES:(k + 1) * LANES] = fin[k]
            return

        p = decay[d]
        for _ in range(cpv.bit_length() - 1):
            p_re, p_im = dup(p)
            p = _cmul(p, p_re, p_im * sgn8)
        v_re, v_im = dup(p)
        v_sg = v_im * sgn8
        seg = lax.broadcasted_iota(jnp.int32, (n_virt, LANES), 0) & (n_seg - 1)
        has_pred = seg != ((n_seg - 1) if reverse else 0)
        shift = (n_virt - 1) if reverse else 1
        cin = []
        for k in range(gpb):
            ck = jnp.zeros((n_virt, LANES), F32)
            for _ in range(n_seg - 1):
                nxt = fin[k] + ck * v_re[k:k + 1, :] + _swap(ck) * v_sg[k:k + 1, :]
                ck = jnp.where(has_pred, pltpu.roll(nxt, shift, 0), 0.0)
            cin.append(ck)
        cin_sw = [_swap(x) for x in cin]

        def fix(i, carry):
            q_re, q_im = carry
            rows = rows_of(i)
            q_sg = q_im * sgn8
            for k in range(gpb):
                scr[k, rows, :] += cin[k] * q_re[k:k + 1, :] + cin_sw[k] * q_sg[k:k + 1, :]
            return q_re * a_re - q_im * a_im, q_re * a_im + q_im * a_re

        q = (jnp.ones((gpb, LANES), F32), jnp.zeros((gpb, LANES), F32))
        for i in range(cpv):
            q = fix(i, q)

    scan(sf_scr, swf_scr, 0, False)
    scan(sb_scr, swb_scr, 1, True)

    s_f = jnp.concatenate([sf_scr[k].astype(BF16) for k in range(gpb)], axis=1)
    s_b = jnp.concatenate([sb_scr[k].astype(BF16) for k in range(gpb)], axis=1)
    yall = y_within + _dot_nt(s_f, e_scr[0]) + _dot_nt(s_b, e_scr[1])
    dsk = dsk_ref[...]
    for t in range(CHUNK):
        rows = pl.ds(t, nc, stride=CHUNK)
        y_ref[rows, :] = yall[:, t * LANES:(t + 1) * LANES] + h_ref[rows, :] * dsk


def _s5_params(lam_re, lam_im, log_dt, b_re, b_im, c_re, c_im):
    lamr = jnp.concatenate([lam_re, lam_re], axis=-1).astype(F32)
    lami = jnp.concatenate([lam_im, lam_im], axis=-1).astype(F32)
    ldt = jnp.broadcast_to(log_dt.astype(F32)[..., None], lamr.shape)
    bt = jnp.concatenate([b_re.transpose(0, 1, 3, 2), b_im.transpose(0, 1, 3, 2)], axis=-1)
    cp = jnp.concatenate([c_re, c_im], axis=-1)
    return (lamr, lami, ldt, bt.reshape(2, D_MODEL, LANES).astype(F32), cp.reshape(2, D_MODEL, LANES).astype(F32))


def _s5(h, params, d_skip, s0, n_seg):
    lamr, lami, ldt, bt, cp = params
    _, cpv, n_virt, _, _ = h.shape
    assert cpv & (cpv - 1) == 0 and n_virt % SUBLANES == 0 and n_seg & (n_seg - 1) == 0
    nc = cpv * n_virt
    ntok = nc * CHUNK
    hspec = pl.BlockSpec((None, ntok, LANES), lambda g: (g, 0, 0))
    kdim = CHUNK * LANES
    gspec = pl.BlockSpec((2, GROUPS_PER_BLOCK, LANES), lambda g: (0, g, 0))
    rspec = pl.BlockSpec((2, LANES, LANES), lambda g: (0, g, 0))
    sspec = pl.BlockSpec((None, 2, n_virt, STATE_LANES), lambda g: (g, 0, 0, 0))
    state_scr = pltpu.VMEM((GROUPS_PER_BLOCK, nc, LANES), F32)
    out_specs = [hspec]
    out_shape = [jax.ShapeDtypeStruct((N_GROUP_BLOCKS, ntok, LANES), F32)]
    if n_seg == 1:
        out_specs.append(sspec)
        out_shape.append(jax.ShapeDtypeStruct((N_GROUP_BLOCKS, 2, n_virt, STATE_LANES), F32))
    outs = pl.pallas_call(
        functools.partial(_s5_kernel, n_virt, n_seg),
        grid=(N_GROUP_BLOCKS,),
        in_specs=[
            hspec,
            gspec, gspec, gspec, rspec, rspec,
            pl.BlockSpec((1, LANES), lambda g: (0, g)),
            sspec,
        ],
        out_specs=out_specs,
        out_shape=out_shape,
        scratch_shapes=[
            pltpu.VMEM((2, kdim, STATE_LANES), BF16),
            pltpu.VMEM((2, kdim, STATE_LANES), BF16),
            pltpu.VMEM((kdim, kdim), BF16),
            state_scr, state_scr, state_scr, state_scr,
        ],
        compiler_params=_cparams(("arbitrary",)),
        name="s5_chunked_scan",
    )(h.reshape(N_GROUP_BLOCKS, ntok, LANES), lamr, lami, ldt, bt, cp, d_skip, s0)
    y = outs[0].reshape(h.shape)
    return (y, outs[1]) if n_seg == 1 else (y, None)


def _state_to_blocks(s):
    b = s.shape[0]
    s = s.reshape(b, 2, 2, N_GROUP_BLOCKS, GROUPS_PER_BLOCK, STATE_DIM)
    return s.transpose(3, 1, 0, 4, 2, 5).reshape(N_GROUP_BLOCKS, 2, b, STATE_LANES)


def _blocks_to_state(s):
    b = s.shape[2]
    s = s.reshape(N_GROUP_BLOCKS, 2, b, GROUPS_PER_BLOCK, 2, STATE_DIM)
    return s.transpose(2, 1, 4, 0, 3, 5).reshape(b, 2, 2, N_GROUPS, STATE_DIM)


def _rope_tables(n_tokens):
    pos = np.arange(n_tokens)
    n_freq = HEAD_DIM // 4
    freqs = ROPE_BASE ** (-np.arange(n_freq, dtype=np.float64) / n_freq)
    ang_r = (pos // GRID_W)[:, None] * freqs
    ang_c = (pos % GRID_W)[:, None] * freqs
    cos_h = np.concatenate([np.cos(ang_r), np.cos(ang_r), np.cos(ang_c), np.cos(ang_c)], axis=1)
    sin_h = np.concatenate([-np.sin(ang_r), np.sin(ang_r), -np.sin(ang_c), np.sin(ang_c)], axis=1)
    return jnp.asarray(np.tile(cos_h, (1, 2)), F32), jnp.asarray(np.tile(sin_h, (1, 2)), F32)


def kernel(x_prompt, x_sample, cache_k, cache_v, state_ssm, c, c_ctx, norm1_g, norm2_g, w_mod, b_mod,
           w_qkv, w_o, attn_sink, ssm_lam_re, ssm_lam_im, ssm_log_dt, ssm_b_re, ssm_b_im, ssm_c_re,
           ssm_c_im, ssm_d, glu_w_a, glu_w_b, mlp_w1, mlp_w2, final_norm_g):
    bp, lp, _ = x_prompt.shape
    bx, lx, _ = x_sample.shape
    assert lx % TOKEN_TILE == 0 and (bp * lp) % TOKEN_TILE == 0
    tiles_per_lat = lx // TOKEN_TILE

    xp = x_prompt.reshape(bp * lp, D_MODEL)
    xx = x_sample.reshape(bx * lx, D_MODEL)

    cvecs = jnp.zeros((8, D_MODEL), F32).at[0].set(c_ctx).at[1:1 + bx].set(c)
    mod = _modulation(cvecs, w_mod, b_mod)

    ctx_row = lambda i: 0
    lat_row = lambda i: 1 + i // tiles_per_lat

    assert lx % QKV_TILE == 0 and (bp * lp) % QKV_TILE == 0
    qkv_tiles_per_lat = lx // QKV_TILE
    rope = _rope_tables(lx) + (lambda i: i % qkv_tiles_per_lat,)
    wqkv = w_qkv[0].astype(BF16)
    g1 = norm1_g[0].reshape(1, D_MODEL)
    sink = attn_sink[0].astype(F32)
    qp, krp, vrp, kp, vp = _qkv(xp, mod[0], ctx_row, g1, wqkv, None, lp)
    qx, krx, vrx = _qkv(xx, mod[0], lambda i: 1 + i // qkv_tiles_per_lat, g1, wqkv, rope, 0)
    op = _ctx_attention(sink, qp, krp, vrp, bp, lp)
    rep = lambda t: jnp.tile(t[:, 0].transpose(0, 2, 1, 3), (1, 1, 1, LANES // HEAD_DIM)).astype(BF16)
    ox, (w1, w2, wo, wa, wb) = _lat_attention(
        sink, qx, krx, vrx, rep(cache_k), rep(cache_v), bx, lx,
        [mlp_w1, mlp_w2, w_o[0], glu_w_a[0], glu_w_b[0]])
    g2 = norm2_g.reshape(-1, 1, D_MODEL)
    gn = norm1_g[1].reshape(1, D_MODEL)
    vpt_p = TOKEN_TILE // lp
    vpt_x = 1
    n_seg_x = tiles_per_lat
    xp, hp = _post(xp, op, mod[0], ctx_row, g2[0], wo, None, w1, w2, 0, vpt_p, mod_next=mod[1], g_next=gn)
    xx, hx = _post(xx, ox, mod[0], lat_row, g2[0], wo, None, w1, w2, 0, vpt_x, mod_next=mod[1], g_next=gn)

    params = _s5_params(ssm_lam_re[0], ssm_lam_im[0], ssm_log_dt[0], ssm_b_re[0], ssm_b_im[0],
                        ssm_c_re[0], ssm_c_im[0])
    dsk = ssm_d[0].astype(F32).reshape(1, D_MODEL)
    s0p = jnp.zeros((N_GROUP_BLOCKS, 2, bp, STATE_LANES), F32)
    sx = _state_to_blocks(state_ssm[:, 0].astype(F32))
    s0x = jnp.zeros((N_GROUP_BLOCKS, 2, bx, n_seg_x, STATE_LANES), F32)
    s0x = s0x.at[:, 0, :, 0].set(sx[:, 0]).at[:, 1, :, n_seg_x - 1].set(sx[:, 1])
    s0x = s0x.reshape(N_GROUP_BLOCKS, 2, bx * n_seg_x, STATE_LANES)
    yp, sfin = _s5(hp, params, dsk, s0p, 1)
    yx, _ = _s5(hx, params, dsk, s0x, n_seg_x)
    new_state = _blocks_to_state(sfin)[:, None]

    fg = final_norm_g.reshape(1, D_MODEL)
    (yp_out,) = _post(xp, yp, mod[1], ctx_row, g2[1], wa, wb, w1, w2, 1, vpt_p, final_g=fg)
    (yx_out,) = _post(xx, yx, mod[1], lat_row, g2[1], wa, wb, w1, w2, 1, vpt_x, final_g=fg)

    to_cache = lambda t: t.reshape(bp, N_KV_HEADS, HEAD_DIM, lp).transpose(0, 3, 1, 2)[:, None]
    new_k = to_cache(kp)
    new_v = to_cache(vp)
    return (yp_out.reshape(bp, lp, D_MODEL), yx_out.reshape(bx, lx, D_MODEL), new_k, new_v, new_state)
```

```python
import functools
import math

import numpy as np
import jax
import jax.numpy as jnp
from jax import lax
from jax.experimental import pallas as pl
from jax.experimental.pallas import tpu as pltpu

F32 = jnp.float32
BF16 = jnp.bfloat16

D_MODEL = 1024
N_HEADS = 16
N_KV_HEADS = 4
HEAD_DIM = 64
Q_PER_KV = N_HEADS // N_KV_HEADS
KV_DIM = N_KV_HEADS * HEAD_DIM
QKV_DIM = D_MODEL + 2 * KV_DIM
BLOCK = 128
GRID_W = 64
ROPE_BASE = 10000.0
ROT_HALF = HEAD_DIM // 4
ATTN_SCALE = HEAD_DIM ** -0.5
N_GROUPS = 64
GROUP_CH = 16
STATE_DIM = 64
D_FF = 4 * D_MODEL
N_MOD = 6
RMS_EPS = 1e-6
NEG_INF = -1e30

LANES = 128
SUBLANES = 8
GROUPS_PER_BLOCK = LANES // GROUP_CH
N_GROUP_BLOCKS = N_GROUPS // GROUPS_PER_BLOCK
STATE_LANES = GROUPS_PER_BLOCK * 2 * STATE_DIM
CHUNK = SUBLANES
TOKEN_TILE = 512
FF_TILE = 1024
POST_SUBTILES = 2
QKV_TILE = 1024
QKV_SUBTILES = 4
VMEM_LIMIT = 56 * 1024 * 1024


def _cparams(semantics):
    return pltpu.CompilerParams(dimension_semantics=semantics, vmem_limit_bytes=VMEM_LIMIT)


def _rms(x):
    return x * lax.rsqrt(jnp.mean(x * x, axis=-1, keepdims=True) + RMS_EPS)


def _dot(a, b):
    return jnp.dot(a, b, preferred_element_type=F32)


def _dot_nt(a, b):
    return lax.dot_general(a, b, (((1,), (1,)), ((), ())), preferred_element_type=F32)


def _mod_kernel(cv_ref, w_ref, b_ref, o_ref):
    cv = cv_ref[...]
    s = (cv * jax.nn.sigmoid(cv)).astype(BF16)
    o_ref[0] = _dot(s, w_ref[0].astype(BF16)) + b_ref[0]


def _modulation(cvecs, w_mod, b_mod):
    depth = w_mod.shape[0]
    width = 2 * D_MODEL
    out = pl.pallas_call(
        _mod_kernel,
        grid=(depth, N_MOD * D_MODEL // width),
        in_specs=[
            pl.BlockSpec((8, D_MODEL), lambda l, j: (0, 0)),
            pl.BlockSpec((1, D_MODEL, width), lambda l, j: (l, 0, j)),
            pl.BlockSpec((1, 1, width), lambda l, j: (l, 0, j)),
        ],
        out_specs=pl.BlockSpec((1, 8, width), lambda l, j: (l, 0, j)),
        out_shape=jax.ShapeDtypeStruct((depth, 8, N_MOD * D_MODEL), F32),
        compiler_params=_cparams(("arbitrary", "arbitrary")),
        name="modulation",
    )(cvecs, w_mod, b_mod.reshape(depth, 1, N_MOD * D_MODEL))
    return out.reshape(depth, 8, N_MOD, D_MODEL)


def _head_pair(blk, odd):
    lo = lax.broadcasted_iota(jnp.int32, blk.shape, 1) < HEAD_DIM
    other = pltpu.roll(blk, HEAD_DIM, 1)
    return (jnp.where(lo, other, blk) if odd else jnp.where(lo, blk, other)).astype(BF16)


def _qkv_kernel(cache_seq, rope, x_ref, mod_ref, g_ref, w_ref, *refs):
    refs = list(refs)
    cos_ref, sin_ref = (refs.pop(0), refs.pop(0)) if rope else (None, None)
    q_ref, krep_ref, vrep_ref = refs[:3]
    kv_refs = refs[3:]
    sub = QKV_TILE // QKV_SUBTILES
    starts = [k * sub for k in range(QKV_SUBTILES)]

    def project(r0):
        h = _rms(x_ref[r0:r0 + sub, :]) * g_ref[...] * (1.0 + mod_ref[1:2, :]) + mod_ref[0:1, :]
        return _dot(h.astype(BF16), w_ref[...])

    def emit_cache(ref, blk, r0, c0):
        for s in range(sub // cache_seq):
            ref[r0 // cache_seq + s, c0:c0 + LANES, :] = blk[s * cache_seq:(s + 1) * cache_seq, :].T

    def finish(r0, qkv):
        rows = slice(r0, r0 + sub)
        if rope:
            cos = cos_ref[rows, :]
            sin = sin_ref[rows, :]
            lane = lax.broadcasted_iota(jnp.int32, cos.shape, 1)
            first = (lane & (2 * ROT_HALF - 1)) < ROT_HALF
        for blk in range((D_MODEL + KV_DIM) // LANES):
            r = qkv[:, blk * LANES:(blk + 1) * LANES]
            if rope:
                partner = jnp.where(first, pltpu.roll(r, LANES - ROT_HALF, 1), pltpu.roll(r, ROT_HALF, 1))
                r = r * cos + partner * sin
            if blk < D_MODEL // LANES:
                q_ref[rows, blk * LANES:(blk + 1) * LANES] = (r * ATTN_SCALE).astype(BF16)
            else:
                c0 = blk * LANES - D_MODEL
                if cache_seq:
                    emit_cache(kv_refs[0], r, r0, c0)
                for half in range(2):
                    krep_ref[c0 // HEAD_DIM + half, rows, :] = _head_pair(r, half)
        v = qkv[:, D_MODEL + KV_DIM:]
        for c0 in range(0, KV_DIM, LANES):
            blk = v[:, c0:c0 + LANES]
            if cache_seq:
                emit_cache(kv_refs[1], blk, r0, c0)
            for half in range(2):
                vrep_ref[c0 // HEAD_DIM + half, rows, :] = _head_pair(blk, half)

    for r0, qkv in zip(starts, [project(r0) for r0 in starts]):
        finish(r0, qkv)


def _qkv(x, mod, mod_row, g, w_qkv, rope, cache_seq):
    ntok = x.shape[0]
    nt = ntok // QKV_TILE
    emit_kv = cache_seq > 0
    assert not emit_kv or (QKV_TILE // QKV_SUBTILES) % cache_seq == 0
    rep_spec = pl.BlockSpec((N_KV_HEADS, QKV_TILE, LANES), lambda i: (0, i, 0))
    rep_shape = jax.ShapeDtypeStruct((N_KV_HEADS, ntok, LANES), BF16)
    in_specs = [
        pl.BlockSpec((QKV_TILE, D_MODEL), lambda i: (i, 0)),
        pl.BlockSpec((None, N_MOD, D_MODEL), lambda i: (mod_row(i), 0, 0)),
        pl.BlockSpec((1, D_MODEL), lambda i: (0, 0)),
        pl.BlockSpec((D_MODEL, QKV_DIM), lambda i: (0, 0)),
    ]
    args = [x, mod, g, w_qkv]
    if rope is not None:
        cos_t, sin_t, rope_blk = rope
        in_specs += [pl.BlockSpec((QKV_TILE, LANES), lambda i: (rope_blk(i), 0))] * 2
        args += [cos_t, sin_t]
    out_specs = [pl.BlockSpec((QKV_TILE, D_MODEL), lambda i: (i, 0)), rep_spec, rep_spec]
    out_shape = [jax.ShapeDtypeStruct((ntok, D_MODEL), BF16), rep_shape, rep_shape]
    if emit_kv:
        spt = QKV_TILE // cache_seq
        out_specs += [pl.BlockSpec((spt, KV_DIM, cache_seq), lambda i: (i, 0, 0))] * 2
        out_shape += [jax.ShapeDtypeStruct((ntok // cache_seq, KV_DIM, cache_seq), F32)] * 2
    return pl.pallas_call(
        functools.partial(_qkv_kernel, cache_seq, rope is not None),
        grid=(nt,),
        in_specs=in_specs,
        out_specs=out_specs,
        out_shape=out_shape,
        compiler_params=_cparams(("arbitrary",)),
        name="norm_qkv_rope",
    )(*args)


def _group_scores(q_ref, kv, key_parts, bias):
    nq = q_ref.shape[0]
    lo = lax.broadcasted_iota(jnp.int32, (nq, LANES), 1) < HEAD_DIM
    zero = jnp.zeros((), BF16)
    rows = []
    for b in range(KV_DIM // LANES):
        blk = q_ref[:, kv * KV_DIM + b * LANES:kv * KV_DIM + (b + 1) * LANES]
        rows += [jnp.where(lo, blk, zero), jnp.where(lo, zero, blk)]
    q4 = jnp.concatenate(rows, axis=0)
    parts = [_dot_nt(q4, keys) for keys in key_parts]
    if bias is not None:
        s0 = parts[0].reshape(Q_PER_KV, nq, -1) + bias[None]
        parts[0] = s0.reshape(Q_PER_KV * nq, -1)
    return parts


def _group_softmax(parts, sink_ref, kv):
    nq = parts[0].shape[0] // Q_PER_KV
    sink = jnp.concatenate(
        [jnp.full((nq, LANES), sink_ref[kv * Q_PER_KV + g], F32) for g in range(Q_PER_KV)], axis=0)
    blocks = [[s[:, j:j + LANES] for j in range(0, s.shape[1], LANES)] for s in parts]
    fold = None
    for b in sum(blocks, []):
        fold = b if fold is None else jnp.maximum(fold, b)
    m = jnp.maximum(jnp.max(fold, axis=-1, keepdims=True), sink)
    probs = [[jnp.exp(b - m) for b in bs] for bs in blocks]
    fold = None
    for p in sum(probs, []):
        fold = p if fold is None else fold + p
    den = jnp.sum(fold, axis=-1, keepdims=True) + jnp.exp(sink - m)
    return [jnp.concatenate(ps, axis=1).astype(BF16) for ps in probs], 1.0 / den


def _group_output(probs, inv_den, value_parts, o_ref, kv):
    nq = probs[0].shape[0] // Q_PER_KV
    r = _dot(probs[0], value_parts[0])
    for p, v in zip(probs[1:], value_parts[1:]):
        r = r + _dot(p, v)
    r = r * inv_den
    lo = lax.broadcasted_iota(jnp.int32, (nq, LANES), 1) < HEAD_DIM
    for b in range(KV_DIM // LANES):
        pair = jnp.where(lo, r[2 * b * nq:(2 * b + 1) * nq], r[(2 * b + 1) * nq:(2 * b + 2) * nq])
        o_ref[:, kv * KV_DIM + b * LANES:kv * KV_DIM + (b + 1) * LANES] = pair.astype(BF16)


def _attend(q_ref, sink_ref, o_ref, keys_of, values_of, bias):
    s_next = _group_scores(q_ref, 0, keys_of(0), bias)
    for kv in range(N_KV_HEADS):
        s = s_next
        if kv + 1 < N_KV_HEADS:
            s_next = _group_scores(q_ref, kv + 1, keys_of(kv + 1), bias)
        probs, inv_den = _group_softmax(s, sink_ref, kv)
        _group_output(probs, inv_den, values_of(kv), o_ref, kv)


def _ctx_attn_kernel(sink_ref, q_ref, k_ref, v_ref, o_ref):
    _attend(q_ref, sink_ref, o_ref, lambda kv: [k_ref[kv]], lambda kv: [v_ref[kv]], None)


def _ctx_attention(sink, q, krep, vrep, n_batch, seq):
    rep_spec = pl.BlockSpec((N_KV_HEADS, seq, LANES), lambda b: (0, b, 0))
    return pl.pallas_call(
        _ctx_attn_kernel,
        grid=(n_batch,),
        in_specs=[
            pl.BlockSpec(memory_space=pltpu.SMEM),
            pl.BlockSpec((seq, D_MODEL), lambda b: (b, 0)),
            rep_spec, rep_spec,
        ],
        out_specs=pl.BlockSpec((seq, D_MODEL), lambda b: (b, 0)),
        out_shape=jax.ShapeDtypeStruct((n_batch * seq, D_MODEL), BF16),
        compiler_params=_cparams(("arbitrary",)),
        name="context_attention",
    )(sink, q, krep, vrep)


def _window_start(n, seq):
    return jnp.clip((n - 1) * BLOCK, 0, seq - 3 * BLOCK)


def _band_bias():
    r = np.arange(BLOCK)[:, None]
    j = np.arange(3 * BLOCK)[None, :]
    out = [np.where(np.abs(j - d * BLOCK - r) <= BLOCK, 0.0, NEG_INF) for d in range(3)]
    return jnp.asarray(np.stack(out), F32)


def _lat_attn_kernel(seq, n_cast, sink_ref, q_ref, k_ref, v_ref, ck_ref, cv_ref, bias_ref, *refs):
    o_ref = refs[n_cast]
    for src, dst in zip(refs[:n_cast], refs[n_cast + 1:]):
        dst[...] = src[...].astype(BF16)
    win = 3 * BLOCK
    start = pl.multiple_of(_window_start(pl.program_id(1), seq), BLOCK)
    keys_of = lambda kv: [k_ref[kv, pl.ds(start, win), :], ck_ref[kv]]
    values_of = lambda kv: [v_ref[kv, pl.ds(start, win), :], cv_ref[kv]]
    _attend(q_ref, sink_ref, o_ref, keys_of, values_of, bias_ref[...])


def _lat_attention(sink, q, krep, vrep, ckrep, cvrep, n_batch, seq, cast_weights):
    nb = seq // BLOCK
    steps = n_batch * nb
    past = ckrep.shape[2]
    rep_spec = pl.BlockSpec((N_KV_HEADS, seq, LANES), lambda b, n: (0, b, 0))
    crep_spec = pl.BlockSpec((None, N_KV_HEADS, past, LANES), lambda b, n: (b, 0, 0, 0))
    flat = [w.reshape(-1, w.shape[-1]) for w in cast_weights]
    assert all(w.shape[0] % (steps * 2 * SUBLANES) == 0 for w in flat)
    slabs = [pl.BlockSpec((w.shape[0] // steps, w.shape[1]), lambda b, n: (b * nb + n, 0)) for w in flat]
    outs = pl.pallas_call(
        functools.partial(_lat_attn_kernel, seq, len(flat)),
        grid=(n_batch, nb),
        in_specs=[
            pl.BlockSpec(memory_space=pltpu.SMEM),
            pl.BlockSpec((BLOCK, D_MODEL), lambda b, n: (b * nb + n, 0)),
            rep_spec, rep_spec, crep_spec, crep_spec,
            pl.BlockSpec((None, BLOCK, 3 * BLOCK), lambda b, n: (n - _window_start(n, seq) // BLOCK, 0, 0)),
        ] + slabs,
        out_specs=[pl.BlockSpec((BLOCK, D_MODEL), lambda b, n: (b * nb + n, 0))] + slabs,
        out_shape=[jax.ShapeDtypeStruct((n_batch * seq, D_MODEL), BF16)]
        + [jax.ShapeDtypeStruct(w.shape, BF16) for w in flat],
        compiler_params=_cparams(("arbitrary", "arbitrary")),
        name="latent_attention",
    )(sink, q, krep, vrep, ckrep, cvrep, _band_bias(), *flat)
    return outs[0], [o.reshape(w.shape) for o, w in zip(outs[1:], cast_weights)]


def _gelu_tanh(x):
    c = math.sqrt(2.0 / math.pi)
    return x * (0.5 * (1.0 + jnp.tanh(c * (x + 0.044715 * (x * x * x)))))


def _post_kernel(is_attn, emit_next, final, vpt, *refs):
    rows_per_v = TOKEN_TILE // vpt
    refs = list(refs)
    x_ref, mix_ref, mod_ref, g2_ref, wa_ref = refs[:5]
    refs = refs[5:]
    wb_ref = None if is_attn else refs.pop(0)
    w1_ref, w2_ref = refs[:2]
    refs = refs[2:]
    modn_ref = gn_ref = fg_ref = hn_ref = None
    if emit_next:
        modn_ref, gn_ref = refs[:2]
        refs = refs[2:]
    if final:
        fg_ref = refs.pop(0)
    xo_ref = refs.pop(0)
    if emit_next:
        hn_ref = refs.pop(0)
    assert not refs

    sub = TOKEN_TILE // POST_SUBTILES
    assert rows_per_v % sub == 0
    n_chunk = sub // CHUNK
    starts = [k * sub for k in range(POST_SUBTILES)]

    def s5_slot(r0):
        return r0 // rows_per_v, (r0 % rows_per_v) // CHUNK

    def project(r0):
        if is_attn:
            return _dot(mix_ref[r0:r0 + sub, :], wa_ref[...])
        s, c0 = s5_slot(r0)
        y = jnp.concatenate(
            [mix_ref[g, c0:c0 + n_chunk, s].reshape(sub, LANES) for g in range(N_GROUP_BLOCKS)], axis=1)
        yg = _gelu_tanh(y).astype(BF16)
        return _dot(yg, wa_ref[...]) * jax.nn.sigmoid(_dot(yg, wb_ref[...]))

    def prologue(r0, mix):
        x1 = x_ref[r0:r0 + sub, :] + mod_ref[2:3, :] * mix
        h2 = _rms(x1) * g2_ref[...] * (1.0 + mod_ref[4:5, :]) + mod_ref[3:4, :]
        return x1, h2.astype(BF16)

    def mlp(h2):
        acc = None
        for c in range(D_FF // FF_TILE):
            a = jnp.maximum(_dot(h2, w1_ref[:, c * FF_TILE:(c + 1) * FF_TILE]), 0.0)
            t = _dot((a * a).astype(BF16), w2_ref[c * FF_TILE:(c + 1) * FF_TILE, :])
            acc = t if acc is None else acc + t
        return acc

    def epilogue(r0, x1, acc):
        x2 = x1 + mod_ref[5:6, :] * acc
        if emit_next:
            hn = _rms(x2) * gn_ref[...] * (1.0 + modn_ref[1:2, :]) + modn_ref[0:1, :]
            s, c0 = s5_slot(r0)
            for g in range(N_GROUP_BLOCKS):
                blk = hn[:, g * LANES:(g + 1) * LANES]
                hn_ref[g, c0:c0 + n_chunk, s] = blk.reshape(n_chunk, CHUNK, LANES)
        xo_ref[r0:r0 + sub, :] = _rms(x2) * fg_ref[...] if final else x2

    mixes = [project(r0) for r0 in starts]
    pro = [prologue(r0, mix) for r0, mix in zip(starts, mixes)]
    accs = [mlp(h2) for _, h2 in pro]
    for r0, (x1, _), acc in zip(starts, pro, accs):
        epilogue(r0, x1, acc)


def _post(x, mix, mod, mod_row, g2, w_a, w_b, w1, w2, layer, vpt, mod_next=None, g_next=None, final_g=None):
    is_attn = w_b is None
    emit_next = mod_next is not None
    final = final_g is not None
    ntok = x.shape[0]
    nt = ntok // TOKEN_TILE
    cpv = TOKEN_TILE // (vpt * CHUNK)
    n_virt = nt * vpt
    tile = pl.BlockSpec((TOKEN_TILE, D_MODEL), lambda i: (i, 0))
    row = pl.BlockSpec((1, D_MODEL), lambda i: (0, 0))
    modspec = pl.BlockSpec((None, N_MOD, D_MODEL), lambda i: (mod_row(i), 0, 0))
    resident = lambda shape: pl.BlockSpec(shape, lambda i: (0, 0), pipeline_mode=pl.Buffered(1))
    wsq = resident((D_MODEL, D_MODEL))
    gtile = pl.BlockSpec((N_GROUP_BLOCKS, cpv, vpt, CHUNK, LANES), lambda i: (0, 0, i, 0, 0))
    in_specs = [tile, tile if is_attn else gtile, modspec, row, wsq]
    args = [x, mix, mod, g2, w_a]
    if not is_attn:
        in_specs.append(wsq)
        args.append(w_b)
    in_specs += [pl.BlockSpec((None, D_MODEL, D_FF), lambda i: (layer, 0, 0), pipeline_mode=pl.Buffered(1)),
                 pl.BlockSpec((None, D_FF, D_MODEL), lambda i: (layer, 0, 0), pipeline_mode=pl.Buffered(1))]
    args += [w1, w2]
    if emit_next:
        in_specs += [modspec, row]
        args += [mod_next, g_next]
    if final:
        in_specs.append(row)
        args.append(final_g)
    out_specs = [tile]
    out_shape = [jax.ShapeDtypeStruct((ntok, D_MODEL), F32)]
    if emit_next:
        out_specs.append(gtile)
        out_shape.append(jax.ShapeDtypeStruct((N_GROUP_BLOCKS, cpv, n_virt, CHUNK, LANES), F32))
    return pl.pallas_call(
        functools.partial(_post_kernel, is_attn, emit_next, final, vpt),
        grid=(nt,),
        in_specs=in_specs,
        out_specs=out_specs,
        out_shape=out_shape,
        compiler_params=_cparams(("arbitrary",)),
        name="attn_proj_mlp" if is_attn else "glu_mlp_final",
    )(*args)


def _swap(x):
    return pltpu.roll(x, LANES // 2, 1)


def _cmul(z, w_r, w_i):
    return z * w_r + _swap(z) * w_i


def _multiplier(z, lo):
    zs = _swap(z)
    return jnp.where(lo, z, zs), jnp.where(lo, -zs, z)


def _rep_rows(x):
    return jnp.concatenate(
        [jnp.broadcast_to(x[g:g + 1, :], (GROUP_CH, LANES)) for g in range(GROUPS_PER_BLOCK)], axis=0)


def _s5_kernel(n_virt, n_seg, h_ref, lamr_ref, lami_ref, ldt_ref, bt_ref, cp_ref, dsk_ref, s0_ref, *refs):
    if n_seg == 1:
        y_ref, sfin_ref = refs[:2]
        refs = refs[2:]
    else:
        y_ref, sfin_ref = refs[0], None
        refs = refs[1:]
    f_scr, e_scr, k_scr, sf_scr, sb_scr, swf_scr, swb_scr = refs
    ntok = h_ref.shape[0]
    nc = ntok // CHUNK
    cpv = nc // n_virt
    gpb = GROUPS_PER_BLOCK

    lo8 = lax.broadcasted_iota(jnp.int32, (gpb, LANES), 1) < STATE_DIM
    lo = lax.broadcasted_iota(jnp.int32, (LANES, LANES), 1) < STATE_DIM
    conj = jnp.where(lo, 1.0, -1.0)
    row_g = lax.broadcasted_iota(jnp.int32, (LANES, STATE_LANES), 0) // GROUP_CH
    col_g = lax.broadcasted_iota(jnp.int32, (LANES, STATE_LANES), 1) // LANES
    diag_wide = row_g == col_g
    diag = (lax.broadcasted_iota(jnp.int32, (LANES, LANES), 0) // GROUP_CH) == (
        lax.broadcasted_iota(jnp.int32, (LANES, LANES), 1) // GROUP_CH)

    def expand(w):
        return jnp.where(diag_wide, jnp.concatenate([w] * gpb, axis=1), jnp.zeros((), BF16))

    decay = []
    lag = []
    for d in range(2):
        lam_r = lamr_ref[d]
        lam_i = lami_ref[d]
        dt = jnp.exp(ldt_ref[d])
        mag = jnp.exp(lam_r * dt)
        ang = lam_i * dt
        a_r = mag * jnp.cos(ang)
        a_im = mag * jnp.sin(ang)
        a_i = jnp.where(lo8, -a_im, a_im)
        den = lam_r * lam_r + lam_i * lam_i
        num = jnp.where(lo8, a_r - 1.0, a_im)
        f = _cmul(num, lam_r / den, jnp.where(lo8, lam_i, -lam_i) / den)
        pw = [jnp.where(lo8, 1.0, 0.0)]
        for _ in range(CHUNK):
            pw.append(_cmul(pw[-1], a_r, a_i))
        decay.append(pw[CHUNK])
        pw = [_rep_rows(p) for p in pw]
        f_r, f_i = _multiplier(_rep_rows(f), lo)
        bb_r, bb_i = _multiplier(_cmul(bt_ref[d], f_r, f_i), lo)
        c_r, c_i = _multiplier(cp_ref[d], lo)
        cm = (cp_ref[d] * conj).astype(BF16)
        fpow = [_cmul(p, bb_r, bb_i).astype(BF16) for p in pw[:CHUNK]]
        for j in range(CHUNK):
            e = (CHUNK - 1 - j) if d == 0 else j
            f_scr[d, j * LANES:(j + 1) * LANES, :] = expand(fpow[e])
        for t in range(CHUNK):
            e = (t + 1) if d == 0 else (CHUNK - t)
            w = _cmul(pw[e], c_r, c_i) * conj
            e_scr[d, t * LANES:(t + 1) * LANES, :] = expand(w.astype(BF16))
        lag_all = _dot_nt(jnp.concatenate(fpow, axis=0), cm)
        lag.append([jnp.where(diag, lag_all[k * LANES:(k + 1) * LANES], 0.0) for k in range(CHUNK)])

    for j in range(CHUNK):
        for t in range(CHUNK):
            k = t - j
            tile = lag[0][k] if k > 0 else (lag[1][-k] if k < 0 else lag[0][0] + lag[1][0])
            k_scr[j * LANES:(j + 1) * LANES, t * LANES:(t + 1) * LANES] = tile.astype(BF16)

    xcat = jnp.concatenate(
        [h_ref[pl.ds(j, nc, stride=CHUNK), :].astype(BF16) for j in range(CHUNK)], axis=1)

    for d, scr, sw_scr in ((0, sf_scr, swf_scr), (1, sb_scr, swb_scr)):
        loc_all = _dot(xcat, f_scr[d])
        for k in range(gpb):
            loc = loc_all[:, k * LANES:(k + 1) * LANES]
            scr[k] = loc
            sw_scr[k] = _swap(loc)
    y_within = _dot(xcat, k_scr[...])

    sgn8 = jnp.where(lo8, -1.0, 1.0)

    def dup(z):
        zs = _swap(z)
        return jnp.where(lo8, z, zs), jnp.where(lo8, zs, z)

    def scan(scr, sw_scr, d, reverse):
        a_re, a_im = dup(decay[d])
        a_sg = a_im * sgn8
        a_r = [a_re[k:k + 1, :] for k in range(gpb)]
        a_i = [a_sg[k:k + 1, :] for k in range(gpb)]

        def rows_of(i):
            c = (cpv - 1 - i) if reverse else i
            return pl.ds(c * n_virt, n_virt)

        def body(i, carry):
            st, sw = carry
            rows = rows_of(i)
            new_st, new_sw = [], []
            for k in range(gpb):
                loc = scr[k, rows, :]
                loc_sw = sw_scr[k, rows, :]
                scr[k, rows, :] = st[k]
                new_st.append(a_r[k] * st[k] + a_i[k] * sw[k] + loc)
                new_sw.append(a_r[k] * sw[k] - a_i[k] * st[k] + loc_sw)
            return tuple(new_st), tuple(new_sw)

        st0 = tuple(s0_ref[d, :, k * LANES:(k + 1) * LANES] for k in range(gpb))
        sw0 = tuple(_swap(s) for s in st0)
        carry = (st0, sw0)
        for i in range(cpv):
            carry = body(i, carry)
        fin = carry[0]
        if n_seg == 1:
            for k in range(gpb):
                sfin_ref[d, :, k * LANES:(k + 1) * LANES] = fin[k]
            return

        p = decay[d]
        for _ in range(cpv.bit_length() - 1):
            p_re, p_im = dup(p)
            p = _cmul(p, p_re, p_im * sgn8)
        v_re, v_im = dup(p)
        v_sg = v_im * sgn8
        seg = lax.broadcasted_iota(jnp.int32, (n_virt, LANES), 0) & (n_seg - 1)
        has_pred = seg != ((n_seg - 1) if reverse else 0)
        shift = (n_virt - 1) if reverse else 1
        cin = []
        for k in range(gpb):
            ck = jnp.zeros((n_virt, LANES), F32)
            for _ in range(n_seg - 1):
                nxt = fin[k] + ck * v_re[k:k + 1, :] + _swap(ck) * v_sg[k:k + 1, :]
                ck = jnp.where(has_pred, pltpu.roll(nxt, shift, 0), 0.0)
            cin.append(ck)
        cin_sw = [_swap(x) for x in cin]

        def fix(i, carry):
            q_re, q_im = carry
            rows = rows_of(i)
            q_sg = q_im * sgn8
            for k in range(gpb):
                scr[k, rows, :] += cin[k] * q_re[k:k + 1, :] + cin_sw[k] * q_sg[k:k + 1, :]
            return q_re * a_re - q_im * a_im, q_re * a_im + q_im * a_re

        q = (jnp.ones((gpb, LANES), F32), jnp.zeros((gpb, LANES), F32))
        for i in range(cpv):
            q = fix(i, q)

    scan(sf_scr, swf_scr, 0, False)
    scan(sb_scr, swb_scr, 1, True)

    s_f = jnp.concatenate([sf_scr[k].astype(BF16) for k in range(gpb)], axis=1)
    s_b = jnp.concatenate([sb_scr[k].astype(BF16) for k in range(gpb)], axis=1)
    yall = y_within + _dot_nt(s_f, e_scr[0]) + _dot_nt(s_b, e_scr[1])
    dsk = dsk_ref[...]
    for t in range(CHUNK):
        rows = pl.ds(t, nc, stride=CHUNK)
        y_ref[rows, :] = yall[:, t * LANES:(t + 1) * LANES] + h_ref[rows, :] * dsk


def _s5_params(lam_re, lam_im, log_dt, b_re, b_im, c_re, c_im):
    lamr = jnp.concatenate([lam_re, lam_re], axis=-1).astype(F32)
    lami = jnp.concatenate([lam_im, lam_im], axis=-1).astype(F32)
    ldt = jnp.broadcast_to(log_dt.astype(F32)[..., None], lamr.shape)
    bt = jnp.concatenate([b_re.transpose(0, 1, 3, 2), b_im.transpose(0, 1, 3, 2)], axis=-1)
    cp = jnp.concatenate([c_re, c_im], axis=-1)
    return (lamr, lami, ldt, bt.reshape(2, D_MODEL, LANES).astype(F32), cp.reshape(2, D_MODEL, LANES).astype(F32))


def _s5(h, params, d_skip, s0, n_seg):
    lamr, lami, ldt, bt, cp = params
    _, cpv, n_virt, _, _ = h.shape
    assert cpv & (cpv - 1) == 0 and n_virt % SUBLANES == 0 and n_seg & (n_seg - 1) == 0
    nc = cpv * n_virt
    ntok = nc * CHUNK
    hspec = pl.BlockSpec((None, ntok, LANES), lambda g: (g, 0, 0))
    kdim = CHUNK * LANES
    gspec = pl.BlockSpec((2, GROUPS_PER_BLOCK, LANES), lambda g: (0, g, 0))
    rspec = pl.BlockSpec((2, LANES, LANES), lambda g: (0, g, 0))
    sspec = pl.BlockSpec((None, 2, n_virt, STATE_LANES), lambda g: (g, 0, 0, 0))
    state_scr = pltpu.VMEM((GROUPS_PER_BLOCK, nc, LANES), F32)
    out_specs = [hspec]
    out_shape = [jax.ShapeDtypeStruct((N_GROUP_BLOCKS, ntok, LANES), F32)]
    if n_seg == 1:
        out_specs.append(sspec)
        out_shape.append(jax.ShapeDtypeStruct((N_GROUP_BLOCKS, 2, n_virt, STATE_LANES), F32))
    outs = pl.pallas_call(
        functools.partial(_s5_kernel, n_virt, n_seg),
        grid=(N_GROUP_BLOCKS,),
        in_specs=[
            hspec,
            gspec, gspec, gspec, rspec, rspec,
            pl.BlockSpec((1, LANES), lambda g: (0, g)),
            sspec,
        ],
        out_specs=out_specs,
        out_shape=out_shape,
        scratch_shapes=[
            pltpu.VMEM((2, kdim, STATE_LANES), BF16),
            pltpu.VMEM((2, kdim, STATE_LANES), BF16),
            pltpu.VMEM((kdim, kdim), BF16),
            state_scr, state_scr, state_scr, state_scr,
        ],
        compiler_params=_cparams(("arbitrary",)),
        name="s5_chunked_scan",
    )(h.reshape(N_GROUP_BLOCKS, ntok, LANES), lamr, lami, ldt, bt, cp, d_skip, s0)
    y = outs[0].reshape(h.shape)
    return (y, outs[1]) if n_seg == 1 else (y, None)


def _state_to_blocks(s):
    b = s.shape[0]
    s = s.reshape(b, 2, 2, N_GROUP_BLOCKS, GROUPS_PER_BLOCK, STATE_DIM)
    return s.transpose(3, 1, 0, 4, 2, 5).reshape(N_GROUP_BLOCKS, 2, b, STATE_LANES)


def _blocks_to_state(s):
    b = s.shape[2]
    s = s.reshape(N_GROUP_BLOCKS, 2, b, GROUPS_PER_BLOCK, 2, STATE_DIM)
    return s.transpose(2, 1, 4, 0, 3, 5).reshape(b, 2, 2, N_GROUPS, STATE_DIM)


def _rope_tables(n_tokens):
    pos = np.arange(n_tokens)
    n_freq = HEAD_DIM // 4
    freqs = ROPE_BASE ** (-np.arange(n_freq, dtype=np.float64) / n_freq)
    ang_r = (pos // GRID_W)[:, None] * freqs
    ang_c = (pos % GRID_W)[:, None] * freqs
    cos_h = np.concatenate([np.cos(ang_r), np.cos(ang_r), np.cos(ang_c), np.cos(ang_c)], axis=1)
    sin_h = np.concatenate([-np.sin(ang_r), np.sin(ang_r), -np.sin(ang_c), np.sin(ang_c)], axis=1)
    return jnp.asarray(np.tile(cos_h, (1, 2)), F32), jnp.asarray(np.tile(sin_h, (1, 2)), F32)


def kernel(x_prompt, x_sample, cache_k, cache_v, state_ssm, c, c_ctx, norm1_g, norm2_g, w_mod, b_mod,
           w_qkv, w_o, attn_sink, ssm_lam_re, ssm_lam_im, ssm_log_dt, ssm_b_re, ssm_b_im, ssm_c_re,
           ssm_c_im, ssm_d, glu_w_a, glu_w_b, mlp_w1, mlp_w2, final_norm_g):
    bp, lp, _ = x_prompt.shape
    bx, lx, _ = x_sample.shape
    assert lx % TOKEN_TILE == 0 and (bp * lp) % TOKEN_TILE == 0
    tiles_per_lat = lx // TOKEN_TILE

    xp = x_prompt.reshape(bp * lp, D_MODEL)
    xx = x_sample.reshape(bx * lx, D_MODEL)

    cvecs = jnp.zeros((8, D_MODEL), F32).at[0].set(c_ctx).at[1:1 + bx].set(c)
    mod = _modulation(cvecs, w_mod, b_mod)

    ctx_row = lambda i: 0
    lat_row = lambda i: 1 + i // tiles_per_lat

    assert lx % QKV_TILE == 0 and (bp * lp) % QKV_TILE == 0
    qkv_tiles_per_lat = lx // QKV_TILE
    rope = _rope_tables(lx) + (lambda i: i % qkv_tiles_per_lat,)
    wqkv = w_qkv[0].astype(BF16)
    g1 = norm1_g[0].reshape(1, D_MODEL)
    sink = attn_sink[0].astype(F32)
    qp, krp, vrp, kp, vp = _qkv(xp, mod[0], ctx_row, g1, wqkv, None, lp)
    qx, krx, vrx = _qkv(xx, mod[0], lambda i: 1 + i // qkv_tiles_per_lat, g1, wqkv, rope, 0)
    op = _ctx_attention(sink, qp, krp, vrp, bp, lp)
    rep = lambda t: jnp.tile(t[:, 0].transpose(0, 2, 1, 3), (1, 1, 1, LANES // HEAD_DIM)).astype(BF16)
    ox, (w1, w2, wo, wa, wb) = _lat_attention(
        sink, qx, krx, vrx, rep(cache_k), rep(cache_v), bx, lx,
        [mlp_w1, mlp_w2, w_o[0], glu_w_a[0], glu_w_b[0]])
    g2 = norm2_g.reshape(-1, 1, D_MODEL)
    gn = norm1_g[1].reshape(1, D_MODEL)
    vpt_p = TOKEN_TILE // lp
    vpt_x = 1
    n_seg_x = tiles_per_lat
    xp, hp = _post(xp, op, mod[0], ctx_row, g2[0], wo, None, w1, w2, 0, vpt_p, mod_next=mod[1], g_next=gn)
    xx, hx = _post(xx, ox, mod[0], lat_row, g2[0], wo, None, w1, w2, 0, vpt_x, mod_next=mod[1], g_next=gn)

    params = _s5_params(ssm_lam_re[0], ssm_lam_im[0], ssm_log_dt[0], ssm_b_re[0], ssm_b_im[0],
                        ssm_c_re[0], ssm_c_im[0])
    dsk = ssm_d[0].astype(F32).reshape(1, D_MODEL)
    s0p = jnp.zeros((N_GROUP_BLOCKS, 2, bp, STATE_LANES), F32)
    sx = _state_to_blocks(state_ssm[:, 0].astype(F32))
    s0x = jnp.zeros((N_GROUP_BLOCKS, 2, bx, n_seg_x, STATE_LANES), F32)
    s0x = s0x.at[:, 0, :, 0].set(sx[:, 0]).at[:, 1, :, n_seg_x - 1].set(sx[:, 1])
    s0x = s0x.reshape(N_GROUP_BLOCKS, 2, bx * n_seg_x, STATE_LANES)
    yp, sfin = _s5(hp, params, dsk, s0p, 1)
    yx, _ = _s5(hx, params, dsk, s0x, n_seg_x)
    new_state = _blocks_to_state(sfin)[:, None]

    fg = final_norm_g.reshape(1, D_MODEL)
    (yp_out,) = _post(xp, yp, mod[1], ctx_row, g2[1], wa, wb, w1, w2, 1, vpt_p, final_g=fg)
    (yx_out,) = _post(xx, yx, mod[1], lat_row, g2[1], wa, wb, w1, w2, 1, vpt_x, final_g=fg)

    to_cache = lambda t: t.reshape(bp, N_KV_HEADS, HEAD_DIM, lp).transpose(0, 3, 1, 2)[:, None]
    new_k = to_cache(kp)
    new_v = to_cache(vp)
    return (yp_out.reshape(bp, lp, D_MODEL), yx_out.reshape(bx, lx, D_MODEL), new_k, new_v, new_state)
```

```python
import functools
import math

import numpy as np
import jax
import jax.numpy as jnp
from jax import lax
from jax.experimental import pallas as pl
from jax.experimental.pallas import tpu as pltpu

F32 = jnp.float32
BF16 = jnp.bfloat16

D_MODEL = 1024
N_HEADS = 16
N_KV_HEADS = 4
HEAD_DIM = 64
Q_PER_KV = N_HEADS // N_KV_HEADS
KV_DIM = N_KV_HEADS * HEAD_DIM
QKV_DIM = D_MODEL + 2 * KV_DIM
BLOCK = 128
GRID_W = 64
ROPE_BASE = 10000.0
ROT_HALF = HEAD_DIM // 4
ATTN_SCALE = HEAD_DIM ** -0.5
N_GROUPS = 64
GROUP_CH = 16
STATE_DIM = 64
D_FF = 4 * D_MODEL
N_MOD = 6
RMS_EPS = 1e-6
NEG_INF = -1e30

LANES = 128
SUBLANES = 8
GROUPS_PER_BLOCK = LANES // GROUP_CH
N_GROUP_BLOCKS = N_GROUPS // GROUPS_PER_BLOCK
PAIR_CH = 2 * GROUP_CH
N_PAIRS = GROUPS_PER_BLOCK // 2
STATE_LANES = GROUPS_PER_BLOCK * 2 * STATE_DIM
CHUNK = SUBLANES
TOKEN_TILE = 512
FF_TILE = 1024
POST_SUBTILES = 2
QKV_TILE = 1024
QKV_SUBTILES = 4
VMEM_LIMIT = 56 * 1024 * 1024


def _cparams(semantics):
    return pltpu.CompilerParams(dimension_semantics=semantics, vmem_limit_bytes=VMEM_LIMIT)


def _rms(x):
    return x * lax.rsqrt(jnp.mean(x * x, axis=-1, keepdims=True) + RMS_EPS)


def _dot(a, b):
    return jnp.dot(a, b, preferred_element_type=F32)


def _dot_nt(a, b):
    return lax.dot_general(a, b, (((1,), (1,)), ((), ())), preferred_element_type=F32)


def _mod_kernel(cv_ref, w_ref, b_ref, o_ref):
    cv = cv_ref[...]
    s = (cv * jax.nn.sigmoid(cv)).astype(BF16)
    o_ref[0] = _dot(s, w_ref[0].astype(BF16)) + b_ref[0]


def _modulation(cvecs, w_mod, b_mod):
    depth = w_mod.shape[0]
    width = 2 * D_MODEL
    out = pl.pallas_call(
        _mod_kernel,
        grid=(depth, N_MOD * D_MODEL // width),
        in_specs=[
            pl.BlockSpec((8, D_MODEL), lambda l, j: (0, 0)),
            pl.BlockSpec((1, D_MODEL, width), lambda l, j: (l, 0, j)),
            pl.BlockSpec((1, 1, width), lambda l, j: (l, 0, j)),
        ],
        out_specs=pl.BlockSpec((1, 8, width), lambda l, j: (l, 0, j)),
        out_shape=jax.ShapeDtypeStruct((depth, 8, N_MOD * D_MODEL), F32),
        compiler_params=_cparams(("arbitrary", "arbitrary")),
        name="modulation",
    )(cvecs, w_mod, b_mod.reshape(depth, 1, N_MOD * D_MODEL))
    return out.reshape(depth, 8, N_MOD, D_MODEL)


def _head_pair(blk, odd):
    lo = lax.broadcasted_iota(jnp.int32, blk.shape, 1) < HEAD_DIM
    other = pltpu.roll(blk, HEAD_DIM, 1)
    return (jnp.where(lo, other, blk) if odd else jnp.where(lo, blk, other)).astype(BF16)


def _qkv_kernel(cache_seq, rope, x_ref, mod_ref, g_ref, w_ref, *refs):
    refs = list(refs)
    cos_ref, sin_ref = (refs.pop(0), refs.pop(0)) if rope else (None, None)
    q_ref, krep_ref, vrep_ref = refs[:3]
    kv_refs = refs[3:]
    sub = QKV_TILE // QKV_SUBTILES
    starts = [k * sub for k in range(QKV_SUBTILES)]

    def project(r0):
        h = _rms(x_ref[r0:r0 + sub, :]) * g_ref[...] * (1.0 + mod_ref[1:2, :]) + mod_ref[0:1, :]
        return _dot(h.astype(BF16), w_ref[...])

    def emit_cache(ref, blk, r0, c0):
        for s in range(sub // cache_seq):
            ref[r0 // cache_seq + s, c0:c0 + LANES, :] = blk[s * cache_seq:(s + 1) * cache_seq, :].T

    def finish(r0, qkv):
        rows = slice(r0, r0 + sub)
        if rope:
            cos = cos_ref[rows, :]
            sin = sin_ref[rows, :]
            lane = lax.broadcasted_iota(jnp.int32, cos.shape, 1)
            first = (lane & (2 * ROT_HALF - 1)) < ROT_HALF
        for blk in range((D_MODEL + KV_DIM) // LANES):
            r = qkv[:, blk * LANES:(blk + 1) * LANES]
            if rope:
                partner = jnp.where(first, pltpu.roll(r, LANES - ROT_HALF, 1), pltpu.roll(r, ROT_HALF, 1))
                r = r * cos + partner * sin
            if blk < D_MODEL // LANES:
                q_ref[rows, blk * LANES:(blk + 1) * LANES] = (r * ATTN_SCALE).astype(BF16)
            else:
                c0 = blk * LANES - D_MODEL
                if cache_seq:
                    emit_cache(kv_refs[0], r, r0, c0)
                for half in range(2):
                    krep_ref[c0 // HEAD_DIM + half, rows, :] = _head_pair(r, half)
        v = qkv[:, D_MODEL + KV_DIM:]
        for c0 in range(0, KV_DIM, LANES):
            blk = v[:, c0:c0 + LANES]
            if cache_seq:
                emit_cache(kv_refs[1], blk, r0, c0)
            for half in range(2):
                vrep_ref[c0 // HEAD_DIM + half, rows, :] = _head_pair(blk, half)

    for r0, qkv in zip(starts, [project(r0) for r0 in starts]):
        finish(r0, qkv)


def _qkv(x, mod, mod_row, g, w_qkv, rope, cache_seq):
    ntok = x.shape[0]
    nt = ntok // QKV_TILE
    emit_kv = cache_seq > 0
    assert not emit_kv or (QKV_TILE // QKV_SUBTILES) % cache_seq == 0
    rep_spec = pl.BlockSpec((N_KV_HEADS, QKV_TILE, LANES), lambda i: (0, i, 0))
    rep_shape = jax.ShapeDtypeStruct((N_KV_HEADS, ntok, LANES), BF16)
    in_specs = [
        pl.BlockSpec((QKV_TILE, D_MODEL), lambda i: (i, 0)),
        pl.BlockSpec((None, N_MOD, D_MODEL), lambda i: (mod_row(i), 0, 0)),
        pl.BlockSpec((1, D_MODEL), lambda i: (0, 0)),
        pl.BlockSpec((D_MODEL, QKV_DIM), lambda i: (0, 0)),
    ]
    args = [x, mod, g, w_qkv]
    if rope is not None:
        cos_t, sin_t, rope_blk = rope
        in_specs += [pl.BlockSpec((QKV_TILE, LANES), lambda i: (rope_blk(i), 0))] * 2
        args += [cos_t, sin_t]
    out_specs = [pl.BlockSpec((QKV_TILE, D_MODEL), lambda i: (i, 0)), rep_spec, rep_spec]
    out_shape = [jax.ShapeDtypeStruct((ntok, D_MODEL), BF16), rep_shape, rep_shape]
    if emit_kv:
        spt = QKV_TILE // cache_seq
        out_specs += [pl.BlockSpec((spt, KV_DIM, cache_seq), lambda i: (i, 0, 0))] * 2
        out_shape += [jax.ShapeDtypeStruct((ntok // cache_seq, KV_DIM, cache_seq), F32)] * 2
    return pl.pallas_call(
        functools.partial(_qkv_kernel, cache_seq, rope is not None),
        grid=(nt,),
        in_specs=in_specs,
        out_specs=out_specs,
        out_shape=out_shape,
        compiler_params=_cparams(("arbitrary",)),
        name="norm_qkv_rope",
    )(*args)


def _group_scores(q_ref, kv, key_parts, bias):
    nq = q_ref.shape[0]
    lo = lax.broadcasted_iota(jnp.int32, (nq, LANES), 1) < HEAD_DIM
    zero = jnp.zeros((), BF16)
    rows = []
    for b in range(KV_DIM // LANES):
        blk = q_ref[:, kv * KV_DIM + b * LANES:kv * KV_DIM + (b + 1) * LANES]
        rows += [jnp.where(lo, blk, zero), jnp.where(lo, zero, blk)]
    q4 = jnp.concatenate(rows, axis=0)
    parts = [_dot_nt(q4, keys) for keys in key_parts]
    if bias is not None:
        s0 = parts[0].reshape(Q_PER_KV, nq, -1) + bias[None]
        parts[0] = s0.reshape(Q_PER_KV * nq, -1)
    return parts


def _group_softmax(parts, sink_ref, kv):
    nq = parts[0].shape[0] // Q_PER_KV
    sink = jnp.concatenate(
        [jnp.full((nq, LANES), sink_ref[kv * Q_PER_KV + g], F32) for g in range(Q_PER_KV)], axis=0)
    blocks = [[s[:, j:j + LANES] for j in range(0, s.shape[1], LANES)] for s in parts]
    fold = None
    for b in sum(blocks, []):
        fold = b if fold is None else jnp.maximum(fold, b)
    m = jnp.maximum(jnp.max(fold, axis=-1, keepdims=True), sink)
    probs = [[jnp.exp(b - m) for b in bs] for bs in blocks]
    fold = None
    for p in sum(probs, []):
        fold = p if fold is None else fold + p
    den = jnp.sum(fold, axis=-1, keepdims=True) + jnp.exp(sink - m)
    return [jnp.concatenate(ps, axis=1).astype(BF16) for ps in probs], 1.0 / den


def _group_output(probs, inv_den, value_parts, o_ref, kv):
    nq = probs[0].shape[0] // Q_PER_KV
    r = _dot(probs[0], value_parts[0])
    for p, v in zip(probs[1:], value_parts[1:]):
        r = r + _dot(p, v)
    r = r * inv_den
    lo = lax.broadcasted_iota(jnp.int32, (nq, LANES), 1) < HEAD_DIM
    for b in range(KV_DIM // LANES):
        pair = jnp.where(lo, r[2 * b * nq:(2 * b + 1) * nq], r[(2 * b + 1) * nq:(2 * b + 2) * nq])
        o_ref[:, kv * KV_DIM + b * LANES:kv * KV_DIM + (b + 1) * LANES] = pair.astype(BF16)


def _attend(q_ref, sink_ref, o_ref, keys_of, values_of, bias):
    s_next = _group_scores(q_ref, 0, keys_of(0), bias)
    for kv in range(N_KV_HEADS):
        s = s_next
        if kv + 1 < N_KV_HEADS:
            s_next = _group_scores(q_ref, kv + 1, keys_of(kv + 1), bias)
        probs, inv_den = _group_softmax(s, sink_ref, kv)
        _group_output(probs, inv_den, values_of(kv), o_ref, kv)


def _ctx_attn_kernel(sink_ref, q_ref, k_ref, v_ref, o_ref):
    _attend(q_ref, sink_ref, o_ref, lambda kv: [k_ref[kv]], lambda kv: [v_ref[kv]], None)


def _ctx_attention(sink, q, krep, vrep, n_batch, seq):
    rep_spec = pl.BlockSpec((N_KV_HEADS, seq, LANES), lambda b: (0, b, 0))
    return pl.pallas_call(
        _ctx_attn_kernel,
        grid=(n_batch,),
        in_specs=[
            pl.BlockSpec(memory_space=pltpu.SMEM),
            pl.BlockSpec((seq, D_MODEL), lambda b: (b, 0)),
            rep_spec, rep_spec,
        ],
        out_specs=pl.BlockSpec((seq, D_MODEL), lambda b: (b, 0)),
        out_shape=jax.ShapeDtypeStruct((n_batch * seq, D_MODEL), BF16),
        compiler_params=_cparams(("arbitrary",)),
        name="context_attention",
    )(sink, q, krep, vrep)


def _window_start(n, seq):
    return jnp.clip((n - 1) * BLOCK, 0, seq - 3 * BLOCK)


def _band_bias():
    r = np.arange(BLOCK)[:, None]
    j = np.arange(3 * BLOCK)[None, :]
    out = [np.where(np.abs(j - d * BLOCK - r) <= BLOCK, 0.0, NEG_INF) for d in range(3)]
    return jnp.asarray(np.stack(out), F32)


def _lat_attn_kernel(seq, n_cast, sink_ref, q_ref, k_ref, v_ref, ck_ref, cv_ref, bias_ref, *refs):
    o_ref = refs[n_cast]
    for src, dst in zip(refs[:n_cast], refs[n_cast + 1:]):
        dst[...] = src[...].astype(BF16)
    win = 3 * BLOCK
    start = pl.multiple_of(_window_start(pl.program_id(1), seq), BLOCK)
    keys_of = lambda kv: [k_ref[kv, pl.ds(start, win), :], ck_ref[kv]]
    values_of = lambda kv: [v_ref[kv, pl.ds(start, win), :], cv_ref[kv]]
    _attend(q_ref, sink_ref, o_ref, keys_of, values_of, bias_ref[...])


def _lat_attention(sink, q, krep, vrep, ckrep, cvrep, n_batch, seq, cast_weights):
    nb = seq // BLOCK
    steps = n_batch * nb
    past = ckrep.shape[2]
    rep_spec = pl.BlockSpec((N_KV_HEADS, seq, LANES), lambda b, n: (0, b, 0))
    crep_spec = pl.BlockSpec((None, N_KV_HEADS, past, LANES), lambda b, n: (b, 0, 0, 0))
    flat = [w.reshape(-1, w.shape[-1]) for w in cast_weights]
    assert all(w.shape[0] % (steps * 2 * SUBLANES) == 0 for w in flat)
    slabs = [pl.BlockSpec((w.shape[0] // steps, w.shape[1]), lambda b, n: (b * nb + n, 0)) for w in flat]
    outs = pl.pallas_call(
        functools.partial(_lat_attn_kernel, seq, len(flat)),
        grid=(n_batch, nb),
        in_specs=[
            pl.BlockSpec(memory_space=pltpu.SMEM),
            pl.BlockSpec((BLOCK, D_MODEL), lambda b, n: (b * nb + n, 0)),
            rep_spec, rep_spec, crep_spec, crep_spec,
            pl.BlockSpec((None, BLOCK, 3 * BLOCK), lambda b, n: (n - _window_start(n, seq) // BLOCK, 0, 0)),
        ] + slabs,
        out_specs=[pl.BlockSpec((BLOCK, D_MODEL), lambda b, n: (b * nb + n, 0))] + slabs,
        out_shape=[jax.ShapeDtypeStruct((n_batch * seq, D_MODEL), BF16)]
        + [jax.ShapeDtypeStruct(w.shape, BF16) for w in flat],
        compiler_params=_cparams(("arbitrary", "arbitrary")),
        name="latent_attention",
    )(sink, q, krep, vrep, ckrep, cvrep, _band_bias(), *flat)
    return outs[0], [o.reshape(w.shape) for o, w in zip(outs[1:], cast_weights)]


def _gelu_tanh(x):
    c = math.sqrt(2.0 / math.pi)
    return x * (0.5 * (1.0 + jnp.tanh(c * (x + 0.044715 * (x * x * x)))))


def _post_kernel(is_attn, emit_next, final, vpt, *refs):
    rows_per_v = TOKEN_TILE // vpt
    refs = list(refs)
    x_ref, mix_ref, mod_ref, g2_ref, wa_ref = refs[:5]
    refs = refs[5:]
    wb_ref = None if is_attn else refs.pop(0)
    w1_ref, w2_ref = refs[:2]
    refs = refs[2:]
    modn_ref = gn_ref = fg_ref = hn_ref = None
    if emit_next:
        modn_ref, gn_ref = refs[:2]
        refs = refs[2:]
    if final:
        fg_ref = refs.pop(0)
    xo_ref = refs.pop(0)
    if emit_next:
        hn_ref = refs.pop(0)
    assert not refs

    sub = TOKEN_TILE // POST_SUBTILES
    assert rows_per_v % sub == 0
    n_chunk = sub // CHUNK
    starts = [k * sub for k in range(POST_SUBTILES)]

    def s5_slot(r0):
        return r0 // rows_per_v, (r0 % rows_per_v) // CHUNK

    def project(r0):
        if is_attn:
            return _dot(mix_ref[r0:r0 + sub, :], wa_ref[...])
        s, c0 = s5_slot(r0)
        y = jnp.concatenate(
            [mix_ref[g, c0:c0 + n_chunk, s].reshape(sub, LANES) for g in range(N_GROUP_BLOCKS)], axis=1)
        yg = _gelu_tanh(y).astype(BF16)
        return _dot(yg, wa_ref[...]) * jax.nn.sigmoid(_dot(yg, wb_ref[...]))

    def prologue(r0, mix):
        x1 = x_ref[r0:r0 + sub, :] + mod_ref[2:3, :] * mix
        h2 = _rms(x1) * g2_ref[...] * (1.0 + mod_ref[4:5, :]) + mod_ref[3:4, :]
        return x1, h2.astype(BF16)

    def mlp(h2):
        acc = None
        for c in range(D_FF // FF_TILE):
            a = jnp.maximum(_dot(h2, w1_ref[:, c * FF_TILE:(c + 1) * FF_TILE]), 0.0)
            t = _dot((a * a).astype(BF16), w2_ref[c * FF_TILE:(c + 1) * FF_TILE, :])
            acc = t if acc is None else acc + t
        return acc

    def epilogue(r0, x1, acc):
        x2 = x1 + mod_ref[5:6, :] * acc
        if emit_next:
            hn = _rms(x2) * gn_ref[...] * (1.0 + modn_ref[1:2, :]) + modn_ref[0:1, :]
            s, c0 = s5_slot(r0)
            for g in range(N_GROUP_BLOCKS):
                blk = hn[:, g * LANES:(g + 1) * LANES]
                hn_ref[g, c0:c0 + n_chunk, s] = blk.reshape(n_chunk, CHUNK, LANES)
        xo_ref[r0:r0 + sub, :] = _rms(x2) * fg_ref[...] if final else x2

    mixes = [project(r0) for r0 in starts]
    pro = [prologue(r0, mix) for r0, mix in zip(starts, mixes)]
    accs = [mlp(h2) for _, h2 in pro]
    for r0, (x1, _), acc in zip(starts, pro, accs):
        epilogue(r0, x1, acc)


def _post(x, mix, mod, mod_row, g2, w_a, w_b, w1, w2, layer, vpt, mod_next=None, g_next=None, final_g=None):
    is_attn = w_b is None
    emit_next = mod_next is not None
    final = final_g is not None
    ntok = x.shape[0]
    nt = ntok // TOKEN_TILE
    cpv = TOKEN_TILE // (vpt * CHUNK)
    n_virt = nt * vpt
    tile = pl.BlockSpec((TOKEN_TILE, D_MODEL), lambda i: (i, 0))
    row = pl.BlockSpec((1, D_MODEL), lambda i: (0, 0))
    modspec = pl.BlockSpec((None, N_MOD, D_MODEL), lambda i: (mod_row(i), 0, 0))
    resident = lambda shape: pl.BlockSpec(shape, lambda i: (0, 0), pipeline_mode=pl.Buffered(1))
    wsq = resident((D_MODEL, D_MODEL))
    gtile = pl.BlockSpec((N_GROUP_BLOCKS, cpv, vpt, CHUNK, LANES), lambda i: (0, 0, i, 0, 0))
    in_specs = [tile, tile if is_attn else gtile, modspec, row, wsq]
    args = [x, mix, mod, g2, w_a]
    if not is_attn:
        in_specs.append(wsq)
        args.append(w_b)
    in_specs += [pl.BlockSpec((None, D_MODEL, D_FF), lambda i: (layer, 0, 0), pipeline_mode=pl.Buffered(1)),
                 pl.BlockSpec((None, D_FF, D_MODEL), lambda i: (layer, 0, 0), pipeline_mode=pl.Buffered(1))]
    args += [w1, w2]
    if emit_next:
        in_specs += [modspec, row]
        args += [mod_next, g_next]
    if final:
        in_specs.append(row)
        args.append(final_g)
    out_specs = [tile]
    out_shape = [jax.ShapeDtypeStruct((ntok, D_MODEL), F32)]
    if emit_next:
        out_specs.append(gtile)
        out_shape.append(jax.ShapeDtypeStruct((N_GROUP_BLOCKS, cpv, n_virt, CHUNK, LANES), F32))
    return pl.pallas_call(
        functools.partial(_post_kernel, is_attn, emit_next, final, vpt),
        grid=(nt,),
        in_specs=in_specs,
        out_specs=out_specs,
        out_shape=out_shape,
        compiler_params=_cparams(("arbitrary",)),
        name="attn_proj_mlp" if is_attn else "glu_mlp_final",
    )(*args)


def _swap(x):
    return pltpu.roll(x, LANES // 2, 1)


def _cmul(z, w_r, w_i):
    return z * w_r + _swap(z) * w_i


def _multiplier(z, lo):
    zs = _swap(z)
    return jnp.where(lo, z, zs), jnp.where(lo, -zs, z)


def _rep_rows(x):
    return jnp.concatenate(
        [jnp.broadcast_to(x[g:g + 1, :], (GROUP_CH, LANES)) for g in range(GROUPS_PER_BLOCK)], axis=0)


def _s5_kernel(n_virt, n_seg, h_ref, lamr_ref, lami_ref, ldt_ref, bt_ref, cp_ref, dsk_ref, s0_ref, *refs):
    if n_seg == 1:
        y_ref, sfin_ref = refs[:2]
        refs = refs[2:]
    else:
        y_ref, sfin_ref = refs[0], None
        refs = refs[1:]
    f_scr, e_scr, k_scr, sf_scr, sb_scr, swf_scr, swb_scr = refs
    ntok = h_ref.shape[0]
    nc = ntok // CHUNK
    cpv = nc // n_virt
    gpb = GROUPS_PER_BLOCK

    lo8 = lax.broadcasted_iota(jnp.int32, (gpb, LANES), 1) < STATE_DIM
    lo = lax.broadcasted_iota(jnp.int32, (LANES, LANES), 1) < STATE_DIM
    conj = jnp.where(lo, 1.0, -1.0)
    row_g = lax.broadcasted_iota(jnp.int32, (LANES, STATE_LANES), 0) // GROUP_CH
    col_g = lax.broadcasted_iota(jnp.int32, (LANES, STATE_LANES), 1) // LANES
    diag_wide = row_g == col_g
    diag = (lax.broadcasted_iota(jnp.int32, (LANES, LANES), 0) // GROUP_CH) == (
        lax.broadcasted_iota(jnp.int32, (LANES, LANES), 1) // GROUP_CH)

    pair_diag = (lax.broadcasted_iota(jnp.int32, (PAIR_CH, 2 * LANES), 0) // GROUP_CH) == (
        lax.broadcasted_iota(jnp.int32, (PAIR_CH, 2 * LANES), 1) // LANES)

    def expand(w):
        return jnp.where(diag_wide, jnp.concatenate([w] * gpb, axis=1), jnp.zeros((), BF16))

    decay = []
    lag = []
    for d in range(2):
        lam_r = lamr_ref[d]
        lam_i = lami_ref[d]
        dt = jnp.exp(ldt_ref[d])
        mag = jnp.exp(lam_r * dt)
        ang = lam_i * dt
        a_r = mag * jnp.cos(ang)
        a_im = mag * jnp.sin(ang)
        a_i = jnp.where(lo8, -a_im, a_im)
        den = lam_r * lam_r + lam_i * lam_i
        num = jnp.where(lo8, a_r - 1.0, a_im)
        f = _cmul(num, lam_r / den, jnp.where(lo8, lam_i, -lam_i) / den)
        pw = [jnp.where(lo8, 1.0, 0.0)]
        for _ in range(CHUNK):
            pw.append(_cmul(pw[-1], a_r, a_i))
        decay.append(pw[CHUNK])
        pw = [_rep_rows(p) for p in pw]
        f_r, f_i = _multiplier(_rep_rows(f), lo)
        bb_r, bb_i = _multiplier(_cmul(bt_ref[d], f_r, f_i), lo)
        c_r, c_i = _multiplier(cp_ref[d], lo)
        cm = (cp_ref[d] * conj).astype(BF16)
        fpow = [_cmul(p, bb_r, bb_i).astype(BF16) for p in pw[:CHUNK]]
        for q in range(N_PAIRS):
            for j in range(CHUNK):
                e = (CHUNK - 1 - j) if d == 0 else j
                w = fpow[e][q * PAIR_CH:(q + 1) * PAIR_CH, :]
                f_scr[d, q, j * PAIR_CH:(j + 1) * PAIR_CH, :] = jnp.where(
                    pair_diag, jnp.concatenate([w, w], axis=1), jnp.zeros((), BF16))
        for t in range(CHUNK):
            e = (t + 1) if d == 0 else (CHUNK - t)
            w = _cmul(pw[e], c_r, c_i) * conj
            e_scr[d, t * LANES:(t + 1) * LANES, :] = expand(w.astype(BF16))
        lag.append([jnp.where(diag, _dot_nt(fp, cm), 0.0) for fp in fpow])

    for j in range(CHUNK):
        for t in range(CHUNK):
            k = t - j
            tile = lag[0][k] if k > 0 else (lag[1][-k] if k < 0 else lag[0][0] + lag[1][0])
            k_scr[j * LANES:(j + 1) * LANES, t * LANES:(t + 1) * LANES] = tile.astype(BF16)

    xs = [h_ref[pl.ds(j, nc, stride=CHUNK), :] for j in range(CHUNK)]
    xcat = jnp.concatenate([x.astype(BF16) for x in xs], axis=1)

    lane_blk = lax.broadcasted_iota(jnp.int32, (nc, LANES), 1) // PAIR_CH
    per_col = LANES // PAIR_CH
    for q in range(N_PAIRS):
        cols = []
        for half in range(CHUNK // per_col):
            col = jnp.zeros((nc, LANES), F32)
            for jj in range(per_col):
                x = xs[half * per_col + jj]
                shift = ((jj - q) * PAIR_CH) % LANES
                moved = pltpu.roll(x, shift, 1) if shift else x
                col = jnp.where(lane_blk == jj, moved, col)
            cols.append(col.astype(BF16))
        xq = jnp.concatenate(cols, axis=1)
        for d, scr, sw_scr in ((0, sf_scr, swf_scr), (1, sb_scr, swb_scr)):
            loc_pair = _dot(xq, f_scr[d, q])
            for g in range(2):
                loc = loc_pair[:, g * LANES:(g + 1) * LANES]
                scr[2 * q + g] = loc
                sw_scr[2 * q + g] = _swap(loc)
    y_within = _dot(xcat, k_scr[...])

    sgn8 = jnp.where(lo8, -1.0, 1.0)

    def dup(z):
        zs = _swap(z)
        return jnp.where(lo8, z, zs), jnp.where(lo8, zs, z)

    def scan(scr, sw_scr, d, reverse):
        a_re, a_im = dup(decay[d])
        a_sg = a_im * sgn8
        a_r = [a_re[k:k + 1, :] for k in range(gpb)]
        a_i = [a_sg[k:k + 1, :] for k in range(gpb)]

        def rows_of(i):
            c = (cpv - 1 - i) if reverse else i
            return pl.ds(c * n_virt, n_virt)

        def body(i, carry):
            st, sw = carry
            rows = rows_of(i)
            new_st, new_sw = [], []
            for k in range(gpb):
                loc = scr[k, rows, :]
                loc_sw = sw_scr[k, rows, :]
                scr[k, rows, :] = st[k]
                new_st.append(a_r[k] * st[k] + a_i[k] * sw[k] + loc)
                new_sw.append(a_r[k] * sw[k] - a_i[k] * st[k] + loc_sw)
            return tuple(new_st), tuple(new_sw)

        st0 = tuple(s0_ref[d, :, k * LANES:(k + 1) * LANES] for k in range(gpb))
        sw0 = tuple(_swap(s) for s in st0)
        carry = (st0, sw0)
        for i in range(cpv):
            carry = body(i, carry)
        fin = carry[0]
        if n_seg == 1:
            for k in range(gpb):
                sfin_ref[d, :, k * LANES:(k + 1) * LANES] = fin[k]
            return

        p = decay[d]
        for _ in range(cpv.bit_length() - 1):
            p_re, p_im = dup(p)
            p = _cmul(p, p_re, p_im * sgn8)
        v_re, v_im = dup(p)
        v_sg = v_im * sgn8
        seg = lax.broadcasted_iota(jnp.int32, (n_virt, LANES), 0) & (n_seg - 1)
        has_pred = seg != ((n_seg - 1) if reverse else 0)
        shift = (n_virt - 1) if reverse else 1
        cin = []
        for k in range(gpb):
            ck = jnp.zeros((n_virt, LANES), F32)
            for _ in range(n_seg - 1):
                nxt = fin[k] + ck * v_re[k:k + 1, :] + _swap(ck) * v_sg[k:k + 1, :]
                ck = jnp.where(has_pred, pltpu.roll(nxt, shift, 0), 0.0)
            cin.append(ck)
        cin_sw = [_swap(x) for x in cin]

        def fix(i, carry):
            q_re, q_im = carry
            rows = rows_of(i)
            q_sg = q_im * sgn8
            for k in range(gpb):
                scr[k, rows, :] += cin[k] * q_re[k:k + 1, :] + cin_sw[k] * q_sg[k:k + 1, :]
            return q_re * a_re - q_im * a_im, q_re * a_im + q_im * a_re

        q = (jnp.ones((gpb, LANES), F32), jnp.zeros((gpb, LANES), F32))
        for i in range(cpv):
            q = fix(i, q)

    scan(sf_scr, swf_scr, 0, False)
    scan(sb_scr, swb_scr, 1, True)

    s_f = jnp.concatenate([sf_scr[k].astype(BF16) for k in range(gpb)], axis=1)
    s_b = jnp.concatenate([sb_scr[k].astype(BF16) for k in range(gpb)], axis=1)
    yall = y_within + _dot_nt(s_f, e_scr[0]) + _dot_nt(s_b, e_scr[1])
    dsk = dsk_ref[...]
    for t in range(CHUNK):
        rows = pl.ds(t, nc, stride=CHUNK)
        y_ref[rows, :] = yall[:, t * LANES:(t + 1) * LANES] + h_ref[rows, :] * dsk


def _s5_params(lam_re, lam_im, log_dt, b_re, b_im, c_re, c_im):
    lamr = jnp.concatenate([lam_re, lam_re], axis=-1).astype(F32)
    lami = jnp.concatenate([lam_im, lam_im], axis=-1).astype(F32)
    ldt = jnp.broadcast_to(log_dt.astype(F32)[..., None], lamr.shape)
    bt = jnp.concatenate([b_re.transpose(0, 1, 3, 2), b_im.transpose(0, 1, 3, 2)], axis=-1)
    cp = jnp.concatenate([c_re, c_im], axis=-1)
    return (lamr, lami, ldt, bt.reshape(2, D_MODEL, LANES).astype(F32), cp.reshape(2, D_MODEL, LANES).astype(F32))


def _s5(h, params, d_skip, s0, n_seg):
    lamr, lami, ldt, bt, cp = params
    _, cpv, n_virt, _, _ = h.shape
    assert cpv & (cpv - 1) == 0 and n_virt % SUBLANES == 0 and n_seg & (n_seg - 1) == 0
    nc = cpv * n_virt
    ntok = nc * CHUNK
    hspec = pl.BlockSpec((None, ntok, LANES), lambda g: (g, 0, 0))
    kdim = CHUNK * LANES
    gspec = pl.BlockSpec((2, GROUPS_PER_BLOCK, LANES), lambda g: (0, g, 0))
    rspec = pl.BlockSpec((2, LANES, LANES), lambda g: (0, g, 0))
    sspec = pl.BlockSpec((None, 2, n_virt, STATE_LANES), lambda g: (g, 0, 0, 0))
    state_scr = pltpu.VMEM((GROUPS_PER_BLOCK, nc, LANES), F32)
    out_specs = [hspec]
    out_shape = [jax.ShapeDtypeStruct((N_GROUP_BLOCKS, ntok, LANES), F32)]
    if n_seg == 1:
        out_specs.append(sspec)
        out_shape.append(jax.ShapeDtypeStruct((N_GROUP_BLOCKS, 2, n_virt, STATE_LANES), F32))
    outs = pl.pallas_call(
        functools.partial(_s5_kernel, n_virt, n_seg),
        grid=(N_GROUP_BLOCKS,),
        in_specs=[
            hspec,
            gspec, gspec, gspec, rspec, rspec,
            pl.BlockSpec((1, LANES), lambda g: (0, g)),
            sspec,
        ],
        out_specs=out_specs,
        out_shape=out_shape,
        scratch_shapes=[
            pltpu.VMEM((2, N_PAIRS, CHUNK * PAIR_CH, 2 * LANES), BF16),
            pltpu.VMEM((2, kdim, STATE_LANES), BF16),
            pltpu.VMEM((kdim, kdim), BF16),
            state_scr, state_scr, state_scr, state_scr,
        ],
        compiler_params=_cparams(("arbitrary",)),
        name="s5_chunked_scan",
    )(h.reshape(N_GROUP_BLOCKS, ntok, LANES), lamr, lami, ldt, bt, cp, d_skip, s0)
    y = outs[0].reshape(h.shape)
    return (y, outs[1]) if n_seg == 1 else (y, None)


def _state_to_blocks(s):
    b = s.shape[0]
    s = s.reshape(b, 2, 2, N_GROUP_BLOCKS, GROUPS_PER_BLOCK, STATE_DIM)
    return s.transpose(3, 1, 0, 4, 2, 5).reshape(N_GROUP_BLOCKS, 2, b, STATE_LANES)


def _blocks_to_state(s):
    b = s.shape[2]
    s = s.reshape(N_GROUP_BLOCKS, 2, b, GROUPS_PER_BLOCK, 2, STATE_DIM)
    return s.transpose(2, 1, 4, 0, 3, 5).reshape(b, 2, 2, N_GROUPS, STATE_DIM)


def _rope_tables(n_tokens):
    pos = np.arange(n_tokens)
    n_freq = HEAD_DIM // 4
    freqs = ROPE_BASE ** (-np.arange(n_freq, dtype=np.float64) / n_freq)
    ang_r = (pos // GRID_W)[:, None] * freqs
    ang_c = (pos % GRID_W)[:, None] * freqs
    cos_h = np.concatenate([np.cos(ang_r), np.cos(ang_r), np.cos(ang_c), np.cos(ang_c)], axis=1)
    sin_h = np.concatenate([-np.sin(ang_r), np.sin(ang_r), -np.sin(ang_c), np.sin(ang_c)], axis=1)
    return jnp.asarray(np.tile(cos_h, (1, 2)), F32), jnp.asarray(np.tile(sin_h, (1, 2)), F32)


def kernel(x_prompt, x_sample, cache_k, cache_v, state_ssm, c, c_ctx, norm1_g, norm2_g, w_mod, b_mod,
           w_qkv, w_o, attn_sink, ssm_lam_re, ssm_lam_im, ssm_log_dt, ssm_b_re, ssm_b_im, ssm_c_re,
           ssm_c_im, ssm_d, glu_w_a, glu_w_b, mlp_w1, mlp_w2, final_norm_g):
    bp, lp, _ = x_prompt.shape
    bx, lx, _ = x_sample.shape
    assert lx % TOKEN_TILE == 0 and (bp * lp) % TOKEN_TILE == 0
    tiles_per_lat = lx // TOKEN_TILE

    xp = x_prompt.reshape(bp * lp, D_MODEL)
    xx = x_sample.reshape(bx * lx, D_MODEL)

    cvecs = jnp.zeros((8, D_MODEL), F32).at[0].set(c_ctx).at[1:1 + bx].set(c)
    mod = _modulation(cvecs, w_mod, b_mod)

    ctx_row = lambda i: 0
    lat_row = lambda i: 1 + i // tiles_per_lat

    assert lx % QKV_TILE == 0 and (bp * lp) % QKV_TILE == 0
    qkv_tiles_per_lat = lx // QKV_TILE
    rope = _rope_tables(lx) + (lambda i: i % qkv_tiles_per_lat,)
    wqkv = w_qkv[0].astype(BF16)
    g1 = norm1_g[0].reshape(1, D_MODEL)
    sink = attn_sink[0].astype(F32)
    qp, krp, vrp, kp, vp = _qkv(xp, mod[0], ctx_row, g1, wqkv, None, lp)
    qx, krx, vrx = _qkv(xx, mod[0], lambda i: 1 + i // qkv_tiles_per_lat, g1, wqkv, rope, 0)
    op = _ctx_attention(sink, qp, krp, vrp, bp, lp)
    rep = lambda t: jnp.tile(t[:, 0].transpose(0, 2, 1, 3), (1, 1, 1, LANES // HEAD_DIM)).astype(BF16)
    ox, (w1, w2, wo, wa, wb) = _lat_attention(
        sink, qx, krx, vrx, rep(cache_k), rep(cache_v), bx, lx,
        [mlp_w1, mlp_w2, w_o[0], glu_w_a[0], glu_w_b[0]])
    g2 = norm2_g.reshape(-1, 1, D_MODEL)
    gn = norm1_g[1].reshape(1, D_MODEL)
    vpt_p = TOKEN_TILE // lp
    vpt_x = 1
    n_seg_x = tiles_per_lat
    xp, hp = _post(xp, op, mod[0], ctx_row, g2[0], wo, None, w1, w2, 0, vpt_p, mod_next=mod[1], g_next=gn)
    xx, hx = _post(xx, ox, mod[0], lat_row, g2[0], wo, None, w1, w2, 0, vpt_x, mod_next=mod[1], g_next=gn)

    params = _s5_params(ssm_lam_re[0], ssm_lam_im[0], ssm_log_dt[0], ssm_b_re[0], ssm_b_im[0],
                        ssm_c_re[0], ssm_c_im[0])
    dsk = ssm_d[0].astype(F32).reshape(1, D_MODEL)
    s0p = jnp.zeros((N_GROUP_BLOCKS, 2, bp, STATE_LANES), F32)
    sx = _state_to_blocks(state_ssm[:, 0].astype(F32))
    s0x = jnp.zeros((N_GROUP_BLOCKS, 2, bx, n_seg_x, STATE_LANES), F32)
    s0x = s0x.at[:, 0, :, 0].set(sx[:, 0]).at[:, 1, :, n_seg_x - 1].set(sx[:, 1])
    s0x = s0x.reshape(N_GROUP_BLOCKS, 2, bx * n_seg_x, STATE_LANES)
    yp, sfin = _s5(hp, params, dsk, s0p, 1)
    yx, _ = _s5(hx, params, dsk, s0x, n_seg_x)
    new_state = _blocks_to_state(sfin)[:, None]

    fg = final_norm_g.reshape(1, D_MODEL)
    (yp_out,) = _post(xp, yp, mod[1], ctx_row, g2[1], wa, wb, w1, w2, 1, vpt_p, final_g=fg)
    (yx_out,) = _post(xx, yx, mod[1], lat_row, g2[1], wa, wb, w1, w2, 1, vpt_x, final_g=fg)

    to_cache = lambda t: t.reshape(bp, N_KV_HEADS, HEAD_DIM, lp).transpose(0, 3, 1, 2)[:, None]
    new_k = to_cache(kp)
    new_v = to_cache(vp)
    return (yp_out.reshape(bp, lp, D_MODEL), yx_out.reshape(bx, lx, D_MODEL), new_k, new_v, new_state)
```

```python
import functools
import math

import numpy as np
import jax
import jax.numpy as jnp
from jax import lax
from jax.experimental import pallas as pl
from jax.experimental.pallas import tpu as pltpu

F32 = jnp.float32
BF16 = jnp.bfloat16

D_MODEL = 1024
N_HEADS = 16
N_KV_HEADS = 4
HEAD_DIM = 64
Q_PER_KV = N_HEADS // N_KV_HEADS
KV_DIM = N_KV_HEADS * HEAD_DIM
QKV_DIM = D_MODEL + 2 * KV_DIM
BLOCK = 128
GRID_W = 64
ROPE_BASE = 10000.0
ROT_HALF = HEAD_DIM // 4
ATTN_SCALE = HEAD_DIM ** -0.5
N_GROUPS = 64
GROUP_CH = 16
STATE_DIM = 64
D_FF = 4 * D_MODEL
N_MOD = 6
RMS_EPS = 1e-6
NEG_INF = -1e30

LANES = 128
SUBLANES = 8
GROUPS_PER_BLOCK = LANES // GROUP_CH
N_GROUP_BLOCKS = N_GROUPS // GROUPS_PER_BLOCK
PAIR_CH = 2 * GROUP_CH
N_PAIRS = GROUPS_PER_BLOCK // 2
STATE_LANES = GROUPS_PER_BLOCK * 2 * STATE_DIM
CHUNK = SUBLANES
TOKEN_TILE = 512
FF_TILE = 1024
POST_SUBTILES = 2
QKV_TILE = 1024
QKV_SUBTILES = 4
VMEM_LIMIT = 56 * 1024 * 1024


def _cparams(semantics):
    return pltpu.CompilerParams(dimension_semantics=semantics, vmem_limit_bytes=VMEM_LIMIT)


def _rms(x):
    return x * lax.rsqrt(jnp.mean(x * x, axis=-1, keepdims=True) + RMS_EPS)


def _dot(a, b):
    return jnp.dot(a, b, preferred_element_type=F32)


def _dot_nt(a, b):
    return lax.dot_general(a, b, (((1,), (1,)), ((), ())), preferred_element_type=F32)


def _mod_kernel(cv_ref, w_ref, b_ref, o_ref):
    cv = cv_ref[...]
    s = (cv * jax.nn.sigmoid(cv)).astype(BF16)
    o_ref[0] = _dot(s, w_ref[0].astype(BF16)) + b_ref[0]


def _modulation(cvecs, w_mod, b_mod):
    depth = w_mod.shape[0]
    width = 2 * D_MODEL
    out = pl.pallas_call(
        _mod_kernel,
        grid=(depth, N_MOD * D_MODEL // width),
        in_specs=[
            pl.BlockSpec((8, D_MODEL), lambda l, j: (0, 0)),
            pl.BlockSpec((1, D_MODEL, width), lambda l, j: (l, 0, j)),
            pl.BlockSpec((1, 1, width), lambda l, j: (l, 0, j)),
        ],
        out_specs=pl.BlockSpec((1, 8, width), lambda l, j: (l, 0, j)),
        out_shape=jax.ShapeDtypeStruct((depth, 8, N_MOD * D_MODEL), F32),
        compiler_params=_cparams(("arbitrary", "arbitrary")),
        name="modulation",
    )(cvecs, w_mod, b_mod.reshape(depth, 1, N_MOD * D_MODEL))
    return out.reshape(depth, 8, N_MOD, D_MODEL)


def _head_pair(blk, odd):
    lo = lax.broadcasted_iota(jnp.int32, blk.shape, 1) < HEAD_DIM
    other = pltpu.roll(blk, HEAD_DIM, 1)
    return (jnp.where(lo, other, blk) if odd else jnp.where(lo, blk, other)).astype(BF16)


def _qkv_kernel(cache_seq, rope, x_ref, mod_ref, g_ref, w_ref, *refs):
    refs = list(refs)
    cos_ref, sin_ref = (refs.pop(0), refs.pop(0)) if rope else (None, None)
    q_ref, krep_ref, vrep_ref = refs[:3]
    kv_refs = refs[3:]
    sub = QKV_TILE // QKV_SUBTILES
    starts = [k * sub for k in range(QKV_SUBTILES)]

    def project(r0):
        h = _rms(x_ref[r0:r0 + sub, :]) * g_ref[...] * (1.0 + mod_ref[1:2, :]) + mod_ref[0:1, :]
        return _dot(h.astype(BF16), w_ref[...])

    def emit_cache(ref, blk, r0, c0):
        for s in range(sub // cache_seq):
            ref[r0 // cache_seq + s, c0:c0 + LANES, :] = blk[s * cache_seq:(s + 1) * cache_seq, :].T

    def finish(r0, qkv):
        rows = slice(r0, r0 + sub)
        if rope:
            cos = cos_ref[rows, :]
            sin = sin_ref[rows, :]
            lane = lax.broadcasted_iota(jnp.int32, cos.shape, 1)
            first = (lane & (2 * ROT_HALF - 1)) < ROT_HALF
        for blk in range((D_MODEL + KV_DIM) // LANES):
            r = qkv[:, blk * LANES:(blk + 1) * LANES]
            if rope:
                partner = jnp.where(first, pltpu.roll(r, LANES - ROT_HALF, 1), pltpu.roll(r, ROT_HALF, 1))
                r = r * cos + partner * sin
            if blk < D_MODEL // LANES:
                q_ref[rows, blk * LANES:(blk + 1) * LANES] = (r * ATTN_SCALE).astype(BF16)
            else:
                c0 = blk * LANES - D_MODEL
                if cache_seq:
                    emit_cache(kv_refs[0], r, r0, c0)
                for half in range(2):
                    krep_ref[c0 // HEAD_DIM + half, rows, :] = _head_pair(r, half)
        v = qkv[:, D_MODEL + KV_DIM:]
        for c0 in range(0, KV_DIM, LANES):
            blk = v[:, c0:c0 + LANES]
            if cache_seq:
                emit_cache(kv_refs[1], blk, r0, c0)
            for half in range(2):
                vrep_ref[c0 // HEAD_DIM + half, rows, :] = _head_pair(blk, half)

    for r0, qkv in zip(starts, [project(r0) for r0 in starts]):
        finish(r0, qkv)


def _qkv(x, mod, mod_row, g, w_qkv, rope, cache_seq):
    ntok = x.shape[0]
    nt = ntok // QKV_TILE
    emit_kv = cache_seq > 0
    assert not emit_kv or (QKV_TILE // QKV_SUBTILES) % cache_seq == 0
    rep_spec = pl.BlockSpec((N_KV_HEADS, QKV_TILE, LANES), lambda i: (0, i, 0))
    rep_shape = jax.ShapeDtypeStruct((N_KV_HEADS, ntok, LANES), BF16)
    in_specs = [
        pl.BlockSpec((QKV_TILE, D_MODEL), lambda i: (i, 0)),
        pl.BlockSpec((None, N_MOD, D_MODEL), lambda i: (mod_row(i), 0, 0)),
        pl.BlockSpec((1, D_MODEL), lambda i: (0, 0)),
        pl.BlockSpec((D_MODEL, QKV_DIM), lambda i: (0, 0)),
    ]
    args = [x, mod, g, w_qkv]
    if rope is not None:
        cos_t, sin_t, rope_blk = rope
        in_specs += [pl.BlockSpec((QKV_TILE, LANES), lambda i: (rope_blk(i), 0))] * 2
        args += [cos_t, sin_t]
    out_specs = [pl.BlockSpec((QKV_TILE, D_MODEL), lambda i: (i, 0)), rep_spec, rep_spec]
    out_shape = [jax.ShapeDtypeStruct((ntok, D_MODEL), BF16), rep_shape, rep_shape]
    if emit_kv:
        spt = QKV_TILE // cache_seq
        out_specs += [pl.BlockSpec((spt, KV_DIM, cache_seq), lambda i: (i, 0, 0))] * 2
        out_shape += [jax.ShapeDtypeStruct((ntok // cache_seq, KV_DIM, cache_seq), F32)] * 2
    return pl.pallas_call(
        functools.partial(_qkv_kernel, cache_seq, rope is not None),
        grid=(nt,),
        in_specs=in_specs,
        out_specs=out_specs,
        out_shape=out_shape,
        compiler_params=_cparams(("arbitrary",)),
        name="norm_qkv_rope",
    )(*args)


def _group_scores(q_ref, kv, key_parts, bias):
    nq = q_ref.shape[0]
    lo = lax.broadcasted_iota(jnp.int32, (nq, LANES), 1) < HEAD_DIM
    zero = jnp.zeros((), BF16)
    rows = []
    for b in range(KV_DIM // LANES):
        blk = q_ref[:, kv * KV_DIM + b * LANES:kv * KV_DIM + (b + 1) * LANES]
        rows += [jnp.where(lo, blk, zero), jnp.where(lo, zero, blk)]
    q4 = jnp.concatenate(rows, axis=0)
    parts = [_dot_nt(q4, keys) for keys in key_parts]
    if bias is not None:
        s0 = parts[0].reshape(Q_PER_KV, nq, -1) + bias[None]
        parts[0] = s0.reshape(Q_PER_KV * nq, -1)
    return parts


def _group_softmax(parts, sink_ref, kv):
    nq = parts[0].shape[0] // Q_PER_KV
    sink = jnp.concatenate(
        [jnp.full((nq, LANES), sink_ref[kv * Q_PER_KV + g], F32) for g in range(Q_PER_KV)], axis=0)
    blocks = [[s[:, j:j + LANES] for j in range(0, s.shape[1], LANES)] for s in parts]
    fold = None
    for b in sum(blocks, []):
        fold = b if fold is None else jnp.maximum(fold, b)
    m = jnp.maximum(jnp.max(fold, axis=-1, keepdims=True), sink)
    probs = [[jnp.exp(b - m) for b in bs] for bs in blocks]
    fold = None
    for p in sum(probs, []):
        fold = p if fold is None else fold + p
    den = jnp.sum(fold, axis=-1, keepdims=True) + jnp.exp(sink - m)
    return [jnp.concatenate(ps, axis=1).astype(BF16) for ps in probs], 1.0 / den


def _group_output(probs, inv_den, value_parts, o_ref, kv):
    nq = probs[0].shape[0] // Q_PER_KV
    r = _dot(probs[0], value_parts[0])
    for p, v in zip(probs[1:], value_parts[1:]):
        r = r + _dot(p, v)
    r = r * inv_den
    lo = lax.broadcasted_iota(jnp.int32, (nq, LANES), 1) < HEAD_DIM
    for b in range(KV_DIM // LANES):
        pair = jnp.where(lo, r[2 * b * nq:(2 * b + 1) * nq], r[(2 * b + 1) * nq:(2 * b + 2) * nq])
        o_ref[:, kv * KV_DIM + b * LANES:kv * KV_DIM + (b + 1) * LANES] = pair.astype(BF16)


def _attend(q_ref, sink_ref, o_ref, keys_of, values_of, bias):
    s_next = _group_scores(q_ref, 0, keys_of(0), bias)
    for kv in range(N_KV_HEADS):
        s = s_next
        if kv + 1 < N_KV_HEADS:
            s_next = _group_scores(q_ref, kv + 1, keys_of(kv + 1), bias)
        probs, inv_den = _group_softmax(s, sink_ref, kv)
        _group_output(probs, inv_den, values_of(kv), o_ref, kv)


def _ctx_attn_kernel(sink_ref, q_ref, k_ref, v_ref, o_ref):
    _attend(q_ref, sink_ref, o_ref, lambda kv: [k_ref[kv]], lambda kv: [v_ref[kv]], None)


def _ctx_attention(sink, q, krep, vrep, n_batch, seq):
    rep_spec = pl.BlockSpec((N_KV_HEADS, seq, LANES), lambda b: (0, b, 0))
    return pl.pallas_call(
        _ctx_attn_kernel,
        grid=(n_batch,),
        in_specs=[
            pl.BlockSpec(memory_space=pltpu.SMEM),
            pl.BlockSpec((seq, D_MODEL), lambda b: (b, 0)),
            rep_spec, rep_spec,
        ],
        out_specs=pl.BlockSpec((seq, D_MODEL), lambda b: (b, 0)),
        out_shape=jax.ShapeDtypeStruct((n_batch * seq, D_MODEL), BF16),
        compiler_params=_cparams(("arbitrary",)),
        name="context_attention",
    )(sink, q, krep, vrep)


def _window_start(n, seq):
    return jnp.clip((n - 1) * BLOCK, 0, seq - 3 * BLOCK)


def _band_bias():
    r = np.arange(BLOCK)[:, None]
    j = np.arange(3 * BLOCK)[None, :]
    out = [np.where(np.abs(j - d * BLOCK - r) <= BLOCK, 0.0, NEG_INF) for d in range(3)]
    return jnp.asarray(np.stack(out), F32)


def _lat_attn_kernel(seq, n_cast, sink_ref, q_ref, k_ref, v_ref, ck_ref, cv_ref, bias_ref, *refs):
    o_ref = refs[n_cast]
    for src, dst in zip(refs[:n_cast], refs[n_cast + 1:]):
        dst[...] = src[...].astype(BF16)
    win = 3 * BLOCK
    start = pl.multiple_of(_window_start(pl.program_id(1), seq), BLOCK)
    keys_of = lambda kv: [k_ref[kv, pl.ds(start, win), :], ck_ref[kv]]
    values_of = lambda kv: [v_ref[kv, pl.ds(start, win), :], cv_ref[kv]]
    _attend(q_ref, sink_ref, o_ref, keys_of, values_of, bias_ref[...])


def _lat_attention(sink, q, krep, vrep, ckrep, cvrep, n_batch, seq, cast_weights):
    nb = seq // BLOCK
    steps = n_batch * nb
    past = ckrep.shape[2]
    rep_spec = pl.BlockSpec((N_KV_HEADS, seq, LANES), lambda b, n: (0, b, 0))
    crep_spec = pl.BlockSpec((None, N_KV_HEADS, past, LANES), lambda b, n: (b, 0, 0, 0))
    flat = [w.reshape(-1, w.shape[-1]) for w in cast_weights]
    assert all(w.shape[0] % (steps * 2 * SUBLANES) == 0 for w in flat)
    slabs = [pl.BlockSpec((w.shape[0] // steps, w.shape[1]), lambda b, n: (b * nb + n, 0)) for w in flat]
    outs = pl.pallas_call(
        functools.partial(_lat_attn_kernel, seq, len(flat)),
        grid=(n_batch, nb),
        in_specs=[
            pl.BlockSpec(memory_space=pltpu.SMEM),
            pl.BlockSpec((BLOCK, D_MODEL), lambda b, n: (b * nb + n, 0)),
            rep_spec, rep_spec, crep_spec, crep_spec,
            pl.BlockSpec((None, BLOCK, 3 * BLOCK), lambda b, n: (n - _window_start(n, seq) // BLOCK, 0, 0)),
        ] + slabs,
        out_specs=[pl.BlockSpec((BLOCK, D_MODEL), lambda b, n: (b * nb + n, 0))] + slabs,
        out_shape=[jax.ShapeDtypeStruct((n_batch * seq, D_MODEL), BF16)]
        + [jax.ShapeDtypeStruct(w.shape, BF16) for w in flat],
        compiler_params=_cparams(("arbitrary", "arbitrary")),
        name="latent_attention",
    )(sink, q, krep, vrep, ckrep, cvrep, _band_bias(), *flat)
    return outs[0], [o.reshape(w.shape) for o, w in zip(outs[1:], cast_weights)]


def _gelu_tanh(x):
    c = math.sqrt(2.0 / math.pi)
    return x * (0.5 * (1.0 + jnp.tanh(c * (x + 0.044715 * (x * x * x)))))


def _post_kernel(is_attn, emit_next, final, vpt, *refs):
    rows_per_v = TOKEN_TILE // vpt
    refs = list(refs)
    x_ref, mix_ref, mod_ref, g2_ref, wa_ref = refs[:5]
    refs = refs[5:]
    wb_ref = None if is_attn else refs.pop(0)
    w1_ref, w2_ref = refs[:2]
    refs = refs[2:]
    modn_ref = gn_ref = fg_ref = hn_ref = None
    if emit_next:
        modn_ref, gn_ref = refs[:2]
        refs = refs[2:]
    if final:
        fg_ref = refs.pop(0)
    xo_ref = refs.pop(0)
    if emit_next:
        hn_ref = refs.pop(0)
    assert not refs

    sub = TOKEN_TILE // POST_SUBTILES
    assert rows_per_v % sub == 0
    n_chunk = sub // CHUNK
    starts = [k * sub for k in range(POST_SUBTILES)]

    def s5_slot(r0):
        return r0 // rows_per_v, (r0 % rows_per_v) // CHUNK

    def project(r0):
        if is_attn:
            return _dot(mix_ref[r0:r0 + sub, :], wa_ref[...])
        s, c0 = s5_slot(r0)
        y = jnp.concatenate(
            [mix_ref[g, c0:c0 + n_chunk, s].reshape(sub, LANES) for g in range(N_GROUP_BLOCKS)], axis=1)
        yg = _gelu_tanh(y).astype(BF16)
        return _dot(yg, wa_ref[...]) * jax.nn.sigmoid(_dot(yg, wb_ref[...]))

    def prologue(r0, mix):
        x1 = x_ref[r0:r0 + sub, :] + mod_ref[2:3, :] * mix
        h2 = _rms(x1) * g2_ref[...] * (1.0 + mod_ref[4:5, :]) + mod_ref[3:4, :]
        return x1, h2.astype(BF16)

    def mlp(h2):
        acc = None
        for c in range(D_FF // FF_TILE):
            a = jnp.maximum(_dot(h2, w1_ref[:, c * FF_TILE:(c + 1) * FF_TILE]), 0.0)
            t = _dot((a * a).astype(BF16), w2_ref[c * FF_TILE:(c + 1) * FF_TILE, :])
            acc = t if acc is None else acc + t
        return acc

    def epilogue(r0, x1, acc):
        x2 = x1 + mod_ref[5:6, :] * acc
        if emit_next:
            hn = _rms(x2) * gn_ref[...] * (1.0 + modn_ref[1:2, :]) + modn_ref[0:1, :]
            s, c0 = s5_slot(r0)
            for g in range(N_GROUP_BLOCKS):
                blk = hn[:, g * LANES:(g + 1) * LANES]
                hn_ref[g, c0:c0 + n_chunk, s] = blk.reshape(n_chunk, CHUNK, LANES)
        xo_ref[r0:r0 + sub, :] = _rms(x2) * fg_ref[...] if final else x2

    mixes = [project(r0) for r0 in starts]
    pro = [prologue(r0, mix) for r0, mix in zip(starts, mixes)]
    accs = [mlp(h2) for _, h2 in pro]
    for r0, (x1, _), acc in zip(starts, pro, accs):
        epilogue(r0, x1, acc)


def _post(x, mix, mod, mod_row, g2, w_a, w_b, w1, w2, layer, vpt, mod_next=None, g_next=None, final_g=None):
    is_attn = w_b is None
    emit_next = mod_next is not None
    final = final_g is not None
    ntok = x.shape[0]
    nt = ntok // TOKEN_TILE
    cpv = TOKEN_TILE // (vpt * CHUNK)
    n_virt = nt * vpt
    tile = pl.BlockSpec((TOKEN_TILE, D_MODEL), lambda i: (i, 0))
    row = pl.BlockSpec((1, D_MODEL), lambda i: (0, 0))
    modspec = pl.BlockSpec((None, N_MOD, D_MODEL), lambda i: (mod_row(i), 0, 0))
    resident = lambda shape: pl.BlockSpec(shape, lambda i: (0, 0), pipeline_mode=pl.Buffered(1))
    wsq = resident((D_MODEL, D_MODEL))
    gtile = pl.BlockSpec((N_GROUP_BLOCKS, cpv, vpt, CHUNK, LANES), lambda i: (0, 0, i, 0, 0))
    in_specs = [tile, tile if is_attn else gtile, modspec, row, wsq]
    args = [x, mix, mod, g2, w_a]
    if not is_attn:
        in_specs.append(wsq)
        args.append(w_b)
    in_specs += [pl.BlockSpec((None, D_MODEL, D_FF), lambda i: (layer, 0, 0), pipeline_mode=pl.Buffered(1)),
                 pl.BlockSpec((None, D_FF, D_MODEL), lambda i: (layer, 0, 0), pipeline_mode=pl.Buffered(1))]
    args += [w1, w2]
    if emit_next:
        in_specs += [modspec, row]
        args += [mod_next, g_next]
    if final:
        in_specs.append(row)
        args.append(final_g)
    out_specs = [tile]
    out_shape = [jax.ShapeDtypeStruct((ntok, D_MODEL), F32)]
    if emit_next:
        out_specs.append(gtile)
        out_shape.append(jax.ShapeDtypeStruct((N_GROUP_BLOCKS, cpv, n_virt, CHUNK, LANES), F32))
    return pl.pallas_call(
        functools.partial(_post_kernel, is_attn, emit_next, final, vpt),
        grid=(nt,),
        in_specs=in_specs,
        out_specs=out_specs,
        out_shape=out_shape,
        compiler_params=_cparams(("arbitrary",)),
        name="attn_proj_mlp" if is_attn else "glu_mlp_final",
    )(*args)


def _swap(x):
    return pltpu.roll(x, LANES // 2, 1)


def _cmul(z, w_r, w_i):
    return z * w_r + _swap(z) * w_i


def _multiplier(z, lo):
    zs = _swap(z)
    return jnp.where(lo, z, zs), jnp.where(lo, -zs, z)


def _rep_rows(x):
    return jnp.concatenate(
        [jnp.broadcast_to(x[g:g + 1, :], (GROUP_CH, LANES)) for g in range(GROUPS_PER_BLOCK)], axis=0)


def _s5_kernel(n_virt, n_seg, h_ref, lamr_ref, lami_ref, ldt_ref, bt_ref, cp_ref, dsk_ref, s0_ref, *refs):
    if n_seg == 1:
        y_ref, sfin_ref = refs[:2]
        refs = refs[2:]
    else:
        y_ref, sfin_ref = refs[0], None
        refs = refs[1:]
    f_scr, e_scr, k_scr, sf_scr, sb_scr = refs
    ntok = h_ref.shape[0]
    nc = ntok // CHUNK
    cpv = nc // n_virt
    gpb = GROUPS_PER_BLOCK

    lo8 = lax.broadcasted_iota(jnp.int32, (gpb, LANES), 1) < STATE_DIM
    lo = lax.broadcasted_iota(jnp.int32, (LANES, LANES), 1) < STATE_DIM
    conj = jnp.where(lo, 1.0, -1.0)
    row_g = lax.broadcasted_iota(jnp.int32, (LANES, STATE_LANES), 0) // GROUP_CH
    col = lax.broadcasted_iota(jnp.int32, (LANES, STATE_LANES), 1)
    diag_wide = row_g == 2 * (col // (2 * LANES)) + (col % LANES) // STATE_DIM
    diag = (lax.broadcasted_iota(jnp.int32, (LANES, LANES), 0) // GROUP_CH) == (
        lax.broadcasted_iota(jnp.int32, (LANES, LANES), 1) // GROUP_CH)
    pair_diag = (lax.broadcasted_iota(jnp.int32, (PAIR_CH, 2 * LANES), 0) // GROUP_CH) == (
        (lax.broadcasted_iota(jnp.int32, (PAIR_CH, 2 * LANES), 1) % LANES) // STATE_DIM)

    def split(w):
        ws = _swap(w)
        return jnp.where(lo, w, ws), jnp.where(lo, ws, w)

    def expand(w):
        w_re, w_im = split(w)
        wide = jnp.concatenate([w_re.astype(BF16), w_im.astype(BF16)] * N_PAIRS, axis=1)
        return jnp.where(diag_wide, wide, jnp.zeros((), BF16))

    decay = []
    lag = []
    for d in range(2):
        lam_r = lamr_ref[d]
        lam_i = lami_ref[d]
        dt = jnp.exp(ldt_ref[d])
        mag = jnp.exp(lam_r * dt)
        ang = lam_i * dt
        a_r = mag * jnp.cos(ang)
        a_im = mag * jnp.sin(ang)
        a_i = jnp.where(lo8, -a_im, a_im)
        den = lam_r * lam_r + lam_i * lam_i
        num = jnp.where(lo8, a_r - 1.0, a_im)
        f = _cmul(num, lam_r / den, jnp.where(lo8, lam_i, -lam_i) / den)
        pw = [jnp.where(lo8, 1.0, 0.0)]
        for _ in range(CHUNK):
            pw.append(_cmul(pw[-1], a_r, a_i))
        decay.append(pw[CHUNK])
        pw = [_rep_rows(p) for p in pw]
        f_r, f_i = _multiplier(_rep_rows(f), lo)
        bb_r, bb_i = _multiplier(_cmul(bt_ref[d], f_r, f_i), lo)
        c_r, c_i = _multiplier(cp_ref[d], lo)
        cm = (cp_ref[d] * conj).astype(BF16)
        fpow32 = [_cmul(p, bb_r, bb_i) for p in pw[:CHUNK]]
        fpow = [w.astype(BF16) for w in fpow32]
        for j in range(CHUNK):
            e = (CHUNK - 1 - j) if d == 0 else j
            w_re, w_im = split(fpow32[e])
            w = jnp.concatenate([w_re.astype(BF16), w_im.astype(BF16)], axis=1)
            for q in range(N_PAIRS):
                f_scr[d, q, j * PAIR_CH:(j + 1) * PAIR_CH, :] = jnp.where(
                    pair_diag, w[q * PAIR_CH:(q + 1) * PAIR_CH, :], jnp.zeros((), BF16))
        for t in range(CHUNK):
            e = (t + 1) if d == 0 else (CHUNK - t)
            w = _cmul(pw[e], c_r, c_i) * conj
            e_scr[d, t * LANES:(t + 1) * LANES, :] = expand(w)
        lag.append([jnp.where(diag, _dot_nt(fp, cm), 0.0) for fp in fpow])

    for j in range(CHUNK):
        for t in range(CHUNK):
            k = t - j
            tile = lag[0][k] if k > 0 else (lag[1][-k] if k < 0 else lag[0][0] + lag[1][0])
            k_scr[j * LANES:(j + 1) * LANES, t * LANES:(t + 1) * LANES] = tile.astype(BF16)

    xs = [h_ref[pl.ds(j, nc, stride=CHUNK), :] for j in range(CHUNK)]
    xcat = jnp.concatenate([x.astype(BF16) for x in xs], axis=1)

    lane_blk = lax.broadcasted_iota(jnp.int32, (nc, LANES), 1) // PAIR_CH
    per_col = LANES // PAIR_CH
    for q in range(N_PAIRS):
        cols = []
        for half in range(CHUNK // per_col):
            col = jnp.zeros((nc, LANES), F32)
            for jj in range(per_col):
                x = xs[half * per_col + jj]
                shift = ((jj - q) * PAIR_CH) % LANES
                moved = pltpu.roll(x, shift, 1) if shift else x
                col = jnp.where(lane_blk == jj, moved, col)
            cols.append(col.astype(BF16))
        xq = jnp.concatenate(cols, axis=1)
        for d, scr in ((0, sf_scr), (1, sb_scr)):
            loc_pair = _dot(xq, f_scr[d, q])
            scr[2 * q] = loc_pair[:, :LANES]
            scr[2 * q + 1] = loc_pair[:, LANES:]
    y_within = _dot(xcat, k_scr[...])

    sgn8 = jnp.where(lo8, -1.0, 1.0)

    def dup(z):
        zs = _swap(z)
        return jnp.where(lo8, z, zs), jnp.where(lo8, zs, z)

    def pair_rows(z):
        return jnp.concatenate(
            [jnp.where(lo8[:1], z[2 * q:2 * q + 1, :], z[2 * q + 1:2 * q + 2, :]) for q in range(N_PAIRS)], axis=0)

    def scan(scr, d, reverse):
        a_re, a_im = (pair_rows(z) for z in dup(decay[d]))
        a_r = [a_re[q:q + 1, :] for q in range(N_PAIRS)]
        a_i = [a_im[q:q + 1, :] for q in range(N_PAIRS)]

        def rows_of(i):
            c = (cpv - 1 - i) if reverse else i
            return pl.ds(c * n_virt, n_virt)

        def body(i, carry):
            s_re, s_im = carry
            rows = rows_of(i)
            new_re, new_im = [], []
            for q in range(N_PAIRS):
                loc_re = scr[2 * q, rows, :]
                loc_im = scr[2 * q + 1, rows, :]
                scr[2 * q, rows, :] = s_re[q]
                scr[2 * q + 1, rows, :] = s_im[q]
                new_re.append(a_r[q] * s_re[q] - a_i[q] * s_im[q] + loc_re)
                new_im.append(a_r[q] * s_im[q] + a_i[q] * s_re[q] + loc_im)
            return tuple(new_re), tuple(new_im)

        carry = tuple(tuple(s0_ref[d, :, (2 * q + r) * LANES:(2 * q + r + 1) * LANES] for q in range(N_PAIRS))
                      for r in range(2))
        for i in range(cpv):
            carry = body(i, carry)
        fin_re, fin_im = carry
        if n_seg == 1:
            for q in range(N_PAIRS):
                sfin_ref[d, :, 2 * q * LANES:(2 * q + 1) * LANES] = fin_re[q]
                sfin_ref[d, :, (2 * q + 1) * LANES:(2 * q + 2) * LANES] = fin_im[q]
            return

        p = decay[d]
        for _ in range(cpv.bit_length() - 1):
            p_re, p_im = dup(p)
            p = _cmul(p, p_re, p_im * sgn8)
        v_re, v_im = (pair_rows(z) for z in dup(p))
        seg = lax.broadcasted_iota(jnp.int32, (n_virt, LANES), 0) & (n_seg - 1)
        has_pred = seg != ((n_seg - 1) if reverse else 0)
        shift = (n_virt - 1) if reverse else 1
        cin_re, cin_im = [], []
        for q in range(N_PAIRS):
            c_re = jnp.zeros((n_virt, LANES), F32)
            c_im = jnp.zeros((n_virt, LANES), F32)
            for _ in range(n_seg - 1):
                nxt_re = fin_re[q] + c_re * v_re[q:q + 1, :] - c_im * v_im[q:q + 1, :]
                nxt_im = fin_im[q] + c_re * v_im[q:q + 1, :] + c_im * v_re[q:q + 1, :]
                c_re = jnp.where(has_pred, pltpu.roll(nxt_re, shift, 0), 0.0)
                c_im = jnp.where(has_pred, pltpu.roll(nxt_im, shift, 0), 0.0)
            cin_re.append(c_re)
            cin_im.append(c_im)

        def fix(i, carry):
            q_re, q_im = carry
            rows = rows_of(i)
            for q in range(N_PAIRS):
                m_re = q_re[q:q + 1, :]
                m_im = q_im[q:q + 1, :]
                scr[2 * q, rows, :] += cin_re[q] * m_re - cin_im[q] * m_im
                scr[2 * q + 1, rows, :] += cin_re[q] * m_im + cin_im[q] * m_re
            return q_re * a_re - q_im * a_im, q_re * a_im + q_im * a_re

        mult = (jnp.ones((N_PAIRS, LANES), F32), jnp.zeros((N_PAIRS, LANES), F32))
        for i in range(cpv):
            mult = fix(i, mult)

    scan(sf_scr, 0, False)
    scan(sb_scr, 1, True)

    s_f = jnp.concatenate([sf_scr[k].astype(BF16) for k in range(gpb)], axis=1)
    s_b = jnp.concatenate([sb_scr[k].astype(BF16) for k in range(gpb)], axis=1)
    yall = y_within + _dot_nt(s_f, e_scr[0]) + _dot_nt(s_b, e_scr[1])
    dsk = dsk_ref[...]
    for t in range(CHUNK):
        rows = pl.ds(t, nc, stride=CHUNK)
        y_ref[rows, :] = yall[:, t * LANES:(t + 1) * LANES] + h_ref[rows, :] * dsk


def _s5_params(lam_re, lam_im, log_dt, b_re, b_im, c_re, c_im):
    lamr = jnp.concatenate([lam_re, lam_re], axis=-1).astype(F32)
    lami = jnp.concatenate([lam_im, lam_im], axis=-1).astype(F32)
    ldt = jnp.broadcast_to(log_dt.astype(F32)[..., None], lamr.shape)
    bt = jnp.concatenate([b_re.transpose(0, 1, 3, 2), b_im.transpose(0, 1, 3, 2)], axis=-1)
    cp = jnp.concatenate([c_re, c_im], axis=-1)
    return (lamr, lami, ldt, bt.reshape(2, D_MODEL, LANES).astype(F32), cp.reshape(2, D_MODEL, LANES).astype(F32))


def _s5(h, params, d_skip, s0, n_seg):
    lamr, lami, ldt, bt, cp = params
    _, cpv, n_virt, _, _ = h.shape
    assert cpv & (cpv - 1) == 0 and n_virt % SUBLANES == 0 and n_seg & (n_seg - 1) == 0
    nc = cpv * n_virt
    ntok = nc * CHUNK
    hspec = pl.BlockSpec((None, ntok, LANES), lambda g: (g, 0, 0))
    kdim = CHUNK * LANES
    gspec = pl.BlockSpec((2, GROUPS_PER_BLOCK, LANES), lambda g: (0, g, 0))
    rspec = pl.BlockSpec((2, LANES, LANES), lambda g: (0, g, 0))
    sspec = pl.BlockSpec((None, 2, n_virt, STATE_LANES), lambda g: (g, 0, 0, 0))
    state_scr = pltpu.VMEM((GROUPS_PER_BLOCK, nc, LANES), F32)
    out_specs = [hspec]
    out_shape = [jax.ShapeDtypeStruct((N_GROUP_BLOCKS, ntok, LANES), F32)]
    if n_seg == 1:
        out_specs.append(sspec)
        out_shape.append(jax.ShapeDtypeStruct((N_GROUP_BLOCKS, 2, n_virt, STATE_LANES), F32))
    outs = pl.pallas_call(
        functools.partial(_s5_kernel, n_virt, n_seg),
        grid=(N_GROUP_BLOCKS,),
        in_specs=[
            hspec,
            gspec, gspec, gspec, rspec, rspec,
            pl.BlockSpec((1, LANES), lambda g: (0, g)),
            sspec,
        ],
        out_specs=out_specs,
        out_shape=out_shape,
        scratch_shapes=[
            pltpu.VMEM((2, N_PAIRS, CHUNK * PAIR_CH, 2 * LANES), BF16),
            pltpu.VMEM((2, kdim, STATE_LANES), BF16),
            pltpu.VMEM((kdim, kdim), BF16),
            state_scr, state_scr,
        ],
        compiler_params=_cparams(("arbitrary",)),
        name="s5_chunked_scan",
    )(h.reshape(N_GROUP_BLOCKS, ntok, LANES), lamr, lami, ldt, bt, cp, d_skip, s0)
    y = outs[0].reshape(h.shape)
    return (y, outs[1]) if n_seg == 1 else (y, None)


def _state_to_blocks(s):
    b = s.shape[0]
    s = s.reshape(b, 2, 2, N_GROUP_BLOCKS, N_PAIRS, 2, STATE_DIM)
    return s.transpose(3, 1, 0, 4, 2, 5, 6).reshape(N_GROUP_BLOCKS, 2, b, STATE_LANES)


def _blocks_to_state(s):
    b = s.shape[2]
    s = s.reshape(N_GROUP_BLOCKS, 2, b, N_PAIRS, 2, 2, STATE_DIM)
    return s.transpose(2, 1, 4, 0, 3, 5, 6).reshape(b, 2, 2, N_GROUPS, STATE_DIM)


def _rope_tables(n_tokens):
    pos = np.arange(n_tokens)
    n_freq = HEAD_DIM // 4
    freqs = ROPE_BASE ** (-np.arange(n_freq, dtype=np.float64) / n_freq)
    ang_r = (pos // GRID_W)[:, None] * freqs
    ang_c = (pos % GRID_W)[:, None] * freqs
    cos_h = np.concatenate([np.cos(ang_r), np.cos(ang_r), np.cos(ang_c), np.cos(ang_c)], axis=1)
    sin_h = np.concatenate([-np.sin(ang_r), np.sin(ang_r), -np.sin(ang_c), np.sin(ang_c)], axis=1)
    return jnp.asarray(np.tile(cos_h, (1, 2)), F32), jnp.asarray(np.tile(sin_h, (1, 2)), F32)


def kernel(x_prompt, x_sample, cache_k, cache_v, state_ssm, c, c_ctx, norm1_g, norm2_g, w_mod, b_mod,
           w_qkv, w_o, attn_sink, ssm_lam_re, ssm_lam_im, ssm_log_dt, ssm_b_re, ssm_b_im, ssm_c_re,
           ssm_c_im, ssm_d, glu_w_a, glu_w_b, mlp_w1, mlp_w2, final_norm_g):
    bp, lp, _ = x_prompt.shape
    bx, lx, _ = x_sample.shape
    assert lx % TOKEN_TILE == 0 and (bp * lp) % TOKEN_TILE == 0
    tiles_per_lat = lx // TOKEN_TILE

    xp = x_prompt.reshape(bp * lp, D_MODEL)
    xx = x_sample.reshape(bx * lx, D_MODEL)

    cvecs = jnp.zeros((8, D_MODEL), F32).at[0].set(c_ctx).at[1:1 + bx].set(c)
    mod = _modulation(cvecs, w_mod, b_mod)

    ctx_row = lambda i: 0
    lat_row = lambda i: 1 + i // tiles_per_lat

    assert lx % QKV_TILE == 0 and (bp * lp) % QKV_TILE == 0
    qkv_tiles_per_lat = lx // QKV_TILE
    rope = _rope_tables(lx) + (lambda i: i % qkv_tiles_per_lat,)
    wqkv = w_qkv[0].astype(BF16)
    g1 = norm1_g[0].reshape(1, D_MODEL)
    sink = attn_sink[0].astype(F32)
    qp, krp, vrp, kp, vp = _qkv(xp, mod[0], ctx_row, g1, wqkv, None, lp)
    qx, krx, vrx = _qkv(xx, mod[0], lambda i: 1 + i // qkv_tiles_per_lat, g1, wqkv, rope, 0)
    op = _ctx_attention(sink, qp, krp, vrp, bp, lp)
    rep = lambda t: jnp.tile(t[:, 0].transpose(0, 2, 1, 3), (1, 1, 1, LANES // HEAD_DIM)).astype(BF16)
    ox, (w1, w2, wo, wa, wb) = _lat_attention(
        sink, qx, krx, vrx, rep(cache_k), rep(cache_v), bx, lx,
        [mlp_w1, mlp_w2, w_o[0], glu_w_a[0], glu_w_b[0]])
    g2 = norm2_g.reshape(-1, 1, D_MODEL)
    gn = norm1_g[1].reshape(1, D_MODEL)
    vpt_p = TOKEN_TILE // lp
    vpt_x = 1
    n_seg_x = tiles_per_lat
    xp, hp = _post(xp, op, mod[0], ctx_row, g2[0], wo, None, w1, w2, 0, vpt_p, mod_next=mod[1], g_next=gn)
    xx, hx = _post(xx, ox, mod[0], lat_row, g2[0], wo, None, w1, w2, 0, vpt_x, mod_next=mod[1], g_next=gn)

    params = _s5_params(ssm_lam_re[0], ssm_lam_im[0], ssm_log_dt[0], ssm_b_re[0], ssm_b_im[0],
                        ssm_c_re[0], ssm_c_im[0])
    dsk = ssm_d[0].astype(F32).reshape(1, D_MODEL)
    s0p = jnp.zeros((N_GROUP_BLOCKS, 2, bp, STATE_LANES), F32)
    sx = _state_to_blocks(state_ssm[:, 0].astype(F32))
    s0x = jnp.zeros((N_GROUP_BLOCKS, 2, bx, n_seg_x, STATE_LANES), F32)
    s0x = s0x.at[:, 0, :, 0].set(sx[:, 0]).at[:, 1, :, n_seg_x - 1].set(sx[:, 1])
    s0x = s0x.reshape(N_GROUP_BLOCKS, 2, bx * n_seg_x, STATE_LANES)
    yp, sfin = _s5(hp, params, dsk, s0p, 1)
    yx, _ = _s5(hx, params, dsk, s0x, n_seg_x)
    new_state = _blocks_to_state(sfin)[:, None]

    fg = final_norm_g.reshape(1, D_MODEL)
    (yp_out,) = _post(xp, yp, mod[1], ctx_row, g2[1], wa, wb, w1, w2, 1, vpt_p, final_g=fg)
    (yx_out,) = _post(xx, yx, mod[1], lat_row, g2[1], wa, wb, w1, w2, 1, vpt_x, final_g=fg)

    to_cache = lambda t: t.reshape(bp, N_KV_HEADS, HEAD_DIM, lp).transpose(0, 3, 1, 2)[:, None]
    new_k = to_cache(kp)
    new_v = to_cache(vp)
    return (yp_out.reshape(bp, lp, D_MODEL), yx_out.reshape(bx, lx, D_MODEL), new_k, new_v, new_state)
```

```python
import functools
import math

import numpy as np
import jax
import jax.numpy as jnp
from jax import lax
from jax.experimental import pallas as pl
from jax.experimental.pallas import tpu as pltpu

F32 = jnp.float32
BF16 = jnp.bfloat16

D_MODEL = 1024
N_HEADS = 16
N_KV_HEADS = 4
HEAD_DIM = 64
Q_PER_KV = N_HEADS // N_KV_HEADS
KV_DIM = N_KV_HEADS * HEAD_DIM
QKV_DIM = D_MODEL + 2 * KV_DIM
BLOCK = 128
GRID_W = 64
ROPE_BASE = 10000.0
ROT_HALF = HEAD_DIM // 4
ATTN_SCALE = HEAD_DIM ** -0.5
N_GROUPS = 64
GROUP_CH = 16
STATE_DIM = 64
D_FF = 4 * D_MODEL
N_MOD = 6
RMS_EPS = 1e-6
NEG_INF = -1e30

LANES = 128
SUBLANES = 8
GROUPS_PER_BLOCK = LANES // GROUP_CH
N_GROUP_BLOCKS = N_GROUPS // GROUPS_PER_BLOCK
PAIR_CH = 2 * GROUP_CH
N_PAIRS = GROUPS_PER_BLOCK // 2
STATE_LANES = GROUPS_PER_BLOCK * 2 * STATE_DIM
CHUNK = SUBLANES
TOKEN_TILE = 512
FF_TILE = 1024
POST_SUBTILES = 2
QKV_TILE = 1024
QKV_SUBTILES = 4
VMEM_LIMIT = 56 * 1024 * 1024


def _cparams(semantics):
    return pltpu.CompilerParams(dimension_semantics=semantics, vmem_limit_bytes=VMEM_LIMIT)


def _rms(x):
    return x * lax.rsqrt(jnp.mean(x * x, axis=-1, keepdims=True) + RMS_EPS)


def _dot(a, b):
    return jnp.dot(a, b, preferred_element_type=F32)


def _dot_nt(a, b):
    return lax.dot_general(a, b, (((1,), (1,)), ((), ())), preferred_element_type=F32)


def _mod_kernel(cv_ref, w_ref, b_ref, o_ref):
    cv = cv_ref[...]
    s = (cv * jax.nn.sigmoid(cv)).astype(BF16)
    o_ref[0] = _dot(s, w_ref[0].astype(BF16)) + b_ref[0]


def _modulation(cvecs, w_mod, b_mod):
    depth = w_mod.shape[0]
    width = 2 * D_MODEL
    out = pl.pallas_call(
        _mod_kernel,
        grid=(depth, N_MOD * D_MODEL // width),
        in_specs=[
            pl.BlockSpec((8, D_MODEL), lambda l, j: (0, 0)),
            pl.BlockSpec((1, D_MODEL, width), lambda l, j: (l, 0, j)),
            pl.BlockSpec((1, 1, width), lambda l, j: (l, 0, j)),
        ],
        out_specs=pl.BlockSpec((1, 8, width), lambda l, j: (l, 0, j)),
        out_shape=jax.ShapeDtypeStruct((depth, 8, N_MOD * D_MODEL), F32),
        compiler_params=_cparams(("arbitrary", "arbitrary")),
        name="modulation",
    )(cvecs, w_mod, b_mod.reshape(depth, 1, N_MOD * D_MODEL))
    return out.reshape(depth, 8, N_MOD, D_MODEL)


def _head_pair(blk, odd):
    lo = lax.broadcasted_iota(jnp.int32, blk.shape, 1) < HEAD_DIM
    other = pltpu.roll(blk, HEAD_DIM, 1)
    return (jnp.where(lo, other, blk) if odd else jnp.where(lo, blk, other)).astype(BF16)


def _qkv_kernel(cache_seq, rope, x_ref, mod_ref, g_ref, w_ref, *refs):
    refs = list(refs)
    cos_ref, sin_ref = (refs.pop(0), refs.pop(0)) if rope else (None, None)
    q_ref, krep_ref, vrep_ref = refs[:3]
    kv_refs = refs[3:]
    sub = QKV_TILE // QKV_SUBTILES
    starts = [k * sub for k in range(QKV_SUBTILES)]

    def project(r0):
        h = _rms(x_ref[r0:r0 + sub, :]) * g_ref[...] * (1.0 + mod_ref[1:2, :]) + mod_ref[0:1, :]
        return _dot(h.astype(BF16), w_ref[...])

    def emit_cache(ref, blk, r0, c0):
        for s in range(sub // cache_seq):
            ref[r0 // cache_seq + s, c0:c0 + LANES, :] = blk[s * cache_seq:(s + 1) * cache_seq, :].T

    def finish(r0, qkv):
        rows = slice(r0, r0 + sub)
        if rope:
            cos = cos_ref[rows, :]
            sin = sin_ref[rows, :]
            lane = lax.broadcasted_iota(jnp.int32, cos.shape, 1)
            first = (lane & (2 * ROT_HALF - 1)) < ROT_HALF
        for blk in range((D_MODEL + KV_DIM) // LANES):
            r = qkv[:, blk * LANES:(blk + 1) * LANES]
            if rope:
                partner = jnp.where(first, pltpu.roll(r, LANES - ROT_HALF, 1), pltpu.roll(r, ROT_HALF, 1))
                r = r * cos + partner * sin
            if blk < D_MODEL // LANES:
                q_ref[rows, blk * LANES:(blk + 1) * LANES] = (r * ATTN_SCALE).astype(BF16)
            else:
                c0 = blk * LANES - D_MODEL
                if cache_seq:
                    emit_cache(kv_refs[0], r, r0, c0)
                for half in range(2):
                    krep_ref[c0 // HEAD_DIM + half, rows, :] = _head_pair(r, half)
        v = qkv[:, D_MODEL + KV_DIM:]
        for c0 in range(0, KV_DIM, LANES):
            blk = v[:, c0:c0 + LANES]
            if cache_seq:
                emit_cache(kv_refs[1], blk, r0, c0)
            for half in range(2):
                vrep_ref[c0 // HEAD_DIM + half, rows, :] = _head_pair(blk, half)

    for r0, qkv in zip(starts, [project(r0) for r0 in starts]):
        finish(r0, qkv)


def _qkv(x, mod, mod_row, g, w_qkv, rope, cache_seq):
    ntok = x.shape[0]
    nt = ntok // QKV_TILE
    emit_kv = cache_seq > 0
    assert not emit_kv or (QKV_TILE // QKV_SUBTILES) % cache_seq == 0
    rep_spec = pl.BlockSpec((N_KV_HEADS, QKV_TILE, LANES), lambda i: (0, i, 0))
    rep_shape = jax.ShapeDtypeStruct((N_KV_HEADS, ntok, LANES), BF16)
    in_specs = [
        pl.BlockSpec((QKV_TILE, D_MODEL), lambda i: (i, 0)),
        pl.BlockSpec((None, N_MOD, D_MODEL), lambda i: (mod_row(i), 0, 0)),
        pl.BlockSpec((1, D_MODEL), lambda i: (0, 0)),
        pl.BlockSpec((D_MODEL, QKV_DIM), lambda i: (0, 0)),
    ]
    args = [x, mod, g, w_qkv]
    if rope is not None:
        cos_t, sin_t, rope_blk = rope
        in_specs += [pl.BlockSpec((QKV_TILE, LANES), lambda i: (rope_blk(i), 0))] * 2
        args += [cos_t, sin_t]
    out_specs = [pl.BlockSpec((QKV_TILE, D_MODEL), lambda i: (i, 0)), rep_spec, rep_spec]
    out_shape = [jax.ShapeDtypeStruct((ntok, D_MODEL), BF16), rep_shape, rep_shape]
    if emit_kv:
        spt = QKV_TILE // cache_seq
        out_specs += [pl.BlockSpec((spt, KV_DIM, cache_seq), lambda i: (i, 0, 0))] * 2
        out_shape += [jax.ShapeDtypeStruct((ntok // cache_seq, KV_DIM, cache_seq), F32)] * 2
    return pl.pallas_call(
        functools.partial(_qkv_kernel, cache_seq, rope is not None),
        grid=(nt,),
        in_specs=in_specs,
        out_specs=out_specs,
        out_shape=out_shape,
        compiler_params=_cparams(("arbitrary",)),
        name="norm_qkv_rope",
    )(*args)


def _group_scores(q_ref, kv, key_parts, bias):
    nq = q_ref.shape[0]
    lo = lax.broadcasted_iota(jnp.int32, (nq, LANES), 1) < HEAD_DIM
    zero = jnp.zeros((), BF16)
    rows = []
    for b in range(KV_DIM // LANES):
        blk = q_ref[:, kv * KV_DIM + b * LANES:kv * KV_DIM + (b + 1) * LANES]
        rows += [jnp.where(lo, blk, zero), jnp.where(lo, zero, blk)]
    q4 = jnp.concatenate(rows, axis=0)
    parts = [_dot_nt(q4, keys) for keys in key_parts]
    if bias is not None:
        s0 = parts[0].reshape(Q_PER_KV, nq, -1) + bias[None]
        parts[0] = s0.reshape(Q_PER_KV * nq, -1)
    return parts


def _group_softmax(parts, sink_ref, kv):
    nq = parts[0].shape[0] // Q_PER_KV
    sink = jnp.concatenate(
        [jnp.full((nq, LANES), sink_ref[kv * Q_PER_KV + g], F32) for g in range(Q_PER_KV)], axis=0)
    blocks = [[s[:, j:j + LANES] for j in range(0, s.shape[1], LANES)] for s in parts]
    fold = None
    for b in sum(blocks, []):
        fold = b if fold is None else jnp.maximum(fold, b)
    m = jnp.maximum(jnp.max(fold, axis=-1, keepdims=True), sink)
    probs = [[jnp.exp(b - m) for b in bs] for bs in blocks]
    fold = None
    for p in sum(probs, []):
        fold = p if fold is None else fold + p
    den = jnp.sum(fold, axis=-1, keepdims=True) + jnp.exp(sink - m)
    return [jnp.concatenate(ps, axis=1).astype(BF16) for ps in probs], 1.0 / den


def _group_output(probs, inv_den, value_parts, o_ref, kv):
    nq = probs[0].shape[0] // Q_PER_KV
    r = _dot(probs[0], value_parts[0])
    for p, v in zip(probs[1:], value_parts[1:]):
        r = r + _dot(p, v)
    r = r * inv_den
    lo = lax.broadcasted_iota(jnp.int32, (nq, LANES), 1) < HEAD_DIM
    for b in range(KV_DIM // LANES):
        pair = jnp.where(lo, r[2 * b * nq:(2 * b + 1) * nq], r[(2 * b + 1) * nq:(2 * b + 2) * nq])
        o_ref[:, kv * KV_DIM + b * LANES:kv * KV_DIM + (b + 1) * LANES] = pair.astype(BF16)


def _attend(q_ref, sink_ref, o_ref, keys_of, values_of, bias):
    s_next = _group_scores(q_ref, 0, keys_of(0), bias)
    for kv in range(N_KV_HEADS):
        s = s_next
        if kv + 1 < N_KV_HEADS:
            s_next = _group_scores(q_ref, kv + 1, keys_of(kv + 1), bias)
        probs, inv_den = _group_softmax(s, sink_ref, kv)
        _group_output(probs, inv_den, values_of(kv), o_ref, kv)


def _ctx_attn_kernel(sink_ref, q_ref, k_ref, v_ref, o_ref):
    _attend(q_ref, sink_ref, o_ref, lambda kv: [k_ref[kv]], lambda kv: [v_ref[kv]], None)


def _ctx_attention(sink, q, krep, vrep, n_batch, seq):
    rep_spec = pl.BlockSpec((N_KV_HEADS, seq, LANES), lambda b: (0, b, 0))
    return pl.pallas_call(
        _ctx_attn_kernel,
        grid=(n_batch,),
        in_specs=[
            pl.BlockSpec(memory_space=pltpu.SMEM),
            pl.BlockSpec((seq, D_MODEL), lambda b: (b, 0)),
            rep_spec, rep_spec,
        ],
        out_specs=pl.BlockSpec((seq, D_MODEL), lambda b: (b, 0)),
        out_shape=jax.ShapeDtypeStruct((n_batch * seq, D_MODEL), BF16),
        compiler_params=_cparams(("arbitrary",)),
        name="context_attention",
    )(sink, q, krep, vrep)


def _window_start(n, seq):
    return jnp.clip((n - 1) * BLOCK, 0, seq - 3 * BLOCK)


def _band_bias():
    r = np.arange(BLOCK)[:, None]
    j = np.arange(3 * BLOCK)[None, :]
    out = [np.where(np.abs(j - d * BLOCK - r) <= BLOCK, 0.0, NEG_INF) for d in range(3)]
    return jnp.asarray(np.stack(out), F32)


def _lat_attn_kernel(seq, n_cast, sink_ref, q_ref, k_ref, v_ref, ck_ref, cv_ref, bias_ref, *refs):
    o_ref = refs[n_cast]
    for src, dst in zip(refs[:n_cast], refs[n_cast + 1:]):
        dst[...] = src[...].astype(BF16)
    win = 3 * BLOCK
    start = pl.multiple_of(_window_start(pl.program_id(1), seq), BLOCK)
    keys_of = lambda kv: [k_ref[kv, pl.ds(start, win), :], ck_ref[kv]]
    values_of = lambda kv: [v_ref[kv, pl.ds(start, win), :], cv_ref[kv]]
    _attend(q_ref, sink_ref, o_ref, keys_of, values_of, bias_ref[...])


def _lat_attention(sink, q, krep, vrep, ckrep, cvrep, n_batch, seq, cast_weights):
    nb = seq // BLOCK
    steps = n_batch * nb
    past = ckrep.shape[2]
    rep_spec = pl.BlockSpec((N_KV_HEADS, seq, LANES), lambda b, n: (0, b, 0))
    crep_spec = pl.BlockSpec((None, N_KV_HEADS, past, LANES), lambda b, n: (b, 0, 0, 0))
    flat = [w.reshape(-1, w.shape[-1]) for w in cast_weights]
    assert all(w.shape[0] % (steps * 2 * SUBLANES) == 0 for w in flat)
    slabs = [pl.BlockSpec((w.shape[0] // steps, w.shape[1]), lambda b, n: (b * nb + n, 0)) for w in flat]
    outs = pl.pallas_call(
        functools.partial(_lat_attn_kernel, seq, len(flat)),
        grid=(n_batch, nb),
        in_specs=[
            pl.BlockSpec(memory_space=pltpu.SMEM),
            pl.BlockSpec((BLOCK, D_MODEL), lambda b, n: (b * nb + n, 0)),
            rep_spec, rep_spec, crep_spec, crep_spec,
            pl.BlockSpec((None, BLOCK, 3 * BLOCK), lambda b, n: (n - _window_start(n, seq) // BLOCK, 0, 0)),
        ] + slabs,
        out_specs=[pl.BlockSpec((BLOCK, D_MODEL), lambda b, n: (b * nb + n, 0))] + slabs,
        out_shape=[jax.ShapeDtypeStruct((n_batch * seq, D_MODEL), BF16)]
        + [jax.ShapeDtypeStruct(w.shape, BF16) for w in flat],
        compiler_params=_cparams(("arbitrary", "arbitrary")),
        name="latent_attention",
    )(sink, q, krep, vrep, ckrep, cvrep, _band_bias(), *flat)
    return outs[0], [o.reshape(w.shape) for o, w in zip(outs[1:], cast_weights)]


def _gelu_tanh(x):
    c = math.sqrt(2.0 / math.pi)
    return x * (0.5 * (1.0 + jnp.tanh(c * (x + 0.044715 * (x * x * x)))))


def _post_kernel(is_attn, emit_next, final, vpt, *refs):
    rows_per_v = TOKEN_TILE // vpt
    refs = list(refs)
    x_ref, mix_ref, mod_ref, g2_ref, wa_ref = refs[:5]
    refs = refs[5:]
    wb_ref = None if is_attn else refs.pop(0)
    w1_ref, w2_ref = refs[:2]
    refs = refs[2:]
    modn_ref = gn_ref = fg_ref = hn_ref = None
    if emit_next:
        modn_ref, gn_ref = refs[:2]
        refs = refs[2:]
    if final:
        fg_ref = refs.pop(0)
    xo_ref = refs.pop(0)
    if emit_next:
        hn_ref = refs.pop(0)
    assert not refs

    sub = TOKEN_TILE // POST_SUBTILES
    assert rows_per_v % sub == 0
    n_chunk = sub // CHUNK
    starts = [k * sub for k in range(POST_SUBTILES)]

    def s5_slot(r0):
        return r0 // rows_per_v, (r0 % rows_per_v) // CHUNK

    def project(r0):
        if is_attn:
            return _dot(mix_ref[r0:r0 + sub, :], wa_ref[...])
        s, c0 = s5_slot(r0)
        y = jnp.concatenate(
            [mix_ref[g, c0:c0 + n_chunk, s].reshape(sub, LANES) for g in range(N_GROUP_BLOCKS)], axis=1)
        yg = _gelu_tanh(y).astype(BF16)
        return _dot(yg, wa_ref[...]) * jax.nn.sigmoid(_dot(yg, wb_ref[...]))

    def prologue(r0, mix):
        x1 = x_ref[r0:r0 + sub, :] + mod_ref[2:3, :] * mix
        h2 = _rms(x1) * g2_ref[...] * (1.0 + mod_ref[4:5, :]) + mod_ref[3:4, :]
        return x1, h2.astype(BF16)

    def mlp(h2):
        acc = None
        for c in range(D_FF // FF_TILE):
            a = jnp.maximum(_dot(h2, w1_ref[:, c * FF_TILE:(c + 1) * FF_TILE]), 0.0)
            t = _dot((a * a).astype(BF16), w2_ref[c * FF_TILE:(c + 1) * FF_TILE, :])
            acc = t if acc is None else acc + t
        return acc

    def epilogue(r0, x1, acc):
        x2 = x1 + mod_ref[5:6, :] * acc
        if emit_next:
            hn = _rms(x2) * gn_ref[...] * (1.0 + modn_ref[1:2, :]) + modn_ref[0:1, :]
            s, c0 = s5_slot(r0)
            for g in range(N_GROUP_BLOCKS):
                blk = hn[:, g * LANES:(g + 1) * LANES]
                hn_ref[g, c0:c0 + n_chunk, s] = blk.reshape(n_chunk, CHUNK, LANES)
        xo_ref[r0:r0 + sub, :] = _rms(x2) * fg_ref[...] if final else x2

    mixes = [project(r0) for r0 in starts]
    pro = [prologue(r0, mix) for r0, mix in zip(starts, mixes)]
    accs = [mlp(h2) for _, h2 in pro]
    for r0, (x1, _), acc in zip(starts, pro, accs):
        epilogue(r0, x1, acc)


def _post(x, mix, mod, mod_row, g2, w_a, w_b, w1, w2, layer, vpt, mod_next=None, g_next=None, final_g=None):
    is_attn = w_b is None
    emit_next = mod_next is not None
    final = final_g is not None
    ntok = x.shape[0]
    nt = ntok // TOKEN_TILE
    cpv = TOKEN_TILE // (vpt * CHUNK)
    n_virt = nt * vpt
    tile = pl.BlockSpec((TOKEN_TILE, D_MODEL), lambda i: (i, 0))
    row = pl.BlockSpec((1, D_MODEL), lambda i: (0, 0))
    modspec = pl.BlockSpec((None, N_MOD, D_MODEL), lambda i: (mod_row(i), 0, 0))
    resident = lambda shape: pl.BlockSpec(shape, lambda i: (0, 0), pipeline_mode=pl.Buffered(1))
    wsq = resident((D_MODEL, D_MODEL))
    gtile = pl.BlockSpec((N_GROUP_BLOCKS, cpv, vpt, CHUNK, LANES), lambda i: (0, 0, i, 0, 0))
    in_specs = [tile, tile if is_attn else gtile, modspec, row, wsq]
    args = [x, mix, mod, g2, w_a]
    if not is_attn:
        in_specs.append(wsq)
        args.append(w_b)
    in_specs += [pl.BlockSpec((None, D_MODEL, D_FF), lambda i: (layer, 0, 0), pipeline_mode=pl.Buffered(1)),
                 pl.BlockSpec((None, D_FF, D_MODEL), lambda i: (layer, 0, 0), pipeline_mode=pl.Buffered(1))]
    args += [w1, w2]
    if emit_next:
        in_specs += [modspec, row]
        args += [mod_next, g_next]
    if final:
        in_specs.append(row)
        args.append(final_g)
    out_specs = [tile]
    out_shape = [jax.ShapeDtypeStruct((ntok, D_MODEL), F32)]
    if emit_next:
        out_specs.append(gtile)
        out_shape.append(jax.ShapeDtypeStruct((N_GROUP_BLOCKS, cpv, n_virt, CHUNK, LANES), F32))
    return pl.pallas_call(
        functools.partial(_post_kernel, is_attn, emit_next, final, vpt),
        grid=(nt,),
        in_specs=in_specs,
        out_specs=out_specs,
        out_shape=out_shape,
        compiler_params=_cparams(("arbitrary",)),
        name="attn_proj_mlp" if is_attn else "glu_mlp_final",
    )(*args)


def _swap(x):
    return pltpu.roll(x, LANES // 2, 1)


def _cmul(z, w_r, w_i):
    return z * w_r + _swap(z) * w_i


def _multiplier(z, lo):
    zs = _swap(z)
    return jnp.where(lo, z, zs), jnp.where(lo, -zs, z)


def _rep_rows(x):
    return jnp.concatenate(
        [jnp.broadcast_to(x[g:g + 1, :], (GROUP_CH, LANES)) for g in range(GROUPS_PER_BLOCK)], axis=0)


def _s5_kernel(n_virt, n_seg, h_ref, lamr_ref, lami_ref, ldt_ref, bt_ref, cp_ref, dsk_ref, s0_ref, *refs):
    if n_seg == 1:
        y_ref, sfin_ref = refs[:2]
        refs = refs[2:]
    else:
        y_ref, sfin_ref = refs[0], None
        refs = refs[1:]
    f_scr, e_scr, k_scr, sf_scr, sb_scr, swf_scr, swb_scr = refs
    ntok = h_ref.shape[0]
    nc = ntok // CHUNK
    cpv = nc // n_virt
    gpb = GROUPS_PER_BLOCK

    lo8 = lax.broadcasted_iota(jnp.int32, (gpb, LANES), 1) < STATE_DIM
    lo = lax.broadcasted_iota(jnp.int32, (LANES, LANES), 1) < STATE_DIM
    conj = jnp.where(lo, 1.0, -1.0)
    row_g = lax.broadcasted_iota(jnp.int32, (LANES, STATE_LANES), 0) // GROUP_CH
    col_g = lax.broadcasted_iota(jnp.int32, (LANES, STATE_LANES), 1) // LANES
    diag_wide = row_g == col_g
    diag = (lax.broadcasted_iota(jnp.int32, (LANES, LANES), 0) // GROUP_CH) == (
        lax.broadcasted_iota(jnp.int32, (LANES, LANES), 1) // GROUP_CH)

    pair_diag = (lax.broadcasted_iota(jnp.int32, (PAIR_CH, 2 * LANES), 0) // GROUP_CH) == (
        lax.broadcasted_iota(jnp.int32, (PAIR_CH, 2 * LANES), 1) // LANES)

    def expand(w):
        return jnp.where(diag_wide, jnp.concatenate([w] * gpb, axis=1), jnp.zeros((), BF16))

    decay = []
    lag = []
    for d in range(2):
        lam_r = lamr_ref[d]
        lam_i = lami_ref[d]
        dt = jnp.exp(ldt_ref[d])
        mag = jnp.exp(lam_r * dt)
        ang = lam_i * dt
        a_r = mag * jnp.cos(ang)
        a_im = mag * jnp.sin(ang)
        a_i = jnp.where(lo8, -a_im, a_im)
        den = lam_r * lam_r + lam_i * lam_i
        num = jnp.where(lo8, a_r - 1.0, a_im)
        f = _cmul(num, lam_r / den, jnp.where(lo8, lam_i, -lam_i) / den)
        pw = [jnp.where(lo8, 1.0, 0.0)]
        for _ in range(CHUNK):
            pw.append(_cmul(pw[-1], a_r, a_i))
        decay.append(pw[CHUNK])
        pw = [_rep_rows(p) for p in pw]
        f_r, f_i = _multiplier(_rep_rows(f), lo)
        bb_r, bb_i = _multiplier(_cmul(bt_ref[d], f_r, f_i), lo)
        c_r, c_i = _multiplier(cp_ref[d], lo)
        cm = (cp_ref[d] * conj).astype(BF16)
        fpow = [_cmul(p, bb_r, bb_i).astype(BF16) for p in pw[:CHUNK]]
        for q in range(N_PAIRS):
            for j in range(CHUNK):
                e = (CHUNK - 1 - j) if d == 0 else j
                w = fpow[e][q * PAIR_CH:(q + 1) * PAIR_CH, :]
                f_scr[d, q, j * PAIR_CH:(j + 1) * PAIR_CH, :] = jnp.where(
                    pair_diag, jnp.concatenate([w, w], axis=1), jnp.zeros((), BF16))
        for t in range(CHUNK):
            e = (t + 1) if d == 0 else (CHUNK - t)
            w = _cmul(pw[e], c_r, c_i) * conj
            e_scr[d, t * LANES:(t + 1) * LANES, :] = expand(w.astype(BF16))
        lag.append([jnp.where(diag, _dot_nt(fp, cm), 0.0) for fp in fpow])

    for j in range(CHUNK):
        for t in range(CHUNK):
            k = t - j
            tile = lag[0][k] if k > 0 else (lag[1][-k] if k < 0 else lag[0][0] + lag[1][0])
            k_scr[j * LANES:(j + 1) * LANES, t * LANES:(t + 1) * LANES] = tile.astype(BF16)

    xs = [h_ref[pl.ds(j, nc, stride=CHUNK), :] for j in range(CHUNK)]
    xs = [x.astype(BF16) for x in xs]
    xcat = jnp.concatenate(xs, axis=1)

    lane_blk = lax.broadcasted_iota(jnp.int32, (nc, LANES), 1) // PAIR_CH
    per_col = LANES // PAIR_CH
    for q in range(N_PAIRS):
        cols = []
        for half in range(CHUNK // per_col):
            col = jnp.zeros((nc, LANES), BF16)
            for jj in range(per_col):
                x = xs[half * per_col + jj]
                shift = ((jj - q) * PAIR_CH) % LANES
                moved = pltpu.roll(x, shift, 1) if shift else x
                col = jnp.where(lane_blk == jj, moved, col)
            cols.append(col)
        xq = jnp.concatenate(cols, axis=1)
        for d, scr, sw_scr in ((0, sf_scr, swf_scr), (1, sb_scr, swb_scr)):
            loc_pair = _dot(xq, f_scr[d, q])
            for g in range(2):
                loc = loc_pair[:, g * LANES:(g + 1) * LANES]
                scr[2 * q + g] = loc
                sw_scr[2 * q + g] = _swap(loc)
    y_within = _dot(xcat, k_scr[...])

    sgn8 = jnp.where(lo8, -1.0, 1.0)

    def dup(z):
        zs = _swap(z)
        return jnp.where(lo8, z, zs), jnp.where(lo8, zs, z)

    def scan(scr, sw_scr, d, reverse):
        a_re, a_im = dup(decay[d])
        a_sg = a_im * sgn8
        a_r = [a_re[k:k + 1, :] for k in range(gpb)]
        a_i = [a_sg[k:k + 1, :] for k in range(gpb)]

        def rows_of(i):
            c = (cpv - 1 - i) if reverse else i
            return pl.ds(c * n_virt, n_virt)

        def body(i, carry):
            st, sw = carry
            rows = rows_of(i)
            new_st, new_sw = [], []
            for k in range(gpb):
                loc = scr[k, rows, :]
                loc_sw = sw_scr[k, rows, :]
                scr[k, rows, :] = st[k]
                new_st.append(a_r[k] * st[k] + a_i[k] * sw[k] + loc)
                new_sw.append(a_r[k] * sw[k] - a_i[k] * st[k] + loc_sw)
            return tuple(new_st), tuple(new_sw)

        st0 = tuple(s0_ref[d, :, k * LANES:(k + 1) * LANES] for k in range(gpb))
        sw0 = tuple(_swap(s) for s in st0)
        carry = (st0, sw0)
        for i in range(cpv):
            carry = body(i, carry)
        fin = carry[0]
        if n_seg == 1:
            for k in range(gpb):
                sfin_ref[d, :, k * LANES:(k + 1) * LANES] = fin[k]
            return

        p = decay[d]
        for _ in range(cpv.bit_length() - 1):
            p_re, p_im = dup(p)
            p = _cmul(p, p_re, p_im * sgn8)
        v_re, v_im = dup(p)
        v_sg = v_im * sgn8
        seg = lax.broadcasted_iota(jnp.int32, (n_virt, LANES), 0) & (n_seg - 1)
        has_pred = seg != ((n_seg - 1) if reverse else 0)
        shift = (n_virt - 1) if reverse else 1
        cin = []
        for k in range(gpb):
            ck = jnp.zeros((n_virt, LANES), F32)
            for _ in range(n_seg - 1):
                nxt = fin[k] + ck * v_re[k:k + 1, :] + _swap(ck) * v_sg[k:k + 1, :]
                ck = jnp.where(has_pred, pltpu.roll(nxt, shift, 0), 0.0)
            cin.append(ck)
        cin_sw = [_swap(x) for x in cin]

        def fix(i, carry):
            q_re, q_im = carry
            rows = rows_of(i)
            q_sg = q_im * sgn8
            for k in range(gpb):
                scr[k, rows, :] += cin[k] * q_re[k:k + 1, :] + cin_sw[k] * q_sg[k:k + 1, :]
            return q_re * a_re - q_im * a_im, q_re * a_im + q_im * a_re

        q = (jnp.ones((gpb, LANES), F32), jnp.zeros((gpb, LANES), F32))
        for i in range(cpv):
            q = fix(i, q)

    scan(sf_scr, swf_scr, 0, False)
    scan(sb_scr, swb_scr, 1, True)

    s_f = jnp.concatenate([sf_scr[k].astype(BF16) for k in range(gpb)], axis=1)
    s_b = jnp.concatenate([sb_scr[k].astype(BF16) for k in range(gpb)], axis=1)
    yall = y_within + _dot_nt(s_f, e_scr[0]) + _dot_nt(s_b, e_scr[1])
    dsk = dsk_ref[...]
    for t in range(CHUNK):
        rows = pl.ds(t, nc, stride=CHUNK)
        y_ref[rows, :] = yall[:, t * LANES:(t + 1) * LANES] + h_ref[rows, :] * dsk


def _s5_params(lam_re, lam_im, log_dt, b_re, b_im, c_re, c_im):
    lamr = jnp.concatenate([lam_re, lam_re], axis=-1).astype(F32)
    lami = jnp.concatenate([lam_im, lam_im], axis=-1).astype(F32)
    ldt = jnp.broadcast_to(log_dt.astype(F32)[..., None], lamr.shape)
    bt = jnp.concatenate([b_re.transpose(0, 1, 3, 2), b_im.transpose(0, 1, 3, 2)], axis=-1)
    cp = jnp.concatenate([c_re, c_im], axis=-1)
    return (lamr, lami, ldt, bt.reshape(2, D_MODEL, LANES).astype(F32), cp.reshape(2, D_MODEL, LANES).astype(F32))


def _s5(h, params, d_skip, s0, n_seg):
    lamr, lami, ldt, bt, cp = params
    _, cpv, n_virt, _, _ = h.shape
    assert cpv & (cpv - 1) == 0 and n_virt % SUBLANES == 0 and n_seg & (n_seg - 1) == 0
    nc = cpv * n_virt
    ntok = nc * CHUNK
    hspec = pl.BlockSpec((None, ntok, LANES), lambda g: (g, 0, 0))
    kdim = CHUNK * LANES
    gspec = pl.BlockSpec((2, GROUPS_PER_BLOCK, LANES), lambda g: (0, g, 0))
    rspec = pl.BlockSpec((2, LANES, LANES), lambda g: (0, g, 0))
    sspec = pl.BlockSpec((None, 2, n_virt, STATE_LANES), lambda g: (g, 0, 0, 0))
    state_scr = pltpu.VMEM((GROUPS_PER_BLOCK, nc, LANES), F32)
    out_specs = [hspec]
    out_shape = [jax.ShapeDtypeStruct((N_GROUP_BLOCKS, ntok, LANES), F32)]
    if n_seg == 1:
        out_specs.append(sspec)
        out_shape.append(jax.ShapeDtypeStruct((N_GROUP_BLOCKS, 2, n_virt, STATE_LANES), F32))
    outs = pl.pallas_call(
        functools.partial(_s5_kernel, n_virt, n_seg),
        grid=(N_GROUP_BLOCKS,),
        in_specs=[
            hspec,
            gspec, gspec, gspec, rspec, rspec,
            pl.BlockSpec((1, LANES), lambda g: (0, g)),
            sspec,
        ],
        out_specs=out_specs,
        out_shape=out_shape,
        scratch_shapes=[
            pltpu.VMEM((2, N_PAIRS, CHUNK * PAIR_CH, 2 * LANES), BF16),
            pltpu.VMEM((2, kdim, STATE_LANES), BF16),
            pltpu.VMEM((kdim, kdim), BF16),
            state_scr, state_scr, state_scr, state_scr,
        ],
        compiler_params=_cparams(("arbitrary",)),
        name="s5_chunked_scan",
    )(h.reshape(N_GROUP_BLOCKS, ntok, LANES), lamr, lami, ldt, bt, cp, d_skip, s0)
    y = outs[0].reshape(h.shape)
    return (y, outs[1]) if n_seg == 1 else (y, None)


def _state_to_blocks(s):
    b = s.shape[0]
    s = s.reshape(b, 2, 2, N_GROUP_BLOCKS, GROUPS_PER_BLOCK, STATE_DIM)
    return s.transpose(3, 1, 0, 4, 2, 5).reshape(N_GROUP_BLOCKS, 2, b, STATE_LANES)


def _blocks_to_state(s):
    b = s.shape[2]
    s = s.reshape(N_GROUP_BLOCKS, 2, b, GROUPS_PER_BLOCK, 2, STATE_DIM)
    return s.transpose(2, 1, 4, 0, 3, 5).reshape(b, 2, 2, N_GROUPS, STATE_DIM)


def _rope_tables(n_tokens):
    pos = np.arange(n_tokens)
    n_freq = HEAD_DIM // 4
    freqs = ROPE_BASE ** (-np.arange(n_freq, dtype=np.float64) / n_freq)
    ang_r = (pos // GRID_W)[:, None] * freqs
    ang_c = (pos % GRID_W)[:, None] * freqs
    cos_h = np.concatenate([np.cos(ang_r), np.cos(ang_r), np.cos(ang_c), np.cos(ang_c)], axis=1)
    sin_h = np.concatenate([-np.sin(ang_r), np.sin(ang_r), -np.sin(ang_c), np.sin(ang_c)], axis=1)
    return jnp.asarray(np.tile(cos_h, (1, 2)), F32), jnp.asarray(np.tile(sin_h, (1, 2)), F32)


def kernel(x_prompt, x_sample, cache_k, cache_v, state_ssm, c, c_ctx, norm1_g, norm2_g, w_mod, b_mod,
           w_qkv, w_o, attn_sink, ssm_lam_re, ssm_lam_im, ssm_log_dt, ssm_b_re, ssm_b_im, ssm_c_re,
           ssm_c_im, ssm_d, glu_w_a, glu_w_b, mlp_w1, mlp_w2, final_norm_g):
    bp, lp, _ = x_prompt.shape
    bx, lx, _ = x_sample.shape
    assert lx % TOKEN_TILE == 0 and (bp * lp) % TOKEN_TILE == 0
    tiles_per_lat = lx // TOKEN_TILE

    xp = x_prompt.reshape(bp * lp, D_MODEL)
    xx = x_sample.reshape(bx * lx, D_MODEL)

    cvecs = jnp.zeros((8, D_MODEL), F32).at[0].set(c_ctx).at[1:1 + bx].set(c)
    mod = _modulation(cvecs, w_mod, b_mod)

    ctx_row = lambda i: 0
    lat_row = lambda i: 1 + i // tiles_per_lat

    assert lx % QKV_TILE == 0 and (bp * lp) % QKV_TILE == 0
    qkv_tiles_per_lat = lx // QKV_TILE
    rope = _rope_tables(lx) + (lambda i: i % qkv_tiles_per_lat,)
    wqkv = w_qkv[0].astype(BF16)
    g1 = norm1_g[0].reshape(1, D_MODEL)
    sink = attn_sink[0].astype(F32)
    qp, krp, vrp, kp, vp = _qkv(xp, mod[0], ctx_row, g1, wqkv, None, lp)
    qx, krx, vrx = _qkv(xx, mod[0], lambda i: 1 + i // qkv_tiles_per_lat, g1, wqkv, rope, 0)
    op = _ctx_attention(sink, qp, krp, vrp, bp, lp)
    rep = lambda t: jnp.tile(t[:, 0].transpose(0, 2, 1, 3), (1, 1, 1, LANES // HEAD_DIM)).astype(BF16)
    ox, (w1, w2, wo, wa, wb) = _lat_attention(
        sink, qx, krx, vrx, rep(cache_k), rep(cache_v), bx, lx,
        [mlp_w1, mlp_w2, w_o[0], glu_w_a[0], glu_w_b[0]])
    g2 = norm2_g.reshape(-1, 1, D_MODEL)
    gn = norm1_g[1].reshape(1, D_MODEL)
    vpt_p = TOKEN_TILE // lp
    vpt_x = 1
    n_seg_x = tiles_per_lat
    xp, hp = _post(xp, op, mod[0], ctx_row, g2[0], wo, None, w1, w2, 0, vpt_p, mod_next=mod[1], g_next=gn)
    xx, hx = _post(xx, ox, mod[0], lat_row, g2[0], wo, None, w1, w2, 0, vpt_x, mod_next=mod[1], g_next=gn)

    params = _s5_params(ssm_lam_re[0], ssm_lam_im[0], ssm_log_dt[0], ssm_b_re[0], ssm_b_im[0],
                        ssm_c_re[0], ssm_c_im[0])
    dsk = ssm_d[0].astype(F32).reshape(1, D_MODEL)
    s0p = jnp.zeros((N_GROUP_BLOCKS, 2, bp, STATE_LANES), F32)
    sx = _state_to_blocks(state_ssm[:, 0].astype(F32))
    s0x = jnp.zeros((N_GROUP_BLOCKS, 2, bx, n_seg_x, STATE_LANES), F32)
    s0x = s0x.at[:, 0, :, 0].set(sx[:, 0]).at[:, 1, :, n_seg_x - 1].set(sx[:, 1])
    s0x = s0x.reshape(N_GROUP_BLOCKS, 2, bx * n_seg_x, STATE_LANES)
    yp, sfin = _s5(hp, params, dsk, s0p, 1)
    yx, _ = _s5(hx, params, dsk, s0x, n_seg_x)
    new_state = _blocks_to_state(sfin)[:, None]

    fg = final_norm_g.reshape(1, D_MODEL)
    (yp_out,) = _post(xp, yp, mod[1], ctx_row, g2[1], wa, wb, w1, w2, 1, vpt_p, final_g=fg)
    (yx_out,) = _post(xx, yx, mod[1], lat_row, g2[1], wa, wb, w1, w2, 1, vpt_x, final_g=fg)

    to_cache = lambda t: t.reshape(bp, N_KV_HEADS, HEAD_DIM, lp).transpose(0, 3, 1, 2)[:, None]
    new_k = to_cache(kp)
    new_v = to_cache(vp)
    return (yp_out.reshape(bp, lp, D_MODEL), yx_out.reshape(bx, lx, D_MODEL), new_k, new_v, new_state)
```

```python
import functools
import math

import numpy as np
import jax
import jax.numpy as jnp
from jax import lax
from jax.experimental import pallas as pl
from jax.experimental.pallas import tpu as pltpu

F32 = jnp.float32
BF16 = jnp.bfloat16

D_MODEL = 1024
N_HEADS = 16
N_KV_HEADS = 4
HEAD_DIM = 64
Q_PER_KV = N_HEADS // N_KV_HEADS
KV_DIM = N_KV_HEADS * HEAD_DIM
QKV_DIM = D_MODEL + 2 * KV_DIM
BLOCK = 128
GRID_W = 64
ROPE_BASE = 10000.0
ROT_HALF = HEAD_DIM // 4
ATTN_SCALE = HEAD_DIM ** -0.5
N_GROUPS = 64
GROUP_CH = 16
STATE_DIM = 64
D_FF = 4 * D_MODEL
N_MOD = 6
RMS_EPS = 1e-6
NEG_INF = -1e30

LANES = 128
SUBLANES = 8
GROUPS_PER_BLOCK = LANES // GROUP_CH
N_GROUP_BLOCKS = N_GROUPS // GROUPS_PER_BLOCK
PAIR_CH = 2 * GROUP_CH
N_PAIRS = GROUPS_PER_BLOCK // 2
STATE_LANES = GROUPS_PER_BLOCK * 2 * STATE_DIM
CHUNK = SUBLANES
TOKEN_TILE = 512
FF_TILE = 1024
POST_SUBTILES = 2
QKV_TILE = 1024
QKV_SUBTILES = 4
VMEM_LIMIT = 56 * 1024 * 1024


def _cparams(semantics):
    return pltpu.CompilerParams(dimension_semantics=semantics, vmem_limit_bytes=VMEM_LIMIT)


def _rms(x):
    return x * lax.rsqrt(jnp.mean(x * x, axis=-1, keepdims=True) + RMS_EPS)


def _dot(a, b):
    return jnp.dot(a, b, preferred_element_type=F32)


def _dot_nt(a, b):
    return lax.dot_general(a, b, (((1,), (1,)), ((), ())), preferred_element_type=F32)


def _mod_kernel(cv_ref, w_ref, b_ref, o_ref):
    cv = cv_ref[...]
    s = (cv * jax.nn.sigmoid(cv)).astype(BF16)
    o_ref[0] = _dot(s, w_ref[0].astype(BF16)) + b_ref[0]


def _modulation(cvecs, w_mod, b_mod):
    depth = w_mod.shape[0]
    width = 2 * D_MODEL
    out = pl.pallas_call(
        _mod_kernel,
        grid=(depth, N_MOD * D_MODEL // width),
        in_specs=[
            pl.BlockSpec((8, D_MODEL), lambda l, j: (0, 0)),
            pl.BlockSpec((1, D_MODEL, width), lambda l, j: (l, 0, j)),
            pl.BlockSpec((1, 1, width), lambda l, j: (l, 0, j)),
        ],
        out_specs=pl.BlockSpec((1, 8, width), lambda l, j: (l, 0, j)),
        out_shape=jax.ShapeDtypeStruct((depth, 8, N_MOD * D_MODEL), F32),
        compiler_params=_cparams(("arbitrary", "arbitrary")),
        name="modulation",
    )(cvecs, w_mod, b_mod.reshape(depth, 1, N_MOD * D_MODEL))
    return out.reshape(depth, 8, N_MOD, D_MODEL)


def _head_pair(blk, odd):
    lo = lax.broadcasted_iota(jnp.int32, blk.shape, 1) < HEAD_DIM
    other = pltpu.roll(blk, HEAD_DIM, 1)
    return (jnp.where(lo, other, blk) if odd else jnp.where(lo, blk, other)).astype(BF16)


def _qkv_kernel(cache_seq, rope, x_ref, mod_ref, g_ref, w_ref, *refs):
    refs = list(refs)
    cos_ref, sin_ref = (refs.pop(0), refs.pop(0)) if rope else (None, None)
    q_ref, krep_ref, vrep_ref = refs[:3]
    kv_refs = refs[3:]
    sub = QKV_TILE // QKV_SUBTILES
    starts = [k * sub for k in range(QKV_SUBTILES)]

    def project(r0):
        h = _rms(x_ref[r0:r0 + sub, :]) * g_ref[...] * (1.0 + mod_ref[1:2, :]) + mod_ref[0:1, :]
        return _dot(h.astype(BF16), w_ref[...])

    def emit_cache(ref, blk, r0, c0):
        for s in range(sub // cache_seq):
            ref[r0 // cache_seq + s, c0:c0 + LANES, :] = blk[s * cache_seq:(s + 1) * cache_seq, :].T

    def finish(r0, qkv):
        rows = slice(r0, r0 + sub)
        if rope:
            cos = cos_ref[rows, :]
            sin = sin_ref[rows, :]
            lane = lax.broadcasted_iota(jnp.int32, cos.shape, 1)
            first = (lane & (2 * ROT_HALF - 1)) < ROT_HALF
        for blk in range((D_MODEL + KV_DIM) // LANES):
            r = qkv[:, blk * LANES:(blk + 1) * LANES]
            if rope:
                partner = jnp.where(first, pltpu.roll(r, LANES - ROT_HALF, 1), pltpu.roll(r, ROT_HALF, 1))
                r = r * cos + partner * sin
            if blk < D_MODEL // LANES:
                q_ref[rows, blk * LANES:(blk + 1) * LANES] = (r * ATTN_SCALE).astype(BF16)
            else:
                c0 = blk * LANES - D_MODEL
                if cache_seq:
                    emit_cache(kv_refs[0], r, r0, c0)
                for half in range(2):
                    krep_ref[c0 // HEAD_DIM + half, rows, :] = _head_pair(r, half)
        v = qkv[:, D_MODEL + KV_DIM:]
        for c0 in range(0, KV_DIM, LANES):
            blk = v[:, c0:c0 + LANES]
            if cache_seq:
                emit_cache(kv_refs[1], blk, r0, c0)
            for half in range(2):
                vrep_ref[c0 // HEAD_DIM + half, rows, :] = _head_pair(blk, half)

    for r0, qkv in zip(starts, [project(r0) for r0 in starts]):
        finish(r0, qkv)


def _qkv(x, mod, mod_row, g, w_qkv, rope, cache_seq):
    ntok = x.shape[0]
    nt = ntok // QKV_TILE
    emit_kv = cache_seq > 0
    assert not emit_kv or (QKV_TILE // QKV_SUBTILES) % cache_seq == 0
    rep_spec = pl.BlockSpec((N_KV_HEADS, QKV_TILE, LANES), lambda i: (0, i, 0))
    rep_shape = jax.ShapeDtypeStruct((N_KV_HEADS, ntok, LANES), BF16)
    in_specs = [
        pl.BlockSpec((QKV_TILE, D_MODEL), lambda i: (i, 0)),
        pl.BlockSpec((None, N_MOD, D_MODEL), lambda i: (mod_row(i), 0, 0)),
        pl.BlockSpec((1, D_MODEL), lambda i: (0, 0)),
        pl.BlockSpec((D_MODEL, QKV_DIM), lambda i: (0, 0)),
    ]
    args = [x, mod, g, w_qkv]
    if rope is not None:
        cos_t, sin_t, rope_blk = rope
        in_specs += [pl.BlockSpec((QKV_TILE, LANES), lambda i: (rope_blk(i), 0))] * 2
        args += [cos_t, sin_t]
    out_specs = [pl.BlockSpec((QKV_TILE, D_MODEL), lambda i: (i, 0)), rep_spec, rep_spec]
    out_shape = [jax.ShapeDtypeStruct((ntok, D_MODEL), BF16), rep_shape, rep_shape]
    if emit_kv:
        spt = QKV_TILE // cache_seq
        out_specs += [pl.BlockSpec((spt, KV_DIM, cache_seq), lambda i: (i, 0, 0))] * 2
        out_shape += [jax.ShapeDtypeStruct((ntok // cache_seq, KV_DIM, cache_seq), F32)] * 2
    return pl.pallas_call(
        functools.partial(_qkv_kernel, cache_seq, rope is not None),
        grid=(nt,),
        in_specs=in_specs,
        out_specs=out_specs,
        out_shape=out_shape,
        compiler_params=_cparams(("arbitrary",)),
        name="norm_qkv_rope",
    )(*args)


def _group_scores(q_ref, kv, key_parts, bias):
    nq = q_ref.shape[0]
    lo = lax.broadcasted_iota(jnp.int32, (nq, LANES), 1) < HEAD_DIM
    zero = jnp.zeros((), BF16)
    rows = []
    for b in range(KV_DIM // LANES):
        blk = q_ref[:, kv * KV_DIM + b * LANES:kv * KV_DIM + (b + 1) * LANES]
        rows += [jnp.where(lo, blk, zero), jnp.where(lo, zero, blk)]
    q4 = jnp.concatenate(rows, axis=0)
    parts = [_dot_nt(q4, keys) for keys in key_parts]
    if bias is not None:
        s0 = parts[0].reshape(Q_PER_KV, nq, -1) + bias[None]
        parts[0] = s0.reshape(Q_PER_KV * nq, -1)
    return parts


def _group_softmax(parts, sink_ref, kv):
    nq = parts[0].shape[0] // Q_PER_KV
    sink = jnp.concatenate(
        [jnp.full((nq, LANES), sink_ref[kv * Q_PER_KV + g], F32) for g in range(Q_PER_KV)], axis=0)
    blocks = [[s[:, j:j + LANES] for j in range(0, s.shape[1], LANES)] for s in parts]
    fold = None
    for b in sum(blocks, []):
        fold = b if fold is None else jnp.maximum(fold, b)
    m = jnp.maximum(jnp.max(fold, axis=-1, keepdims=True), sink)
    probs = [[jnp.exp(b - m) for b in bs] for bs in blocks]
    fold = None
    for p in sum(probs, []):
        fold = p if fold is None else fold + p
    den = jnp.sum(fold, axis=-1, keepdims=True) + jnp.exp(sink - m)
    return [jnp.concatenate(ps, axis=1).astype(BF16) for ps in probs], 1.0 / den


def _group_output(probs, inv_den, value_parts, o_ref, kv):
    nq = probs[0].shape[0] // Q_PER_KV
    r = _dot(probs[0], value_parts[0])
    for p, v in zip(probs[1:], value_parts[1:]):
        r = r + _dot(p, v)
    r = r * inv_den
    lo = lax.broadcasted_iota(jnp.int32, (nq, LANES), 1) < HEAD_DIM
    for b in range(KV_DIM // LANES):
        pair = jnp.where(lo, r[2 * b * nq:(2 * b + 1) * nq], r[(2 * b + 1) * nq:(2 * b + 2) * nq])
        o_ref[:, kv * KV_DIM + b * LANES:kv * KV_DIM + (b + 1) * LANES] = pair.astype(BF16)


def _attend(q_ref, sink_ref, o_ref, keys_of, values_of, bias):
    s_next = _group_scores(q_ref, 0, keys_of(0), bias)
    for kv in range(N_KV_HEADS):
        s = s_next
        if kv + 1 < N_KV_HEADS:
            s_next = _group_scores(q_ref, kv + 1, keys_of(kv + 1), bias)
        probs, inv_den = _group_softmax(s, sink_ref, kv)
        _group_output(probs, inv_den, values_of(kv), o_ref, kv)


def _ctx_attn_kernel(sink_ref, q_ref, k_ref, v_ref, o_ref):
    _attend(q_ref, sink_ref, o_ref, lambda kv: [k_ref[kv]], lambda kv: [v_ref[kv]], None)


def _ctx_attention(sink, q, krep, vrep, n_batch, seq):
    rep_spec = pl.BlockSpec((N_KV_HEADS, seq, LANES), lambda b: (0, b, 0))
    return pl.pallas_call(
        _ctx_attn_kernel,
        grid=(n_batch,),
        in_specs=[
            pl.BlockSpec(memory_space=pltpu.SMEM),
            pl.BlockSpec((seq, D_MODEL), lambda b: (b, 0)),
            rep_spec, rep_spec,
        ],
        out_specs=pl.BlockSpec((seq, D_MODEL), lambda b: (b, 0)),
        out_shape=jax.ShapeDtypeStruct((n_batch * seq, D_MODEL), BF16),
        compiler_params=_cparams(("arbitrary",)),
        name="context_attention",
    )(sink, q, krep, vrep)


def _window_start(n, seq):
    return jnp.clip((n - 1) * BLOCK, 0, seq - 3 * BLOCK)


def _band_bias():
    r = np.arange(BLOCK)[:, None]
    j = np.arange(3 * BLOCK)[None, :]
    out = [np.where(np.abs(j - d * BLOCK - r) <= BLOCK, 0.0, NEG_INF) for d in range(3)]
    return jnp.asarray(np.stack(out), F32)


def _lat_attn_kernel(seq, n_cast, sink_ref, q_ref, k_ref, v_ref, ck_ref, cv_ref, bias_ref, *refs):
    o_ref = refs[n_cast]
    for src, dst in zip(refs[:n_cast], refs[n_cast + 1:]):
        dst[...] = src[...].astype(BF16)
    win = 3 * BLOCK
    start = pl.multiple_of(_window_start(pl.program_id(1), seq), BLOCK)
    keys_of = lambda kv: [k_ref[kv, pl.ds(start, win), :], ck_ref[kv]]
    values_of = lambda kv: [v_ref[kv, pl.ds(start, win), :], cv_ref[kv]]
    _attend(q_ref, sink_ref, o_ref, keys_of, values_of, bias_ref[...])


def _lat_attention(sink, q, krep, vrep, ckrep, cvrep, n_batch, seq, cast_weights):
    nb = seq // BLOCK
    steps = n_batch * nb
    past = ckrep.shape[2]
    rep_spec = pl.BlockSpec((N_KV_HEADS, seq, LANES), lambda b, n: (0, b, 0))
    crep_spec = pl.BlockSpec((None, N_KV_HEADS, past, LANES), lambda b, n: (b, 0, 0, 0))
    flat = [w.reshape(-1, w.shape[-1]) for w in cast_weights]
    assert all(w.shape[0] % (steps * 2 * SUBLANES) == 0 for w in flat)
    slabs = [pl.BlockSpec((w.shape[0] // steps, w.shape[1]), lambda b, n: (b * nb + n, 0)) for w in flat]
    outs = pl.pallas_call(
        functools.partial(_lat_attn_kernel, seq, len(flat)),
        grid=(n_batch, nb),
        in_specs=[
            pl.BlockSpec(memory_space=pltpu.SMEM),
            pl.BlockSpec((BLOCK, D_MODEL), lambda b, n: (b * nb + n, 0)),
            rep_spec, rep_spec, crep_spec, crep_spec,
            pl.BlockSpec((None, BLOCK, 3 * BLOCK), lambda b, n: (n - _window_start(n, seq) // BLOCK, 0, 0)),
        ] + slabs,
        out_specs=[pl.BlockSpec((BLOCK, D_MODEL), lambda b, n: (b * nb + n, 0))] + slabs,
        out_shape=[jax.ShapeDtypeStruct((n_batch * seq, D_MODEL), BF16)]
        + [jax.ShapeDtypeStruct(w.shape, BF16) for w in flat],
        compiler_params=_cparams(("arbitrary", "arbitrary")),
        name="latent_attention",
    )(sink, q, krep, vrep, ckrep, cvrep, _band_bias(), *flat)
    return outs[0], [o.reshape(w.shape) for o, w in zip(outs[1:], cast_weights)]


def _gelu_tanh(x):
    c = math.sqrt(2.0 / math.pi)
    return x * (0.5 * (1.0 + jnp.tanh(c * (x + 0.044715 * (x * x * x)))))


def _post_kernel(is_attn, emit_next, final, vpt, *refs):
    rows_per_v = TOKEN_TILE // vpt
    refs = list(refs)
    x_ref, mix_ref, mod_ref, g2_ref, wa_ref = refs[:5]
    refs = refs[5:]
    wb_ref = None if is_attn else refs.pop(0)
    w1_ref, w2_ref = refs[:2]
    refs = refs[2:]
    modn_ref = gn_ref = fg_ref = hn_ref = None
    if emit_next:
        modn_ref, gn_ref = refs[:2]
        refs = refs[2:]
    if final:
        fg_ref = refs.pop(0)
    xo_ref = refs.pop(0)
    if emit_next:
        hn_ref = refs.pop(0)
    assert not refs

    sub = TOKEN_TILE // POST_SUBTILES
    assert rows_per_v % sub == 0
    n_chunk = sub // CHUNK
    starts = [k * sub for k in range(POST_SUBTILES)]

    def s5_slot(r0):
        return r0 // rows_per_v, (r0 % rows_per_v) // CHUNK

    def project(r0):
        if is_attn:
            return _dot(mix_ref[r0:r0 + sub, :], wa_ref[...])
        s, c0 = s5_slot(r0)
        y = jnp.concatenate(
            [mix_ref[g, c0:c0 + n_chunk, s].reshape(sub, LANES) for g in range(N_GROUP_BLOCKS)], axis=1)
        yg = _gelu_tanh(y).astype(BF16)
        return _dot(yg, wa_ref[...]) * jax.nn.sigmoid(_dot(yg, wb_ref[...]))

    def prologue(r0, mix):
        x1 = x_ref[r0:r0 + sub, :] + mod_ref[2:3, :] * mix
        h2 = _rms(x1) * g2_ref[...] * (1.0 + mod_ref[4:5, :]) + mod_ref[3:4, :]
        return x1, h2.astype(BF16)

    def mlp(h2):
        acc = None
        for c in range(D_FF // FF_TILE):
            a = jnp.maximum(_dot(h2, w1_ref[:, c * FF_TILE:(c + 1) * FF_TILE]), 0.0)
            t = _dot((a * a).astype(BF16), w2_ref[c * FF_TILE:(c + 1) * FF_TILE, :])
            acc = t if acc is None else acc + t
        return acc

    def epilogue(r0, x1, acc):
        x2 = x1 + mod_ref[5:6, :] * acc
        if emit_next:
            hn = _rms(x2) * gn_ref[...] * (1.0 + modn_ref[1:2, :]) + modn_ref[0:1, :]
            s, c0 = s5_slot(r0)
            for g in range(N_GROUP_BLOCKS):
                blk = hn[:, g * LANES:(g + 1) * LANES]
                hn_ref[g, c0:c0 + n_chunk, s] = blk.reshape(n_chunk, CHUNK, LANES)
        xo_ref[r0:r0 + sub, :] = _rms(x2) * fg_ref[...] if final else x2

    mixes = [project(r0) for r0 in starts]
    pro = [prologue(r0, mix) for r0, mix in zip(starts, mixes)]
    accs = [mlp(h2) for _, h2 in pro]
    for r0, (x1, _), acc in zip(starts, pro, accs):
        epilogue(r0, x1, acc)


def _post(x, mix, mod, mod_row, g2, w_a, w_b, w1, w2, layer, vpt, mod_next=None, g_next=None, final_g=None):
    is_attn = w_b is None
    emit_next = mod_next is not None
    final = final_g is not None
    ntok = x.shape[0]
    nt = ntok // TOKEN_TILE
    cpv = TOKEN_TILE // (vpt * CHUNK)
    n_virt = nt * vpt
    tile = pl.BlockSpec((TOKEN_TILE, D_MODEL), lambda i: (i, 0))
    row = pl.BlockSpec((1, D_MODEL), lambda i: (0, 0))
    modspec = pl.BlockSpec((None, N_MOD, D_MODEL), lambda i: (mod_row(i), 0, 0))
    resident = lambda shape: pl.BlockSpec(shape, lambda i: (0, 0), pipeline_mode=pl.Buffered(1))
    wsq = resident((D_MODEL, D_MODEL))
    gtile = pl.BlockSpec((N_GROUP_BLOCKS, cpv, vpt, CHUNK, LANES), lambda i: (0, 0, i, 0, 0))
    in_specs = [tile, tile if is_attn else gtile, modspec, row, wsq]
    args = [x, mix, mod, g2, w_a]
    if not is_attn:
        in_specs.append(wsq)
        args.append(w_b)
    in_specs += [pl.BlockSpec((None, D_MODEL, D_FF), lambda i: (layer, 0, 0), pipeline_mode=pl.Buffered(1)),
                 pl.BlockSpec((None, D_FF, D_MODEL), lambda i: (layer, 0, 0), pipeline_mode=pl.Buffered(1))]
    args += [w1, w2]
    if emit_next:
        in_specs += [modspec, row]
        args += [mod_next, g_next]
    if final:
        in_specs.append(row)
        args.append(final_g)
    out_specs = [tile]
    out_shape = [jax.ShapeDtypeStruct((ntok, D_MODEL), F32)]
    if emit_next:
        out_specs.append(gtile)
        out_shape.append(jax.ShapeDtypeStruct((N_GROUP_BLOCKS, cpv, n_virt, CHUNK, LANES), F32))
    return pl.pallas_call(
        functools.partial(_post_kernel, is_attn, emit_next, final, vpt),
        grid=(nt,),
        in_specs=in_specs,
        out_specs=out_specs,
        out_shape=out_shape,
        compiler_params=_cparams(("arbitrary",)),
        name="attn_proj_mlp" if is_attn else "glu_mlp_final",
    )(*args)


def _swap(x):
    return pltpu.roll(x, LANES // 2, 1)


def _cmul(z, w_r, w_i):
    return z * w_r + _swap(z) * w_i


def _multiplier(z, lo):
    zs = _swap(z)
    return jnp.where(lo, z, zs), jnp.where(lo, -zs, z)


def _rep_rows(x):
    return jnp.concatenate(
        [jnp.broadcast_to(x[g:g + 1, :], (GROUP_CH, LANES)) for g in range(GROUPS_PER_BLOCK)], axis=0)


def _s5_kernel(n_virt, n_seg, h_ref, *refs):
    own_params, next_params = refs[:5], refs[5:10]
    dsk_ref, s0_ref = refs[10:12]
    refs = refs[12:]
    if n_seg == 1:
        y_ref, sfin_ref = refs[:2]
        refs = refs[2:]
    else:
        y_ref, sfin_ref = refs[0], None
        refs = refs[1:]
    ops_a, ops_b, state_scr = refs[:4], refs[4:8], refs[8:]

    @pl.when(pl.program_id(0) == 0)
    def _():
        _s5_build(own_params, 0, *ops_a)

    def block(blk, ops):
        _s5_block(n_virt, n_seg, h_ref.at[blk], dsk_ref.at[blk], s0_ref.at[blk], y_ref.at[blk],
                  None if sfin_ref is None else sfin_ref.at[blk], *ops, *state_scr)

    block(0, ops_a)
    _s5_build(own_params, 1, *ops_b)
    block(1, ops_b)
    _s5_build(next_params, 0, *ops_a)


def _s5_build(params, half, f_scr, e_scr, k_scr, dec_scr):
    gpb = GROUPS_PER_BLOCK
    rows8 = slice(half * gpb, (half + 1) * gpb)
    rows128 = slice(half * LANES, (half + 1) * LANES)
    lo8 = lax.broadcasted_iota(jnp.int32, (gpb, LANES), 1) < STATE_DIM
    lo = lax.broadcasted_iota(jnp.int32, (LANES, LANES), 1) < STATE_DIM
    conj = jnp.where(lo, 1.0, -1.0)
    row_g = lax.broadcasted_iota(jnp.int32, (LANES, STATE_LANES), 0) // GROUP_CH
    col_g = lax.broadcasted_iota(jnp.int32, (LANES, STATE_LANES), 1) // LANES
    diag_wide = row_g == col_g
    diag = (lax.broadcasted_iota(jnp.int32, (LANES, LANES), 0) // GROUP_CH) == (
        lax.broadcasted_iota(jnp.int32, (LANES, LANES), 1) // GROUP_CH)

    pair_diag = (lax.broadcasted_iota(jnp.int32, (PAIR_CH, 2 * LANES), 0) // GROUP_CH) == (
        lax.broadcasted_iota(jnp.int32, (PAIR_CH, 2 * LANES), 1) // LANES)

    def expand(w):
        return jnp.where(diag_wide, jnp.concatenate([w] * gpb, axis=1), jnp.zeros((), BF16))

    lamr_ref, lami_ref, ldt_ref, bt_ref, cp_ref = params
    lag = []

    def build_direction(d):
        lam_r = lamr_ref[d, rows8]
        lam_i = lami_ref[d, rows8]
        dt = jnp.exp(ldt_ref[d, rows8])
        bt = bt_ref[d, rows128]
        cp = cp_ref[d, rows128]
        mag = jnp.exp(lam_r * dt)
        ang = lam_i * dt
        a_r = mag * jnp.cos(ang)
        a_im = mag * jnp.sin(ang)
        a_i = jnp.where(lo8, -a_im, a_im)
        den = lam_r * lam_r + lam_i * lam_i
        num = jnp.where(lo8, a_r - 1.0, a_im)
        f = _cmul(num, lam_r / den, jnp.where(lo8, lam_i, -lam_i) / den)
        pw = [jnp.where(lo8, 1.0, 0.0)]
        for _ in range(CHUNK):
            pw.append(_cmul(pw[-1], a_r, a_i))
        dec_scr[d] = pw[CHUNK]
        pw = [_rep_rows(p) for p in pw]
        f_r, f_i = _multiplier(_rep_rows(f), lo)
        bb_r, bb_i = _multiplier(_cmul(bt, f_r, f_i), lo)
        c_r, c_i = _multiplier(cp, lo)
        cm = (cp * conj).astype(BF16)
        fpow = [_cmul(p, bb_r, bb_i).astype(BF16) for p in pw[:CHUNK]]
        for q in range(N_PAIRS):
            for j in range(CHUNK):
                e = (CHUNK - 1 - j) if d == 0 else j
                w = fpow[e][q * PAIR_CH:(q + 1) * PAIR_CH, :]
                f_scr[d, q, j * PAIR_CH:(j + 1) * PAIR_CH, :] = jnp.where(
                    pair_diag, jnp.concatenate([w, w], axis=1), jnp.zeros((), BF16))
        for t in range(CHUNK):
            e = (t + 1) if d == 0 else (CHUNK - t)
            w = _cmul(pw[e], c_r, c_i) * conj
            e_scr[d, t * LANES:(t + 1) * LANES, :] = expand(w.astype(BF16))
        lag.append([jnp.where(diag, _dot_nt(fp, cm), 0.0) for fp in fpow])

    for d in range(2):
        build_direction(d)
    for j in range(CHUNK):
        for t in range(CHUNK):
            k = t - j
            tile = lag[0][k] if k > 0 else (lag[1][-k] if k < 0 else lag[0][0] + lag[1][0])
            k_scr[j * LANES:(j + 1) * LANES, t * LANES:(t + 1) * LANES] = tile.astype(BF16)


def _s5_block(n_virt, n_seg, h_ref, dsk_ref, s0_ref, y_ref, sfin_ref, f_scr, e_scr, k_scr, dec_scr,
              sf_scr, sb_scr, swf_scr, swb_scr):
    ntok = h_ref.shape[0]
    nc = ntok // CHUNK
    cpv = nc // n_virt
    gpb = GROUPS_PER_BLOCK
    lo8 = lax.broadcasted_iota(jnp.int32, (gpb, LANES), 1) < STATE_DIM
    decay = [dec_scr[d] for d in range(2)]

    xs = [h_ref[pl.ds(j, nc, stride=CHUNK), :] for j in range(CHUNK)]
    xs = [x.astype(BF16) for x in xs]
    xcat = jnp.concatenate(xs, axis=1)

    lane_blk = lax.broadcasted_iota(jnp.int32, (nc, LANES), 1) // PAIR_CH
    per_col = LANES // PAIR_CH
    for q in range(N_PAIRS):
        cols = []
        for half in range(CHUNK // per_col):
            col = jnp.zeros((nc, LANES), BF16)
            for jj in range(per_col):
                x = xs[half * per_col + jj]
                shift = ((jj - q) * PAIR_CH) % LANES
                moved = pltpu.roll(x, shift, 1) if shift else x
                col = jnp.where(lane_blk == jj, moved, col)
            cols.append(col)
        xq = jnp.concatenate(cols, axis=1)
        for d, scr, sw_scr in ((0, sf_scr, swf_scr), (1, sb_scr, swb_scr)):
            loc_pair = _dot(xq, f_scr[d, q])
            for g in range(2):
                loc = loc_pair[:, g * LANES:(g + 1) * LANES]
                scr[2 * q + g] = loc
                sw_scr[2 * q + g] = _swap(loc)
    y_within = _dot(xcat, k_scr[...])

    sgn8 = jnp.where(lo8, -1.0, 1.0)

    def dup(z):
        zs = _swap(z)
        return jnp.where(lo8, z, zs), jnp.where(lo8, zs, z)

    def scan(scr, sw_scr, d, reverse):
        a_re, a_im = dup(decay[d])
        a_sg = a_im * sgn8
        a_r = [a_re[k:k + 1, :] for k in range(gpb)]
        a_i = [a_sg[k:k + 1, :] for k in range(gpb)]

        def rows_of(i):
            c = (cpv - 1 - i) if reverse else i
            return pl.ds(c * n_virt, n_virt)

        def body(i, carry):
            st, sw = carry
            rows = rows_of(i)
            new_st, new_sw = [], []
            for k in range(gpb):
                loc = scr[k, rows, :]
                loc_sw = sw_scr[k, rows, :]
                scr[k, rows, :] = st[k]
                new_st.append(a_r[k] * st[k] + a_i[k] * sw[k] + loc)
                new_sw.append(a_r[k] * sw[k] - a_i[k] * st[k] + loc_sw)
            return tuple(new_st), tuple(new_sw)

        st0 = tuple(s0_ref[d, :, k * LANES:(k + 1) * LANES] for k in range(gpb))
        sw0 = tuple(_swap(s) for s in st0)
        carry = (st0, sw0)
        for i in range(cpv):
            carry = body(i, carry)
        fin = carry[0]
        if n_seg == 1:
            for k in range(gpb):
                sfin_ref[d, :, k * LANES:(k + 1) * LANES] = fin[k]
            return

        p = decay[d]
        for _ in range(cpv.bit_length() - 1):
            p_re, p_im = dup(p)
            p = _cmul(p, p_re, p_im * sgn8)
        v_re, v_im = dup(p)
        v_sg = v_im * sgn8
        seg = lax.broadcasted_iota(jnp.int32, (n_virt, LANES), 0) & (n_seg - 1)
        has_pred = seg != ((n_seg - 1) if reverse else 0)
        shift = (n_virt - 1) if reverse else 1
        cin = []
        for k in range(gpb):
            ck = jnp.zeros((n_virt, LANES), F32)
            for _ in range(n_seg - 1):
                nxt = fin[k] + ck * v_re[k:k + 1, :] + _swap(ck) * v_sg[k:k + 1, :]
                ck = jnp.where(has_pred, pltpu.roll(nxt, shift, 0), 0.0)
            cin.append(ck)
        cin_sw = [_swap(x) for x in cin]

        def fix(i, carry):
            q_re, q_im = carry
            rows = rows_of(i)
            q_sg = q_im * sgn8
            for k in range(gpb):
                scr[k, rows, :] += cin[k] * q_re[k:k + 1, :] + cin_sw[k] * q_sg[k:k + 1, :]
            return q_re * a_re - q_im * a_im, q_re * a_im + q_im * a_re

        q = (jnp.ones((gpb, LANES), F32), jnp.zeros((gpb, LANES), F32))
        for i in range(cpv):
            q = fix(i, q)

    scan(sf_scr, swf_scr, 0, False)
    scan(sb_scr, swb_scr, 1, True)

    s_f = jnp.concatenate([sf_scr[k].astype(BF16) for k in range(gpb)], axis=1)
    s_b = jnp.concatenate([sb_scr[k].astype(BF16) for k in range(gpb)], axis=1)
    yall = y_within + _dot_nt(s_f, e_scr[0]) + _dot_nt(s_b, e_scr[1])
    dsk = dsk_ref[...]
    for t in range(CHUNK):
        rows = pl.ds(t, nc, stride=CHUNK)
        y_ref[rows, :] = yall[:, t * LANES:(t + 1) * LANES] + h_ref[rows, :] * dsk


def _s5_params(lam_re, lam_im, log_dt, b_re, b_im, c_re, c_im):
    lamr = jnp.concatenate([lam_re, lam_re], axis=-1).astype(F32)
    lami = jnp.concatenate([lam_im, lam_im], axis=-1).astype(F32)
    ldt = jnp.broadcast_to(log_dt.astype(F32)[..., None], lamr.shape)
    bt = jnp.concatenate([b_re.transpose(0, 1, 3, 2), b_im.transpose(0, 1, 3, 2)], axis=-1)
    cp = jnp.concatenate([c_re, c_im], axis=-1)
    return (lamr, lami, ldt, bt.reshape(2, D_MODEL, LANES).astype(F32), cp.reshape(2, D_MODEL, LANES).astype(F32))


def _s5(h, params, d_skip, s0, n_seg):
    lamr, lami, ldt, bt, cp = params
    _, cpv, n_virt, _, _ = h.shape
    assert cpv & (cpv - 1) == 0 and n_virt % SUBLANES == 0 and n_seg & (n_seg - 1) == 0
    nc = cpv * n_virt
    ntok = nc * CHUNK
    hspec = pl.BlockSpec((2, ntok, LANES), lambda g: (g, 0, 0))
    kdim = CHUNK * LANES
    gspec = pl.BlockSpec((2, 2 * GROUPS_PER_BLOCK, LANES), lambda g: (0, g, 0))
    rspec = pl.BlockSpec((2, 2 * LANES, LANES), lambda g: (0, g, 0))
    nxt = lambda g: (0, jnp.minimum(2 * g + 2, N_GROUP_BLOCKS - 1), 0)
    gnext = pl.BlockSpec((2, GROUPS_PER_BLOCK, LANES), nxt)
    rnext = pl.BlockSpec((2, LANES, LANES), nxt)
    sspec = pl.BlockSpec((2, 2, n_virt, STATE_LANES), lambda g: (g, 0, 0, 0))
    state_scr = pltpu.VMEM((GROUPS_PER_BLOCK, nc, LANES), F32)
    ops_scr = [
        pltpu.VMEM((2, N_PAIRS, CHUNK * PAIR_CH, 2 * LANES), BF16),
        pltpu.VMEM((2, kdim, STATE_LANES), BF16),
        pltpu.VMEM((kdim, kdim), BF16),
        pltpu.VMEM((2, GROUPS_PER_BLOCK, LANES), F32),
    ]
    out_specs = [hspec]
    out_shape = [jax.ShapeDtypeStruct((N_GROUP_BLOCKS, ntok, LANES), F32)]
    if n_seg == 1:
        out_specs.append(sspec)
        out_shape.append(jax.ShapeDtypeStruct((N_GROUP_BLOCKS, 2, n_virt, STATE_LANES), F32))
    outs = pl.pallas_call(
        functools.partial(_s5_kernel, n_virt, n_seg),
        grid=(N_GROUP_BLOCKS // 2,),
        in_specs=[
            hspec,
            gspec, gspec, gspec, rspec, rspec,
            gnext, gnext, gnext, rnext, rnext,
            pl.BlockSpec((2, 1, LANES), lambda g: (g, 0, 0)),
            sspec,
        ],
        out_specs=out_specs,
        out_shape=out_shape,
        scratch_shapes=ops_scr + ops_scr + [state_scr, state_scr, state_scr, state_scr],
        compiler_params=_cparams(("arbitrary",)),
        name="s5_chunked_scan",
    )(h.reshape(N_GROUP_BLOCKS, ntok, LANES), lamr, lami, ldt, bt, cp, lamr, lami, ldt, bt, cp,
      d_skip.reshape(N_GROUP_BLOCKS, 1, LANES), s0)
    y = outs[0].reshape(h.shape)
    return (y, outs[1]) if n_seg == 1 else (y, None)


def _state_to_blocks(s):
    b = s.shape[0]
    s = s.reshape(b, 2, 2, N_GROUP_BLOCKS, GROUPS_PER_BLOCK, STATE_DIM)
    return s.transpose(3, 1, 0, 4, 2, 5).reshape(N_GROUP_BLOCKS, 2, b, STATE_LANES)


def _blocks_to_state(s):
    b = s.shape[2]
    s = s.reshape(N_GROUP_BLOCKS, 2, b, GROUPS_PER_BLOCK, 2, STATE_DIM)
    return s.transpose(2, 1, 4, 0, 3, 5).reshape(b, 2, 2, N_GROUPS, STATE_DIM)


def _rope_tables(n_tokens):
    pos = np.arange(n_tokens)
    n_freq = HEAD_DIM // 4
    freqs = ROPE_BASE ** (-np.arange(n_freq, dtype=np.float64) / n_freq)
    ang_r = (pos // GRID_W)[:, None] * freqs
    ang_c = (pos % GRID_W)[:, None] * freqs
    cos_h = np.concatenate([np.cos(ang_r), np.cos(ang_r), np.cos(ang_c), np.cos(ang_c)], axis=1)
    sin_h = np.concatenate([-np.sin(ang_r), np.sin(ang_r), -np.sin(ang_c), np.sin(ang_c)], axis=1)
    return jnp.asarray(np.tile(cos_h, (1, 2)), F32), jnp.asarray(np.tile(sin_h, (1, 2)), F32)


def kernel(x_prompt, x_sample, cache_k, cache_v, state_ssm, c, c_ctx, norm1_g, norm2_g, w_mod, b_mod,
           w_qkv, w_o, attn_sink, ssm_lam_re, ssm_lam_im, ssm_log_dt, ssm_b_re, ssm_b_im, ssm_c_re,
           ssm_c_im, ssm_d, glu_w_a, glu_w_b, mlp_w1, mlp_w2, final_norm_g):
    bp, lp, _ = x_prompt.shape
    bx, lx, _ = x_sample.shape
    assert lx % TOKEN_TILE == 0 and (bp * lp) % TOKEN_TILE == 0
    tiles_per_lat = lx // TOKEN_TILE

    xp = x_prompt.reshape(bp * lp, D_MODEL)
    xx = x_sample.reshape(bx * lx, D_MODEL)

    cvecs = jnp.zeros((8, D_MODEL), F32).at[0].set(c_ctx).at[1:1 + bx].set(c)
    mod = _modulation(cvecs, w_mod, b_mod)

    ctx_row = lambda i: 0
    lat_row = lambda i: 1 + i // tiles_per_lat

    assert lx % QKV_TILE == 0 and (bp * lp) % QKV_TILE == 0
    qkv_tiles_per_lat = lx // QKV_TILE
    rope = _rope_tables(lx) + (lambda i: i % qkv_tiles_per_lat,)
    wqkv = w_qkv[0].astype(BF16)
    g1 = norm1_g[0].reshape(1, D_MODEL)
    sink = attn_sink[0].astype(F32)
    qp, krp, vrp, kp, vp = _qkv(xp, mod[0], ctx_row, g1, wqkv, None, lp)
    qx, krx, vrx = _qkv(xx, mod[0], lambda i: 1 + i // qkv_tiles_per_lat, g1, wqkv, rope, 0)
    op = _ctx_attention(sink, qp, krp, vrp, bp, lp)
    rep = lambda t: jnp.tile(t[:, 0].transpose(0, 2, 1, 3), (1, 1, 1, LANES // HEAD_DIM)).astype(BF16)
    ox, (w1, w2, wo, wa, wb) = _lat_attention(
        sink, qx, krx, vrx, rep(cache_k), rep(cache_v), bx, lx,
        [mlp_w1, mlp_w2, w_o[0], glu_w_a[0], glu_w_b[0]])
    g2 = norm2_g.reshape(-1, 1, D_MODEL)
    gn = norm1_g[1].reshape(1, D_MODEL)
    vpt_p = TOKEN_TILE // lp
    vpt_x = 1
    n_seg_x = tiles_per_lat
    xp, hp = _post(xp, op, mod[0], ctx_row, g2[0], wo, None, w1, w2, 0, vpt_p, mod_next=mod[1], g_next=gn)
    xx, hx = _post(xx, ox, mod[0], lat_row, g2[0], wo, None, w1, w2, 0, vpt_x, mod_next=mod[1], g_next=gn)

    params = _s5_params(ssm_lam_re[0], ssm_lam_im[0], ssm_log_dt[0], ssm_b_re[0], ssm_b_im[0],
                        ssm_c_re[0], ssm_c_im[0])
    dsk = ssm_d[0].astype(F32).reshape(1, D_MODEL)
    s0p = jnp.zeros((N_GROUP_BLOCKS, 2, bp, STATE_LANES), F32)
    sx = _state_to_blocks(state_ssm[:, 0].astype(F32))
    s0x = jnp.zeros((N_GROUP_BLOCKS, 2, bx, n_seg_x, STATE_LANES), F32)
    s0x = s0x.at[:, 0, :, 0].set(sx[:, 0]).at[:, 1, :, n_seg_x - 1].set(sx[:, 1])
    s0x = s0x.reshape(N_GROUP_BLOCKS, 2, bx * n_seg_x, STATE_LANES)
    yp, sfin = _s5(hp, params, dsk, s0p, 1)
    yx, _ = _s5(hx, params, dsk, s0x, n_seg_x)
    new_state = _blocks_to_state(sfin)[:, None]

    fg = final_norm_g.reshape(1, D_MODEL)
    (yp_out,) = _post(xp, yp, mod[1], ctx_row, g2[1], wa, wb, w1, w2, 1, vpt_p, final_g=fg)
    (yx_out,) = _post(xx, yx, mod[1], lat_row, g2[1], wa, wb, w1, w2, 1, vpt_x, final_g=fg)

    to_cache = lambda t: t.reshape(bp, N_KV_HEADS, HEAD_DIM, lp).transpose(0, 3, 1, 2)[:, None]
    new_k = to_cache(kp)
    new_v = to_cache(vp)
    return (yp_out.reshape(bp, lp, D_MODEL), yx_out.reshape(bx, lx, D_MODEL), new_k, new_v, new_state)
```
